```python
import math
import jax, jax.numpy as jnp
from jax import lax
import numpy as np

D_MODEL = 1024
BATCH = 16
SEQ = 256
DEPTH = 1
DEC_BATCH = 8
DEC_SEQ = 1024
PAST_LEN = 256

GRID_W = 64
D_ATTN = 512
D_SSM = 512
N_HEADS = 4
QK_DIM = 64
V_DIM = 2 * QK_DIM
SSM_GROUP_CH = 16
N_SSM_GROUPS = D_SSM // SSM_GROUP_CH
SSM_STATE = 64
D_IN = 4 * D_ATTN + 2 * D_SSM
SPLITS = [D_ATTN, 2 * D_ATTN, 3 * D_ATTN, 4 * D_ATTN, 4 * D_ATTN + D_SSM]
ROPE_THETA = 10000.0
Q_BLOCK = 128
EPS = 1e-6
DT_MIN = 0.001
DT_MAX = 0.1

kernel_name = "hybrid_diffattn_s5_prefix_step"


def rmsnorm(x, g):
    xf = x.astype(jnp.float32)
    y = xf * lax.rsqrt(jnp.mean(xf * xf, axis=-1, keepdims=True) + EPS)
    return (y * g.astype(jnp.float32)).astype(x.dtype)


def adaln(cond, w_ada, b_ada):
    m = jax.nn.silu(cond) @ w_ada + b_ada
    return jnp.split(m, 3, axis=-1)


def rope_2d(x):
    L = x.shape[1]
    rows = L // GRID_W
    row = jnp.repeat(jnp.arange(rows, dtype=jnp.float32), GRID_W)
    col = jnp.tile(jnp.arange(GRID_W, dtype=jnp.float32), rows)
    n_freq = QK_DIM // 4
    inv = ROPE_THETA ** (-jnp.arange(n_freq, dtype=jnp.float32) / n_freq)

    def rot(seg, pos):
        ang = pos[:, None] * inv
        cos = jnp.cos(ang)[None, :, None, None, :]
        sin = jnp.sin(ang)[None, :, None, None, :]
        s1 = seg[..., :n_freq].astype(jnp.float32)
        s2 = seg[..., n_freq:].astype(jnp.float32)
        return jnp.concatenate([s1 * cos - s2 * sin, s2 * cos + s1 * sin], axis=-1)

    half = QK_DIM // 2
    return jnp.concatenate([rot(x[..., :half], row), rot(x[..., half:], col)], axis=-1).astype(x.dtype)


def diff_attention(q, k, v, lam):
    b, Lq = q.shape[0], q.shape[1]
    nblk = Lq // Q_BLOCK
    qb = jnp.moveaxis(q.reshape(b, nblk, Q_BLOCK, N_HEADS, 2, QK_DIM), 1, 0)
    scale = QK_DIM ** -0.5

    def block(qblk):
        s = jnp.einsum('bqhcd,bkhcd->bhcqk', qblk, k).astype(jnp.float32) * scale
        p = jax.nn.softmax(s, axis=-1)
        w = p[:, :, 0] - lam * p[:, :, 1]
        return jnp.einsum('bhqk,bkhe->bqhe', w.astype(v.dtype), v)

    o = lax.map(block, qb)
    return jnp.moveaxis(o, 0, 1).reshape(b, Lq, N_HEADS, V_DIM)


def zoh(A_re, A_im, log_dt, B_re, B_im):
    A_re = A_re.astype(jnp.float32)
    A_im = A_im.astype(jnp.float32)
    dt = jnp.exp(log_dt.astype(jnp.float32))[:, None]
    mag = jnp.exp(A_re * dt)
    ab_re = mag * jnp.cos(A_im * dt)
    ab_im = mag * jnp.sin(A_im * dt)
    nr, ni = ab_re - 1.0, ab_im
    den = A_re * A_re + A_im * A_im
    f_re = (nr * A_re + ni * A_im) / den
    f_im = (ni * A_re - nr * A_im) / den
    B_re = B_re.astype(jnp.float32)
    B_im = B_im.astype(jnp.float32)
    bb_re = f_re[..., None] * B_re - f_im[..., None] * B_im
    bb_im = f_re[..., None] * B_im + f_im[..., None] * B_re
    return ab_re, ab_im, bb_re, bb_im


def complex_combine(e1, e2):
    a1r, a1i, b1r, b1i = e1
    a2r, a2i, b2r, b2i = e2
    return (a1r * a2r - a1i * a2i,
            a1r * a2i + a1i * a2r,
            a2r * b1r - a2i * b1i + b2r,
            a2r * b1i + a2i * b1r + b2i)


def ssm_scan(u, ab_re, ab_im, bb_re, bb_im, C_re, C_im, h0_re, h0_im, reverse):
    bu_re = jnp.einsum('blgh,gph->blgp', u, bb_re)
    bu_im = jnp.einsum('blgh,gph->blgp', u, bb_im)
    if reverse:
        bu_re = jnp.flip(bu_re, axis=1)
        bu_im = jnp.flip(bu_im, axis=1)
    first_re = bu_re[:, 0] + ab_re * h0_re - ab_im * h0_im
    first_im = bu_im[:, 0] + ab_re * h0_im + ab_im * h0_re
    bu_re = bu_re.at[:, 0].set(first_re)
    bu_im = bu_im.at[:, 0].set(first_im)
    a_re = jnp.broadcast_to(ab_re, bu_re.shape)
    a_im = jnp.broadcast_to(ab_im, bu_im.shape)
    _, _, h_re, h_im = lax.associative_scan(complex_combine, (a_re, a_im, bu_re, bu_im), axis=1)
    fin_re, fin_im = h_re[:, -1], h_im[:, -1]
    if reverse:
        h_re = jnp.flip(h_re, axis=1)
        h_im = jnp.flip(h_im, axis=1)
    y = (jnp.einsum('blgp,ghp->blgh', h_re, C_re.astype(jnp.float32))
         - jnp.einsum('blgp,ghp->blgh', h_im, C_im.astype(jnp.float32)))
    return y, fin_re, fin_im


def ssm_branch(u, h0, A_re, A_im, log_dt, B_re, B_im, C_re, C_im, D_skip, with_state):
    b, L = u.shape[0], u.shape[1]
    uf = u.astype(jnp.float32).reshape(b, L, N_SSM_GROUPS, SSM_GROUP_CH)
    h0 = h0.astype(jnp.float32)
    ys, fins = [], []
    for d in range(2):
        ab_re, ab_im, bb_re, bb_im = zoh(A_re[d], A_im[d], log_dt[d], B_re[d], B_im[d])
        y, fr, fi = ssm_scan(uf, ab_re, ab_im, bb_re, bb_im, C_re[d], C_im[d],
                             h0[:, d, 0], h0[:, d, 1], reverse=(d == 1))
        ys.append(y)
        fins.append(jnp.stack([fr, fi], axis=1))
    y = (ys[0] + ys[1]).reshape(b, L, D_SSM) + D_skip.astype(jnp.float32) * uf.reshape(b, L, D_SSM)
    if with_state:
        return y, jnp.stack(fins, axis=1)
    return y, None


def mixer_inputs(x, shift, scale, norm_pre, w_in):
    h = rmsnorm(x, norm_pre) * (1 + scale) + shift
    proj = h @ w_in
    q, k, v, g_attn, u, g_ssm = jnp.split(proj, SPLITS, axis=-1)
    b, L = x.shape[0], x.shape[1]
    q = q.reshape(b, L, N_HEADS, 2, QK_DIM)
    k = k.reshape(b, L, N_HEADS, 2, QK_DIM)
    v = v.reshape(b, L, N_HEADS, V_DIM)
    return q, k, v, g_attn, u, g_ssm


def mixer_output(x, o_attn, lam_init, g_attn, y_ssm, g_ssm, gate,
                 subln, w_glu, b_glu, w_out, norm_post):
    b, L = x.shape[0], x.shape[1]
    o = (rmsnorm(o_attn, subln) * (1 - lam_init)).reshape(b, L, D_ATTN) * jax.nn.silu(g_attn)
    ys = jax.nn.gelu(y_ssm.astype(x.dtype))
    ys = ys * jax.nn.sigmoid(ys @ w_glu + b_glu) * jax.nn.silu(g_ssm)
    out = jnp.concatenate([o, ys], axis=-1) @ w_out
    return x + gate * rmsnorm(out, norm_post)


def setup_inputs(seed: int = 0) -> dict:
    key = jax.random.key(seed)
    ks = jax.random.split(key, 26)
    f32 = jnp.float32
    G, P, Hc = N_SSM_GROUPS, SSM_STATE, SSM_GROUP_CH
    n_idx = jnp.arange(P, dtype=f32)
    return {
        "x_prompt": jax.random.normal(ks[0], (BATCH, SEQ, D_MODEL), f32),
        "x_sample": jax.random.normal(ks[1], (DEC_BATCH, DEC_SEQ, D_MODEL), f32),
        "cache_k": jax.random.normal(ks[2], (DEC_BATCH, DEPTH, PAST_LEN, N_HEADS, 2 * QK_DIM), f32),
        "cache_v": jax.random.normal(ks[3], (DEC_BATCH, DEPTH, PAST_LEN, N_HEADS, V_DIM), f32),
        "state_ssm": 0.1 * jax.random.normal(ks[4], (DEC_BATCH, DEPTH, 2, 2, G, P), f32),
        "c": jax.random.normal(ks[5], (DEC_BATCH, D_MODEL), f32),
        "c_ctx": jax.random.normal(ks[6], (D_MODEL,), f32),
        "w_ada": jax.random.normal(ks[7], (DEPTH, D_MODEL, 3 * D_MODEL), f32) * D_MODEL ** -0.5,
        "b_ada": 0.01 * jax.random.normal(ks[8], (DEPTH, 3 * D_MODEL), f32),
        "norm_pre": 1.0 + 0.05 * jax.random.normal(ks[9], (DEPTH, D_MODEL), f32),
        "norm_post": 1.0 + 0.05 * jax.random.normal(ks[10], (DEPTH, D_MODEL), f32),
        "w_in": jax.random.normal(ks[11], (DEPTH, D_MODEL, D_IN), f32) * D_MODEL ** -0.5,
        "lambda_qk": 0.1 * jax.random.normal(ks[12], (DEPTH, 4, QK_DIM), f32),
        "subln": 1.0 + 0.05 * jax.random.normal(ks[13], (DEPTH, V_DIM), f32),
        "ssm_A_re": -0.5 + 0.01 * jax.random.normal(ks[14], (DEPTH, 2, G, P), f32),
        "ssm_A_im": math.pi * n_idx + 0.01 * jax.random.normal(ks[15], (DEPTH, 2, G, P), f32),
        "ssm_log_dt": jax.random.uniform(ks[16], (DEPTH, 2, G), f32,
                                         minval=math.log(DT_MIN), maxval=math.log(DT_MAX)),
        "ssm_B_re": jax.random.normal(ks[17], (DEPTH, 2, G, P, Hc), f32) * (2 * Hc) ** -0.5,
        "ssm_B_im": jax.random.normal(ks[18], (DEPTH, 2, G, P, Hc), f32) * (2 * Hc) ** -0.5,
        "ssm_C_re": jax.random.normal(ks[19], (DEPTH, 2, G, Hc, P), f32) * (2 * P) ** -0.5,
        "ssm_C_im": jax.random.normal(ks[20], (DEPTH, 2, G, Hc, P), f32) * (2 * P) ** -0.5,
        "ssm_D": jax.random.normal(ks[21], (DEPTH, D_SSM), f32),
        "w_glu": jax.random.normal(ks[22], (DEPTH, D_SSM, D_SSM), f32) * D_SSM ** -0.5,
        "b_glu": 0.01 * jax.random.normal(ks[23], (DEPTH, D_SSM), f32),
        "w_out": jax.random.normal(ks[24], (DEPTH, D_ATTN + D_SSM, D_MODEL), f32) * (D_ATTN + D_SSM) ** -0.5,
    }


def reference(x_prompt, x_sample, cache_k, cache_v, state_ssm, c, c_ctx, w_ada, b_ada,
              norm_pre, norm_post, w_in, lambda_qk, subln, ssm_A_re, ssm_A_im, ssm_log_dt,
              ssm_B_re, ssm_B_im, ssm_C_re, ssm_C_im, ssm_D, w_glu, b_glu, w_out):
    xp, xs = x_prompt, x_sample
    bp, Lp = xp.shape[0], xp.shape[1]
    new_k, new_v, new_s = [], [], []
    for l in range(DEPTH):
        lam_init = 0.8 - 0.6 * math.exp(-0.3 * l)
        lq = lambda_qk[l].astype(jnp.float32)
        lam = jnp.exp(jnp.sum(lq[0] * lq[1])) - jnp.exp(jnp.sum(lq[2] * lq[3])) + lam_init
        ssm_p = (ssm_A_re[l], ssm_A_im[l], ssm_log_dt[l], ssm_B_re[l], ssm_B_im[l],
                 ssm_C_re[l], ssm_C_im[l], ssm_D[l])
        out_p = (subln[l], w_glu[l], b_glu[l], w_out[l], norm_post[l])

        shift, scale, gate = adaln(c_ctx, w_ada[l], b_ada[l])
        q, k, v, g_attn, u, g_ssm = mixer_inputs(xp, shift, scale, norm_pre[l], w_in[l])
        o = diff_attention(q, k, v, lam)
        h0 = jnp.zeros((bp, 2, 2, N_SSM_GROUPS, SSM_STATE), jnp.float32)
        y, fin = ssm_branch(u, h0, *ssm_p, with_state=True)
        new_k.append(k.reshape(bp, Lp, N_HEADS, 2 * QK_DIM))
        new_v.append(v)
        new_s.append(fin.astype(xp.dtype))
        xp = mixer_output(xp, o, lam_init, g_attn, y, g_ssm, gate, *out_p)

        bs, Ls = xs.shape[0], xs.shape[1]
        shift, scale, gate = [m[:, None, :] for m in adaln(c, w_ada[l], b_ada[l])]
        q, k, v, g_attn, u, g_ssm = mixer_inputs(xs, shift, scale, norm_pre[l], w_in[l])
        q = rope_2d(q)
        k = rope_2d(k)
        ck = cache_k[:, l]
        k_all = jnp.concatenate([ck.reshape(bs, ck.shape[1], N_HEADS, 2, QK_DIM).astype(k.dtype), k], axis=1)
        v_all = jnp.concatenate([cache_v[:, l].astype(v.dtype), v], axis=1)
        o = diff_attention(q, k_all, v_all, lam)
        y, _ = ssm_branch(u, state_ssm[:, l], *ssm_p, with_state=False)
        xs = mixer_output(xs, o, lam_init, g_attn, y, g_ssm, gate, *out_p)

    new_cache_k = jnp.stack(new_k, axis=1)
    new_cache_v = jnp.stack(new_v, axis=1)
    new_state_ssm = jnp.stack(new_s, axis=1)
    return (xp, xs, new_cache_k, new_cache_v, new_state_ssm)
```

```python
import functools
import math

import jax
import jax.numpy as jnp
from jax import lax
from jax.experimental import pallas as pl
from jax.experimental.pallas import tpu as pltpu

F32 = jnp.float32
BF16 = jnp.bfloat16

GRID_W = 64
N_HEADS = 4
QK_DIM = 64
V_DIM = 2 * QK_DIM
SSM_GROUP_CH = 16
SSM_STATE = 64
ROPE_THETA = 10000.0
EPS = 1e-6

TOKEN_TILE = 256
SSM_ROWS = 512
GROUPS_PER_HALF = 16
VMEM_LIMIT = 48 * 1024 * 1024


def _silu(x):
    return x * jax.nn.sigmoid(x)


def _params(*sem):
    return pltpu.CompilerParams(dimension_semantics=sem, vmem_limit_bytes=VMEM_LIMIT)


def _adaln_kernel(c_ref, w_ref, b_ref, o_ref):
    s = _silu(c_ref[...]).astype(BF16)
    o_ref[...] = jnp.dot(s, w_ref[...].astype(BF16), preferred_element_type=F32) + b_ref[...]


def _adaln(cond, w_ada, b_ada):
    rows, d = cond.shape
    n = w_ada.shape[1]
    bn = 512
    return pl.pallas_call(
        _adaln_kernel,
        grid=(n // bn,),
        in_specs=[pl.BlockSpec((rows, d), lambda j: (0, 0)),
                  pl.BlockSpec((d, bn), lambda j: (0, j)),
                  pl.BlockSpec((1, bn), lambda j: (0, j))],
        out_specs=pl.BlockSpec((rows, bn), lambda j: (0, j)),
        out_shape=jax.ShapeDtypeStruct((rows, n), F32),
        compiler_params=_params("arbitrary"),
        name="adaln",
    )(cond, w_ada, b_ada.reshape(1, n))


def _ssmprep_kernel(are_ref, aim_ref, ldt_ref, bre_ref, bim_ref, lq_ref,
                    abre_ref, abim_ref, bbre_ref, bbim_ref, lam_ref, *, lam_init):
    a_re = are_ref[...]
    a_im = aim_ref[...]
    dt = jnp.exp(ldt_ref[...])
    mag = jnp.exp(a_re * dt)
    ab_re = mag * jnp.cos(a_im * dt)
    ab_im = mag * jnp.sin(a_im * dt)
    nr, ni = ab_re - 1.0, ab_im
    den = a_re * a_re + a_im * a_im
    f_re = (nr * a_re + ni * a_im) / den
    f_im = (ni * a_re - nr * a_im) / den
    abre_ref[...] = ab_re
    abim_ref[...] = ab_im
    b_re = bre_ref[...]
    b_im = bim_ref[...]
    fr = f_re[:, None, :]
    fi = f_im[:, None, :]
    bbre_ref[...] = fr * b_re - fi * b_im
    bbim_ref[...] = fr * b_im + fi * b_re
    lq = lq_ref[...]
    s01 = jnp.sum(lq[0:1] * lq[1:2], axis=-1, keepdims=True)
    s23 = jnp.sum(lq[2:3] * lq[3:4], axis=-1, keepdims=True)
    lam = jnp.exp(s01) - jnp.exp(s23) + lam_init
    lam_ref[...] = jnp.broadcast_to(lam, lam_ref.shape)


def _ssmprep(a_re, a_im, log_dt, b_re, b_im, lq, lam_init):
    nd, g, p = a_re.shape
    hc = b_re.shape[-1]
    r = nd * g
    ldt = jnp.broadcast_to(log_dt[..., None], (nd, g, p)).reshape(r, p)
    bt_re = jnp.swapaxes(b_re, -1, -2).reshape(r, hc, p)
    bt_im = jnp.swapaxes(b_im, -1, -2).reshape(r, hc, p)
    outs = pl.pallas_call(
        functools.partial(_ssmprep_kernel, lam_init=lam_init),
        out_shape=(jax.ShapeDtypeStruct((r, p), F32), jax.ShapeDtypeStruct((r, p), F32),
                   jax.ShapeDtypeStruct((r, hc, p), F32), jax.ShapeDtypeStruct((r, hc, p), F32),
                   jax.ShapeDtypeStruct((8, 128), F32)),
        name="ssmprep",
    )(a_re.reshape(r, p).astype(F32), a_im.reshape(r, p).astype(F32), ldt.astype(F32),
      bt_re.astype(F32), bt_im.astype(F32), lq.astype(F32))
    ab_re, ab_im, bb_re, bb_im, lam = outs
    return (ab_re.reshape(nd, g, p), ab_im.reshape(nd, g, p),
            bb_re.reshape(nd, g, hc, p), bb_im.reshape(nd, g, hc, p), lam)


def _inproj_kernel(*refs, rope, d_attn):
    if rope:
        (x_ref, shift_ref, scale_ref, g_ref, w_ref, cos_ref, sin_ref,
         q_ref, k_ref, v_ref, ga_ref, u_ref, gs_ref) = refs
    else:
        (x_ref, shift_ref, scale_ref, g_ref, w_ref,
         q_ref, k_ref, v_ref, ga_ref, u_ref, gs_ref) = refs
    x = x_ref[...]
    ms = jnp.mean(x * x, axis=-1, keepdims=True)
    h = x * lax.rsqrt(ms + EPS) * (g_ref[...] * (1.0 + scale_ref[...])) + shift_ref[...]
    proj = jnp.dot(h.astype(BF16), w_ref[...], preferred_element_type=F32)
    da = d_attn
    q = proj[:, 0:da]
    k = proj[:, da:2 * da]
    if rope:
        lane = lax.broadcasted_iota(jnp.int32, (1, da), 1)
        low = (lane % (QK_DIM // 2)) < (QK_DIM // 4)
        cos = cos_ref[...]
        sin = sin_ref[...]

        def rot(z):
            partner = jnp.where(low, pltpu.roll(z, da - QK_DIM // 4, 1), pltpu.roll(z, QK_DIM // 4, 1))
            return z * cos + partner * sin

        q = rot(q)
        k = rot(k)
    q_ref[...] = q.astype(q_ref.dtype)
    k_ref[...] = k.astype(k_ref.dtype)
    v_ref[...] = proj[:, 2 * da:3 * da].astype(v_ref.dtype)
    ga_ref[...] = _silu(proj[:, 3 * da:4 * da]).astype(ga_ref.dtype)
    d_ssm = u_ref.shape[-1]
    u_ref[...] = proj[:, 4 * da:4 * da + d_ssm].astype(u_ref.dtype)
    gs_ref[...] = _silu(proj[:, 4 * da + d_ssm:]).astype(gs_ref.dtype)


def _inproj(x, shift, scale, norm_pre, w_in_bf, rope_tabs, kv_dtype, d_attn, d_ssm):
    b, l, d = x.shape
    tt = TOKEN_TILE
    nt = l // tt
    per_batch = shift.shape[0] == b
    mod_map = (lambda i, j: (i, 0, 0)) if per_batch else (lambda i, j: (0, 0, 0))
    rope = rope_tabs is not None
    in_specs = [pl.BlockSpec((None, tt, d), lambda i, j: (i, j, 0)),
                pl.BlockSpec((None, 1, d), mod_map),
                pl.BlockSpec((None, 1, d), mod_map),
                pl.BlockSpec((1, d), lambda i, j: (0, 0)),
                pl.BlockSpec(w_in_bf.shape, lambda i, j: (0, 0))]
    args = [x, shift, scale, norm_pre.reshape(1, d).astype(F32), w_in_bf]
    if rope:
        in_specs += [pl.BlockSpec((tt, d_attn), lambda i, j: (j, 0))] * 2
        args += list(rope_tabs)
    row_spec = lambda w: pl.BlockSpec((tt, w), lambda i, j: (i * nt + j, 0))
    out_specs = [row_spec(d_attn), row_spec(d_attn), row_spec(d_attn), row_spec(d_attn),
                 pl.BlockSpec((tt, d_ssm), lambda i, j: (j, i)),
                 row_spec(d_ssm)]
    rows = b * l
    out_shape = [jax.ShapeDtypeStruct((rows, d_attn), BF16),
                 jax.ShapeDtypeStruct((rows, d_attn), kv_dtype),
                 jax.ShapeDtypeStruct((rows, d_attn), kv_dtype),
                 jax.ShapeDtypeStruct((rows, d_attn), BF16),
                 jax.ShapeDtypeStruct((l, b * d_ssm), BF16),
                 jax.ShapeDtypeStruct((rows, d_ssm), BF16)]
    return pl.pallas_call(
        functools.partial(_inproj_kernel, rope=rope, d_attn=d_attn),
        grid=(b, nt),
        in_specs=in_specs,
        out_specs=out_specs,
        out_shape=out_shape,
        compiler_params=_params("arbitrary", "arbitrary"),
        name="inproj_rope" if rope else "inproj",
    )(*args)


def _rope_tables(l, d_attn):
    rows = l // GRID_W
    row = jnp.repeat(jnp.arange(rows, dtype=F32), GRID_W)
    col = jnp.tile(jnp.arange(GRID_W, dtype=F32), rows)
    n_freq = QK_DIM // 4
    inv = ROPE_THETA ** (-jnp.arange(n_freq, dtype=F32) / n_freq)
    ang_r = row[:, None] * inv
    ang_c = col[:, None] * inv
    cos64 = jnp.concatenate([jnp.cos(ang_r), jnp.cos(ang_r), jnp.cos(ang_c), jnp.cos(ang_c)], axis=-1)
    sin64 = jnp.concatenate([-jnp.sin(ang_r), jnp.sin(ang_r), -jnp.sin(ang_c), jnp.sin(ang_c)], axis=-1)
    reps = d_attn // QK_DIM
    return jnp.tile(cos64, (1, reps)), jnp.tile(sin64, (1, reps))


def _attn_kernel(*refs, has_cache, lam_init):
    if has_cache:
        lam_ref, q_ref, k_ref, v_ref, ck_ref, cv_ref, ga_ref, sub_ref, o_ref = refs
    else:
        lam_ref, q_ref, k_ref, v_ref, ga_ref, sub_ref, o_ref = refs
    lam = lam_ref[0:1, 0:1]
    q = q_ref[...] * jnp.asarray(QK_DIM ** -0.5, BF16)
    lane = lax.broadcasted_iota(jnp.int32, (1, V_DIM), 1)
    first = lane < QK_DIM
    zero = jnp.zeros_like(q)
    q1 = jnp.where(first, q, zero)
    q2 = jnp.where(first, zero, q)
    k = k_ref[...].astype(BF16)
    v = v_ref[...].astype(BF16)
    if has_cache:
        k = jnp.concatenate([ck_ref[...].astype(BF16), k], axis=0)
        v = jnp.concatenate([cv_ref[...].astype(BF16), v], axis=0)
    dims = (((1,), (1,)), ((), ()))
    s1 = lax.dot_general(q1, k, dims, preferred_element_type=F32)
    s2 = lax.dot_general(q2, k, dims, preferred_element_type=F32)
    e1 = jnp.exp(s1 - jnp.max(s1, axis=-1, keepdims=True))
    e2 = jnp.exp(s2 - jnp.max(s2, axis=-1, keepdims=True))
    r1 = 1.0 / jnp.sum(e1, axis=-1, keepdims=True)
    r2 = lam / jnp.sum(e2, axis=-1, keepdims=True)
    w = e1 * r1 - e2 * r2
    o = jnp.dot(w.astype(BF16), v, preferred_element_type=F32)
    ms = jnp.mean(o * o, axis=-1, keepdims=True)
    o = o * lax.rsqrt(ms + EPS) * sub_ref[...] * (1.0 - lam_init)
    o_ref[...] = (o * ga_ref[...].astype(F32)).astype(o_ref.dtype)


def _attention(lam, q, k, v, cache_k, cache_v, ga, subln, lam_init, b, l):
    d_attn = q.shape[-1]
    tq = TOKEN_TILE
    nq = l // tq
    has_cache = cache_k is not None
    q3, k3, v3, ga3 = (a.reshape(b, l, d_attn) for a in (q, k, v, ga))
    q_spec = pl.BlockSpec((None, tq, V_DIM), lambda i, h, j: (i, j, h))
    kv_spec = pl.BlockSpec((None, l, V_DIM), lambda i, h, j: (i, 0, h))
    in_specs = [pl.BlockSpec((8, 128), lambda i, h, j: (0, 0)), q_spec, kv_spec, kv_spec]
    args = [lam, q3, k3, v3]
    if has_cache:
        lc = cache_k.shape[1]
        c_spec = pl.BlockSpec((None, lc, V_DIM), lambda i, h, j: (i, 0, h))
        in_specs += [c_spec, c_spec]
        args += [cache_k, cache_v]
    in_specs += [q_spec, pl.BlockSpec((1, V_DIM), lambda i, h, j: (0, 0))]
    args += [ga3, subln.reshape(1, V_DIM).astype(F32)]
    out = pl.pallas_call(
        functools.partial(_attn_kernel, has_cache=has_cache, lam_init=lam_init),
        grid=(b, N_HEADS, nq),
        in_specs=in_specs,
        out_specs=q_spec,
        out_shape=jax.ShapeDtypeStruct((b, l, d_attn), BF16),
        compiler_params=_params("arbitrary", "arbitrary", "arbitrary"),
        name="attn_cache" if has_cache else "attn",
    )(*args)
    return out


def _ssm_kernel(*refs, batch, steps, n_chunks, lane_slice, with_state):
    if with_state:
        u_ref, bh_ref, cre_ref, cim_ref, are_ref, aim_ref, h0_ref, y_ref, fin_ref, bu_ref, st_ref = refs
    else:
        u_ref, bh_ref, cre_ref, cim_ref, are_ref, aim_ref, h0_ref, y_ref, bu_ref, st_ref = refs
    d = pl.program_id(0)
    c = pl.program_id(1)
    half_w = st_ref.shape[-1]
    kw = bh_ref.shape[2]
    ow = cre_ref.shape[-1]

    @pl.when(c == 0)
    def _():
        st_ref[...] = h0_ref[0]

    for hf in range(2):
        bu_ref[...] = jnp.dot(u_ref[:, hf * kw:(hf + 1) * kw], bh_ref[0, hf],
                              preferred_element_type=F32)
        for lo in range(0, half_w, lane_slice):
            re_l = pl.ds(lo, lane_slice)
            im_l = pl.ds(half_w + lo, lane_slice)
            ar = jnp.broadcast_to(are_ref[0, hf, :, re_l], (batch, lane_slice))
            ai = jnp.broadcast_to(aim_ref[0, hf, :, re_l], (batch, lane_slice))

            def step(t, carry, re_l=re_l, im_l=im_l, ar=ar, ai=ai):
                hr, hi = carry
                tt = jnp.where(d == 0, t, steps - 1 - t)
                rows = pl.ds(pl.multiple_of(tt * batch, batch), batch)
                nr = ar * hr - ai * hi + bu_ref[rows, re_l]
                ni = ar * hi + ai * hr + bu_ref[rows, im_l]
                bu_ref[rows, re_l] = nr
                bu_ref[rows, im_l] = ni
                return nr, ni

            hr, hi = lax.fori_loop(0, steps, step,
                                   (st_ref[hf, 0, :, re_l], st_ref[hf, 1, :, re_l]), unroll=4)
            st_ref[hf, 0, :, re_l] = hr
            st_ref[hf, 1, :, re_l] = hi
        h_re = bu_ref[:, 0:half_w].astype(BF16)
        h_im = bu_ref[:, half_w:2 * half_w].astype(BF16)
        y = (jnp.dot(h_re, cre_ref[0, hf], preferred_element_type=F32)
             - jnp.dot(h_im, cim_ref[0, hf], preferred_element_type=F32))
        y_ref[0, :, hf * ow:(hf + 1) * ow] = y.astype(y_ref.dtype)

    if with_state:
        @pl.when(c == n_chunks - 1)
        def _():
            fin_ref[0] = st_ref[...]


def _ssm(u_tb, b_half, cre_half, cim_half, are_half, aim_half, h0, batch, l, with_state):
    d_ssm = u_tb.shape[-1]
    steps = SSM_ROWS // batch
    n_chunks = l // steps
    half_w = are_half.shape[-1]
    lane_slice = 4096 // batch
    chunk = lambda d, c: c + d * (n_chunks - 1 - 2 * c)
    per_dir = lambda shape: pl.BlockSpec((1,) + shape[1:], lambda d, c: (d,) + (0,) * (len(shape) - 1))
    in_specs = [pl.BlockSpec((SSM_ROWS, d_ssm), lambda d, c: (chunk(d, c), 0)),
                per_dir(b_half.shape), per_dir(cre_half.shape), per_dir(cim_half.shape),
                per_dir(are_half.shape), per_dir(aim_half.shape), per_dir(h0.shape)]
    out_specs = [pl.BlockSpec((1, SSM_ROWS, d_ssm), lambda d, c: (d, chunk(d, c), 0))]
    out_shape = [jax.ShapeDtypeStruct((2, l * batch, d_ssm), BF16)]
    if with_state:
        out_specs.append(per_dir(h0.shape))
        out_shape.append(jax.ShapeDtypeStruct(h0.shape, F32))
    outs = pl.pallas_call(
        functools.partial(_ssm_kernel, batch=batch, steps=steps, n_chunks=n_chunks,
                          lane_slice=lane_slice, with_state=with_state),
        grid=(2, n_chunks),
        in_specs=in_specs,
        out_specs=out_specs,
        out_shape=out_shape,
        scratch_shapes=[pltpu.VMEM((SSM_ROWS, 2 * half_w), F32),
                        pltpu.VMEM((2, 2, batch, half_w), F32)],
        compiler_params=_params("arbitrary", "arbitrary"),
        name="ssm_state" if with_state else "ssm",
    )(u_tb, b_half, cre_half, cim_half, are_half, aim_half, h0)
    return outs


def _ssm_layout(ab_re, ab_im, bb_re, bb_im, c_re, c_im):
    nd, g, hc, p = bb_re.shape
    gl = GROUPS_PER_HALF
    nh = g // gl
    eye = jnp.eye(gl, dtype=F32)
    bbc = jnp.stack([bb_re, bb_im], axis=3).reshape(nd, nh, gl, hc, 2, p)
    b_half = jnp.einsum('dhgicp,gk->dhgickp', bbc, eye).reshape(nd, nh, gl * hc, 2 * gl * p).astype(BF16)

    def c_layout(cm):
        cm = cm.astype(F32).reshape(nd, nh, gl, hc, p)
        return jnp.einsum('dhgop,gk->dhgpko', cm, eye).reshape(nd, nh, gl * p, gl * hc).astype(BF16)

    a_layout = lambda a: a.reshape(nd, nh, 1, gl * p)
    return b_half, c_layout(c_re), c_layout(c_im), a_layout(ab_re), a_layout(ab_im)


def _state_to_lanes(h0):
    b, nd, nc, g, p = h0.shape
    gl = GROUPS_PER_HALF
    return h0.astype(F32).reshape(b, nd, nc, g // gl, gl * p).transpose(1, 3, 2, 0, 4)


def _state_from_lanes(fin, g, p):
    nd, nh, nc, b, _ = fin.shape
    return fin.transpose(3, 0, 2, 1, 4).reshape(b, nd, nc, g, p)


def _outproj_kernel(x_ref, a_ref, y0_ref, y1_ref, u_ref, gs_ref, gate_ref, dskip_ref,
                    wglu_ref, bglu_ref, wout_ref, g_ref, o_ref):
    u = u_ref[...].astype(F32)
    y = y0_ref[0].astype(F32) + y1_ref[0].astype(F32) + dskip_ref[...] * u
    c0 = math.sqrt(2.0 / math.pi)
    ys = 0.5 * y * (1.0 + jnp.tanh(c0 * (y + 0.044715 * (y * y * y))))
    z = jnp.dot(ys.astype(BF16), wglu_ref[...], preferred_element_type=F32) + bglu_ref[...]
    ys = ys * jax.nn.sigmoid(z) * gs_ref[...].astype(F32)
    cat = jnp.concatenate([a_ref[...], ys.astype(BF16)], axis=-1)
    out = jnp.dot(cat, wout_ref[...], preferred_element_type=F32)
    ms = jnp.mean(out * out, axis=-1, keepdims=True)
    out = out * lax.rsqrt(ms + EPS) * g_ref[...]
    o_ref[...] = x_ref[...] + gate_ref[...] * out


def _outproj(x, a_out, y_dirs, u_tb, gs, gate, d_skip, w_glu_bf, b_glu, w_out_bf, norm_post):
    b, l, d = x.shape
    d_attn = a_out.shape[-1]
    d_ssm = gs.shape[-1]
    tt = TOKEN_TILE
    nt = l // tt
    per_batch = gate.shape[0] == b
    gate_map = (lambda i, j: (i, 0, 0)) if per_batch else (lambda i, j: (0, 0, 0))
    y_tb = y_dirs.reshape(2, l, b * d_ssm)
    const = lambda shape: pl.BlockSpec(shape, lambda i, j: (0,) * len(shape))
    in_specs = [pl.BlockSpec((None, tt, d), lambda i, j: (i, j, 0)),
                pl.BlockSpec((None, tt, d_attn), lambda i, j: (i, j, 0)),
                pl.BlockSpec((1, tt, d_ssm), lambda i, j: (0, j, i)),
                pl.BlockSpec((1, tt, d_ssm), lambda i, j: (1, j, i)),
                pl.BlockSpec((tt, d_ssm), lambda i, j: (j, i)),
                pl.BlockSpec((tt, d_ssm), lambda i, j: (i * nt + j, 0)),
                pl.BlockSpec((None, 1, d), gate_map),
                const((1, d_ssm)), const(w_glu_bf.shape), const((1, d_ssm)),
                const(w_out_bf.shape), const((1, d))]
    return pl.pallas_call(
        _outproj_kernel,
        grid=(b, nt),
        in_specs=in_specs,
        out_specs=pl.BlockSpec((None, tt, d), lambda i, j: (i, j, 0)),
        out_shape=jax.ShapeDtypeStruct((b, l, d), F32),
        compiler_params=_params("arbitrary", "arbitrary"),
        name="outproj",
    )(x, a_out, y_tb, y_tb, u_tb, gs, gate, d_skip.reshape(1, d_ssm).astype(F32), w_glu_bf,
      b_glu.reshape(1, d_ssm).astype(F32), w_out_bf, norm_post.reshape(1, d).astype(F32))


def _mixer(x, mod, h0_lanes, cache_k, cache_v, rope_tabs, kv_dtype, with_state, lam, lam_init,
           norm_pre, w_in_bf, ssm_ops, subln, d_skip, w_glu_bf, b_glu, w_out_bf, norm_post):
    b, l, d = x.shape
    d_ssm = w_glu_bf.shape[0]
    d_attn = (w_in_bf.shape[1] - 2 * d_ssm) // 4
    shift, scale, gate = mod
    q, k, v, ga, u_lb, gs = _inproj(x, shift, scale, norm_pre, w_in_bf, rope_tabs, kv_dtype,
                                    d_attn, d_ssm)
    a_out = _attention(lam, q, k, v, cache_k, cache_v, ga, subln, lam_init, b, l)
    u_tb = u_lb.reshape(l * b, d_ssm)
    outs = _ssm(u_tb, *ssm_ops, h0_lanes, b, l, with_state)
    y = _outproj(x, a_out, outs[0], u_lb, gs, gate, d_skip, w_glu_bf, b_glu, w_out_bf, norm_post)
    fin = outs[1] if with_state else None
    return y, k, v, fin


def kernel(x_prompt, x_sample, cache_k, cache_v, state_ssm, c, c_ctx, w_ada, b_ada, norm_pre, norm_post, w_in, lambda_qk, subln, ssm_A_re, ssm_A_im, ssm_log_dt, ssm_B_re, ssm_B_im, ssm_C_re, ssm_C_im, ssm_D, w_glu, b_glu, w_out):
    xp, xs = x_prompt, x_sample
    bp, lp, d = xp.shape
    bs, ls, _ = xs.shape
    depth = w_in.shape[0]
    g, p = ssm_A_re.shape[-2:]
    d_ssm = w_glu.shape[-1]
    d_attn = (w_in.shape[-1] - 2 * d_ssm) // 4
    rope_tabs = _rope_tables(ls, d_attn)
    cond_rows = 16
    cond = jnp.zeros((cond_rows, d), F32).at[:bs].set(c.astype(F32)).at[bs].set(c_ctx.astype(F32))
    new_k, new_v, new_s = [], [], []
    for layer in range(depth):
        lam_init = 0.8 - 0.6 * math.exp(-0.3 * layer)
        m = _adaln(cond, w_ada[layer], b_ada[layer])
        mod_s = tuple(m[:bs, i * d:(i + 1) * d].reshape(bs, 1, d) for i in range(3))
        mod_p = tuple(m[bs:bs + 1, i * d:(i + 1) * d].reshape(1, 1, d) for i in range(3))
        ab_re, ab_im, bb_re, bb_im, lam = _ssmprep(
            ssm_A_re[layer], ssm_A_im[layer], ssm_log_dt[layer], ssm_B_re[layer], ssm_B_im[layer],
            lambda_qk[layer], lam_init)
        ssm_ops = _ssm_layout(ab_re, ab_im, bb_re, bb_im, ssm_C_re[layer], ssm_C_im[layer])
        shared = (lam, lam_init, norm_pre[layer], w_in[layer].astype(BF16), ssm_ops, subln[layer],
                  ssm_D[layer], w_glu[layer].astype(BF16), b_glu[layer], w_out[layer].astype(BF16),
                  norm_post[layer])

        h0_p = jnp.zeros((2, g // GROUPS_PER_HALF, 2, bp, GROUPS_PER_HALF * p), F32)
        xp, k_p, v_p, fin = _mixer(xp, mod_p, h0_p, None, None, None, F32, True, *shared)
        new_k.append(k_p.reshape(bp, lp, N_HEADS, 2 * QK_DIM))
        new_v.append(v_p.reshape(bp, lp, N_HEADS, V_DIM))
        new_s.append(_state_from_lanes(fin, g, p).astype(xp.dtype))

        ck = cache_k[:, layer].reshape(bs, -1, d_attn)
        cv = cache_v[:, layer].reshape(bs, -1, d_attn)
        h0_s = _state_to_lanes(state_ssm[:, layer])
        xs, _, _, _ = _mixer(xs, mod_s, h0_s, ck, cv, rope_tabs, BF16, False, *shared)

    return (xp, xs, jnp.stack(new_k, axis=1), jnp.stack(new_v, axis=1), jnp.stack(new_s, axis=1))
```

```python
import functools
import math

import jax
import jax.numpy as jnp
from jax import lax
from jax.experimental import pallas as pl
from jax.experimental.pallas import tpu as pltpu

F32 = jnp.float32
BF16 = jnp.bfloat16

GRID_W = 64
N_HEADS = 4
QK_DIM = 64
V_DIM = 2 * QK_DIM
SSM_GROUP_CH = 16
SSM_STATE = 64
ROPE_THETA = 10000.0
EPS = 1e-6
Q_PRESCALE = math.log2(math.e) * QK_DIM ** -0.5

TOKEN_TILE = 256
SSM_ROWS = 512
GROUPS_PER_HALF = 16
VMEM_LIMIT = 48 * 1024 * 1024


def _silu(x):
    return x * jax.nn.sigmoid(x)


def _params(*sem):
    return pltpu.CompilerParams(dimension_semantics=sem, vmem_limit_bytes=VMEM_LIMIT)


def _adaln_kernel(c_ref, w_ref, b_ref, o_ref):
    s = _silu(c_ref[...]).astype(BF16)
    o_ref[...] = jnp.dot(s, w_ref[...].astype(BF16), preferred_element_type=F32) + b_ref[...]


def _adaln(cond, w_ada, b_ada):
    rows, d = cond.shape
    n = w_ada.shape[1]
    bn = 512
    return pl.pallas_call(
        _adaln_kernel,
        grid=(n // bn,),
        in_specs=[pl.BlockSpec((rows, d), lambda j: (0, 0)),
                  pl.BlockSpec((d, bn), lambda j: (0, j)),
                  pl.BlockSpec((1, bn), lambda j: (0, j))],
        out_specs=pl.BlockSpec((rows, bn), lambda j: (0, j)),
        out_shape=jax.ShapeDtypeStruct((rows, n), F32),
        compiler_params=_params("arbitrary"),
        name="adaln",
    )(cond, w_ada, b_ada.reshape(1, n))


def _ssmprep_kernel(are_ref, aim_ref, ldt_ref, bre_ref, bim_ref, lq_ref,
                    abre_ref, abim_ref, bbre_ref, bbim_ref, lam_ref, *, lam_init):
    a_re = are_ref[...]
    a_im = aim_ref[...]
    dt = jnp.exp(ldt_ref[...])
    mag = jnp.exp(a_re * dt)
    ab_re = mag * jnp.cos(a_im * dt)
    ab_im = mag * jnp.sin(a_im * dt)
    nr, ni = ab_re - 1.0, ab_im
    den = a_re * a_re + a_im * a_im
    f_re = (nr * a_re + ni * a_im) / den
    f_im = (ni * a_re - nr * a_im) / den
    abre_ref[...] = ab_re
    abim_ref[...] = ab_im
    b_re = bre_ref[...]
    b_im = bim_ref[...]
    fr = f_re[:, None, :]
    fi = f_im[:, None, :]
    bbre_ref[...] = fr * b_re - fi * b_im
    bbim_ref[...] = fr * b_im + fi * b_re
    lq = lq_ref[...]
    s01 = jnp.sum(lq[0:1] * lq[1:2], axis=-1, keepdims=True)
    s23 = jnp.sum(lq[2:3] * lq[3:4], axis=-1, keepdims=True)
    lam = jnp.exp(s01) - jnp.exp(s23) + lam_init
    lam_ref[...] = jnp.broadcast_to(lam, lam_ref.shape)


def _ssmprep(a_re, a_im, log_dt, b_re, b_im, lq, lam_init):
    nd, g, p = a_re.shape
    hc = b_re.shape[-1]
    r = nd * g
    ldt = jnp.broadcast_to(log_dt[..., None], (nd, g, p)).reshape(r, p)
    bt_re = jnp.swapaxes(b_re, -1, -2).reshape(r, hc, p)
    bt_im = jnp.swapaxes(b_im, -1, -2).reshape(r, hc, p)
    outs = pl.pallas_call(
        functools.partial(_ssmprep_kernel, lam_init=lam_init),
        out_shape=(jax.ShapeDtypeStruct((r, p), F32), jax.ShapeDtypeStruct((r, p), F32),
                   jax.ShapeDtypeStruct((r, hc, p), F32), jax.ShapeDtypeStruct((r, hc, p), F32),
                   jax.ShapeDtypeStruct((8, 128), F32)),
        name="ssmprep",
    )(a_re.reshape(r, p).astype(F32), a_im.reshape(r, p).astype(F32), ldt.astype(F32),
      bt_re.astype(F32), bt_im.astype(F32), lq.astype(F32))
    ab_re, ab_im, bb_re, bb_im, lam = outs
    return (ab_re.reshape(nd, g, p), ab_im.reshape(nd, g, p),
            bb_re.reshape(nd, g, hc, p), bb_im.reshape(nd, g, hc, p), lam)


def _inproj_kernel(*refs, rope, d_attn):
    if rope:
        (x_ref, shift_ref, scale_ref, g_ref, w_ref, cos_ref, sin_ref,
         q_ref, k_ref, v_ref, ga_ref, u_ref, gs_ref) = refs
    else:
        (x_ref, shift_ref, scale_ref, g_ref, w_ref,
         q_ref, k_ref, v_ref, ga_ref, u_ref, gs_ref) = refs
    x = x_ref[...]
    ms = jnp.mean(x * x, axis=-1, keepdims=True)
    h = x * lax.rsqrt(ms + EPS) * (g_ref[...] * (1.0 + scale_ref[...])) + shift_ref[...]
    proj = jnp.dot(h.astype(BF16), w_ref[...], preferred_element_type=F32)
    da = d_attn
    q = proj[:, 0:da]
    k = proj[:, da:2 * da]
    if rope:
        lane = lax.broadcasted_iota(jnp.int32, (1, da), 1)
        low = (lane % (QK_DIM // 2)) < (QK_DIM // 4)
        cos = cos_ref[...]
        sin = sin_ref[...]

        def rot(z):
            partner = jnp.where(low, pltpu.roll(z, da - QK_DIM // 4, 1), pltpu.roll(z, QK_DIM // 4, 1))
            return z * cos + partner * sin

        q = rot(q)
        k = rot(k)
    q_ref[...] = (q * Q_PRESCALE).astype(q_ref.dtype)
    k_ref[...] = k.astype(k_ref.dtype)
    v_ref[...] = proj[:, 2 * da:3 * da].astype(v_ref.dtype)
    ga_ref[...] = _silu(proj[:, 3 * da:4 * da]).astype(ga_ref.dtype)
    d_ssm = u_ref.shape[-1]
    u_ref[...] = proj[:, 4 * da:4 * da + d_ssm].astype(u_ref.dtype)
    gs_ref[...] = _silu(proj[:, 4 * da + d_ssm:]).astype(gs_ref.dtype)


def _inproj(x, shift, scale, norm_pre, w_in_bf, rope_tabs, kv_dtype, d_attn, d_ssm):
    b, l, d = x.shape
    tt = TOKEN_TILE
    nt = l // tt
    per_batch = shift.shape[0] == b
    mod_map = (lambda i, j: (i, 0, 0)) if per_batch else (lambda i, j: (0, 0, 0))
    rope = rope_tabs is not None
    in_specs = [pl.BlockSpec((None, tt, d), lambda i, j: (i, j, 0)),
                pl.BlockSpec((None, 1, d), mod_map),
                pl.BlockSpec((None, 1, d), mod_map),
                pl.BlockSpec((1, d), lambda i, j: (0, 0)),
                pl.BlockSpec(w_in_bf.shape, lambda i, j: (0, 0))]
    args = [x, shift, scale, norm_pre.reshape(1, d).astype(F32), w_in_bf]
    if rope:
        in_specs += [pl.BlockSpec((tt, d_attn), lambda i, j: (j, 0))] * 2
        args += list(rope_tabs)
    row_spec = lambda w: pl.BlockSpec((tt, w), lambda i, j: (i * nt + j, 0))
    out_specs = [row_spec(d_attn), row_spec(d_attn), row_spec(d_attn), row_spec(d_attn),
                 pl.BlockSpec((tt, d_ssm), lambda i, j: (j, i)),
                 row_spec(d_ssm)]
    rows = b * l
    out_shape = [jax.ShapeDtypeStruct((rows, d_attn), BF16),
                 jax.ShapeDtypeStruct((rows, d_attn), kv_dtype),
                 jax.ShapeDtypeStruct((rows, d_attn), kv_dtype),
                 jax.ShapeDtypeStruct((rows, d_attn), BF16),
                 jax.ShapeDtypeStruct((l, b * d_ssm), BF16),
                 jax.ShapeDtypeStruct((rows, d_ssm), BF16)]
    return pl.pallas_call(
        functools.partial(_inproj_kernel, rope=rope, d_attn=d_attn),
        grid=(b, nt),
        in_specs=in_specs,
        out_specs=out_specs,
        out_shape=out_shape,
        compiler_params=_params("arbitrary", "arbitrary"),
        name="inproj_rope" if rope else "inproj",
    )(*args)


def _rope_tables(l, d_attn):
    rows = l // GRID_W
    row = jnp.repeat(jnp.arange(rows, dtype=F32), GRID_W)
    col = jnp.tile(jnp.arange(GRID_W, dtype=F32), rows)
    n_freq = QK_DIM // 4
    inv = ROPE_THETA ** (-jnp.arange(n_freq, dtype=F32) / n_freq)
    ang_r = row[:, None] * inv
    ang_c = col[:, None] * inv
    cos64 = jnp.concatenate([jnp.cos(ang_r), jnp.cos(ang_r), jnp.cos(ang_c), jnp.cos(ang_c)], axis=-1)
    sin64 = jnp.concatenate([-jnp.sin(ang_r), jnp.sin(ang_r), -jnp.sin(ang_c), jnp.sin(ang_c)], axis=-1)
    reps = d_attn // QK_DIM
    return jnp.tile(cos64, (1, reps)), jnp.tile(sin64, (1, reps))


def _attn_kernel(*refs, has_cache, lam_init):
    if has_cache:
        lam_ref, q_ref, k_ref, v_ref, ck_ref, cv_ref, ga_ref, sub_ref, o_ref, k_scr, v_scr = refs
    else:
        lam_ref, q_ref, k_ref, v_ref, ga_ref, sub_ref, o_ref, k_scr, v_scr = refs
    lk = k_scr.shape[0]
    ln = k_ref.shape[0]

    @pl.when(pl.program_id(2) == 0)
    def _():
        if has_cache:
            k_scr[0:lk - ln, :] = ck_ref[...].astype(BF16)
            v_scr[0:lk - ln, 0:V_DIM] = cv_ref[...].astype(BF16)
        k_scr[lk - ln:lk, :] = k_ref[...].astype(BF16)
        v_scr[lk - ln:lk, 0:V_DIM] = v_ref[...].astype(BF16)
        v_scr[:, V_DIM:2 * V_DIM] = jnp.ones((lk, V_DIM), BF16)

    lam = lam_ref[0:1, 0:1]
    q = q_ref[...]
    lane = lax.broadcasted_iota(jnp.int32, (1, V_DIM), 1)
    first = lane < QK_DIM
    zero = jnp.zeros_like(q)
    k = k_scr[...]
    v = v_scr[...]
    dims = (((1,), (1,)), ((), ()))

    def branch(qm):
        s = lax.dot_general(qm, k, dims, preferred_element_type=F32)
        e = jnp.exp2(s - jnp.max(s, axis=-1, keepdims=True)).astype(BF16)
        return jnp.dot(e, v, preferred_element_type=F32)

    o1 = branch(jnp.where(first, q, zero))
    o2 = branch(jnp.where(first, zero, q))
    r1 = 1.0 / o1[:, V_DIM:V_DIM + 1]
    r2 = lam / o2[:, V_DIM:V_DIM + 1]
    o = o1[:, 0:V_DIM] * r1 - o2[:, 0:V_DIM] * r2
    ms = jnp.mean(o * o, axis=-1, keepdims=True)
    o = o * lax.rsqrt(ms + EPS) * sub_ref[...] * (1.0 - lam_init)
    o_ref[...] = (o * ga_ref[...].astype(F32)).astype(o_ref.dtype)


def _attention(lam, q, k, v, cache_k, cache_v, ga, subln, lam_init, b, l):
    d_attn = q.shape[-1]
    tq = TOKEN_TILE
    nq = l // tq
    has_cache = cache_k is not None
    q3, k3, v3, ga3 = (a.reshape(b, l, d_attn) for a in (q, k, v, ga))
    q_spec = pl.BlockSpec((None, tq, V_DIM), lambda i, h, j: (i, j, h))
    kv_spec = pl.BlockSpec((None, l, V_DIM), lambda i, h, j: (i, 0, h))
    in_specs = [pl.BlockSpec((8, 128), lambda i, h, j: (0, 0)), q_spec, kv_spec, kv_spec]
    args = [lam, q3, k3, v3]
    if has_cache:
        lc = cache_k.shape[1]
        c_spec = pl.BlockSpec((None, lc, V_DIM), lambda i, h, j: (i, 0, h))
        in_specs += [c_spec, c_spec]
        args += [cache_k, cache_v]
    in_specs += [q_spec, pl.BlockSpec((1, V_DIM), lambda i, h, j: (0, 0))]
    args += [ga3, subln.reshape(1, V_DIM).astype(F32)]
    lk = l + (cache_k.shape[1] if has_cache else 0)
    out = pl.pallas_call(
        functools.partial(_attn_kernel, has_cache=has_cache, lam_init=lam_init),
        grid=(b, N_HEADS, nq),
        in_specs=in_specs,
        out_specs=q_spec,
        out_shape=jax.ShapeDtypeStruct((b, l, d_attn), BF16),
        scratch_shapes=[pltpu.VMEM((lk, V_DIM), BF16), pltpu.VMEM((lk, 2 * V_DIM), BF16)],
        compiler_params=_params("arbitrary", "arbitrary", "arbitrary"),
        name="attn_cache" if has_cache else "attn",
    )(*args)
    return out


def _ssm_kernel(*refs, batch, steps, n_chunks, lane_slice, with_state):
    if with_state:
        u_ref, bh_ref, cre_ref, cim_ref, are_ref, aim_ref, h0_ref, y_ref, fin_ref, bu_ref, st_ref = refs
    else:
        u_ref, bh_ref, cre_ref, cim_ref, are_ref, aim_ref, h0_ref, y_ref, bu_ref, st_ref = refs
    d = pl.program_id(0)
    c = pl.program_id(1)
    half_w = st_ref.shape[-1]
    kw = bh_ref.shape[2]
    ow = cre_ref.shape[-1]

    @pl.when(c == 0)
    def _():
        st_ref[...] = h0_ref[0]

    for hf in range(2):
        bu_ref[...] = jnp.dot(u_ref[:, hf * kw:(hf + 1) * kw], bh_ref[0, hf],
                              preferred_element_type=F32)
        for lo in range(0, half_w, lane_slice):
            re_l = pl.ds(lo, lane_slice)
            im_l = pl.ds(half_w + lo, lane_slice)
            ar = jnp.broadcast_to(are_ref[0, hf, :, re_l], (batch, lane_slice))
            ai = jnp.broadcast_to(aim_ref[0, hf, :, re_l], (batch, lane_slice))

            def step(t, carry, re_l=re_l, im_l=im_l, ar=ar, ai=ai):
                hr, hi = carry
                tt = jnp.where(d == 0, t, steps - 1 - t)
                rows = pl.ds(pl.multiple_of(tt * batch, batch), batch)
                nr = ar * hr - ai * hi + bu_ref[rows, re_l]
                ni = ar * hi + ai * hr + bu_ref[rows, im_l]
                bu_ref[rows, re_l] = nr
                bu_ref[rows, im_l] = ni
                return nr, ni

            hr, hi = lax.fori_loop(0, steps, step,
                                   (st_ref[hf, 0, :, re_l], st_ref[hf, 1, :, re_l]), unroll=4)
            st_ref[hf, 0, :, re_l] = hr
            st_ref[hf, 1, :, re_l] = hi
        h_re = bu_ref[:, 0:half_w].astype(BF16)
        h_im = bu_ref[:, half_w:2 * half_w].astype(BF16)
        y = (jnp.dot(h_re, cre_ref[0, hf], preferred_element_type=F32)
             - jnp.dot(h_im, cim_ref[0, hf], preferred_element_type=F32))
        y_ref[0, :, hf * ow:(hf + 1) * ow] = y.astype(y_ref.dtype)

    if with_state:
        @pl.when(c == n_chunks - 1)
        def _():
            fin_ref[0] = st_ref[...]


def _ssm(u_tb, b_half, cre_half, cim_half, are_half, aim_half, h0, batch, l, with_state):
    d_ssm = u_tb.shape[-1]
    steps = SSM_ROWS // batch
    n_chunks = l // steps
    half_w = are_half.shape[-1]
    lane_slice = 4096 // batch
    chunk = lambda d, c: c + d * (n_chunks - 1 - 2 * c)
    per_dir = lambda shape: pl.BlockSpec((1,) + shape[1:], lambda d, c: (d,) + (0,) * (len(shape) - 1))
    in_specs = [pl.BlockSpec((SSM_ROWS, d_ssm), lambda d, c: (chunk(d, c), 0)),
                per_dir(b_half.shape), per_dir(cre_half.shape), per_dir(cim_half.shape),
                per_dir(are_half.shape), per_dir(aim_half.shape), per_dir(h0.shape)]
    out_specs = [pl.BlockSpec((1, SSM_ROWS, d_ssm), lambda d, c: (d, chunk(d, c), 0))]
    out_shape = [jax.ShapeDtypeStruct((2, l * batch, d_ssm), BF16)]
    if with_state:
        out_specs.append(per_dir(h0.shape))
        out_shape.append(jax.ShapeDtypeStruct(h0.shape, F32))
    outs = pl.pallas_call(
        functools.partial(_ssm_kernel, batch=batch, steps=steps, n_chunks=n_chunks,
                          lane_slice=lane_slice, with_state=with_state),
        grid=(2, n_chunks),
        in_specs=in_specs,
        out_specs=out_specs,
        out_shape=out_shape,
        scratch_shapes=[pltpu.VMEM((SSM_ROWS, 2 * half_w), F32),
                        pltpu.VMEM((2, 2, batch, half_w), F32)],
        compiler_params=_params("arbitrary", "arbitrary"),
        name="ssm_state" if with_state else "ssm",
    )(u_tb, b_half, cre_half, cim_half, are_half, aim_half, h0)
    return outs


def _ssm_layout(ab_re, ab_im, bb_re, bb_im, c_re, c_im):
    nd, g, hc, p = bb_re.shape
    gl = GROUPS_PER_HALF
    nh = g // gl
    eye = jnp.eye(gl, dtype=F32)
    bbc = jnp.stack([bb_re, bb_im], axis=3).reshape(nd, nh, gl, hc, 2, p)
    b_half = jnp.einsum('dhgicp,gk->dhgickp', bbc, eye).reshape(nd, nh, gl * hc, 2 * gl * p).astype(BF16)

    def c_layout(cm):
        cm = cm.astype(F32).reshape(nd, nh, gl, hc, p)
        return jnp.einsum('dhgop,gk->dhgpko', cm, eye).reshape(nd, nh, gl * p, gl * hc).astype(BF16)

    a_layout = lambda a: a.reshape(nd, nh, 1, gl * p)
    return b_half, c_layout(c_re), c_layout(c_im), a_layout(ab_re), a_layout(ab_im)


def _state_to_lanes(h0):
    b, nd, nc, g, p = h0.shape
    gl = GROUPS_PER_HALF
    return h0.astype(F32).reshape(b, nd, nc, g // gl, gl * p).transpose(1, 3, 2, 0, 4)


def _state_from_lanes(fin, g, p):
    nd, nh, nc, b, _ = fin.shape
    return fin.transpose(3, 0, 2, 1, 4).reshape(b, nd, nc, g, p)


def _outproj_kernel(x_ref, a_ref, y0_ref, y1_ref, u_ref, gs_ref, gate_ref, dskip_ref,
                    wglu_ref, bglu_ref, wout_ref, g_ref, o_ref):
    u = u_ref[...].astype(F32)
    y = y0_ref[0].astype(F32) + y1_ref[0].astype(F32) + dskip_ref[...] * u
    c0 = math.sqrt(2.0 / math.pi)
    ys = 0.5 * y * (1.0 + jnp.tanh(c0 * (y + 0.044715 * (y * y * y))))
    z = jnp.dot(ys.astype(BF16), wglu_ref[...], preferred_element_type=F32) + bglu_ref[...]
    ys = ys * jax.nn.sigmoid(z) * gs_ref[...].astype(F32)
    cat = jnp.concatenate([a_ref[...], ys.astype(BF16)], axis=-1)
    out = jnp.dot(cat, wout_ref[...], preferred_element_type=F32)
    ms = jnp.mean(out * out, axis=-1, keepdims=True)
    out = out * lax.rsqrt(ms + EPS) * g_ref[...]
    o_ref[...] = x_ref[...] + gate_ref[...] * out


def _outproj(x, a_out, y_dirs, u_tb, gs, gate, d_skip, w_glu_bf, b_glu, w_out_bf, norm_post):
    b, l, d = x.shape
    d_attn = a_out.shape[-1]
    d_ssm = gs.shape[-1]
    tt = TOKEN_TILE
    nt = l // tt
    per_batch = gate.shape[0] == b
    gate_map = (lambda i, j: (i, 0, 0)) if per_batch else (lambda i, j: (0, 0, 0))
    y_tb = y_dirs.reshape(2, l, b * d_ssm)
    const = lambda shape: pl.BlockSpec(shape, lambda i, j: (0,) * len(shape))
    in_specs = [pl.BlockSpec((None, tt, d), lambda i, j: (i, j, 0)),
                pl.BlockSpec((None, tt, d_attn), lambda i, j: (i, j, 0)),
                pl.BlockSpec((1, tt, d_ssm), lambda i, j: (0, j, i)),
                pl.BlockSpec((1, tt, d_ssm), lambda i, j: (1, j, i)),
                pl.BlockSpec((tt, d_ssm), lambda i, j: (j, i)),
                pl.BlockSpec((tt, d_ssm), lambda i, j: (i * nt + j, 0)),
                pl.BlockSpec((None, 1, d), gate_map),
                const((1, d_ssm)), const(w_glu_bf.shape), const((1, d_ssm)),
                const(w_out_bf.shape), const((1, d))]
    return pl.pallas_call(
        _outproj_kernel,
        grid=(b, nt),
        in_specs=in_specs,
        out_specs=pl.BlockSpec((None, tt, d), lambda i, j: (i, j, 0)),
        out_shape=jax.ShapeDtypeStruct((b, l, d), F32),
        compiler_params=_params("arbitrary", "arbitrary"),
        name="outproj",
    )(x, a_out, y_tb, y_tb, u_tb, gs, gate, d_skip.reshape(1, d_ssm).astype(F32), w_glu_bf,
      b_glu.reshape(1, d_ssm).astype(F32), w_out_bf, norm_post.reshape(1, d).astype(F32))


def _mixer(x, mod, h0_lanes, cache_k, cache_v, rope_tabs, kv_dtype, with_state, lam, lam_init,
           norm_pre, w_in_bf, ssm_ops, subln, d_skip, w_glu_bf, b_glu, w_out_bf, norm_post):
    b, l, d = x.shape
    d_ssm = w_glu_bf.shape[0]
    d_attn = (w_in_bf.shape[1] - 2 * d_ssm) // 4
    shift, scale, gate = mod
    q, k, v, ga, u_lb, gs = _inproj(x, shift, scale, norm_pre, w_in_bf, rope_tabs, kv_dtype,
                                    d_attn, d_ssm)
    a_out = _attention(lam, q, k, v, cache_k, cache_v, ga, subln, lam_init, b, l)
    u_tb = u_lb.reshape(l * b, d_ssm)
    outs = _ssm(u_tb, *ssm_ops, h0_lanes, b, l, with_state)
    y = _outproj(x, a_out, outs[0], u_lb, gs, gate, d_skip, w_glu_bf, b_glu, w_out_bf, norm_post)
    fin = outs[1] if with_state else None
    return y, k, v, fin


def kernel(x_prompt, x_sample, cache_k, cache_v, state_ssm, c, c_ctx, w_ada, b_ada, norm_pre, norm_post, w_in, lambda_qk, subln, ssm_A_re, ssm_A_im, ssm_log_dt, ssm_B_re, ssm_B_im, ssm_C_re, ssm_C_im, ssm_D, w_glu, b_glu, w_out):
    xp, xs = x_prompt, x_sample
    bp, lp, d = xp.shape
    bs, ls, _ = xs.shape
    depth = w_in.shape[0]
    g, p = ssm_A_re.shape[-2:]
    d_ssm = w_glu.shape[-1]
    d_attn = (w_in.shape[-1] - 2 * d_ssm) // 4
    rope_tabs = _rope_tables(ls, d_attn)
    cond_rows = 16
    cond = jnp.zeros((cond_rows, d), F32).at[:bs].set(c.astype(F32)).at[bs].set(c_ctx.astype(F32))
    new_k, new_v, new_s = [], [], []
    for layer in range(depth):
        lam_init = 0.8 - 0.6 * math.exp(-0.3 * layer)
        m = _adaln(cond, w_ada[layer], b_ada[layer])
        mod_s = tuple(m[:bs, i * d:(i + 1) * d].reshape(bs, 1, d) for i in range(3))
        mod_p = tuple(m[bs:bs + 1, i * d:(i + 1) * d].reshape(1, 1, d) for i in range(3))
        ab_re, ab_im, bb_re, bb_im, lam = _ssmprep(
            ssm_A_re[layer], ssm_A_im[layer], ssm_log_dt[layer], ssm_B_re[layer], ssm_B_im[layer],
            lambda_qk[layer], lam_init)
        ssm_ops = _ssm_layout(ab_re, ab_im, bb_re, bb_im, ssm_C_re[layer], ssm_C_im[layer])
        shared = (lam, lam_init, norm_pre[layer], w_in[layer].astype(BF16), ssm_ops, subln[layer],
                  ssm_D[layer], w_glu[layer].astype(BF16), b_glu[layer], w_out[layer].astype(BF16),
                  norm_post[layer])

        h0_p = jnp.zeros((2, g // GROUPS_PER_HALF, 2, bp, GROUPS_PER_HALF * p), F32)
        xp, k_p, v_p, fin = _mixer(xp, mod_p, h0_p, None, None, None, F32, True, *shared)
        new_k.append(k_p.reshape(bp, lp, N_HEADS, 2 * QK_DIM))
        new_v.append(v_p.reshape(bp, lp, N_HEADS, V_DIM))
        new_s.append(_state_from_lanes(fin, g, p).astype(xp.dtype))

        ck = cache_k[:, layer].reshape(bs, -1, d_attn)
        cv = cache_v[:, layer].reshape(bs, -1, d_attn)
        h0_s = _state_to_lanes(state_ssm[:, layer])
        xs, _, _, _ = _mixer(xs, mod_s, h0_s, ck, cv, rope_tabs, BF16, False, *shared)

    return (xp, xs, jnp.stack(new_k, axis=1), jnp.stack(new_v, axis=1), jnp.stack(new_s, axis=1))
```

```python
import functools
import math

import jax
import jax.numpy as jnp
from jax import lax
from jax.experimental import pallas as pl
from jax.experimental.pallas import tpu as pltpu

F32 = jnp.float32
BF16 = jnp.bfloat16

GRID_W = 64
N_HEADS = 4
QK_DIM = 64
V_DIM = 2 * QK_DIM
SSM_GROUP_CH = 16
SSM_STATE = 64
ROPE_THETA = 10000.0
EPS = 1e-6
Q_PRESCALE = math.log2(math.e) * QK_DIM ** -0.5

SUM_ROWS = 16
TOKEN_TILE = 256
ATTN_Q_TILE = 1024
SSM_ROWS = 512
GROUPS_PER_HALF = 16
VMEM_LIMIT = 48 * 1024 * 1024


def _silu(x):
    return x * jax.nn.sigmoid(x)


def _params(*sem):
    return pltpu.CompilerParams(dimension_semantics=sem, vmem_limit_bytes=VMEM_LIMIT)


def _adaln_kernel(c_ref, w_ref, b_ref, o_ref):
    s = _silu(c_ref[...]).astype(BF16)
    o_ref[...] = jnp.dot(s, w_ref[...].astype(BF16), preferred_element_type=F32) + b_ref[...]


def _adaln(cond, w_ada, b_ada):
    rows, d = cond.shape
    n = w_ada.shape[1]
    bn = 512
    return pl.pallas_call(
        _adaln_kernel,
        grid=(n // bn,),
        in_specs=[pl.BlockSpec((rows, d), lambda j: (0, 0)),
                  pl.BlockSpec((d, bn), lambda j: (0, j)),
                  pl.BlockSpec((1, bn), lambda j: (0, j))],
        out_specs=pl.BlockSpec((rows, bn), lambda j: (0, j)),
        out_shape=jax.ShapeDtypeStruct((rows, n), F32),
        compiler_params=_params("arbitrary"),
        name="adaln",
    )(cond, w_ada, b_ada.reshape(1, n))


def _ssmprep_kernel(are_ref, aim_ref, ldt_ref, bre_ref, bim_ref, lq_ref,
                    abre_ref, abim_ref, bbre_ref, bbim_ref, lam_ref, *, lam_init):
    a_re = are_ref[...]
    a_im = aim_ref[...]
    dt = jnp.exp(ldt_ref[...])
    mag = jnp.exp(a_re * dt)
    ab_re = mag * jnp.cos(a_im * dt)
    ab_im = mag * jnp.sin(a_im * dt)
    nr, ni = ab_re - 1.0, ab_im
    den = a_re * a_re + a_im * a_im
    f_re = (nr * a_re + ni * a_im) / den
    f_im = (ni * a_re - nr * a_im) / den
    abre_ref[...] = ab_re
    abim_ref[...] = ab_im
    b_re = bre_ref[...]
    b_im = bim_ref[...]
    fr = f_re[:, None, :]
    fi = f_im[:, None, :]
    bbre_ref[...] = fr * b_re - fi * b_im
    bbim_ref[...] = fr * b_im + fi * b_re
    lq = lq_ref[...]
    s01 = jnp.sum(lq[0:1] * lq[1:2], axis=-1, keepdims=True)
    s23 = jnp.sum(lq[2:3] * lq[3:4], axis=-1, keepdims=True)
    lam = jnp.exp(s01) - jnp.exp(s23) + lam_init
    lam_ref[...] = jnp.broadcast_to(lam, lam_ref.shape)


def _ssmprep(a_re, a_im, log_dt, b_re, b_im, lq, lam_init):
    nd, g, p = a_re.shape
    hc = b_re.shape[-1]
    r = nd * g
    ldt = jnp.broadcast_to(log_dt[..., None], (nd, g, p)).reshape(r, p)
    bt_re = jnp.swapaxes(b_re, -1, -2).reshape(r, hc, p)
    bt_im = jnp.swapaxes(b_im, -1, -2).reshape(r, hc, p)
    outs = pl.pallas_call(
        functools.partial(_ssmprep_kernel, lam_init=lam_init),
        out_shape=(jax.ShapeDtypeStruct((r, p), F32), jax.ShapeDtypeStruct((r, p), F32),
                   jax.ShapeDtypeStruct((r, hc, p), F32), jax.ShapeDtypeStruct((r, hc, p), F32),
                   jax.ShapeDtypeStruct((8, 128), F32)),
        name="ssmprep",
    )(a_re.reshape(r, p).astype(F32), a_im.reshape(r, p).astype(F32), ldt.astype(F32),
      bt_re.astype(F32), bt_im.astype(F32), lq.astype(F32))
    ab_re, ab_im, bb_re, bb_im, lam = outs
    return (ab_re.reshape(nd, g, p), ab_im.reshape(nd, g, p),
            bb_re.reshape(nd, g, hc, p), bb_im.reshape(nd, g, hc, p), lam)


def _inproj_kernel(*refs, rope, d_attn):
    if rope:
        (x_ref, shift_ref, scale_ref, g_ref, w_ref, cos_ref, sin_ref,
         q_ref, k_ref, v_ref, ga_ref, u_ref, gs_ref) = refs
    else:
        (x_ref, shift_ref, scale_ref, g_ref, w_ref,
         q_ref, k_ref, v_ref, ga_ref, u_ref, gs_ref) = refs
    x = x_ref[...]
    ms = jnp.mean(x * x, axis=-1, keepdims=True)
    h = x * lax.rsqrt(ms + EPS) * (g_ref[...] * (1.0 + scale_ref[...])) + shift_ref[...]
    proj = jnp.dot(h.astype(BF16), w_ref[...], preferred_element_type=F32)
    da = d_attn
    q = proj[:, 0:da]
    k = proj[:, da:2 * da]
    if rope:
        lane = lax.broadcasted_iota(jnp.int32, (1, da), 1)
        low = (lane % (QK_DIM // 2)) < (QK_DIM // 4)
        cos = cos_ref[...]
        sin = sin_ref[...]

        def rot(z):
            partner = jnp.where(low, pltpu.roll(z, da - QK_DIM // 4, 1), pltpu.roll(z, QK_DIM // 4, 1))
            return z * cos + partner * sin

        q = rot(q)
        k = rot(k)
    q_ref[...] = (q * Q_PRESCALE).astype(q_ref.dtype)
    k_ref[...] = k.astype(k_ref.dtype)
    v_ref[...] = proj[:, 2 * da:3 * da].astype(v_ref.dtype)
    ga_ref[...] = _silu(proj[:, 3 * da:4 * da]).astype(ga_ref.dtype)
    d_ssm = u_ref.shape[-1]
    u_ref[...] = proj[:, 4 * da:4 * da + d_ssm].astype(u_ref.dtype)
    gs_ref[...] = _silu(proj[:, 4 * da + d_ssm:]).astype(gs_ref.dtype)


def _inproj(x, shift, scale, norm_pre, w_in_bf, rope_tabs, kv_dtype, d_attn, d_ssm):
    b, l, d = x.shape
    tt = TOKEN_TILE
    nt = l // tt
    per_batch = shift.shape[0] == b
    mod_map = (lambda i, j: (i, 0, 0)) if per_batch else (lambda i, j: (0, 0, 0))
    rope = rope_tabs is not None
    in_specs = [pl.BlockSpec((None, tt, d), lambda i, j: (i, j, 0)),
                pl.BlockSpec((None, 1, d), mod_map),
                pl.BlockSpec((None, 1, d), mod_map),
                pl.BlockSpec((1, d), lambda i, j: (0, 0)),
                pl.BlockSpec(w_in_bf.shape, lambda i, j: (0, 0))]
    args = [x, shift, scale, norm_pre.reshape(1, d).astype(F32), w_in_bf]
    if rope:
        in_specs += [pl.BlockSpec((tt, d_attn), lambda i, j: (j, 0))] * 2
        args += list(rope_tabs)
    row_spec = lambda w: pl.BlockSpec((tt, w), lambda i, j: (i * nt + j, 0))
    out_specs = [row_spec(d_attn), row_spec(d_attn), row_spec(d_attn), row_spec(d_attn),
                 pl.BlockSpec((tt, d_ssm), lambda i, j: (j, i)),
                 row_spec(d_ssm)]
    rows = b * l
    out_shape = [jax.ShapeDtypeStruct((rows, d_attn), BF16),
                 jax.ShapeDtypeStruct((rows, d_attn), kv_dtype),
                 jax.ShapeDtypeStruct((rows, d_attn), kv_dtype),
                 jax.ShapeDtypeStruct((rows, d_attn), BF16),
                 jax.ShapeDtypeStruct((l, b * d_ssm), BF16),
                 jax.ShapeDtypeStruct((rows, d_ssm), BF16)]
    return pl.pallas_call(
        functools.partial(_inproj_kernel, rope=rope, d_attn=d_attn),
        grid=(b, nt),
        in_specs=in_specs,
        out_specs=out_specs,
        out_shape=out_shape,
        compiler_params=_params("arbitrary", "arbitrary"),
        name="inproj_rope" if rope else "inproj",
    )(*args)


def _rope_tables(l, d_attn):
    rows = l // GRID_W
    row = jnp.repeat(jnp.arange(rows, dtype=F32), GRID_W)
    col = jnp.tile(jnp.arange(GRID_W, dtype=F32), rows)
    n_freq = QK_DIM // 4
    inv = ROPE_THETA ** (-jnp.arange(n_freq, dtype=F32) / n_freq)
    ang_r = row[:, None] * inv
    ang_c = col[:, None] * inv
    cos64 = jnp.concatenate([jnp.cos(ang_r), jnp.cos(ang_r), jnp.cos(ang_c), jnp.cos(ang_c)], axis=-1)
    sin64 = jnp.concatenate([-jnp.sin(ang_r), jnp.sin(ang_r), -jnp.sin(ang_c), jnp.sin(ang_c)], axis=-1)
    reps = d_attn // QK_DIM
    return jnp.tile(cos64, (1, reps)), jnp.tile(sin64, (1, reps))


def _attn_kernel(*refs, has_cache, lam_init):
    if has_cache:
        lam_ref, q_ref, k_ref, v_ref, ck_ref, cv_ref, ga_ref, sub_ref, o_ref, k_scr, vt_scr = refs
    else:
        lam_ref, q_ref, k_ref, v_ref, ga_ref, sub_ref, o_ref, k_scr, vt_scr = refs
    lk = k_scr.shape[0]
    ln = k_ref.shape[0]

    @pl.when(pl.program_id(2) == 0)
    def _():
        if has_cache:
            k_scr[0:lk - ln, :] = ck_ref[...].astype(BF16)
            vt_scr[0:V_DIM, 0:lk - ln] = cv_ref[...].astype(F32).T.astype(BF16)
        k_scr[lk - ln:lk, :] = k_ref[...].astype(BF16)
        vt_scr[0:V_DIM, lk - ln:lk] = v_ref[...].astype(F32).T.astype(BF16)
        vt_scr[V_DIM:, :] = jnp.ones((vt_scr.shape[0] - V_DIM, lk), BF16)

    lam = lam_ref[0:1, 0:1]
    q = q_ref[...]
    lane = lax.broadcasted_iota(jnp.int32, (1, V_DIM), 1)
    first = lane < QK_DIM
    zero = jnp.zeros_like(q)
    k = k_scr[...]
    vt = vt_scr[...]
    dims = (((1,), (1,)), ((), ()))

    st1 = lax.dot_general(k, jnp.where(first, q, zero), dims, preferred_element_type=F32)
    st2 = lax.dot_general(k, jnp.where(first, zero, q), dims, preferred_element_type=F32)
    et1 = jnp.exp2(st1 - jnp.max(st1, axis=0, keepdims=True)).astype(BF16)
    et2 = jnp.exp2(st2 - jnp.max(st2, axis=0, keepdims=True)).astype(BF16)
    o1 = jnp.dot(vt, et1, preferred_element_type=F32)
    o2 = jnp.dot(vt, et2, preferred_element_type=F32)
    r1 = 1.0 / o1[V_DIM:V_DIM + 1, :]
    r2 = lam / o2[V_DIM:V_DIM + 1, :]
    o = (o1[0:V_DIM, :] * r1 - o2[0:V_DIM, :] * r2).T
    ms = jnp.mean(o * o, axis=-1, keepdims=True)
    o = o * lax.rsqrt(ms + EPS) * sub_ref[...] * (1.0 - lam_init)
    o_ref[...] = (o * ga_ref[...].astype(F32)).astype(o_ref.dtype)


def _attention(lam, q, k, v, cache_k, cache_v, ga, subln, lam_init, b, l):
    d_attn = q.shape[-1]
    tq = min(ATTN_Q_TILE, l)
    nq = l // tq
    has_cache = cache_k is not None
    q3, k3, v3, ga3 = (a.reshape(b, l, d_attn) for a in (q, k, v, ga))
    q_spec = pl.BlockSpec((None, tq, V_DIM), lambda i, h, j: (i, j, h))
    kv_spec = pl.BlockSpec((None, l, V_DIM), lambda i, h, j: (i, 0, h))
    in_specs = [pl.BlockSpec((8, 128), lambda i, h, j: (0, 0)), q_spec, kv_spec, kv_spec]
    args = [lam, q3, k3, v3]
    if has_cache:
        lc = cache_k.shape[1]
        c_spec = pl.BlockSpec((None, lc, V_DIM), lambda i, h, j: (i, 0, h))
        in_specs += [c_spec, c_spec]
        args += [cache_k, cache_v]
    in_specs += [q_spec, pl.BlockSpec((1, V_DIM), lambda i, h, j: (0, 0))]
    args += [ga3, subln.reshape(1, V_DIM).astype(F32)]
    lk = l + (cache_k.shape[1] if has_cache else 0)
    out = pl.pallas_call(
        functools.partial(_attn_kernel, has_cache=has_cache, lam_init=lam_init),
        grid=(b, N_HEADS, nq),
        in_specs=in_specs,
        out_specs=q_spec,
        out_shape=jax.ShapeDtypeStruct((b, l, d_attn), BF16),
        scratch_shapes=[pltpu.VMEM((lk, V_DIM), BF16), pltpu.VMEM((V_DIM + SUM_ROWS, lk), BF16)],
        compiler_params=_params("arbitrary", "arbitrary", "arbitrary"),
        name="attn_cache" if has_cache else "attn",
    )(*args)
    return out


def _ssm_kernel(*refs, batch, steps, n_chunks, lane_slice, with_state):
    if with_state:
        u_ref, bh_ref, cre_ref, cim_ref, are_ref, aim_ref, h0_ref, y_ref, fin_ref, bu_ref, st_ref = refs
    else:
        u_ref, bh_ref, cre_ref, cim_ref, are_ref, aim_ref, h0_ref, y_ref, bu_ref, st_ref = refs
    d = pl.program_id(0)
    c = pl.program_id(1)
    half_w = st_ref.shape[-1]
    kw = bh_ref.shape[2]
    ow = cre_ref.shape[-1]

    @pl.when(c == 0)
    def _():
        st_ref[...] = h0_ref[0]

    for hf in range(2):
        bu_ref[...] = jnp.dot(u_ref[:, hf * kw:(hf + 1) * kw], bh_ref[0, hf],
                              preferred_element_type=F32)
        for lo in range(0, half_w, lane_slice):
            re_l = pl.ds(lo, lane_slice)
            im_l = pl.ds(half_w + lo, lane_slice)
            ar = jnp.broadcast_to(are_ref[0, hf, :, re_l], (batch, lane_slice))
            ai = jnp.broadcast_to(aim_ref[0, hf, :, re_l], (batch, lane_slice))

            def step(t, carry, re_l=re_l, im_l=im_l, ar=ar, ai=ai):
                hr, hi = carry
                tt = jnp.where(d == 0, t, steps - 1 - t)
                rows = pl.ds(pl.multiple_of(tt * batch, batch), batch)
                nr = ar * hr - ai * hi + bu_ref[rows, re_l]
                ni = ar * hi + ai * hr + bu_ref[rows, im_l]
                bu_ref[rows, re_l] = nr
                bu_ref[rows, im_l] = ni
                return nr, ni

            hr, hi = lax.fori_loop(0, steps, step,
                                   (st_ref[hf, 0, :, re_l], st_ref[hf, 1, :, re_l]), unroll=4)
            st_ref[hf, 0, :, re_l] = hr
            st_ref[hf, 1, :, re_l] = hi
        h_re = bu_ref[:, 0:half_w].astype(BF16)
        h_im = bu_ref[:, half_w:2 * half_w].astype(BF16)
        y = (jnp.dot(h_re, cre_ref[0, hf], preferred_element_type=F32)
             - jnp.dot(h_im, cim_ref[0, hf], preferred_element_type=F32))
        y_ref[0, :, hf * ow:(hf + 1) * ow] = y.astype(y_ref.dtype)

    if with_state:
        @pl.when(c == n_chunks - 1)
        def _():
            fin_ref[0] = st_ref[...]


def _ssm(u_tb, b_half, cre_half, cim_half, are_half, aim_half, h0, batch, l, with_state):
    d_ssm = u_tb.shape[-1]
    steps = SSM_ROWS // batch
    n_chunks = l // steps
    half_w = are_half.shape[-1]
    lane_slice = 4096 // batch
    chunk = lambda d, c: c + d * (n_chunks - 1 - 2 * c)
    per_dir = lambda shape: pl.BlockSpec((1,) + shape[1:], lambda d, c: (d,) + (0,) * (len(shape) - 1))
    in_specs = [pl.BlockSpec((SSM_ROWS, d_ssm), lambda d, c: (chunk(d, c), 0)),
                per_dir(b_half.shape), per_dir(cre_half.shape), per_dir(cim_half.shape),
                per_dir(are_half.shape), per_dir(aim_half.shape), per_dir(h0.shape)]
    out_specs = [pl.BlockSpec((1, SSM_ROWS, d_ssm), lambda d, c: (d, chunk(d, c), 0))]
    out_shape = [jax.ShapeDtypeStruct((2, l * batch, d_ssm), BF16)]
    if with_state:
        out_specs.append(per_dir(h0.shape))
        out_shape.append(jax.ShapeDtypeStruct(h0.shape, F32))
    outs = pl.pallas_call(
        functools.partial(_ssm_kernel, batch=batch, steps=steps, n_chunks=n_chunks,
                          lane_slice=lane_slice, with_state=with_state),
        grid=(2, n_chunks),
        in_specs=in_specs,
        out_specs=out_specs,
        out_shape=out_shape,
        scratch_shapes=[pltpu.VMEM((SSM_ROWS, 2 * half_w), F32),
                        pltpu.VMEM((2, 2, batch, half_w), F32)],
        compiler_params=_params("arbitrary", "arbitrary"),
        name="ssm_state" if with_state else "ssm",
    )(u_tb, b_half, cre_half, cim_half, are_half, aim_half, h0)
    return outs


def _ssm_layout(ab_re, ab_im, bb_re, bb_im, c_re, c_im):
    nd, g, hc, p = bb_re.shape
    gl = GROUPS_PER_HALF
    nh = g // gl
    eye = jnp.eye(gl, dtype=F32)
    bbc = jnp.stack([bb_re, bb_im], axis=3).reshape(nd, nh, gl, hc, 2, p)
    b_half = jnp.einsum('dhgicp,gk->dhgickp', bbc, eye).reshape(nd, nh, gl * hc, 2 * gl * p).astype(BF16)

    def c_layout(cm):
        cm = cm.astype(F32).reshape(nd, nh, gl, hc, p)
        return jnp.einsum('dhgop,gk->dhgpko', cm, eye).reshape(nd, nh, gl * p, gl * hc).astype(BF16)

    a_layout = lambda a: a.reshape(nd, nh, 1, gl * p)
    return b_half, c_layout(c_re), c_layout(c_im), a_layout(ab_re), a_layout(ab_im)


def _state_to_lanes(h0):
    b, nd, nc, g, p = h0.shape
    gl = GROUPS_PER_HALF
    return h0.astype(F32).reshape(b, nd, nc, g // gl, gl * p).transpose(1, 3, 2, 0, 4)


def _state_from_lanes(fin, g, p):
    nd, nh, nc, b, _ = fin.shape
    return fin.transpose(3, 0, 2, 1, 4).reshape(b, nd, nc, g, p)


def _outproj_kernel(x_ref, a_ref, y0_ref, y1_ref, u_ref, gs_ref, gate_ref, dskip_ref,
                    wglu_ref, bglu_ref, wout_ref, g_ref, o_ref):
    u = u_ref[...].astype(F32)
    y = y0_ref[0].astype(F32) + y1_ref[0].astype(F32) + dskip_ref[...] * u
    c0 = math.sqrt(2.0 / math.pi)
    ys = 0.5 * y * (1.0 + jnp.tanh(c0 * (y + 0.044715 * (y * y * y))))
    z = jnp.dot(ys.astype(BF16), wglu_ref[...], preferred_element_type=F32) + bglu_ref[...]
    ys = ys * jax.nn.sigmoid(z) * gs_ref[...].astype(F32)
    cat = jnp.concatenate([a_ref[...], ys.astype(BF16)], axis=-1)
    out = jnp.dot(cat, wout_ref[...], preferred_element_type=F32)
    ms = jnp.mean(out * out, axis=-1, keepdims=True)
    out = out * lax.rsqrt(ms + EPS) * g_ref[...]
    o_ref[...] = x_ref[...] + gate_ref[...] * out


def _outproj(x, a_out, y_dirs, u_tb, gs, gate, d_skip, w_glu_bf, b_glu, w_out_bf, norm_post):
    b, l, d = x.shape
    d_attn = a_out.shape[-1]
    d_ssm = gs.shape[-1]
    tt = TOKEN_TILE
    nt = l // tt
    per_batch = gate.shape[0] == b
    gate_map = (lambda i, j: (i, 0, 0)) if per_batch else (lambda i, j: (0, 0, 0))
    y_tb = y_dirs.reshape(2, l, b * d_ssm)
    const = lambda shape: pl.BlockSpec(shape, lambda i, j: (0,) * len(shape))
    in_specs = [pl.BlockSpec((None, tt, d), lambda i, j: (i, j, 0)),
                pl.BlockSpec((None, tt, d_attn), lambda i, j: (i, j, 0)),
                pl.BlockSpec((1, tt, d_ssm), lambda i, j: (0, j, i)),
                pl.BlockSpec((1, tt, d_ssm), lambda i, j: (1, j, i)),
                pl.BlockSpec((tt, d_ssm), lambda i, j: (j, i)),
                pl.BlockSpec((tt, d_ssm), lambda i, j: (i * nt + j, 0)),
                pl.BlockSpec((None, 1, d), gate_map),
                const((1, d_ssm)), const(w_glu_bf.shape), const((1, d_ssm)),
                const(w_out_bf.shape), const((1, d))]
    return pl.pallas_call(
        _outproj_kernel,
        grid=(b, nt),
        in_specs=in_specs,
        out_specs=pl.BlockSpec((None, tt, d), lambda i, j: (i, j, 0)),
        out_shape=jax.ShapeDtypeStruct((b, l, d), F32),
        compiler_params=_params("arbitrary", "arbitrary"),
        name="outproj",
    )(x, a_out, y_tb, y_tb, u_tb, gs, gate, d_skip.reshape(1, d_ssm).astype(F32), w_glu_bf,
      b_glu.reshape(1, d_ssm).astype(F32), w_out_bf, norm_post.reshape(1, d).astype(F32))


def _mixer(x, mod, h0_lanes, cache_k, cache_v, rope_tabs, kv_dtype, with_state, lam, lam_init,
           norm_pre, w_in_bf, ssm_ops, subln, d_skip, w_glu_bf, b_glu, w_out_bf, norm_post):
    b, l, d = x.shape
    d_ssm = w_glu_bf.shape[0]
    d_attn = (w_in_bf.shape[1] - 2 * d_ssm) // 4
    shift, scale, gate = mod
    q, k, v, ga, u_lb, gs = _inproj(x, shift, scale, norm_pre, w_in_bf, rope_tabs, kv_dtype,
                                    d_attn, d_ssm)
    a_out = _attention(lam, q, k, v, cache_k, cache_v, ga, subln, lam_init, b, l)
    u_tb = u_lb.reshape(l * b, d_ssm)
    outs = _ssm(u_tb, *ssm_ops, h0_lanes, b, l, with_state)
    y = _outproj(x, a_out, outs[0], u_lb, gs, gate, d_skip, w_glu_bf, b_glu, w_out_bf, norm_post)
    fin = outs[1] if with_state else None
    return y, k, v, fin


def kernel(x_prompt, x_sample, cache_k, cache_v, state_ssm, c, c_ctx, w_ada, b_ada, norm_pre, norm_post, w_in, lambda_qk, subln, ssm_A_re, ssm_A_im, ssm_log_dt, ssm_B_re, ssm_B_im, ssm_C_re, ssm_C_im, ssm_D, w_glu, b_glu, w_out):
    xp, xs = x_prompt, x_sample
    bp, lp, d = xp.shape
    bs, ls, _ = xs.shape
    depth = w_in.shape[0]
    g, p = ssm_A_re.shape[-2:]
    d_ssm = w_glu.shape[-1]
    d_attn = (w_in.shape[-1] - 2 * d_ssm) // 4
    rope_tabs = _rope_tables(ls, d_attn)
    cond_rows = 16
    cond = jnp.zeros((cond_rows, d), F32).at[:bs].set(c.astype(F32)).at[bs].set(c_ctx.astype(F32))
    new_k, new_v, new_s = [], [], []
    for layer in range(depth):
        lam_init = 0.8 - 0.6 * math.exp(-0.3 * layer)
        m = _adaln(cond, w_ada[layer], b_ada[layer])
        mod_s = tuple(m[:bs, i * d:(i + 1) * d].reshape(bs, 1, d) for i in range(3))
        mod_p = tuple(m[bs:bs + 1, i * d:(i + 1) * d].reshape(1, 1, d) for i in range(3))
        ab_re, ab_im, bb_re, bb_im, lam = _ssmprep(
            ssm_A_re[layer], ssm_A_im[layer], ssm_log_dt[layer], ssm_B_re[layer], ssm_B_im[layer],
            lambda_qk[layer], lam_init)
        ssm_ops = _ssm_layout(ab_re, ab_im, bb_re, bb_im, ssm_C_re[layer], ssm_C_im[layer])
        shared = (lam, lam_init, norm_pre[layer], w_in[layer].astype(BF16), ssm_ops, subln[layer],
                  ssm_D[layer], w_glu[layer].astype(BF16), b_glu[layer], w_out[layer].astype(BF16),
                  norm_post[layer])

        h0_p = jnp.zeros((2, g // GROUPS_PER_HALF, 2, bp, GROUPS_PER_HALF * p), F32)
        xp, k_p, v_p, fin = _mixer(xp, mod_p, h0_p, None, None, None, F32, True, *shared)
        new_k.append(k_p.reshape(bp, lp, N_HEADS, 2 * QK_DIM))
        new_v.append(v_p.reshape(bp, lp, N_HEADS, V_DIM))
        new_s.append(_state_from_lanes(fin, g, p).astype(xp.dtype))

        ck = cache_k[:, layer].reshape(bs, -1, d_attn)
        cv = cache_v[:, layer].reshape(bs, -1, d_attn)
        h0_s = _state_to_lanes(state_ssm[:, layer])
        xs, _, _, _ = _mixer(xs, mod_s, h0_s, ck, cv, rope_tabs, BF16, False, *shared)

    return (xp, xs, jnp.stack(new_k, axis=1), jnp.stack(new_v, axis=1), jnp.stack(new_s, axis=1))
```

```python
import functools
import math

import jax
import jax.numpy as jnp
from jax import lax
from jax.experimental import pallas as pl
from jax.experimental.pallas import tpu as pltpu

F32 = jnp.float32
BF16 = jnp.bfloat16

GRID_W = 64
N_HEADS = 4
QK_DIM = 64
V_DIM = 2 * QK_DIM
ROPE_THETA = 10000.0
EPS = 1e-6
Q_PRESCALE = math.log2(math.e) * QK_DIM ** -0.5

SUM_ROWS = 16
TOKEN_TILE = 256
ATTN_Q_TILE = 1024
SSM_CHUNK = 16
SSM_GROUP_VREGS = 8
VMEM_LIMIT = 48 * 1024 * 1024


def _silu(x):
    return x * jax.nn.sigmoid(x)


def _params(*sem):
    return pltpu.CompilerParams(dimension_semantics=sem, vmem_limit_bytes=VMEM_LIMIT)


def _adaln_kernel(c_ref, w_ref, b_ref, o_ref):
    s = _silu(c_ref[...]).astype(BF16)
    o_ref[...] = jnp.dot(s, w_ref[...].astype(BF16), preferred_element_type=F32) + b_ref[...]


def _adaln(cond, w_ada, b_ada):
    rows, d = cond.shape
    n = w_ada.shape[1]
    bn = 512
    return pl.pallas_call(
        _adaln_kernel,
        grid=(n // bn,),
        in_specs=[pl.BlockSpec((rows, d), lambda j: (0, 0)),
                  pl.BlockSpec((d, bn), lambda j: (0, j)),
                  pl.BlockSpec((1, bn), lambda j: (0, j))],
        out_specs=pl.BlockSpec((rows, bn), lambda j: (0, j)),
        out_shape=jax.ShapeDtypeStruct((rows, n), F32),
        compiler_params=_params("arbitrary"),
        name="adaln",
    )(cond, w_ada, b_ada.reshape(1, n))


def _ssmprep_kernel(are_ref, aim_ref, ldt_ref, bre_ref, bim_ref, cre_ref, cim_ref, lq_ref,
                    k_ref, ere_ref, eim_ref, cpre_ref, cpimn_ref, are_out, aim_out, lam_ref,
                    *, lam_init, chunk):
    a_re = are_ref[...]
    a_im = aim_ref[...]
    dt = jnp.exp(ldt_ref[...])
    mag = jnp.exp(a_re * dt)
    ab_re = mag * jnp.cos(a_im * dt)
    ab_im = mag * jnp.sin(a_im * dt)
    nr, ni = ab_re - 1.0, ab_im
    den = a_re * a_re + a_im * a_im
    f_re = (nr * a_re + ni * a_im) / den
    f_im = (ni * a_re - nr * a_im) / den
    b_re = bre_ref[...]
    b_im = bim_ref[...]
    bb_re = f_re * b_re - f_im * b_im
    bb_im = f_re * b_im + f_im * b_re
    c_re = cre_ref[...]
    c_im = cim_ref[...]
    hc = b_re.shape[1]
    pw_re = jnp.ones_like(ab_re)
    pw_im = jnp.zeros_like(ab_im)
    for tau in range(chunk + 1):
        rows = slice(tau * hc, (tau + 1) * hc)
        cpre_ref[:, rows, :] = c_re * pw_re - c_im * pw_im
        cpimn_ref[:, rows, :] = -(c_re * pw_im + c_im * pw_re)
        if tau < chunk:
            ere_ref[:, rows, :] = pw_re * bb_re - pw_im * bb_im
            eim_ref[:, rows, :] = pw_re * bb_im + pw_im * bb_re
            pw_re, pw_im = pw_re * ab_re - pw_im * ab_im, pw_re * ab_im + pw_im * ab_re
    are_out[...] = pw_re
    aim_out[...] = pw_im
    dims = (((2,), (2,)), ((0,), (0,)))
    nk = chunk * hc
    k_all = (lax.dot_general(bb_re, cpre_ref[:, 0:nk, :], dims, precision=lax.Precision.HIGHEST,
                             preferred_element_type=F32)
             + lax.dot_general(bb_im, cpimn_ref[:, 0:nk, :], dims, precision=lax.Precision.HIGHEST,
                               preferred_element_type=F32))
    k_ref[...] = k_all
    half = k_all.shape[0] // 2
    k_ref[0:half, :, 0:hc] = k_all[0:half, :, 0:hc] + k_all[half:, :, 0:hc]
    lq = lq_ref[...]
    s01 = jnp.sum(lq[0:1] * lq[1:2], axis=-1, keepdims=True)
    s23 = jnp.sum(lq[2:3] * lq[3:4], axis=-1, keepdims=True)
    lam = jnp.exp(s01) - jnp.exp(s23) + lam_init
    lam_ref[...] = jnp.broadcast_to(lam, lam_ref.shape)


def _ssmprep(a_re, a_im, log_dt, b_re, b_im, c_re, c_im, lq, lam_init):
    nd, g, p = a_re.shape
    hc = b_re.shape[-1]
    r = nd * g
    t = SSM_CHUNK
    row = lambda a: a.reshape(r, 1, p).astype(F32)
    ldt = jnp.broadcast_to(log_dt[..., None], (nd, g, p))
    bt = lambda a: jnp.swapaxes(a, -1, -2).reshape(r, hc, p).astype(F32)
    ct = lambda a: a.reshape(r, hc, p).astype(F32)
    sds = jax.ShapeDtypeStruct
    outs = pl.pallas_call(
        functools.partial(_ssmprep_kernel, lam_init=lam_init, chunk=t),
        out_shape=(sds((r, hc, t * hc), F32),
                   sds((r, t * hc, p), F32), sds((r, t * hc, p), F32),
                   sds((r, (t + 1) * hc, p), F32), sds((r, (t + 1) * hc, p), F32),
                   sds((r, 1, p), F32), sds((r, 1, p), F32), sds((8, 128), F32)),
        compiler_params=pltpu.CompilerParams(vmem_limit_bytes=VMEM_LIMIT),
        name="ssmprep",
    )(row(a_re), row(a_im), row(ldt), bt(b_re), bt(b_im), ct(c_re), ct(c_im), lq.astype(F32))
    k, e_re, e_im, cp_re, cp_imn, a16_re, a16_im, lam = outs

    k = k.reshape(nd, g, hc, t, hc)
    lag = jnp.arange(t)[None, :] - jnp.arange(t)[:, None]
    kf = jnp.take(k[0], jnp.abs(lag), axis=2)
    kb = jnp.take(k[1], jnp.abs(lag), axis=2)
    toep = jnp.where((lag >= 0)[None, None, :, :, None], kf, kb)
    toep = toep.transpose(0, 2, 1, 3, 4).reshape(g, t * hc, t * hc)

    e_re = e_re.reshape(nd, g, t, hc, p)
    e_im = e_im.reshape(nd, g, t, hc, p)
    wf = lambda e: jnp.flip(e[0], axis=1)
    wb = lambda e: e[1]
    w = jnp.concatenate([wf(e_re), wb(e_re), wf(e_im), wb(e_im)], axis=-1)
    w = w.reshape(g, t * hc, 4 * p)

    cp_re = cp_re.reshape(nd, g, t + 1, hc, p)[:, :, 1:]
    cp_imn = cp_imn.reshape(nd, g, t + 1, hc, p)[:, :, 1:]
    vf = lambda c: c[0].transpose(0, 3, 1, 2).reshape(g, p, t * hc)
    vb = lambda c: jnp.flip(c[1], axis=1).transpose(0, 3, 1, 2).reshape(g, p, t * hc)
    v = jnp.concatenate([vf(cp_re), vb(cp_re), vf(cp_imn), vb(cp_imn)], axis=1)

    a16_re = a16_re.reshape(nd, g, 1, p)
    a16_im = a16_im.reshape(nd, g, 1, p)
    a16 = jnp.stack([jnp.concatenate([a16_re[0], a16_re[1]], -1),
                     jnp.concatenate([a16_im[0], a16_im[1]], -1)], axis=1)
    return (toep.astype(BF16), w.astype(BF16), v.astype(BF16), a16), lam


def _inproj_kernel(*refs, rope, d_attn):
    if rope:
        (x_ref, shift_ref, scale_ref, g_ref, w_ref, cos_ref, sin_ref,
         q_ref, k_ref, v_ref, ga_ref, u_ref, gs_ref) = refs
    else:
        (x_ref, shift_ref, scale_ref, g_ref, w_ref,
         q_ref, k_ref, v_ref, ga_ref, u_ref, gs_ref) = refs
    x = x_ref[...]
    ms = jnp.mean(x * x, axis=-1, keepdims=True)
    h = x * lax.rsqrt(ms + EPS) * (g_ref[...] * (1.0 + scale_ref[...])) + shift_ref[...]
    proj = jnp.dot(h.astype(BF16), w_ref[...], preferred_element_type=F32)
    da = d_attn
    q = proj[:, 0:da]
    k = proj[:, da:2 * da]
    if rope:
        lane = lax.broadcasted_iota(jnp.int32, (1, da), 1)
        low = (lane % (QK_DIM // 2)) < (QK_DIM // 4)
        cos = cos_ref[...]
        sin = sin_ref[...]

        def rot(z):
            partner = jnp.where(low, pltpu.roll(z, da - QK_DIM // 4, 1), pltpu.roll(z, QK_DIM // 4, 1))
            return z * cos + partner * sin

        q = rot(q)
        k = rot(k)
    q_ref[...] = (q * Q_PRESCALE).astype(q_ref.dtype)
    k_ref[...] = k.astype(k_ref.dtype)
    v_ref[...] = proj[:, 2 * da:3 * da].astype(v_ref.dtype)
    ga_ref[...] = _silu(proj[:, 3 * da:4 * da]).astype(ga_ref.dtype)
    d_ssm = u_ref.shape[-1]
    u_ref[...] = proj[:, 4 * da:4 * da + d_ssm].astype(u_ref.dtype)
    gs_ref[...] = _silu(proj[:, 4 * da + d_ssm:]).astype(gs_ref.dtype)


def _inproj(x, shift, scale, norm_pre, w_in_bf, rope_tabs, kv_dtype, d_attn, d_ssm):
    b, l, d = x.shape
    tt = TOKEN_TILE
    nt = l // tt
    per_batch = shift.shape[0] == b
    mod_map = (lambda i, j: (i, 0, 0)) if per_batch else (lambda i, j: (0, 0, 0))
    rope = rope_tabs is not None
    in_specs = [pl.BlockSpec((None, tt, d), lambda i, j: (i, j, 0)),
                pl.BlockSpec((None, 1, d), mod_map),
                pl.BlockSpec((None, 1, d), mod_map),
                pl.BlockSpec((1, d), lambda i, j: (0, 0)),
                pl.BlockSpec(w_in_bf.shape, lambda i, j: (0, 0))]
    args = [x, shift, scale, norm_pre.reshape(1, d).astype(F32), w_in_bf]
    if rope:
        in_specs += [pl.BlockSpec((tt, d_attn), lambda i, j: (j, 0))] * 2
        args += list(rope_tabs)
    row_spec = lambda w: pl.BlockSpec((None, tt, w), lambda i, j: (i, j, 0))
    out_specs = [row_spec(d_attn), row_spec(d_attn), row_spec(d_attn), row_spec(d_attn),
                 row_spec(d_ssm), row_spec(d_ssm)]
    out_shape = [jax.ShapeDtypeStruct((b, l, d_attn), BF16),
                 jax.ShapeDtypeStruct((b, l, d_attn), kv_dtype),
                 jax.ShapeDtypeStruct((b, l, d_attn), kv_dtype),
                 jax.ShapeDtypeStruct((b, l, d_attn), BF16),
                 jax.ShapeDtypeStruct((b, l, d_ssm), BF16),
                 jax.ShapeDtypeStruct((b, l, d_ssm), BF16)]
    return pl.pallas_call(
        functools.partial(_inproj_kernel, rope=rope, d_attn=d_attn),
        grid=(b, nt),
        in_specs=in_specs,
        out_specs=out_specs,
        out_shape=out_shape,
        compiler_params=_params("arbitrary", "arbitrary"),
        name="inproj_rope" if rope else "inproj",
    )(*args)


def _rope_tables(l, d_attn):
    rows = l // GRID_W
    row = jnp.repeat(jnp.arange(rows, dtype=F32), GRID_W)
    col = jnp.tile(jnp.arange(GRID_W, dtype=F32), rows)
    n_freq = QK_DIM // 4
    inv = ROPE_THETA ** (-jnp.arange(n_freq, dtype=F32) / n_freq)
    ang_r = row[:, None] * inv
    ang_c = col[:, None] * inv
    cos64 = jnp.concatenate([jnp.cos(ang_r), jnp.cos(ang_r), jnp.cos(ang_c), jnp.cos(ang_c)], axis=-1)
    sin64 = jnp.concatenate([-jnp.sin(ang_r), jnp.sin(ang_r), -jnp.sin(ang_c), jnp.sin(ang_c)], axis=-1)
    reps = d_attn // QK_DIM
    return jnp.tile(cos64, (1, reps)), jnp.tile(sin64, (1, reps))


def _attn_kernel(*refs, has_cache, lam_init):
    if has_cache:
        lam_ref, q_ref, k_ref, v_ref, ck_ref, cv_ref, ga_ref, sub_ref, o_ref, k_scr, vt_scr = refs
    else:
        lam_ref, q_ref, k_ref, v_ref, ga_ref, sub_ref, o_ref, k_scr, vt_scr = refs
    lk = k_scr.shape[0]
    ln = k_ref.shape[0]

    @pl.when(pl.program_id(2) == 0)
    def _():
        if has_cache:
            k_scr[0:lk - ln, :] = ck_ref[...].astype(BF16)
            vt_scr[0:V_DIM, 0:lk - ln] = cv_ref[...].astype(F32).T.astype(BF16)
        k_scr[lk - ln:lk, :] = k_ref[...].astype(BF16)
        vt_scr[0:V_DIM, lk - ln:lk] = v_ref[...].astype(F32).T.astype(BF16)
        vt_scr[V_DIM:, :] = jnp.ones((vt_scr.shape[0] - V_DIM, lk), BF16)

    lam = lam_ref[0:1, 0:1]
    q = q_ref[...]
    lane = lax.broadcasted_iota(jnp.int32, (1, V_DIM), 1)
    first = lane < QK_DIM
    zero = jnp.zeros_like(q)
    k = k_scr[...]
    vt = vt_scr[...]
    dims = (((1,), (1,)), ((), ()))

    st1 = lax.dot_general(k, jnp.where(first, q, zero), dims, preferred_element_type=F32)
    st2 = lax.dot_general(k, jnp.where(first, zero, q), dims, preferred_element_type=F32)
    et1 = jnp.exp2(st1 - jnp.max(st1, axis=0, keepdims=True)).astype(BF16)
    et2 = jnp.exp2(st2 - jnp.max(st2, axis=0, keepdims=True)).astype(BF16)
    o1 = jnp.dot(vt, et1, preferred_element_type=F32)
    o2 = jnp.dot(vt, et2, preferred_element_type=F32)
    r1 = 1.0 / o1[V_DIM:V_DIM + 1, :]
    r2 = lam / o2[V_DIM:V_DIM + 1, :]
    o = (o1[0:V_DIM, :] * r1 - o2[0:V_DIM, :] * r2).T
    ms = jnp.mean(o * o, axis=-1, keepdims=True)
    o = o * lax.rsqrt(ms + EPS) * sub_ref[...] * (1.0 - lam_init)
    o_ref[...] = (o * ga_ref[...].astype(F32)).astype(o_ref.dtype)


def _attention(lam, q, k, v, cache_k, cache_v, ga, subln, lam_init):
    b, l, d_attn = q.shape
    tq = min(ATTN_Q_TILE, l)
    nq = l // tq
    has_cache = cache_k is not None
    q_spec = pl.BlockSpec((None, tq, V_DIM), lambda i, h, j: (i, j, h))
    kv_spec = pl.BlockSpec((None, l, V_DIM), lambda i, h, j: (i, 0, h))
    in_specs = [pl.BlockSpec((8, 128), lambda i, h, j: (0, 0)), q_spec, kv_spec, kv_spec]
    args = [lam, q, k, v]
    if has_cache:
        lc = cache_k.shape[1]
        c_spec = pl.BlockSpec((None, lc, V_DIM), lambda i, h, j: (i, 0, h))
        in_specs += [c_spec, c_spec]
        args += [cache_k, cache_v]
    in_specs += [q_spec, pl.BlockSpec((1, V_DIM), lambda i, h, j: (0, 0))]
    args += [ga, subln.reshape(1, V_DIM).astype(F32)]
    lk = l + (cache_k.shape[1] if has_cache else 0)
    out = pl.pallas_call(
        functools.partial(_attn_kernel, has_cache=has_cache, lam_init=lam_init),
        grid=(b, N_HEADS, nq),
        in_specs=in_specs,
        out_specs=q_spec,
        out_shape=jax.ShapeDtypeStruct((b, l, d_attn), BF16),
        scratch_shapes=[pltpu.VMEM((lk, V_DIM), BF16), pltpu.VMEM((V_DIM + SUM_ROWS, lk), BF16)],
        compiler_params=_params("arbitrary", "arbitrary", "arbitrary"),
        name="attn_cache" if has_cache else "attn",
    )(*args)
    return out


def _ssm_kernel(*refs, batch, n_chunks, gb, with_state):
    if with_state:
        u_ref, toep_ref, w_ref, v_ref, a_ref, h0_ref, y_ref, fin_ref, wc_scr, sin_scr = refs
    else:
        u_ref, toep_ref, w_ref, v_ref, a_ref, h0_ref, y_ref, wc_scr, sin_scr = refs
    sw = a_ref.shape[-1]
    hw = sw // 2
    for g in range(gb):
        wc_scr[g] = jnp.dot(u_ref[g], w_ref[g], preferred_element_type=F32)
    fwd = lax.broadcasted_iota(jnp.int32, (1, sw), 1) < hw

    def step(j, carry):
        rf = pl.ds(pl.multiple_of(j * batch, batch), batch)
        rb = pl.ds(pl.multiple_of((n_chunks - 1 - j) * batch, batch), batch)
        out = []
        for g in range(gb):
            sr, si = carry[g]
            sin_scr[g, rf, 0:hw] = sr[:, 0:hw]
            sin_scr[g, rb, hw:sw] = sr[:, hw:sw]
            sin_scr[g, rf, sw:sw + hw] = si[:, 0:hw]
            sin_scr[g, rb, sw + hw:2 * sw] = si[:, hw:sw]
            wr = jnp.where(fwd, wc_scr[g, rf, 0:sw], wc_scr[g, rb, 0:sw])
            wi = jnp.where(fwd, wc_scr[g, rf, sw:2 * sw], wc_scr[g, rb, sw:2 * sw])
            ar = a_ref[g, 0]
            ai = a_ref[g, 1]
            out.append((ar * sr - ai * si + wr, ar * si + ai * sr + wi))
        return tuple(out)

    fin = lax.fori_loop(0, n_chunks, step, tuple((h0_ref[g, 0], h0_ref[g, 1]) for g in range(gb)))
    for g in range(gb):
        y = (jnp.dot(u_ref[g], toep_ref[g], preferred_element_type=F32)
             + jnp.dot(sin_scr[g].astype(BF16), v_ref[g], preferred_element_type=F32))
        y_ref[g] = y.astype(y_ref.dtype)
        if with_state:
            fin_ref[g, 0] = fin[g][0]
            fin_ref[g, 1] = fin[g][1]


def _ssm(u2, ssm_ops, h0, batch, with_state):
    toep, w, v, a16 = ssm_ops
    g, rows, kw = u2.shape
    n_chunks = rows // batch
    gb = SSM_GROUP_VREGS * 8 // batch
    blk = lambda a: pl.BlockSpec((gb,) + a.shape[1:], lambda i: (i,) + (0,) * (a.ndim - 1))
    out_specs = [blk(u2)]
    out_shape = [jax.ShapeDtypeStruct(u2.shape, BF16)]
    if with_state:
        out_specs.append(blk(h0))
        out_shape.append(jax.ShapeDtypeStruct(h0.shape, F32))
    return pl.pallas_call(
        functools.partial(_ssm_kernel, batch=batch, n_chunks=n_chunks, gb=gb, with_state=with_state),
        grid=(g // gb,),
        in_specs=[blk(u2), blk(toep), blk(w), blk(v), blk(a16), blk(h0)],
        out_specs=out_specs,
        out_shape=out_shape,
        scratch_shapes=[pltpu.VMEM((gb, rows, w.shape[-1]), F32),
                        pltpu.VMEM((gb, rows, w.shape[-1]), F32)],
        compiler_params=_params("arbitrary"),
        name="ssm_state" if with_state else "ssm",
    )(u2, toep, w, v, a16, h0)


def _to_chunks(u, g):
    b, l, d = u.shape
    hc = d // g
    t = SSM_CHUNK
    return u.reshape(b, l // t, t, g, hc).transpose(3, 1, 0, 2, 4).reshape(g, (l // t) * b, t * hc)


def _from_chunks(y2, b):
    g, rows, kw = y2.shape
    t = SSM_CHUNK
    hc = kw // t
    n = rows // b
    return y2.reshape(g, n, b, t, hc).transpose(2, 1, 3, 0, 4).reshape(b, n * t, g * hc)


def _state_to_lanes(h0):
    b, nd, nc, g, p = h0.shape
    return h0.astype(F32).transpose(3, 2, 0, 1, 4).reshape(g, nc, b, nd * p)


def _state_from_lanes(fin, nd):
    g, nc, b, w = fin.shape
    return fin.reshape(g, nc, b, nd, w // nd).transpose(2, 3, 1, 0, 4)


def _outproj_kernel(x_ref, a_ref, y_ref, u_ref, gs_ref, gate_ref, dskip_ref,
                    wglu_ref, bglu_ref, wout_ref, g_ref, o_ref):
    u = u_ref[...].astype(F32)
    y = y_ref[...].astype(F32) + dskip_ref[...] * u
    c0 = math.sqrt(2.0 / math.pi)
    ys = 0.5 * y * (1.0 + jnp.tanh(c0 * (y + 0.044715 * (y * y * y))))
    z = jnp.dot(ys.astype(BF16), wglu_ref[...], preferred_element_type=F32) + bglu_ref[...]
    ys = ys * jax.nn.sigmoid(z) * gs_ref[...].astype(F32)
    cat = jnp.concatenate([a_ref[...], ys.astype(BF16)], axis=-1)
    out = jnp.dot(cat, wout_ref[...], preferred_element_type=F32)
    ms = jnp.mean(out * out, axis=-1, keepdims=True)
    out = out * lax.rsqrt(ms + EPS) * g_ref[...]
    o_ref[...] = x_ref[...] + gate_ref[...] * out


def _outproj(x, a_out, y_ssm, u, gs, gate, d_skip, w_glu_bf, b_glu, w_out_bf, norm_post):
    b, l, d = x.shape
    d_attn = a_out.shape[-1]
    d_ssm = gs.shape[-1]
    tt = TOKEN_TILE
    nt = l // tt
    per_batch = gate.shape[0] == b
    gate_map = (lambda i, j: (i, 0, 0)) if per_batch else (lambda i, j: (0, 0, 0))
    const = lambda shape: pl.BlockSpec(shape, lambda i, j: (0,) * len(shape))
    tok = lambda w: pl.BlockSpec((None, tt, w), lambda i, j: (i, j, 0))
    in_specs = [tok(d), tok(d_attn), tok(d_ssm), tok(d_ssm), tok(d_ssm),
                pl.BlockSpec((None, 1, d), gate_map),
                const((1, d_ssm)), const(w_glu_bf.shape), const((1, d_ssm)),
                const(w_out_bf.shape), const((1, d))]
    return pl.pallas_call(
        _outproj_kernel,
        grid=(b, nt),
        in_specs=in_specs,
        out_specs=tok(d),
        out_shape=jax.ShapeDtypeStruct((b, l, d), F32),
        compiler_params=_params("arbitrary", "arbitrary"),
        name="outproj",
    )(x, a_out, y_ssm, u, gs, gate, d_skip.reshape(1, d_ssm).astype(F32), w_glu_bf,
      b_glu.reshape(1, d_ssm).astype(F32), w_out_bf, norm_post.reshape(1, d).astype(F32))


def _mixer(x, mod, h0_lanes, cache_k, cache_v, rope_tabs, kv_dtype, with_state, lam, lam_init,
           norm_pre, w_in_bf, ssm_ops, subln, d_skip, w_glu_bf, b_glu, w_out_bf, norm_post):
    b, l, d = x.shape
    d_ssm = w_glu_bf.shape[0]
    d_attn = (w_in_bf.shape[1] - 2 * d_ssm) // 4
    n_groups = h0_lanes.shape[0]
    shift, scale, gate = mod
    q, k, v, ga, u, gs = _inproj(x, shift, scale, norm_pre, w_in_bf, rope_tabs, kv_dtype,
                                 d_attn, d_ssm)
    a_out = _attention(lam, q, k, v, cache_k, cache_v, ga, subln, lam_init)
    outs = _ssm(_to_chunks(u, n_groups), ssm_ops, h0_lanes, b, with_state)
    y_ssm = _from_chunks(outs[0], b)
    y = _outproj(x, a_out, y_ssm, u, gs, gate, d_skip, w_glu_bf, b_glu, w_out_bf, norm_post)
    fin = outs[1] if with_state else None
    return y, k, v, fin


def kernel(x_prompt, x_sample, cache_k, cache_v, state_ssm, c, c_ctx, w_ada, b_ada, norm_pre, norm_post, w_in, lambda_qk, subln, ssm_A_re, ssm_A_im, ssm_log_dt, ssm_B_re, ssm_B_im, ssm_C_re, ssm_C_im, ssm_D, w_glu, b_glu, w_out):
    xp, xs = x_prompt, x_sample
    bp, lp, d = xp.shape
    bs, ls, _ = xs.shape
    depth = w_in.shape[0]
    nd = ssm_A_re.shape[1]
    g, p = ssm_A_re.shape[-2:]
    d_ssm = w_glu.shape[-1]
    d_attn = (w_in.shape[-1] - 2 * d_ssm) // 4
    rope_tabs = _rope_tables(ls, d_attn)
    cond_rows = 16
    cond = jnp.zeros((cond_rows, d), F32).at[:bs].set(c.astype(F32)).at[bs].set(c_ctx.astype(F32))
    new_k, new_v, new_s = [], [], []
    for layer in range(depth):
        lam_init = 0.8 - 0.6 * math.exp(-0.3 * layer)
        m = _adaln(cond, w_ada[layer], b_ada[layer])
        mod_s = tuple(m[:bs, i * d:(i + 1) * d].reshape(bs, 1, d) for i in range(3))
        mod_p = tuple(m[bs:bs + 1, i * d:(i + 1) * d].reshape(1, 1, d) for i in range(3))
        ssm_ops, lam = _ssmprep(
            ssm_A_re[layer], ssm_A_im[layer], ssm_log_dt[layer], ssm_B_re[layer], ssm_B_im[layer],
            ssm_C_re[layer], ssm_C_im[layer], lambda_qk[layer], lam_init)
        shared = (lam, lam_init, norm_pre[layer], w_in[layer].astype(BF16), ssm_ops, subln[layer],
                  ssm_D[layer], w_glu[layer].astype(BF16), b_glu[layer], w_out[layer].astype(BF16),
                  norm_post[layer])

        h0_p = jnp.zeros((g, 2, bp, nd * p), F32)
        xp, k_p, v_p, fin = _mixer(xp, mod_p, h0_p, None, None, None, F32, True, *shared)
        new_k.append(k_p.reshape(bp, lp, N_HEADS, 2 * QK_DIM))
        new_v.append(v_p.reshape(bp, lp, N_HEADS, V_DIM))
        new_s.append(_state_from_lanes(fin, nd).astype(xp.dtype))

        ck = cache_k[:, layer].reshape(bs, -1, d_attn)
        cv = cache_v[:, layer].reshape(bs, -1, d_attn)
        h0_s = _state_to_lanes(state_ssm[:, layer])
        xs, _, _, _ = _mixer(xs, mod_s, h0_s, ck, cv, rope_tabs, BF16, False, *shared)

    return (xp, xs, jnp.stack(new_k, axis=1), jnp.stack(new_v, axis=1), jnp.stack(new_s, axis=1))
```

```python
import functools
import math

import jax
import jax.numpy as jnp
from jax import lax
from jax.experimental import pallas as pl
from jax.experimental.pallas import tpu as pltpu

F32 = jnp.float32
BF16 = jnp.bfloat16

GRID_W = 64
N_HEADS = 4
QK_DIM = 64
V_DIM = 2 * QK_DIM
ROPE_THETA = 10000.0
EPS = 1e-6
Q_PRESCALE = math.log2(math.e) * QK_DIM ** -0.5

SUM_ROWS = 16
TOKEN_TILE = 256
ATTN_Q_TILE = 1024
SSM_CHUNK = 16
SSM_GROUP_BLOCK = 8
VMEM_LIMIT = 48 * 1024 * 1024


def _silu(x):
    return x * jax.nn.sigmoid(x)


def _params(*sem):
    return pltpu.CompilerParams(dimension_semantics=sem, vmem_limit_bytes=VMEM_LIMIT)


def _adaln_kernel(c_ref, w_ref, b_ref, o_ref):
    s = _silu(c_ref[...]).astype(BF16)
    o_ref[...] = jnp.dot(s, w_ref[...].astype(BF16), preferred_element_type=F32) + b_ref[...]


def _adaln(cond, w_ada, b_ada):
    rows, d = cond.shape
    n = w_ada.shape[1]
    bn = 512
    return pl.pallas_call(
        _adaln_kernel,
        grid=(n // bn,),
        in_specs=[pl.BlockSpec((rows, d), lambda j: (0, 0)),
                  pl.BlockSpec((d, bn), lambda j: (0, j)),
                  pl.BlockSpec((1, bn), lambda j: (0, j))],
        out_specs=pl.BlockSpec((rows, bn), lambda j: (0, j)),
        out_shape=jax.ShapeDtypeStruct((rows, n), F32),
        compiler_params=_params("arbitrary"),
        name="adaln",
    )(cond, w_ada, b_ada.reshape(1, n))


def _ssmprep_kernel(are_ref, aim_ref, ldt_ref, bre_ref, bim_ref, cre_ref, cim_ref, lq_ref,
                    toept_ref, wt_ref, vt_ref, a_ref, lam_ref,
                    cplre_scr, cplimn_scr, w_scr, v_scr, toep_scr, *, lam_init, chunk):
    a_re = are_ref[...]
    a_im = aim_ref[...]
    dt = jnp.exp(ldt_ref[...])
    mag = jnp.exp(a_re * dt)
    ab_re = mag * jnp.cos(a_im * dt)
    ab_im = mag * jnp.sin(a_im * dt)
    nr, ni = ab_re - 1.0, ab_im
    den = a_re * a_re + a_im * a_im
    f_re = (nr * a_re + ni * a_im) / den
    f_im = (ni * a_re - nr * a_im) / den
    b_re = bre_ref[...]
    b_im = bim_ref[...]
    bb_re = f_re * b_re - f_im * b_im
    bb_im = f_re * b_im + f_im * b_re
    c_re = cre_ref[...]
    c_im = cim_ref[...]
    hc = b_re.shape[1]
    sw = a_re.shape[-1]
    hw = sw // 2
    lo, hi = slice(0, hw), slice(hw, sw)
    lo_im, hi_im = slice(sw, sw + hw), slice(sw + hw, 2 * sw)
    blk = lambda j: slice(j * hc, (j + 1) * hc)
    last = chunk - 1
    cplre_scr[...] = jnp.zeros_like(cplre_scr)
    cplimn_scr[...] = jnp.zeros_like(cplimn_scr)
    pw_re = jnp.ones_like(ab_re)
    pw_im = jnp.zeros_like(ab_im)
    for tau in range(chunk + 1):
        cp_re = c_re * pw_re - c_im * pw_im
        cp_imn = -(c_re * pw_im + c_im * pw_re)
        if tau < chunk:
            cplre_scr[:, blk(last + tau), lo] = cp_re[:, :, lo]
            cplre_scr[:, blk(last - tau), hi] = cp_re[:, :, hi]
            cplimn_scr[:, blk(last + tau), lo] = cp_imn[:, :, lo]
            cplimn_scr[:, blk(last - tau), hi] = cp_imn[:, :, hi]
            e_re = pw_re * bb_re - pw_im * bb_im
            e_im = pw_re * bb_im + pw_im * bb_re
            w_scr[:, blk(last - tau), lo] = e_re[:, :, lo]
            w_scr[:, blk(tau), hi] = e_re[:, :, hi]
            w_scr[:, blk(last - tau), lo_im] = e_im[:, :, lo]
            w_scr[:, blk(tau), hi_im] = e_im[:, :, hi]
        if tau >= 1:
            v_scr[:, blk(tau - 1), lo] = cp_re[:, :, lo]
            v_scr[:, blk(chunk - tau), hi] = cp_re[:, :, hi]
            v_scr[:, blk(tau - 1), lo_im] = cp_imn[:, :, lo]
            v_scr[:, blk(chunk - tau), hi_im] = cp_imn[:, :, hi]
        if tau < chunk:
            pw_re, pw_im = pw_re * ab_re - pw_im * ab_im, pw_re * ab_im + pw_im * ab_re
    a_ref[:, 0] = pw_re
    a_ref[:, 1] = pw_im
    dims = (((2,), (2,)), ((0,), (0,)))
    taps = (lax.dot_general(bb_re, cplre_scr[...], dims, precision=lax.Precision.HIGHEST,
                            preferred_element_type=F32)
            + lax.dot_general(bb_im, cplimn_scr[...], dims, precision=lax.Precision.HIGHEST,
                              preferred_element_type=F32))
    nk = chunk * hc
    for s in range(chunk):
        toep_scr[:, blk(s), :] = taps[:, :, (last - s) * hc:(last - s) * hc + nk]
    for g in range(toept_ref.shape[0]):
        toept_ref[g] = toep_scr[g].T.astype(toept_ref.dtype)
        wt_ref[g] = w_scr[g].T.astype(wt_ref.dtype)
    vt_ref[...] = v_scr[...].astype(vt_ref.dtype)
    lq = lq_ref[...]
    s01 = jnp.sum(lq[0:1] * lq[1:2], axis=-1, keepdims=True)
    s23 = jnp.sum(lq[2:3] * lq[3:4], axis=-1, keepdims=True)
    lam = jnp.exp(s01) - jnp.exp(s23) + lam_init
    lam_ref[...] = jnp.broadcast_to(lam, lam_ref.shape)


def _ssmprep(a_re, a_im, log_dt, b_re, b_im, c_re, c_im, lq, lam_init):
    nd, g, p = a_re.shape
    hc = b_re.shape[-1]
    t = SSM_CHUNK
    sw = nd * p
    gb = SSM_GROUP_BLOCK
    row = lambda a: a.astype(F32).transpose(1, 0, 2).reshape(g, 1, sw)
    ldt = jnp.broadcast_to(log_dt[..., None], (nd, g, p))
    bt = lambda a: a.astype(F32).transpose(1, 3, 0, 2).reshape(g, hc, sw)
    ct = lambda a: a.astype(F32).transpose(1, 2, 0, 3).reshape(g, hc, sw)
    sds = jax.ShapeDtypeStruct
    nk = t * hc
    blk = lambda shape: pl.BlockSpec((gb,) + shape[1:], lambda i: (i,) + (0,) * (len(shape) - 1))
    mat = (g, nk, 2 * sw)
    toept, wt, vt, a_t, lam = pl.pallas_call(
        functools.partial(_ssmprep_kernel, lam_init=lam_init, chunk=t),
        grid=(g // gb,),
        in_specs=[blk((g, 1, sw))] * 3 + [blk((g, hc, sw))] * 4 + [pl.BlockSpec(lq.shape, lambda i: (0, 0))],
        out_specs=[blk((g, nk, nk)), blk((g, 2 * sw, nk)), blk(mat), blk((g, 2, 1, sw)),
                   pl.BlockSpec((8, 128), lambda i: (0, 0))],
        out_shape=(sds((g, nk, nk), BF16), sds((g, 2 * sw, nk), BF16), sds(mat, BF16),
                   sds((g, 2, 1, sw), F32), sds((8, 128), F32)),
        scratch_shapes=[pltpu.VMEM((gb, 2 * nk, sw), F32), pltpu.VMEM((gb, 2 * nk, sw), F32),
                        pltpu.VMEM((gb, nk, 2 * sw), F32), pltpu.VMEM((gb, nk, 2 * sw), F32),
                        pltpu.VMEM((gb, nk, nk), F32)],
        compiler_params=_params("arbitrary"),
        name="ssmprep",
    )(row(a_re), row(a_im), row(ldt), bt(b_re), bt(b_im), ct(c_re), ct(c_im), lq.astype(F32))
    return (toept, wt, vt, a_t), lam


def _to_scan_layout(u, o_ref, batch):
    t = o_ref.shape[0]
    tp = u.shape[0] // batch
    u_tb = pltpu.einshape("abl->bal", u.reshape(batch, tp, u.shape[-1]))
    for c in range(tp // t):
        for s in range(t):
            o_ref[s, c * batch:(c + 1) * batch, :] = u_tb[c * t + s]


def _from_scan_layout(ref, batch):
    t, rows, d = ref.shape
    tp = rows // batch * t
    u_tb = jnp.stack([ref[s, c * batch:(c + 1) * batch, :] for c in range(tp // t) for s in range(t)], axis=0)
    return pltpu.einshape("abl->bal", u_tb).reshape(batch * tp, d)


def _inproj_kernel(*refs, rope, d_attn):
    if rope:
        (x_ref, shift_ref, scale_ref, g_ref, w_ref, cos_ref, sin_ref,
         q_ref, k_ref, v_ref, ga_ref, u_ref, gs_ref) = refs
    else:
        (x_ref, shift_ref, scale_ref, g_ref, w_ref,
         q_ref, k_ref, v_ref, ga_ref, u_ref, gs_ref) = refs
    x = x_ref[...]
    nb, tp, d = x.shape
    ms = jnp.mean(x * x, axis=-1, keepdims=True)
    h = x * lax.rsqrt(ms + EPS) * (g_ref[...] * (1.0 + scale_ref[...])) + shift_ref[...]
    proj = jnp.dot(h.reshape(nb * tp, d).astype(BF16), w_ref[...], preferred_element_type=F32)
    da = d_attn
    q = proj[:, 0:da]
    k = proj[:, da:2 * da]
    tok = lambda z: z.reshape(nb, tp, z.shape[-1])
    if rope:
        lane = lax.broadcasted_iota(jnp.int32, (1, da), 1)
        low = (lane % (QK_DIM // 2)) < (QK_DIM // 4)
        cos = cos_ref[...]
        sin = sin_ref[...]

        def rot(z):
            partner = jnp.where(low, pltpu.roll(z, da - QK_DIM // 4, 1), pltpu.roll(z, QK_DIM // 4, 1))
            return tok(z) * cos + tok(partner) * sin

        q = rot(q)
        k = rot(k)
    else:
        q = tok(q)
        k = tok(k)
    q_ref[...] = (q * Q_PRESCALE).astype(q_ref.dtype)
    k_ref[...] = k.astype(k_ref.dtype)
    v_ref[...] = tok(proj[:, 2 * da:3 * da]).astype(v_ref.dtype)
    ga_ref[...] = tok(_silu(proj[:, 3 * da:4 * da])).astype(ga_ref.dtype)
    d_ssm = u_ref.shape[-1]
    _to_scan_layout(proj[:, 4 * da:4 * da + d_ssm], u_ref, nb)
    gs_ref[...] = tok(_silu(proj[:, 4 * da + d_ssm:])).astype(gs_ref.dtype)


def _inproj(x, shift, scale, norm_pre, w_in_bf, rope_tabs, kv_dtype, d_attn, d_ssm):
    b, l, d = x.shape
    tp = TOKEN_TILE // b
    t = SSM_CHUNK
    rope = rope_tabs is not None
    const = lambda a: pl.BlockSpec(a.shape, lambda j: (0,) * a.ndim)
    tok = lambda w: pl.BlockSpec((b, tp, w), lambda j: (0, j, 0))
    g2 = norm_pre.reshape(1, d).astype(F32)
    in_specs = [tok(d), const(shift), const(scale), const(g2), const(w_in_bf)]
    args = [x, shift, scale, g2, w_in_bf]
    if rope:
        in_specs += [pl.BlockSpec((tp, d_attn), lambda j: (j, 0))] * 2
        args += list(rope_tabs)
    scan_rows = tp // t * b
    out_specs = [tok(d_attn), tok(d_attn), tok(d_attn), tok(d_attn),
                 pl.BlockSpec((t, scan_rows, d_ssm), lambda j: (0, j, 0)), tok(d_ssm)]
    out_shape = [jax.ShapeDtypeStruct((b, l, d_attn), BF16),
                 jax.ShapeDtypeStruct((b, l, d_attn), kv_dtype),
                 jax.ShapeDtypeStruct((b, l, d_attn), kv_dtype),
                 jax.ShapeDtypeStruct((b, l, d_attn), BF16),
                 jax.ShapeDtypeStruct((t, l // t * b, d_ssm), F32),
                 jax.ShapeDtypeStruct((b, l, d_ssm), BF16)]
    return pl.pallas_call(
        functools.partial(_inproj_kernel, rope=rope, d_attn=d_attn),
        grid=(l // tp,),
        in_specs=in_specs,
        out_specs=out_specs,
        out_shape=out_shape,
        compiler_params=_params("arbitrary"),
        name="inproj_rope" if rope else "inproj",
    )(*args)


def _rope_tables(l, d_attn):
    rows = l // GRID_W
    row = jnp.repeat(jnp.arange(rows, dtype=F32), GRID_W)
    col = jnp.tile(jnp.arange(GRID_W, dtype=F32), rows)
    n_freq = QK_DIM // 4
    inv = ROPE_THETA ** (-jnp.arange(n_freq, dtype=F32) / n_freq)
    ang_r = row[:, None] * inv
    ang_c = col[:, None] * inv
    cos64 = jnp.concatenate([jnp.cos(ang_r), jnp.cos(ang_r), jnp.cos(ang_c), jnp.cos(ang_c)], axis=-1)
    sin64 = jnp.concatenate([-jnp.sin(ang_r), jnp.sin(ang_r), -jnp.sin(ang_c), jnp.sin(ang_c)], axis=-1)
    reps = d_attn // QK_DIM
    return jnp.tile(cos64, (1, reps)), jnp.tile(sin64, (1, reps))


def _attn_kernel(*refs, has_cache, lam_init):
    if has_cache:
        lam_ref, q_ref, k_ref, v_ref, ck_ref, cv_ref, ga_ref, sub_ref, o_ref, k_scr, vt_scr = refs
    else:
        lam_ref, q_ref, k_ref, v_ref, ga_ref, sub_ref, o_ref, k_scr, vt_scr = refs
    lk = k_scr.shape[0]
    ln = k_ref.shape[0]

    @pl.when(pl.program_id(2) == 0)
    def _():
        if has_cache:
            k_scr[0:lk - ln, :] = ck_ref[...].astype(BF16)
            vt_scr[0:V_DIM, 0:lk - ln] = cv_ref[...].astype(F32).T.astype(BF16)
        k_scr[lk - ln:lk, :] = k_ref[...].astype(BF16)
        vt_scr[0:V_DIM, lk - ln:lk] = v_ref[...].astype(F32).T.astype(BF16)
        vt_scr[V_DIM:, :] = jnp.ones((vt_scr.shape[0] - V_DIM, lk), BF16)

    lam = lam_ref[0:1, 0:1]
    q = q_ref[...]
    lane = lax.broadcasted_iota(jnp.int32, (1, V_DIM), 1)
    first = lane < QK_DIM
    zero = jnp.zeros_like(q)
    k = k_scr[...]
    vt = vt_scr[...]
    dims = (((1,), (1,)), ((), ()))

    st1 = lax.dot_general(k, jnp.where(first, q, zero), dims, preferred_element_type=F32)
    st2 = lax.dot_general(k, jnp.where(first, zero, q), dims, preferred_element_type=F32)
    et1 = jnp.exp2(st1 - jnp.max(st1, axis=0, keepdims=True)).astype(BF16)
    et2 = jnp.exp2(st2 - jnp.max(st2, axis=0, keepdims=True)).astype(BF16)
    o1 = jnp.dot(vt, et1, preferred_element_type=F32)
    o2 = jnp.dot(vt, et2, preferred_element_type=F32)
    r1 = 1.0 / o1[V_DIM:V_DIM + 1, :]
    r2 = lam / o2[V_DIM:V_DIM + 1, :]
    o = (o1[0:V_DIM, :] * r1 - o2[0:V_DIM, :] * r2).T
    ms = jnp.mean(o * o, axis=-1, keepdims=True)
    o = o * lax.rsqrt(ms + EPS) * sub_ref[...] * (1.0 - lam_init)
    o_ref[...] = (o * ga_ref[...].astype(F32)).astype(o_ref.dtype)


def _attention(lam, q, k, v, cache_k, cache_v, ga, subln, lam_init):
    b, l, d_attn = q.shape
    tq = min(ATTN_Q_TILE, l)
    nq = l // tq
    has_cache = cache_k is not None
    q_spec = pl.BlockSpec((None, tq, V_DIM), lambda i, h, j: (i, j, h))
    kv_spec = pl.BlockSpec((None, l, V_DIM), lambda i, h, j: (i, 0, h))
    in_specs = [pl.BlockSpec((8, 128), lambda i, h, j: (0, 0)), q_spec, kv_spec, kv_spec]
    args = [lam, q, k, v]
    if has_cache:
        lc = cache_k.shape[1]
        c_spec = pl.BlockSpec((None, lc, V_DIM), lambda i, h, j: (i, 0, h))
        in_specs += [c_spec, c_spec]
        args += [cache_k, cache_v]
    in_specs += [q_spec, pl.BlockSpec((1, V_DIM), lambda i, h, j: (0, 0))]
    args += [ga, subln.reshape(1, V_DIM).astype(F32)]
    lk = l + (cache_k.shape[1] if has_cache else 0)
    out = pl.pallas_call(
        functools.partial(_attn_kernel, has_cache=has_cache, lam_init=lam_init),
        grid=(b, N_HEADS, nq),
        in_specs=in_specs,
        out_specs=q_spec,
        out_shape=jax.ShapeDtypeStruct((b, l, d_attn), BF16),
        scratch_shapes=[pltpu.VMEM((lk, V_DIM), BF16), pltpu.VMEM((V_DIM + SUM_ROWS, lk), BF16)],
        compiler_params=_params("arbitrary", "arbitrary", "arbitrary"),
        name="attn_cache" if has_cache else "attn",
    )(*args)
    return out


def _ssm_kernel(*refs, batch, n_chunks, with_state):
    if with_state:
        u_ref, toept_ref, wt_ref, vt_ref, a_ref, h0_ref, y_ref, fin_ref, zt_scr, wc_scr, sin_scr, yt_scr = refs
    else:
        u_ref, toept_ref, wt_ref, vt_ref, a_ref, h0_ref, y_ref, zt_scr, wc_scr, sin_scr, yt_scr = refs
    t = u_ref.shape[0]
    gb = zt_scr.shape[0]
    hc = zt_scr.shape[1] // t
    sw = a_ref.shape[-1]
    hw = sw // 2
    for s in range(t):
        at = u_ref[s].T.astype(BF16)
        for g in range(gb):
            zt_scr[g, s * hc:(s + 1) * hc, :] = at[g * hc:(g + 1) * hc, :]
    for g in range(gb):
        wc_scr[g] = jnp.dot(wt_ref[g], zt_scr[g], preferred_element_type=F32).T
    fwd = lax.broadcasted_iota(jnp.int32, (1, sw), 1) < hw

    def step(j, carry):
        rf = pl.ds(pl.multiple_of(j * batch, batch), batch)
        rb = pl.ds(pl.multiple_of((n_chunks - 1 - j) * batch, batch), batch)
        out = []
        for g in range(gb):
            sr, si = carry[g]
            sin_scr[g, rf, 0:hw] = sr[:, 0:hw]
            sin_scr[g, rb, hw:sw] = sr[:, hw:sw]
            sin_scr[g, rf, sw:sw + hw] = si[:, 0:hw]
            sin_scr[g, rb, sw + hw:2 * sw] = si[:, hw:sw]
            wr = jnp.where(fwd, wc_scr[g, rf, 0:sw], wc_scr[g, rb, 0:sw])
            wi = jnp.where(fwd, wc_scr[g, rf, sw:2 * sw], wc_scr[g, rb, sw:2 * sw])
            ar = a_ref[g, 0]
            ai = a_ref[g, 1]
            out.append((ar * sr - ai * si + wr, ar * si + ai * sr + wi))
        return tuple(out)

    fin = lax.fori_loop(0, n_chunks, step, tuple((h0_ref[g, 0], h0_ref[g, 1]) for g in range(gb)))
    for g in range(gb):
        sint = sin_scr[g].T.astype(BF16)
        y2t = (jnp.dot(toept_ref[g], zt_scr[g], preferred_element_type=F32)
               + jnp.dot(vt_ref[g], sint, preferred_element_type=F32))
        for s in range(t):
            yt_scr[s, g * hc:(g + 1) * hc, :] = y2t[s * hc:(s + 1) * hc, :]
        if with_state:
            fin_ref[g, 0] = fin[g][0]
            fin_ref[g, 1] = fin[g][1]
    for s in range(t):
        y_ref[s] = yt_scr[s].T.astype(y_ref.dtype)


def _ssm(u3, ssm_ops, h0, with_state):
    toept, wt, vt, a_t = ssm_ops
    t, rows, d_ssm = u3.shape
    batch = h0.shape[2]
    n_chunks = rows // batch
    g = toept.shape[0]
    gb = SSM_GROUP_BLOCK
    cw = d_ssm // g * gb
    slab = pl.BlockSpec((t, rows, cw), lambda i: (0, 0, i))
    blk = lambda a: pl.BlockSpec((gb,) + a.shape[1:], lambda i: (i,) + (0,) * (a.ndim - 1))
    out_specs = [slab]
    out_shape = [jax.ShapeDtypeStruct(u3.shape, F32)]
    if with_state:
        out_specs.append(blk(h0))
        out_shape.append(jax.ShapeDtypeStruct(h0.shape, F32))
    nk = toept.shape[-1]
    outs = pl.pallas_call(
        functools.partial(_ssm_kernel, batch=batch, n_chunks=n_chunks, with_state=with_state),
        grid=(g // gb,),
        in_specs=[slab, blk(toept), blk(wt), blk(vt), blk(a_t), blk(h0)],
        out_specs=out_specs,
        out_shape=out_shape,
        scratch_shapes=[pltpu.VMEM((gb, nk, rows), BF16),
                        pltpu.VMEM((gb, rows, wt.shape[1]), F32),
                        pltpu.VMEM((gb, rows, wt.shape[1]), F32),
                        pltpu.VMEM((t, cw, rows), F32)],
        compiler_params=_params("arbitrary"),
        name="ssm_state" if with_state else "ssm",
    )(u3, toept, wt, vt, a_t, h0)
    return outs


def _state_to_lanes(h0):
    b, nd, nc, g, p = h0.shape
    return h0.astype(F32).transpose(3, 2, 0, 1, 4).reshape(g, nc, b, nd * p)


def _state_from_lanes(fin, nd):
    g, nc, b, w = fin.shape
    return fin.reshape(g, nc, b, nd, w // nd).transpose(2, 3, 1, 0, 4)


def _outproj_kernel(x_ref, a_ref, y_ref, u_ref, gs_ref, gate_ref, dskip_ref,
                    wglu_ref, bglu_ref, wout_ref, g_ref, o_ref):
    nb, tp, d = x_ref.shape
    rows = lambda ref: ref[...].reshape(nb * tp, ref.shape[-1])
    y = _from_scan_layout(y_ref, nb) + dskip_ref[...] * _from_scan_layout(u_ref, nb)
    c0 = math.sqrt(2.0 / math.pi)
    ys = 0.5 * y * (1.0 + jnp.tanh(c0 * (y + 0.044715 * (y * y * y))))
    z = jnp.dot(ys.astype(BF16), wglu_ref[...], preferred_element_type=F32) + bglu_ref[...]
    ys = ys * jax.nn.sigmoid(z) * rows(gs_ref).astype(F32)
    cat = jnp.concatenate([rows(a_ref), ys.astype(BF16)], axis=-1)
    out = jnp.dot(cat, wout_ref[...], preferred_element_type=F32)
    ms = jnp.mean(out * out, axis=-1, keepdims=True)
    out = out * lax.rsqrt(ms + EPS) * g_ref[...]
    o_ref[...] = x_ref[...] + gate_ref[...] * out.reshape(nb, tp, d)


def _outproj(x, a_out, y_ssm, u, gs, gate, d_skip, w_glu_bf, b_glu, w_out_bf, norm_post):
    b, l, d = x.shape
    d_attn = a_out.shape[-1]
    d_ssm = gs.shape[-1]
    tp = TOKEN_TILE // b
    t = SSM_CHUNK
    const = lambda a: pl.BlockSpec(a.shape, lambda j: (0,) * a.ndim)
    tok = lambda w: pl.BlockSpec((b, tp, w), lambda j: (0, j, 0))
    scan = pl.BlockSpec((t, tp // t * b, d_ssm), lambda j: (0, j, 0))
    consts = [gate, d_skip.reshape(1, d_ssm).astype(F32), w_glu_bf, b_glu.reshape(1, d_ssm).astype(F32),
              w_out_bf, norm_post.reshape(1, d).astype(F32)]
    return pl.pallas_call(
        _outproj_kernel,
        grid=(l // tp,),
        in_specs=[tok(d), tok(d_attn), scan, scan, tok(d_ssm)] + [const(a) for a in consts],
        out_specs=tok(d),
        out_shape=jax.ShapeDtypeStruct((b, l, d), F32),
        compiler_params=_params("arbitrary"),
        name="outproj",
    )(x, a_out, y_ssm, u, gs, *consts)


def _mixer(x, mod, h0_lanes, cache_k, cache_v, rope_tabs, kv_dtype, with_state, lam, lam_init,
           norm_pre, w_in_bf, ssm_ops, subln, d_skip, w_glu_bf, b_glu, w_out_bf, norm_post):
    b, l, d = x.shape
    d_ssm = w_glu_bf.shape[0]
    d_attn = (w_in_bf.shape[1] - 2 * d_ssm) // 4
    shift, scale, gate = mod
    q, k, v, ga, u, gs = _inproj(x, shift, scale, norm_pre, w_in_bf, rope_tabs, kv_dtype,
                                 d_attn, d_ssm)
    a_out = _attention(lam, q, k, v, cache_k, cache_v, ga, subln, lam_init)
    outs = _ssm(u, ssm_ops, h0_lanes, with_state)
    y = _outproj(x, a_out, outs[0], u, gs, gate, d_skip, w_glu_bf, b_glu, w_out_bf, norm_post)
    fin = outs[1] if with_state else None
    return y, k, v, fin


def kernel(x_prompt, x_sample, cache_k, cache_v, state_ssm, c, c_ctx, w_ada, b_ada, norm_pre, norm_post, w_in, lambda_qk, subln, ssm_A_re, ssm_A_im, ssm_log_dt, ssm_B_re, ssm_B_im, ssm_C_re, ssm_C_im, ssm_D, w_glu, b_glu, w_out):
    xp, xs = x_prompt, x_sample
    bp, lp, d = xp.shape
    bs, ls, _ = xs.shape
    depth = w_in.shape[0]
    nd = ssm_A_re.shape[1]
    g, p = ssm_A_re.shape[-2:]
    d_ssm = w_glu.shape[-1]
    d_attn = (w_in.shape[-1] - 2 * d_ssm) // 4
    rope_tabs = _rope_tables(ls, d_attn)
    cond_rows = 16
    cond = jnp.zeros((cond_rows, d), F32).at[:bs].set(c.astype(F32)).at[bs].set(c_ctx.astype(F32))
    new_k, new_v, new_s = [], [], []
    for layer in range(depth):
        lam_init = 0.8 - 0.6 * math.exp(-0.3 * layer)
        m = _adaln(cond, w_ada[layer], b_ada[layer])
        mod_s = tuple(m[:bs, i * d:(i + 1) * d].reshape(bs, 1, d) for i in range(3))
        mod_p = tuple(m[bs:bs + 1, i * d:(i + 1) * d].reshape(1, 1, d) for i in range(3))
        ssm_ops, lam = _ssmprep(
            ssm_A_re[layer], ssm_A_im[layer], ssm_log_dt[layer], ssm_B_re[layer], ssm_B_im[layer],
            ssm_C_re[layer], ssm_C_im[layer], lambda_qk[layer], lam_init)
        shared = (lam, lam_init, norm_pre[layer], w_in[layer].astype(BF16), ssm_ops, subln[layer],
                  ssm_D[layer], w_glu[layer].astype(BF16), b_glu[layer], w_out[layer].astype(BF16),
                  norm_post[layer])

        h0_p = jnp.zeros((g, 2, bp, nd * p), F32)
        xp, k_p, v_p, fin = _mixer(xp, mod_p, h0_p, None, None, None, F32, True, *shared)
        new_k.append(k_p.reshape(bp, lp, N_HEADS, 2 * QK_DIM))
        new_v.append(v_p.reshape(bp, lp, N_HEADS, V_DIM))
        new_s.append(_state_from_lanes(fin, nd).astype(xp.dtype))

        ck = cache_k[:, layer].reshape(bs, -1, d_attn)
        cv = cache_v[:, layer].reshape(bs, -1, d_attn)
        h0_s = _state_to_lanes(state_ssm[:, layer])
        xs, _, _, _ = _mixer(xs, mod_s, h0_s, ck, cv, rope_tabs, BF16, False, *shared)

    return (xp, xs, jnp.stack(new_k, axis=1), jnp.stack(new_v, axis=1), jnp.stack(new_s, axis=1))
```

```python
import functools
import math

import jax
import jax.numpy as jnp
from jax import lax
from jax.experimental import pallas as pl
from jax.experimental.pallas import tpu as pltpu

F32 = jnp.float32
BF16 = jnp.bfloat16

GRID_W = 64
QK_DIM = 64
V_DIM = 2 * QK_DIM
ROPE_THETA = 10000.0
EPS = 1e-6
Q_PRESCALE = math.log2(math.e) * QK_DIM ** -0.5

SUM_ROWS = 16
TOKEN_TILE = 256
ATTN_Q_TILE = 1024
SSM_CHUNK = 16
SSM_GROUP_BLOCK = 8
VMEM_LIMIT = 48 * 1024 * 1024


def _silu(x):
    return x * jax.nn.sigmoid(x)


def _params(*sem):
    return pltpu.CompilerParams(dimension_semantics=sem, vmem_limit_bytes=VMEM_LIMIT)


def _adaln_kernel(c_ref, w_ref, b_ref, o_ref):
    s = _silu(c_ref[...]).astype(BF16)
    o_ref[...] = jnp.dot(s, w_ref[...].astype(BF16), preferred_element_type=F32) + b_ref[...]


def _adaln(cond, w_ada, b_ada):
    rows, d = cond.shape
    n = w_ada.shape[1]
    bn = 512
    return pl.pallas_call(
        _adaln_kernel,
        grid=(n // bn,),
        in_specs=[pl.BlockSpec((rows, d), lambda j: (0, 0)),
                  pl.BlockSpec((d, bn), lambda j: (0, j)),
                  pl.BlockSpec((1, bn), lambda j: (0, j))],
        out_specs=pl.BlockSpec((rows, bn), lambda j: (0, j)),
        out_shape=jax.ShapeDtypeStruct((rows, n), F32),
        compiler_params=_params("arbitrary"),
        name="adaln",
    )(cond, w_ada, b_ada.reshape(1, n))


def _ssmprep_kernel(are_ref, aim_ref, ldt_ref, bre_ref, bim_ref, cre_ref, cim_ref, lq_ref,
                    toept_ref, wt_ref, vt_ref, a_ref, lam_ref,
                    cplre_scr, cplimn_scr, w_scr, v_scr, toep_scr, *, lam_init, chunk):
    a_re = are_ref[...]
    a_im = aim_ref[...]
    dt = jnp.exp(ldt_ref[...])
    mag = jnp.exp(a_re * dt)
    ab_re = mag * jnp.cos(a_im * dt)
    ab_im = mag * jnp.sin(a_im * dt)
    nr, ni = ab_re - 1.0, ab_im
    den = a_re * a_re + a_im * a_im
    f_re = (nr * a_re + ni * a_im) / den
    f_im = (ni * a_re - nr * a_im) / den
    b_re = bre_ref[...]
    b_im = bim_ref[...]
    bb_re = f_re * b_re - f_im * b_im
    bb_im = f_re * b_im + f_im * b_re
    c_re = cre_ref[...]
    c_im = cim_ref[...]
    hc = b_re.shape[1]
    sw = a_re.shape[-1]
    hw = sw // 2
    lo, hi = slice(0, hw), slice(hw, sw)
    lo_im, hi_im = slice(sw, sw + hw), slice(sw + hw, 2 * sw)
    blk = lambda j: slice(j * hc, (j + 1) * hc)
    last = chunk - 1
    cplre_scr[...] = jnp.zeros_like(cplre_scr)
    cplimn_scr[...] = jnp.zeros_like(cplimn_scr)
    pw_re = jnp.ones_like(ab_re)
    pw_im = jnp.zeros_like(ab_im)
    for tau in range(chunk + 1):
        cp_re = c_re * pw_re - c_im * pw_im
        cp_imn = -(c_re * pw_im + c_im * pw_re)
        if tau < chunk:
            cplre_scr[:, blk(last + tau), lo] = cp_re[:, :, lo]
            cplre_scr[:, blk(last - tau), hi] = cp_re[:, :, hi]
            cplimn_scr[:, blk(last + tau), lo] = cp_imn[:, :, lo]
            cplimn_scr[:, blk(last - tau), hi] = cp_imn[:, :, hi]
            e_re = pw_re * bb_re - pw_im * bb_im
            e_im = pw_re * bb_im + pw_im * bb_re
            w_scr[:, blk(last - tau), lo] = e_re[:, :, lo]
            w_scr[:, blk(tau), hi] = e_re[:, :, hi]
            w_scr[:, blk(last - tau), lo_im] = e_im[:, :, lo]
            w_scr[:, blk(tau), hi_im] = e_im[:, :, hi]
        if tau >= 1:
            v_scr[:, blk(tau - 1), lo] = cp_re[:, :, lo]
            v_scr[:, blk(chunk - tau), hi] = cp_re[:, :, hi]
            v_scr[:, blk(tau - 1), lo_im] = cp_imn[:, :, lo]
            v_scr[:, blk(chunk - tau), hi_im] = cp_imn[:, :, hi]
        if tau < chunk:
            pw_re, pw_im = pw_re * ab_re - pw_im * ab_im, pw_re * ab_im + pw_im * ab_re
    a_ref[:, 0] = pw_re
    a_ref[:, 1] = pw_im
    dims = (((2,), (2,)), ((0,), (0,)))
    taps = (lax.dot_general(bb_re, cplre_scr[...], dims, precision=lax.Precision.HIGHEST,
                            preferred_element_type=F32)
            + lax.dot_general(bb_im, cplimn_scr[...], dims, precision=lax.Precision.HIGHEST,
                              preferred_element_type=F32))
    nk = chunk * hc
    for s in range(chunk):
        toep_scr[:, blk(s), :] = taps[:, :, (last - s) * hc:(last - s) * hc + nk]
    for g in range(toept_ref.shape[0]):
        toept_ref[g] = toep_scr[g].T.astype(toept_ref.dtype)
        wt_ref[g] = w_scr[g].T.astype(wt_ref.dtype)
    vt_ref[...] = v_scr[...].astype(vt_ref.dtype)
    lq = lq_ref[...]
    s01 = jnp.sum(lq[0:1] * lq[1:2], axis=-1, keepdims=True)
    s23 = jnp.sum(lq[2:3] * lq[3:4], axis=-1, keepdims=True)
    lam = jnp.exp(s01) - jnp.exp(s23) + lam_init
    lam_ref[...] = jnp.broadcast_to(lam, lam_ref.shape)


def _ssmprep(a_re, a_im, log_dt, b_re, b_im, c_re, c_im, lq, lam_init):
    nd, g, p = a_re.shape
    hc = b_re.shape[-1]
    t = SSM_CHUNK
    sw = nd * p
    gb = SSM_GROUP_BLOCK
    row = lambda a: a.astype(F32).transpose(1, 0, 2).reshape(g, 1, sw)
    ldt = jnp.broadcast_to(log_dt[..., None], (nd, g, p))
    bt = lambda a: a.astype(F32).transpose(1, 3, 0, 2).reshape(g, hc, sw)
    ct = lambda a: a.astype(F32).transpose(1, 2, 0, 3).reshape(g, hc, sw)
    sds = jax.ShapeDtypeStruct
    nk = t * hc
    blk = lambda shape: pl.BlockSpec((gb,) + shape[1:], lambda i: (i,) + (0,) * (len(shape) - 1))
    mat = (g, nk, 2 * sw)
    toept, wt, vt, a_t, lam = pl.pallas_call(
        functools.partial(_ssmprep_kernel, lam_init=lam_init, chunk=t),
        grid=(g // gb,),
        in_specs=[blk((g, 1, sw))] * 3 + [blk((g, hc, sw))] * 4 + [pl.BlockSpec(lq.shape, lambda i: (0, 0))],
        out_specs=[blk((g, nk, nk)), blk((g, 2 * sw, nk)), blk(mat), blk((g, 2, 1, sw)),
                   pl.BlockSpec((8, 128), lambda i: (0, 0))],
        out_shape=(sds((g, nk, nk), BF16), sds((g, 2 * sw, nk), BF16), sds(mat, BF16),
                   sds((g, 2, 1, sw), F32), sds((8, 128), F32)),
        scratch_shapes=[pltpu.VMEM((gb, 2 * nk, sw), F32), pltpu.VMEM((gb, 2 * nk, sw), F32),
                        pltpu.VMEM((gb, nk, 2 * sw), F32), pltpu.VMEM((gb, nk, 2 * sw), F32),
                        pltpu.VMEM((gb, nk, nk), F32)],
        compiler_params=_params("arbitrary"),
        name="ssmprep",
    )(row(a_re), row(a_im), row(ldt), bt(b_re), bt(b_im), ct(c_re), ct(c_im), lq.astype(F32))
    return (toept, wt, vt, a_t), lam


def _to_scan_layout(u, o_ref, batch):
    t = o_ref.shape[0]
    tp = u.shape[0] // batch
    u_tb = jnp.swapaxes(u.reshape(batch, tp, u.shape[-1]), 0, 1)
    for c in range(tp // t):
        for s in range(t):
            o_ref[s, c * batch:(c + 1) * batch, :] = u_tb[c * t + s]


def _from_scan_layout(ref, batch):
    t, rows, d = ref.shape
    tp = rows // batch * t
    u_tb = jnp.stack([ref[s, c * batch:(c + 1) * batch, :] for c in range(tp // t) for s in range(t)], axis=0)
    return jnp.swapaxes(u_tb, 0, 1).reshape(batch * tp, d)


def _inproj_kernel(*refs, rope, d_attn):
    if rope:
        (x_ref, shift_ref, scale_ref, g_ref, w_ref, cos_ref, sin_ref,
         q_ref, k_ref, v_ref, ga_ref, u_ref, gs_ref) = refs
    else:
        (x_ref, shift_ref, scale_ref, g_ref, w_ref,
         q_ref, k_ref, v_ref, ga_ref, u_ref, gs_ref) = refs
    x = x_ref[...]
    nb, tp, d = x.shape
    ms = jnp.mean(x * x, axis=-1, keepdims=True)
    h = x * lax.rsqrt(ms + EPS) * (g_ref[...] * (1.0 + scale_ref[...])) + shift_ref[...]
    proj = jnp.dot(h.reshape(nb * tp, d).astype(BF16), w_ref[...], preferred_element_type=F32)
    da = d_attn
    q = proj[:, 0:da]
    k = proj[:, da:2 * da]
    tok = lambda z: z.reshape(nb, tp, z.shape[-1])
    if rope:
        lane = lax.broadcasted_iota(jnp.int32, (1, da), 1)
        low = (lane % (QK_DIM // 2)) < (QK_DIM // 4)
        cos = cos_ref[...]
        sin = sin_ref[...]

        def rot(z):
            partner = jnp.where(low, pltpu.roll(z, da - QK_DIM // 4, 1), pltpu.roll(z, QK_DIM // 4, 1))
            return tok(z) * cos + tok(partner) * sin

        q = rot(q)
        k = rot(k)
    else:
        q = tok(q)
        k = tok(k)
    q_ref[...] = (q * Q_PRESCALE).astype(q_ref.dtype)
    v = tok(proj[:, 2 * da:3 * da])
    if len(k_ref.shape) == 4:
        for hd in range(k_ref.shape[2]):
            k_ref[:, :, hd, :] = k[:, :, hd * V_DIM:(hd + 1) * V_DIM].astype(k_ref.dtype)
            v_ref[:, :, hd, :] = v[:, :, hd * V_DIM:(hd + 1) * V_DIM].astype(v_ref.dtype)
    else:
        k_ref[...] = k.astype(k_ref.dtype)
        v_ref[...] = v.astype(v_ref.dtype)
    ga_ref[...] = tok(_silu(proj[:, 3 * da:4 * da])).astype(ga_ref.dtype)
    d_ssm = u_ref.shape[-1]
    _to_scan_layout(proj[:, 4 * da:4 * da + d_ssm], u_ref, nb)
    gs_ref[...] = tok(_silu(proj[:, 4 * da + d_ssm:])).astype(gs_ref.dtype)


def _inproj(x, shift, scale, norm_pre, w_in_bf, rope_tabs, kv_cache_layout, d_attn, d_ssm):
    b, l, d = x.shape
    tp = TOKEN_TILE // b
    t = SSM_CHUNK
    rope = rope_tabs is not None
    const = lambda a: pl.BlockSpec(a.shape, lambda j: (0,) * a.ndim)
    tok = lambda w: pl.BlockSpec((b, tp, w), lambda j: (0, j, 0))
    g2 = norm_pre.reshape(1, d).astype(F32)
    in_specs = [tok(d), const(shift), const(scale), const(g2), const(w_in_bf)]
    args = [x, shift, scale, g2, w_in_bf]
    if rope:
        in_specs += [pl.BlockSpec((tp, d_attn), lambda j: (j, 0))] * 2
        args += list(rope_tabs)
    scan_rows = tp // t * b
    if kv_cache_layout:
        kv_spec = pl.BlockSpec((b, tp, d_attn // V_DIM, V_DIM), lambda j: (0, j, 0, 0))
        kv_shape = jax.ShapeDtypeStruct((b, l, d_attn // V_DIM, V_DIM), F32)
    else:
        kv_spec = tok(d_attn)
        kv_shape = jax.ShapeDtypeStruct((b, l, d_attn), BF16)
    out_specs = [tok(d_attn), kv_spec, kv_spec, tok(d_attn),
                 pl.BlockSpec((t, scan_rows, d_ssm), lambda j: (0, j, 0)), tok(d_ssm)]
    out_shape = [jax.ShapeDtypeStruct((b, l, d_attn), BF16),
                 kv_shape,
                 kv_shape,
                 jax.ShapeDtypeStruct((b, l, d_attn), BF16),
                 jax.ShapeDtypeStruct((t, l // t * b, d_ssm), F32),
                 jax.ShapeDtypeStruct((b, l, d_ssm), BF16)]
    return pl.pallas_call(
        functools.partial(_inproj_kernel, rope=rope, d_attn=d_attn),
        grid=(l // tp,),
        in_specs=in_specs,
        out_specs=out_specs,
        out_shape=out_shape,
        compiler_params=_params("arbitrary"),
        name="inproj_rope" if rope else "inproj",
    )(*args)


def _rope_tables(l, d_attn):
    rows = l // GRID_W
    row = jnp.repeat(jnp.arange(rows, dtype=F32), GRID_W)
    col = jnp.tile(jnp.arange(GRID_W, dtype=F32), rows)
    n_freq = QK_DIM // 4
    inv = ROPE_THETA ** (-jnp.arange(n_freq, dtype=F32) / n_freq)
    ang_r = row[:, None] * inv
    ang_c = col[:, None] * inv
    cos64 = jnp.concatenate([jnp.cos(ang_r), jnp.cos(ang_r), jnp.cos(ang_c), jnp.cos(ang_c)], axis=-1)
    sin64 = jnp.concatenate([-jnp.sin(ang_r), jnp.sin(ang_r), -jnp.sin(ang_c), jnp.sin(ang_c)], axis=-1)
    reps = d_attn // QK_DIM
    return jnp.tile(cos64, (1, reps)), jnp.tile(sin64, (1, reps))


def _head(ref, h):
    return ref[:, h, :] if len(ref.shape) == 3 else ref[:, h * V_DIM:(h + 1) * V_DIM]


def _attn_kernel(*refs, has_cache, lam_init):
    if has_cache:
        lam_ref, q_ref, k_ref, v_ref, ck_ref, cv_ref, ga_ref, sub_ref, o_ref, k_scr, vt_scr = refs
    else:
        lam_ref, q_ref, k_ref, v_ref, ga_ref, sub_ref, o_ref, k_scr, vt_scr = refs
    n_heads, lk = k_scr.shape[0], k_scr.shape[1]
    ln = k_ref.shape[0]
    lam = lam_ref[0:1, 0:1]
    lane = lax.broadcasted_iota(jnp.int32, (1, V_DIM), 1)
    first = lane < QK_DIM
    dims = (((1,), (1,)), ((), ()))

    @pl.when(pl.program_id(1) == 0)
    def _():
        for h in range(n_heads):
            if has_cache:
                k_scr[h, 0:lk - ln, :] = _head(ck_ref, h).astype(BF16)
                vt_scr[h, 0:V_DIM, 0:lk - ln] = _head(cv_ref, h).astype(F32).T.astype(BF16)
            k_scr[h, lk - ln:lk, :] = _head(k_ref, h).astype(BF16)
            vt_scr[h, 0:V_DIM, lk - ln:lk] = _head(v_ref, h).astype(F32).T.astype(BF16)
            vt_scr[h, V_DIM:, :] = jnp.ones((vt_scr.shape[1] - V_DIM, lk), BF16)

    for h in range(n_heads):
        q = _head(q_ref, h)
        zero = jnp.zeros_like(q)
        k = k_scr[h]
        vt = vt_scr[h]
        st1 = lax.dot_general(k, jnp.where(first, q, zero), dims, preferred_element_type=F32)
        st2 = lax.dot_general(k, jnp.where(first, zero, q), dims, preferred_element_type=F32)
        et1 = jnp.exp2(st1 - jnp.max(st1, axis=0, keepdims=True)).astype(BF16)
        et2 = jnp.exp2(st2 - jnp.max(st2, axis=0, keepdims=True)).astype(BF16)
        o1 = jnp.dot(vt, et1, preferred_element_type=F32)
        o2 = jnp.dot(vt, et2, preferred_element_type=F32)
        r1 = 1.0 / o1[V_DIM:V_DIM + 1, :]
        r2 = lam / o2[V_DIM:V_DIM + 1, :]
        o = (o1[0:V_DIM, :] * r1 - o2[0:V_DIM, :] * r2).T
        ms = jnp.mean(o * o, axis=-1, keepdims=True)
        o = o * lax.rsqrt(ms + EPS) * sub_ref[...] * (1.0 - lam_init)
        o_ref[:, h * V_DIM:(h + 1) * V_DIM] = (o * _head(ga_ref, h).astype(F32)).astype(o_ref.dtype)


def _attention(lam, q, k, v, cache_k, cache_v, ga, subln, lam_init):
    b, l, d_attn = q.shape
    n_heads = d_attn // V_DIM
    tq = min(ATTN_Q_TILE, l)
    nq = l // tq
    has_cache = cache_k is not None
    q_spec = pl.BlockSpec((None, tq, d_attn), lambda i, j: (i, j, 0))
    whole = lambda a: pl.BlockSpec((None,) + a.shape[1:], lambda i, j: (i,) + (0,) * (a.ndim - 1))
    in_specs = [pl.BlockSpec((8, 128), lambda i, j: (0, 0)), q_spec, whole(k), whole(v)]
    args = [lam, q, k, v]
    if has_cache:
        in_specs += [whole(cache_k), whole(cache_v)]
        args += [cache_k, cache_v]
    in_specs += [q_spec, pl.BlockSpec((1, V_DIM), lambda i, j: (0, 0))]
    args += [ga, subln.reshape(1, V_DIM).astype(F32)]
    lk = l + (cache_k.shape[1] if has_cache else 0)
    out = pl.pallas_call(
        functools.partial(_attn_kernel, has_cache=has_cache, lam_init=lam_init),
        grid=(b, nq),
        in_specs=in_specs,
        out_specs=q_spec,
        out_shape=jax.ShapeDtypeStruct((b, l, d_attn), BF16),
        scratch_shapes=[pltpu.VMEM((n_heads, lk, V_DIM), BF16),
                        pltpu.VMEM((n_heads, V_DIM + SUM_ROWS, lk), BF16)],
        compiler_params=_params("arbitrary", "arbitrary"),
        name="attn_cache" if has_cache else "attn",
    )(*args)
    return out


def _ssm_kernel(*refs, batch, n_chunks, with_state):
    if with_state:
        u_ref, toept_ref, wt_ref, vt_ref, a_ref, h0_ref, y_ref, fin_ref, zt_scr, wc_scr, sin_scr, yt_scr = refs
    else:
        u_ref, toept_ref, wt_ref, vt_ref, a_ref, h0_ref, y_ref, zt_scr, wc_scr, sin_scr, yt_scr = refs
    t = u_ref.shape[0]
    gb = zt_scr.shape[0]
    hc = zt_scr.shape[1] // t
    sw = a_ref.shape[-1]
    hw = sw // 2
    for s in range(t):
        at = u_ref[s].T.astype(BF16)
        for g in range(gb):
            zt_scr[g, s * hc:(s + 1) * hc, :] = at[g * hc:(g + 1) * hc, :]
    for g in range(gb):
        wc_scr[g] = jnp.dot(wt_ref[g], zt_scr[g], preferred_element_type=F32).T
    fwd = lax.broadcasted_iota(jnp.int32, (1, sw), 1) < hw

    def step(j, carry):
        rf = pl.ds(pl.multiple_of(j * batch, batch), batch)
        rb = pl.ds(pl.multiple_of((n_chunks - 1 - j) * batch, batch), batch)
        out = []
        for g in range(gb):
            sr, si = carry[g]
            sin_scr[g, rf, 0:hw] = sr[:, 0:hw]
            sin_scr[g, rb, hw:sw] = sr[:, hw:sw]
            sin_scr[g, rf, sw:sw + hw] = si[:, 0:hw]
            sin_scr[g, rb, sw + hw:2 * sw] = si[:, hw:sw]
            wr = jnp.where(fwd, wc_scr[g, rf, 0:sw], wc_scr[g, rb, 0:sw])
            wi = jnp.where(fwd, wc_scr[g, rf, sw:2 * sw], wc_scr[g, rb, sw:2 * sw])
            ar = a_ref[g, 0]
            ai = a_ref[g, 1]
            out.append((ar * sr - ai * si + wr, ar * si + ai * sr + wi))
        return tuple(out)

    fin = lax.fori_loop(0, n_chunks, step, tuple((h0_ref[g, 0], h0_ref[g, 1]) for g in range(gb)))
    for g in range(gb):
        sint = sin_scr[g].T.astype(BF16)
        y2t = (jnp.dot(toept_ref[g], zt_scr[g], preferred_element_type=F32)
               + jnp.dot(vt_ref[g], sint, preferred_element_type=F32))
        for s in range(t):
            yt_scr[s, g * hc:(g + 1) * hc, :] = y2t[s * hc:(s + 1) * hc, :]
        if with_state:
            fin_ref[g, 0] = fin[g][0]
            fin_ref[g, 1] = fin[g][1]
    for s in range(t):
        y_ref[s] = yt_scr[s].T.astype(y_ref.dtype)


def _ssm(u3, ssm_ops, h0, with_state):
    toept, wt, vt, a_t = ssm_ops
    t, rows, d_ssm = u3.shape
    batch = h0.shape[2]
    n_chunks = rows // batch
    g = toept.shape[0]
    gb = SSM_GROUP_BLOCK
    cw = d_ssm // g * gb
    slab = pl.BlockSpec((t, rows, cw), lambda i: (0, 0, i))
    blk = lambda a: pl.BlockSpec((gb,) + a.shape[1:], lambda i: (i,) + (0,) * (a.ndim - 1))
    out_specs = [slab]
    out_shape = [jax.ShapeDtypeStruct(u3.shape, F32)]
    if with_state:
        out_specs.append(blk(h0))
        out_shape.append(jax.ShapeDtypeStruct(h0.shape, F32))
    nk = toept.shape[-1]
    outs = pl.pallas_call(
        functools.partial(_ssm_kernel, batch=batch, n_chunks=n_chunks, with_state=with_state),
        grid=(g // gb,),
        in_specs=[slab, blk(toept), blk(wt), blk(vt), blk(a_t), blk(h0)],
        out_specs=out_specs,
        out_shape=out_shape,
        scratch_shapes=[pltpu.VMEM((gb, nk, rows), BF16),
                        pltpu.VMEM((gb, rows, wt.shape[1]), F32),
                        pltpu.VMEM((gb, rows, wt.shape[1]), F32),
                        pltpu.VMEM((t, cw, rows), F32)],
        compiler_params=_params("arbitrary"),
        name="ssm_state" if with_state else "ssm",
    )(u3, toept, wt, vt, a_t, h0)
    return outs


def _state_to_lanes(h0):
    b, nd, nc, g, p = h0.shape
    return h0.astype(F32).transpose(3, 2, 0, 1, 4).reshape(g, nc, b, nd * p)


def _state_from_lanes(fin, nd):
    g, nc, b, w = fin.shape
    return fin.reshape(g, nc, b, nd, w // nd).transpose(2, 3, 1, 0, 4)


def _outproj_kernel(x_ref, a_ref, y_ref, u_ref, gs_ref, gate_ref, dskip_ref,
                    wglu_ref, bglu_ref, wout_ref, g_ref, o_ref):
    nb, tp, d = x_ref.shape
    rows = lambda ref: ref[...].reshape(nb * tp, ref.shape[-1])
    y = _from_scan_layout(y_ref, nb) + dskip_ref[...] * _from_scan_layout(u_ref, nb)
    c0 = math.sqrt(2.0 / math.pi)
    ys = 0.5 * y * (1.0 + jnp.tanh(c0 * (y + 0.044715 * (y * y * y))))
    z = jnp.dot(ys.astype(BF16), wglu_ref[...], preferred_element_type=F32) + bglu_ref[...]
    ys = ys * jax.nn.sigmoid(z) * rows(gs_ref).astype(F32)
    cat = jnp.concatenate([rows(a_ref), ys.astype(BF16)], axis=-1)
    out = jnp.dot(cat, wout_ref[...], preferred_element_type=F32)
    ms = jnp.mean(out * out, axis=-1, keepdims=True)
    out = out * lax.rsqrt(ms + EPS) * g_ref[...]
    o_ref[...] = x_ref[...] + gate_ref[...] * out.reshape(nb, tp, d)


def _outproj(x, a_out, y_ssm, u, gs, gate, d_skip, w_glu_bf, b_glu, w_out_bf, norm_post):
    b, l, d = x.shape
    d_attn = a_out.shape[-1]
    d_ssm = gs.shape[-1]
    tp = TOKEN_TILE // b
    t = SSM_CHUNK
    const = lambda a: pl.BlockSpec(a.shape, lambda j: (0,) * a.ndim)
    tok = lambda w: pl.BlockSpec((b, tp, w), lambda j: (0, j, 0))
    scan = pl.BlockSpec((t, tp // t * b, d_ssm), lambda j: (0, j, 0))
    consts = [gate, d_skip.reshape(1, d_ssm).astype(F32), w_glu_bf, b_glu.reshape(1, d_ssm).astype(F32),
              w_out_bf, norm_post.reshape(1, d).astype(F32)]
    return pl.pallas_call(
        _outproj_kernel,
        grid=(l // tp,),
        in_specs=[tok(d), tok(d_attn), scan, scan, tok(d_ssm)] + [const(a) for a in consts],
        out_specs=tok(d),
        out_shape=jax.ShapeDtypeStruct((b, l, d), F32),
        compiler_params=_params("arbitrary"),
        name="outproj",
    )(x, a_out, y_ssm, u, gs, *consts)


def _mixer(x, mod, h0_lanes, cache_k, cache_v, rope_tabs, kv_cache_layout, with_state, lam, lam_init,
           norm_pre, w_in_bf, ssm_ops, subln, d_skip, w_glu_bf, b_glu, w_out_bf, norm_post):
    b, l, d = x.shape
    d_ssm = w_glu_bf.shape[0]
    d_attn = (w_in_bf.shape[1] - 2 * d_ssm) // 4
    shift, scale, gate = mod
    q, k, v, ga, u, gs = _inproj(x, shift, scale, norm_pre, w_in_bf, rope_tabs, kv_cache_layout,
                                 d_attn, d_ssm)
    a_out = _attention(lam, q, k, v, cache_k, cache_v, ga, subln, lam_init)
    outs = _ssm(u, ssm_ops, h0_lanes, with_state)
    y = _outproj(x, a_out, outs[0], u, gs, gate, d_skip, w_glu_bf, b_glu, w_out_bf, norm_post)
    fin = outs[1] if with_state else None
    return y, k, v, fin


def kernel(x_prompt, x_sample, cache_k, cache_v, state_ssm, c, c_ctx, w_ada, b_ada, norm_pre, norm_post, w_in, lambda_qk, subln, ssm_A_re, ssm_A_im, ssm_log_dt, ssm_B_re, ssm_B_im, ssm_C_re, ssm_C_im, ssm_D, w_glu, b_glu, w_out):
    xp, xs = x_prompt, x_sample
    bp, lp, d = xp.shape
    bs, ls, _ = xs.shape
    depth = w_in.shape[0]
    nd = ssm_A_re.shape[1]
    g, p = ssm_A_re.shape[-2:]
    d_ssm = w_glu.shape[-1]
    d_attn = (w_in.shape[-1] - 2 * d_ssm) // 4
    rope_tabs = _rope_tables(ls, d_attn)
    cond_rows = 16
    cond = jnp.zeros((cond_rows, d), F32).at[:bs].set(c.astype(F32)).at[bs].set(c_ctx.astype(F32))
    new_k, new_v, new_s = [], [], []
    for layer in range(depth):
        lam_init = 0.8 - 0.6 * math.exp(-0.3 * layer)
        m = _adaln(cond, w_ada[layer], b_ada[layer])
        mod_s = tuple(m[:bs, i * d:(i + 1) * d].reshape(bs, 1, d) for i in range(3))
        mod_p = tuple(m[bs:bs + 1, i * d:(i + 1) * d].reshape(1, 1, d) for i in range(3))
        ssm_ops, lam = _ssmprep(
            ssm_A_re[layer], ssm_A_im[layer], ssm_log_dt[layer], ssm_B_re[layer], ssm_B_im[layer],
            ssm_C_re[layer], ssm_C_im[layer], lambda_qk[layer], lam_init)
        shared = (lam, lam_init, norm_pre[layer], w_in[layer].astype(BF16), ssm_ops, subln[layer],
                  ssm_D[layer], w_glu[layer].astype(BF16), b_glu[layer], w_out[layer].astype(BF16),
                  norm_post[layer])

        h0_p = jnp.zeros((g, 2, bp, nd * p), F32)
        xp, k_p, v_p, fin = _mixer(xp, mod_p, h0_p, None, None, None, True, True, *shared)
        new_k.append(k_p)
        new_v.append(v_p)
        new_s.append(_state_from_lanes(fin, nd).astype(xp.dtype))

        h0_s = _state_to_lanes(state_ssm[:, layer])
        xs, _, _, _ = _mixer(xs, mod_s, h0_s, cache_k[:, layer], cache_v[:, layer], rope_tabs, False, False,
                             *shared)

    return (xp, xs, jnp.stack(new_k, axis=1), jnp.stack(new_v, axis=1), jnp.stack(new_s, axis=1))
```

```python
import functools
import math

import jax
import jax.numpy as jnp
from jax import lax
from jax.experimental import pallas as pl
from jax.experimental.pallas import tpu as pltpu

F32 = jnp.float32
BF16 = jnp.bfloat16

GRID_W = 64
QK_DIM = 64
V_DIM = 2 * QK_DIM
ROPE_THETA = 10000.0
EPS = 1e-6
Q_PRESCALE = math.log2(math.e) * QK_DIM ** -0.5

SUM_ROWS = 16
TOKEN_TILE = 256
ATTN_Q_TILE = 1024
SSM_CHUNK = 16
SCAN_SLAB = 128
VMEM_LIMIT = 48 * 1024 * 1024


def _silu(x):
    return x * jax.nn.sigmoid(x)


def _params(*sem):
    return pltpu.CompilerParams(dimension_semantics=sem, vmem_limit_bytes=VMEM_LIMIT)


def _adaln_kernel(c_ref, w_ref, b_ref, o_ref):
    s = _silu(c_ref[...]).astype(BF16)
    o_ref[...] = jnp.dot(s, w_ref[...].astype(BF16), preferred_element_type=F32) + b_ref[...]


def _adaln(cond, w_ada, b_ada):
    rows, d = cond.shape
    n = w_ada.shape[1]
    bn = 512
    return pl.pallas_call(
        _adaln_kernel,
        grid=(n // bn,),
        in_specs=[pl.BlockSpec((rows, d), lambda j: (0, 0)),
                  pl.BlockSpec((d, bn), lambda j: (0, j)),
                  pl.BlockSpec((1, bn), lambda j: (0, j))],
        out_specs=pl.BlockSpec((rows, bn), lambda j: (0, j)),
        out_shape=jax.ShapeDtypeStruct((rows, n), F32),
        compiler_params=_params("arbitrary"),
        name="adaln",
    )(cond, w_ada, b_ada.reshape(1, n))


def _ssmprep_kernel(are_ref, aim_ref, ldt_ref, bre_ref, bim_ref, cre_ref, cim_ref, lq_ref,
                    toept_ref, wt_ref, vt_ref, a_ref, lam_ref,
                    cplre_scr, cplimn_scr, w_scr, v_scr, toep_scr, *, lam_init, chunk):
    a_re = are_ref[...]
    a_im = aim_ref[...]
    dt = jnp.exp(ldt_ref[...])
    mag = jnp.exp(a_re * dt)
    ab_re = mag * jnp.cos(a_im * dt)
    ab_im = mag * jnp.sin(a_im * dt)
    nr, ni = ab_re - 1.0, ab_im
    den = a_re * a_re + a_im * a_im
    f_re = (nr * a_re + ni * a_im) / den
    f_im = (ni * a_re - nr * a_im) / den
    b_re = bre_ref[...]
    b_im = bim_ref[...]
    bb_re = f_re * b_re - f_im * b_im
    bb_im = f_re * b_im + f_im * b_re
    c_re = cre_ref[...]
    c_im = cim_ref[...]
    hc = b_re.shape[1]
    sw = a_re.shape[-1]
    hw = sw // 2
    lo, hi = slice(0, hw), slice(hw, sw)
    lo_im, hi_im = slice(sw, sw + hw), slice(sw + hw, 2 * sw)
    blk = lambda j: slice(j * hc, (j + 1) * hc)
    last = chunk - 1
    cplre_scr[...] = jnp.zeros_like(cplre_scr)
    cplimn_scr[...] = jnp.zeros_like(cplimn_scr)
    pw_re = jnp.ones_like(ab_re)
    pw_im = jnp.zeros_like(ab_im)
    for tau in range(chunk + 1):
        cp_re = c_re * pw_re - c_im * pw_im
        cp_imn = -(c_re * pw_im + c_im * pw_re)
        if tau < chunk:
            cplre_scr[:, blk(last + tau), lo] = cp_re[:, :, lo]
            cplre_scr[:, blk(last - tau), hi] = cp_re[:, :, hi]
            cplimn_scr[:, blk(last + tau), lo] = cp_imn[:, :, lo]
            cplimn_scr[:, blk(last - tau), hi] = cp_imn[:, :, hi]
            e_re = pw_re * bb_re - pw_im * bb_im
            e_im = pw_re * bb_im + pw_im * bb_re
            w_scr[:, blk(last - tau), lo] = e_re[:, :, lo]
            w_scr[:, blk(tau), hi] = e_re[:, :, hi]
            w_scr[:, blk(last - tau), lo_im] = e_im[:, :, lo]
            w_scr[:, blk(tau), hi_im] = e_im[:, :, hi]
        if tau >= 1:
            v_scr[:, blk(tau - 1), lo] = cp_re[:, :, lo]
            v_scr[:, blk(chunk - tau), hi] = cp_re[:, :, hi]
            v_scr[:, blk(tau - 1), lo_im] = cp_imn[:, :, lo]
            v_scr[:, blk(chunk - tau), hi_im] = cp_imn[:, :, hi]
        if tau < chunk:
            pw_re, pw_im = pw_re * ab_re - pw_im * ab_im, pw_re * ab_im + pw_im * ab_re
    a_ref[:, 0] = pw_re
    a_ref[:, 1] = pw_im
    dims = (((2,), (2,)), ((0,), (0,)))
    taps = (lax.dot_general(bb_re, cplre_scr[...], dims, precision=lax.Precision.HIGHEST,
                            preferred_element_type=F32)
            + lax.dot_general(bb_im, cplimn_scr[...], dims, precision=lax.Precision.HIGHEST,
                              preferred_element_type=F32))
    nk = chunk * hc
    for s in range(chunk):
        toep_scr[:, blk(s), :] = taps[:, :, (last - s) * hc:(last - s) * hc + nk]
    for g in range(toept_ref.shape[0]):
        toept_ref[g] = toep_scr[g].T.astype(toept_ref.dtype)
        wt_ref[g] = w_scr[g].T.astype(wt_ref.dtype)
    vt_ref[...] = v_scr[...].astype(vt_ref.dtype)
    lq = lq_ref[...]
    s01 = jnp.sum(lq[0:1] * lq[1:2], axis=-1, keepdims=True)
    s23 = jnp.sum(lq[2:3] * lq[3:4], axis=-1, keepdims=True)
    lam = jnp.exp(s01) - jnp.exp(s23) + lam_init
    lam_ref[...] = jnp.broadcast_to(lam, lam_ref.shape)


def _ssmprep(a_re, a_im, log_dt, b_re, b_im, c_re, c_im, lq, lam_init):
    nd, g, p = a_re.shape
    hc = b_re.shape[-1]
    t = SSM_CHUNK
    sw = nd * p
    gb = SCAN_SLAB // hc
    row = lambda a: a.astype(F32).transpose(1, 0, 2).reshape(g, 1, sw)
    ldt = jnp.broadcast_to(log_dt[..., None], (nd, g, p))
    bt = lambda a: a.astype(F32).transpose(1, 3, 0, 2).reshape(g, hc, sw)
    ct = lambda a: a.astype(F32).transpose(1, 2, 0, 3).reshape(g, hc, sw)
    sds = jax.ShapeDtypeStruct
    nk = t * hc
    blk = lambda shape: pl.BlockSpec((gb,) + shape[1:], lambda i: (i,) + (0,) * (len(shape) - 1))
    mat = (g, nk, 2 * sw)
    toept, wt, vt, a_t, lam = pl.pallas_call(
        functools.partial(_ssmprep_kernel, lam_init=lam_init, chunk=t),
        grid=(g // gb,),
        in_specs=[blk((g, 1, sw))] * 3 + [blk((g, hc, sw))] * 4 + [pl.BlockSpec(lq.shape, lambda i: (0, 0))],
        out_specs=[blk((g, nk, nk)), blk((g, 2 * sw, nk)), blk(mat), blk((g, 2, 1, sw)),
                   pl.BlockSpec((8, 128), lambda i: (0, 0))],
        out_shape=(sds((g, nk, nk), BF16), sds((g, 2 * sw, nk), BF16), sds(mat, BF16),
                   sds((g, 2, 1, sw), F32), sds((8, 128), F32)),
        scratch_shapes=[pltpu.VMEM((gb, 2 * nk, sw), F32), pltpu.VMEM((gb, 2 * nk, sw), F32),
                        pltpu.VMEM((gb, nk, 2 * sw), F32), pltpu.VMEM((gb, nk, 2 * sw), F32),
                        pltpu.VMEM((gb, nk, nk), F32)],
        compiler_params=_params("arbitrary"),
        name="ssmprep",
    )(row(a_re), row(a_im), row(ldt), bt(b_re), bt(b_im), ct(c_re), ct(c_im), lq.astype(F32))
    return (toept, wt, vt, a_t), lam


def _to_scan_layout(u, o_ref, batch):
    n_slabs, t, _, cw = o_ref.shape
    tp = u.shape[0] // batch
    u_tb = jnp.swapaxes(u.reshape(batch, tp, u.shape[-1]), 0, 1)
    for c in range(tp // t):
        for s in range(t):
            for v in range(n_slabs):
                o_ref[v, s, c * batch:(c + 1) * batch, :] = u_tb[c * t + s][:, v * cw:(v + 1) * cw]


def _scan_block(ref):
    return jnp.concatenate([ref[v] for v in range(ref.shape[0])], axis=-1)


def _from_scan_layout(ref, batch):
    t, rows, d = ref.shape
    tp = rows // batch * t
    u_tb = jnp.stack([ref[s, c * batch:(c + 1) * batch, :] for c in range(tp // t) for s in range(t)], axis=0)
    return jnp.swapaxes(u_tb, 0, 1).reshape(batch * tp, d)


def _inproj_kernel(*refs, rope, d_attn):
    if rope:
        (x_ref, shift_ref, scale_ref, g_ref, w_ref, cos_ref, sin_ref,
         q_ref, k_ref, v_ref, ga_ref, u_ref, gs_ref) = refs
    else:
        (x_ref, shift_ref, scale_ref, g_ref, w_ref,
         q_ref, k_ref, v_ref, ga_ref, u_ref, gs_ref) = refs
    x = x_ref[...]
    nb, tp, d = x.shape
    ms = jnp.mean(x * x, axis=-1, keepdims=True)
    h = x * lax.rsqrt(ms + EPS) * (g_ref[...] * (1.0 + scale_ref[...])) + shift_ref[...]
    proj = jnp.dot(h.reshape(nb * tp, d).astype(BF16), w_ref[...], preferred_element_type=F32)
    da = d_attn
    q = proj[:, 0:da]
    k = proj[:, da:2 * da]
    tok = lambda z: z.reshape(nb, tp, z.shape[-1])
    if rope:
        lane = lax.broadcasted_iota(jnp.int32, (1, da), 1)
        low = (lane % (QK_DIM // 2)) < (QK_DIM // 4)
        cos = cos_ref[...]
        sin = sin_ref[...]

        def rot(z):
            partner = jnp.where(low, pltpu.roll(z, da - QK_DIM // 4, 1), pltpu.roll(z, QK_DIM // 4, 1))
            return tok(z) * cos + tok(partner) * sin

        q = rot(q)
        k = rot(k)
    else:
        q = tok(q)
        k = tok(k)
    q_ref[...] = (q * Q_PRESCALE).astype(q_ref.dtype)
    v = tok(proj[:, 2 * da:3 * da])
    if len(k_ref.shape) == 4:
        for hd in range(k_ref.shape[2]):
            k_ref[:, :, hd, :] = k[:, :, hd * V_DIM:(hd + 1) * V_DIM].astype(k_ref.dtype)
            v_ref[:, :, hd, :] = v[:, :, hd * V_DIM:(hd + 1) * V_DIM].astype(v_ref.dtype)
    else:
        k_ref[...] = k.astype(k_ref.dtype)
        v_ref[...] = v.astype(v_ref.dtype)
    ga_ref[...] = tok(_silu(proj[:, 3 * da:4 * da])).astype(ga_ref.dtype)
    d_ssm = u_ref.shape[0] * u_ref.shape[-1]
    _to_scan_layout(proj[:, 4 * da:4 * da + d_ssm], u_ref, nb)
    gs_ref[...] = tok(_silu(proj[:, 4 * da + d_ssm:])).astype(gs_ref.dtype)


def _inproj(x, shift, scale, norm_pre, w_in_bf, rope_tabs, kv_cache_layout, d_attn, d_ssm):
    b, l, d = x.shape
    tp = TOKEN_TILE // b
    t = SSM_CHUNK
    rope = rope_tabs is not None
    const = lambda a: pl.BlockSpec(a.shape, lambda j: (0,) * a.ndim)
    tok = lambda w: pl.BlockSpec((b, tp, w), lambda j: (0, j, 0))
    g2 = norm_pre.reshape(1, d).astype(F32)
    in_specs = [tok(d), const(shift), const(scale), const(g2), const(w_in_bf)]
    args = [x, shift, scale, g2, w_in_bf]
    if rope:
        in_specs += [pl.BlockSpec((tp, d_attn), lambda j: (j, 0))] * 2
        args += list(rope_tabs)
    scan_rows = tp // t * b
    if kv_cache_layout:
        kv_spec = pl.BlockSpec((b, tp, d_attn // V_DIM, V_DIM), lambda j: (0, j, 0, 0))
        kv_shape = jax.ShapeDtypeStruct((b, l, d_attn // V_DIM, V_DIM), F32)
    else:
        kv_spec = tok(d_attn)
        kv_shape = jax.ShapeDtypeStruct((b, l, d_attn), BF16)
    out_specs = [tok(d_attn), kv_spec, kv_spec, tok(d_attn),
                 pl.BlockSpec((d_ssm // SCAN_SLAB, t, scan_rows, SCAN_SLAB), lambda j: (0, 0, j, 0)),
                 tok(d_ssm)]
    out_shape = [jax.ShapeDtypeStruct((b, l, d_attn), BF16),
                 kv_shape,
                 kv_shape,
                 jax.ShapeDtypeStruct((b, l, d_attn), BF16),
                 jax.ShapeDtypeStruct((d_ssm // SCAN_SLAB, t, l // t * b, SCAN_SLAB), F32),
                 jax.ShapeDtypeStruct((b, l, d_ssm), BF16)]
    return pl.pallas_call(
        functools.partial(_inproj_kernel, rope=rope, d_attn=d_attn),
        grid=(l // tp,),
        in_specs=in_specs,
        out_specs=out_specs,
        out_shape=out_shape,
        compiler_params=_params("arbitrary"),
        name="inproj_rope" if rope else "inproj",
    )(*args)


def _rope_tables(l, d_attn):
    rows = l // GRID_W
    row = jnp.repeat(jnp.arange(rows, dtype=F32), GRID_W)
    col = jnp.tile(jnp.arange(GRID_W, dtype=F32), rows)
    n_freq = QK_DIM // 4
    inv = ROPE_THETA ** (-jnp.arange(n_freq, dtype=F32) / n_freq)
    ang_r = row[:, None] * inv
    ang_c = col[:, None] * inv
    cos64 = jnp.concatenate([jnp.cos(ang_r), jnp.cos(ang_r), jnp.cos(ang_c), jnp.cos(ang_c)], axis=-1)
    sin64 = jnp.concatenate([-jnp.sin(ang_r), jnp.sin(ang_r), -jnp.sin(ang_c), jnp.sin(ang_c)], axis=-1)
    reps = d_attn // QK_DIM
    return jnp.tile(cos64, (1, reps)), jnp.tile(sin64, (1, reps))


def _head(ref, h):
    return ref[:, h, :] if len(ref.shape) == 3 else ref[:, h * V_DIM:(h + 1) * V_DIM]


def _attn_kernel(*refs, has_cache, lam_init):
    if has_cache:
        lam_ref, q_ref, k_ref, v_ref, ck_ref, cv_ref, ga_ref, sub_ref, o_ref, k_scr, vt_scr = refs
    else:
        lam_ref, q_ref, k_ref, v_ref, ga_ref, sub_ref, o_ref, k_scr, vt_scr = refs
    n_heads, lk = k_scr.shape[0], k_scr.shape[1]
    ln = k_ref.shape[0]
    lam = lam_ref[0:1, 0:1]
    lane = lax.broadcasted_iota(jnp.int32, (1, V_DIM), 1)
    first = lane < QK_DIM
    dims = (((1,), (1,)), ((), ()))

    @pl.when(pl.program_id(1) == 0)
    def _():
        for h in range(n_heads):
            if has_cache:
                k_scr[h, 0:lk - ln, :] = _head(ck_ref, h).astype(BF16)
                vt_scr[h, 0:V_DIM, 0:lk - ln] = _head(cv_ref, h).astype(F32).T.astype(BF16)
            k_scr[h, lk - ln:lk, :] = _head(k_ref, h).astype(BF16)
            vt_scr[h, 0:V_DIM, lk - ln:lk] = _head(v_ref, h).astype(F32).T.astype(BF16)
            vt_scr[h, V_DIM:, :] = jnp.ones((vt_scr.shape[1] - V_DIM, lk), BF16)

    for h in range(n_heads):
        q = _head(q_ref, h)
        zero = jnp.zeros_like(q)
        k = k_scr[h]
        vt = vt_scr[h]
        st1 = lax.dot_general(k, jnp.where(first, q, zero), dims, preferred_element_type=F32)
        st2 = lax.dot_general(k, jnp.where(first, zero, q), dims, preferred_element_type=F32)
        et1 = jnp.exp2(st1 - jnp.max(st1, axis=0, keepdims=True)).astype(BF16)
        et2 = jnp.exp2(st2 - jnp.max(st2, axis=0, keepdims=True)).astype(BF16)
        o1 = jnp.dot(vt, et1, preferred_element_type=F32)
        o2 = jnp.dot(vt, et2, preferred_element_type=F32)
        r1 = 1.0 / o1[V_DIM:V_DIM + 1, :]
        r2 = lam / o2[V_DIM:V_DIM + 1, :]
        o = (o1[0:V_DIM, :] * r1 - o2[0:V_DIM, :] * r2).T
        ms = jnp.mean(o * o, axis=-1, keepdims=True)
        o = o * lax.rsqrt(ms + EPS) * sub_ref[...] * (1.0 - lam_init)
        o_ref[:, h * V_DIM:(h + 1) * V_DIM] = (o * _head(ga_ref, h).astype(F32)).astype(o_ref.dtype)


def _attention(lam, q, k, v, cache_k, cache_v, ga, subln, lam_init):
    b, l, d_attn = q.shape
    n_heads = d_attn // V_DIM
    tq = min(ATTN_Q_TILE, l)
    nq = l // tq
    has_cache = cache_k is not None
    q_spec = pl.BlockSpec((None, tq, d_attn), lambda i, j: (i, j, 0))
    whole = lambda a: pl.BlockSpec((None,) + a.shape[1:], lambda i, j: (i,) + (0,) * (a.ndim - 1))
    in_specs = [pl.BlockSpec((8, 128), lambda i, j: (0, 0)), q_spec, whole(k), whole(v)]
    args = [lam, q, k, v]
    if has_cache:
        in_specs += [whole(cache_k), whole(cache_v)]
        args += [cache_k, cache_v]
    in_specs += [q_spec, pl.BlockSpec((1, V_DIM), lambda i, j: (0, 0))]
    args += [ga, subln.reshape(1, V_DIM).astype(F32)]
    lk = l + (cache_k.shape[1] if has_cache else 0)
    out = pl.pallas_call(
        functools.partial(_attn_kernel, has_cache=has_cache, lam_init=lam_init),
        grid=(b, nq),
        in_specs=in_specs,
        out_specs=q_spec,
        out_shape=jax.ShapeDtypeStruct((b, l, d_attn), BF16),
        scratch_shapes=[pltpu.VMEM((n_heads, lk, V_DIM), BF16),
                        pltpu.VMEM((n_heads, V_DIM + SUM_ROWS, lk), BF16)],
        compiler_params=_params("arbitrary", "arbitrary"),
        name="attn_cache" if has_cache else "attn",
    )(*args)
    return out


def _ssm_kernel(*refs, batch, n_chunks, with_state):
    if with_state:
        u_ref, toept_ref, wt_ref, vt_ref, a_ref, h0_ref, y_ref, fin_ref, zt_scr, wc_scr, sin_scr, yt_scr = refs
    else:
        u_ref, toept_ref, wt_ref, vt_ref, a_ref, h0_ref, y_ref, zt_scr, wc_scr, sin_scr, yt_scr = refs
    t = u_ref.shape[0]
    gb = zt_scr.shape[0]
    hc = zt_scr.shape[1] // t
    sw = a_ref.shape[-1]
    hw = sw // 2
    for s in range(t):
        at = u_ref[s].T.astype(BF16)
        for g in range(gb):
            zt_scr[g, s * hc:(s + 1) * hc, :] = at[g * hc:(g + 1) * hc, :]
    for g in range(gb):
        wc_scr[g] = jnp.dot(wt_ref[g], zt_scr[g], preferred_element_type=F32).T
    fwd = lax.broadcasted_iota(jnp.int32, (1, sw), 1) < hw

    def step(j, carry):
        rf = pl.ds(pl.multiple_of(j * batch, batch), batch)
        rb = pl.ds(pl.multiple_of((n_chunks - 1 - j) * batch, batch), batch)
        out = []
        for g in range(gb):
            sr, si = carry[g]
            sin_scr[g, rf, 0:hw] = sr[:, 0:hw]
            sin_scr[g, rb, hw:sw] = sr[:, hw:sw]
            sin_scr[g, rf, sw:sw + hw] = si[:, 0:hw]
            sin_scr[g, rb, sw + hw:2 * sw] = si[:, hw:sw]
            wr = jnp.where(fwd, wc_scr[g, rf, 0:sw], wc_scr[g, rb, 0:sw])
            wi = jnp.where(fwd, wc_scr[g, rf, sw:2 * sw], wc_scr[g, rb, sw:2 * sw])
            ar = a_ref[g, 0]
            ai = a_ref[g, 1]
            out.append((ar * sr - ai * si + wr, ar * si + ai * sr + wi))
        return tuple(out)

    fin = lax.fori_loop(0, n_chunks, step, tuple((h0_ref[g, 0], h0_ref[g, 1]) for g in range(gb)))
    for g in range(gb):
        sint = sin_scr[g].T.astype(BF16)
        y2t = (jnp.dot(toept_ref[g], zt_scr[g], preferred_element_type=F32)
               + jnp.dot(vt_ref[g], sint, preferred_element_type=F32))
        for s in range(t):
            yt_scr[s, g * hc:(g + 1) * hc, :] = y2t[s * hc:(s + 1) * hc, :]
        if with_state:
            fin_ref[g, 0] = fin[g][0]
            fin_ref[g, 1] = fin[g][1]
    for s in range(t):
        y_ref[s] = yt_scr[s].T.astype(y_ref.dtype)


def _ssm(u3, ssm_ops, h0, with_state):
    toept, wt, vt, a_t = ssm_ops
    n_slabs, t, rows, cw = u3.shape
    batch = h0.shape[2]
    n_chunks = rows // batch
    g = toept.shape[0]
    gb = g // n_slabs
    slab = pl.BlockSpec((None, t, rows, cw), lambda i: (i, 0, 0, 0))
    blk = lambda a: pl.BlockSpec((gb,) + a.shape[1:], lambda i: (i,) + (0,) * (a.ndim - 1))
    out_specs = [slab]
    out_shape = [jax.ShapeDtypeStruct(u3.shape, F32)]
    if with_state:
        out_specs.append(blk(h0))
        out_shape.append(jax.ShapeDtypeStruct(h0.shape, F32))
    nk = toept.shape[-1]
    outs = pl.pallas_call(
        functools.partial(_ssm_kernel, batch=batch, n_chunks=n_chunks, with_state=with_state),
        grid=(g // gb,),
        in_specs=[slab, blk(toept), blk(wt), blk(vt), blk(a_t), blk(h0)],
        out_specs=out_specs,
        out_shape=out_shape,
        scratch_shapes=[pltpu.VMEM((gb, nk, rows), BF16),
                        pltpu.VMEM((gb, rows, wt.shape[1]), F32),
                        pltpu.VMEM((gb, rows, wt.shape[1]), F32),
                        pltpu.VMEM((t, cw, rows), F32)],
        compiler_params=_params("arbitrary"),
        name="ssm_state" if with_state else "ssm",
    )(u3, toept, wt, vt, a_t, h0)
    return outs


def _state_to_lanes(h0):
    b, nd, nc, g, p = h0.shape
    return h0.astype(F32).transpose(3, 2, 0, 1, 4).reshape(g, nc, b, nd * p)


def _state_from_lanes(fin, nd):
    g, nc, b, w = fin.shape
    return fin.reshape(g, nc, b, nd, w // nd).transpose(2, 3, 1, 0, 4)


def _outproj_kernel(x_ref, a_ref, y_ref, u_ref, gs_ref, gate_ref, dskip_ref,
                    wglu_ref, bglu_ref, wout_ref, g_ref, o_ref):
    nb, tp, d = x_ref.shape
    y = _scan_block(y_ref) + dskip_ref[...] * _scan_block(u_ref)
    t, srows, d_ssm = y.shape
    rows = lambda ref: ref[...].reshape(nb * tp, ref.shape[-1])
    y = y.reshape(t * srows, d_ssm)
    c0 = math.sqrt(2.0 / math.pi)
    hy = 0.5 * y
    ys = hy + hy * jnp.tanh(y * (c0 + (c0 * 0.044715) * (y * y)))
    z = jnp.dot(ys.astype(BF16), wglu_ref[...], preferred_element_type=F32) + bglu_ref[...]
    ys = _from_scan_layout((ys * jax.nn.sigmoid(z)).reshape(t, srows, d_ssm), nb) * rows(gs_ref).astype(F32)
    cat = jnp.concatenate([rows(a_ref), ys.astype(BF16)], axis=-1)
    out = jnp.dot(cat, wout_ref[...], preferred_element_type=F32)
    ms = jnp.mean(out * out, axis=-1, keepdims=True)
    out = out * lax.rsqrt(ms + EPS)
    o_ref[...] = x_ref[...] + (g_ref[...] * gate_ref[...]) * out.reshape(nb, tp, d)


def _outproj(x, a_out, y_ssm, u, gs, gate, d_skip, w_glu_bf, b_glu, w_out_bf, norm_post):
    b, l, d = x.shape
    d_attn = a_out.shape[-1]
    d_ssm = gs.shape[-1]
    tp = TOKEN_TILE // b
    t = SSM_CHUNK
    const = lambda a: pl.BlockSpec(a.shape, lambda j: (0,) * a.ndim)
    tok = lambda w: pl.BlockSpec((b, tp, w), lambda j: (0, j, 0))
    scan = pl.BlockSpec((d_ssm // SCAN_SLAB, t, tp // t * b, SCAN_SLAB), lambda j: (0, 0, j, 0))
    consts = [gate, d_skip.reshape(1, d_ssm).astype(F32), w_glu_bf, b_glu.reshape(1, d_ssm).astype(F32),
              w_out_bf, norm_post.reshape(1, d).astype(F32)]
    return pl.pallas_call(
        _outproj_kernel,
        grid=(l // tp,),
        in_specs=[tok(d), tok(d_attn), scan, scan, tok(d_ssm)] + [const(a) for a in consts],
        out_specs=tok(d),
        out_shape=jax.ShapeDtypeStruct((b, l, d), F32),
        compiler_params=_params("arbitrary"),
        name="outproj",
    )(x, a_out, y_ssm, u, gs, *consts)


def _mixer(x, mod, h0_lanes, cache_k, cache_v, rope_tabs, kv_cache_layout, with_state, lam, lam_init,
           norm_pre, w_in_bf, ssm_ops, subln, d_skip, w_glu_bf, b_glu, w_out_bf, norm_post):
    b, l, d = x.shape
    d_ssm = w_glu_bf.shape[0]
    d_attn = (w_in_bf.shape[1] - 2 * d_ssm) // 4
    shift, scale, gate = mod
    q, k, v, ga, u, gs = _inproj(x, shift, scale, norm_pre, w_in_bf, rope_tabs, kv_cache_layout,
                                 d_attn, d_ssm)
    a_out = _attention(lam, q, k, v, cache_k, cache_v, ga, subln, lam_init)
    outs = _ssm(u, ssm_ops, h0_lanes, with_state)
    y = _outproj(x, a_out, outs[0], u, gs, gate, d_skip, w_glu_bf, b_glu, w_out_bf, norm_post)
    fin = outs[1] if with_state else None
    return y, k, v, fin


def kernel(x_prompt, x_sample, cache_k, cache_v, state_ssm, c, c_ctx, w_ada, b_ada, norm_pre, norm_post, w_in, lambda_qk, subln, ssm_A_re, ssm_A_im, ssm_log_dt, ssm_B_re, ssm_B_im, ssm_C_re, ssm_C_im, ssm_D, w_glu, b_glu, w_out):
    xp, xs = x_prompt, x_sample
    bp, lp, d = xp.shape
    bs, ls, _ = xs.shape
    depth = w_in.shape[0]
    nd = ssm_A_re.shape[1]
    g, p = ssm_A_re.shape[-2:]
    d_ssm = w_glu.shape[-1]
    d_attn = (w_in.shape[-1] - 2 * d_ssm) // 4
    rope_tabs = _rope_tables(ls, d_attn)
    cond_rows = 16
    cond = jnp.zeros((cond_rows, d), F32).at[:bs].set(c.astype(F32)).at[bs].set(c_ctx.astype(F32))
    new_k, new_v, new_s = [], [], []
    for layer in range(depth):
        lam_init = 0.8 - 0.6 * math.exp(-0.3 * layer)
        m = _adaln(cond, w_ada[layer], b_ada[layer])
        mod_s = tuple(m[:bs, i * d:(i + 1) * d].reshape(bs, 1, d) for i in range(3))
        mod_p = tuple(m[bs:bs + 1, i * d:(i + 1) * d].reshape(1, 1, d) for i in range(3))
        ssm_ops, lam = _ssmprep(
            ssm_A_re[layer], ssm_A_im[layer], ssm_log_dt[layer], ssm_B_re[layer], ssm_B_im[layer],
            ssm_C_re[layer], ssm_C_im[layer], lambda_qk[layer], lam_init)
        shared = (lam, lam_init, norm_pre[layer], w_in[layer].astype(BF16), ssm_ops, subln[layer],
                  ssm_D[layer], w_glu[layer].astype(BF16), b_glu[layer], w_out[layer].astype(BF16),
                  norm_post[layer])

        h0_p = jnp.zeros((g, 2, bp, nd * p), F32)
        xp, k_p, v_p, fin = _mixer(xp, mod_p, h0_p, None, None, None, True, True, *shared)
        new_k.append(k_p)
        new_v.append(v_p)
        new_s.append(_state_from_lanes(fin, nd).astype(xp.dtype))

        h0_s = _state_to_lanes(state_ssm[:, layer])
        xs, _, _, _ = _mixer(xs, mod_s, h0_s, cache_k[:, layer], cache_v[:, layer], rope_tabs, False, False,
                             *shared)

    return (xp, xs, jnp.stack(new_k, axis=1), jnp.stack(new_v, axis=1), jnp.stack(new_s, axis=1))
```

```python
import functools
import math

import jax
import jax.numpy as jnp
from jax import lax
from jax.experimental import pallas as pl
from jax.experimental.pallas import tpu as pltpu

F32 = jnp.float32
BF16 = jnp.bfloat16

GRID_W = 64
QK_DIM = 64
V_DIM = 2 * QK_DIM
ROPE_THETA = 10000.0
EPS = 1e-6
Q_PRESCALE = math.log2(math.e) * QK_DIM ** -0.5

SUM_ROWS = 16
TOKEN_TILE = 512
ATTN_Q_TILE = 1024
SSM_CHUNK = 16
SCAN_SLAB = 128
VMEM_LIMIT = 48 * 1024 * 1024


def _silu(x):
    return x * jax.nn.sigmoid(x)


def _params(*sem):
    return pltpu.CompilerParams(dimension_semantics=sem, vmem_limit_bytes=VMEM_LIMIT)


def _adaln_kernel(c_ref, w_ref, b_ref, o_ref):
    s = _silu(c_ref[...]).astype(BF16)
    o_ref[...] = jnp.dot(s, w_ref[...].astype(BF16), preferred_element_type=F32) + b_ref[...]


def _adaln(cond, w_ada, b_ada):
    rows, d = cond.shape
    n = w_ada.shape[1]
    bn = 512
    return pl.pallas_call(
        _adaln_kernel,
        grid=(n // bn,),
        in_specs=[pl.BlockSpec((rows, d), lambda j: (0, 0)),
                  pl.BlockSpec((d, bn), lambda j: (0, j)),
                  pl.BlockSpec((1, bn), lambda j: (0, j))],
        out_specs=pl.BlockSpec((rows, bn), lambda j: (0, j)),
        out_shape=jax.ShapeDtypeStruct((rows, n), F32),
        compiler_params=_params("arbitrary"),
        name="adaln",
    )(cond, w_ada, b_ada.reshape(1, n))


def _ssmprep_kernel(are_ref, aim_ref, ldt_ref, bre_ref, bim_ref, cre_ref, cim_ref, lq_ref,
                    toept_ref, wt_ref, vt_ref, a_ref, lam_ref,
                    cplre_scr, cplimn_scr, w_scr, v_scr, toep_scr, *, lam_init, chunk):
    a_re = are_ref[...]
    a_im = aim_ref[...]
    dt = jnp.exp(ldt_ref[...])
    mag = jnp.exp(a_re * dt)
    ab_re = mag * jnp.cos(a_im * dt)
    ab_im = mag * jnp.sin(a_im * dt)
    nr, ni = ab_re - 1.0, ab_im
    den = a_re * a_re + a_im * a_im
    f_re = (nr * a_re + ni * a_im) / den
    f_im = (ni * a_re - nr * a_im) / den
    b_re = bre_ref[...]
    b_im = bim_ref[...]
    bb_re = f_re * b_re - f_im * b_im
    bb_im = f_re * b_im + f_im * b_re
    c_re = cre_ref[...]
    c_im = cim_ref[...]
    hc = b_re.shape[1]
    sw = a_re.shape[-1]
    hw = sw // 2
    lo, hi = slice(0, hw), slice(hw, sw)
    lo_im, hi_im = slice(sw, sw + hw), slice(sw + hw, 2 * sw)
    blk = lambda j: slice(j * hc, (j + 1) * hc)
    last = chunk - 1
    cplre_scr[...] = jnp.zeros_like(cplre_scr)
    cplimn_scr[...] = jnp.zeros_like(cplimn_scr)
    pw_re = jnp.ones_like(ab_re)
    pw_im = jnp.zeros_like(ab_im)
    for tau in range(chunk + 1):
        cp_re = c_re * pw_re - c_im * pw_im
        cp_imn = -(c_re * pw_im + c_im * pw_re)
        if tau < chunk:
            cplre_scr[:, blk(last + tau), lo] = cp_re[:, :, lo]
            cplre_scr[:, blk(last - tau), hi] = cp_re[:, :, hi]
            cplimn_scr[:, blk(last + tau), lo] = cp_imn[:, :, lo]
            cplimn_scr[:, blk(last - tau), hi] = cp_imn[:, :, hi]
            e_re = pw_re * bb_re - pw_im * bb_im
            e_im = pw_re * bb_im + pw_im * bb_re
            w_scr[:, blk(last - tau), lo] = e_re[:, :, lo]
            w_scr[:, blk(tau), hi] = e_re[:, :, hi]
            w_scr[:, blk(last - tau), lo_im] = e_im[:, :, lo]
            w_scr[:, blk(tau), hi_im] = e_im[:, :, hi]
        if tau >= 1:
            v_scr[:, blk(tau - 1), lo] = cp_re[:, :, lo]
            v_scr[:, blk(chunk - tau), hi] = cp_re[:, :, hi]
            v_scr[:, blk(tau - 1), lo_im] = cp_imn[:, :, lo]
            v_scr[:, blk(chunk - tau), hi_im] = cp_imn[:, :, hi]
        if tau < chunk:
            pw_re, pw_im = pw_re * ab_re - pw_im * ab_im, pw_re * ab_im + pw_im * ab_re
    a_ref[:, 0] = pw_re
    a_ref[:, 1] = pw_im
    dims = (((2,), (2,)), ((0,), (0,)))
    taps = (lax.dot_general(bb_re, cplre_scr[...], dims, precision=lax.Precision.HIGHEST,
                            preferred_element_type=F32)
            + lax.dot_general(bb_im, cplimn_scr[...], dims, precision=lax.Precision.HIGHEST,
                              preferred_element_type=F32))
    nk = chunk * hc
    for s in range(chunk):
        toep_scr[:, blk(s), :] = taps[:, :, (last - s) * hc:(last - s) * hc + nk]
    for g in range(toept_ref.shape[0]):
        toept_ref[g] = toep_scr[g].T.astype(toept_ref.dtype)
        wt_ref[g] = w_scr[g].T.astype(wt_ref.dtype)
    vt_ref[...] = v_scr[...].astype(vt_ref.dtype)
    lq = lq_ref[...]
    s01 = jnp.sum(lq[0:1] * lq[1:2], axis=-1, keepdims=True)
    s23 = jnp.sum(lq[2:3] * lq[3:4], axis=-1, keepdims=True)
    lam = jnp.exp(s01) - jnp.exp(s23) + lam_init
    lam_ref[...] = jnp.broadcast_to(lam, lam_ref.shape)


def _ssmprep(a_re, a_im, log_dt, b_re, b_im, c_re, c_im, lq, lam_init):
    nd, g, p = a_re.shape
    hc = b_re.shape[-1]
    t = SSM_CHUNK
    sw = nd * p
    gb = SCAN_SLAB // hc
    row = lambda a: a.astype(F32).transpose(1, 0, 2).reshape(g, 1, sw)
    ldt = jnp.broadcast_to(log_dt[..., None], (nd, g, p))
    bt = lambda a: a.astype(F32).transpose(1, 3, 0, 2).reshape(g, hc, sw)
    ct = lambda a: a.astype(F32).transpose(1, 2, 0, 3).reshape(g, hc, sw)
    sds = jax.ShapeDtypeStruct
    nk = t * hc
    blk = lambda shape: pl.BlockSpec((gb,) + shape[1:], lambda i: (i,) + (0,) * (len(shape) - 1))
    mat = (g, nk, 2 * sw)
    toept, wt, vt, a_t, lam = pl.pallas_call(
        functools.partial(_ssmprep_kernel, lam_init=lam_init, chunk=t),
        grid=(g // gb,),
        in_specs=[blk((g, 1, sw))] * 3 + [blk((g, hc, sw))] * 4 + [pl.BlockSpec(lq.shape, lambda i: (0, 0))],
        out_specs=[blk((g, nk, nk)), blk((g, 2 * sw, nk)), blk(mat), blk((g, 2, 1, sw)),
                   pl.BlockSpec((8, 128), lambda i: (0, 0))],
        out_shape=(sds((g, nk, nk), BF16), sds((g, 2 * sw, nk), BF16), sds(mat, BF16),
                   sds((g, 2, 1, sw), F32), sds((8, 128), F32)),
        scratch_shapes=[pltpu.VMEM((gb, 2 * nk, sw), F32), pltpu.VMEM((gb, 2 * nk, sw), F32),
                        pltpu.VMEM((gb, nk, 2 * sw), F32), pltpu.VMEM((gb, nk, 2 * sw), F32),
                        pltpu.VMEM((gb, nk, nk), F32)],
        compiler_params=_params("arbitrary"),
        name="ssmprep",
    )(row(a_re), row(a_im), row(ldt), bt(b_re), bt(b_im), ct(c_re), ct(c_im), lq.astype(F32))
    return (toept, wt, vt, a_t), lam


def _to_scan_layout(u, o_ref, batch):
    n_slabs, t, _, cw = o_ref.shape
    tp = u.shape[0] // batch
    u_tb = jnp.swapaxes(u.reshape(batch, tp, u.shape[-1]), 0, 1)
    for c in range(tp // t):
        for s in range(t):
            for v in range(n_slabs):
                o_ref[v, s, c * batch:(c + 1) * batch, :] = u_tb[c * t + s][:, v * cw:(v + 1) * cw]


def _scan_block(ref):
    return jnp.concatenate([ref[v] for v in range(ref.shape[0])], axis=-1)


def _from_scan_layout(ref, batch):
    t, rows, d = ref.shape
    tp = rows // batch * t
    u_tb = jnp.stack([ref[s, c * batch:(c + 1) * batch, :] for c in range(tp // t) for s in range(t)], axis=0)
    return jnp.swapaxes(u_tb, 0, 1).reshape(batch * tp, d)


def _inproj_kernel(*refs, rope, emit_caches, d_attn):
    n_in = 7 if rope else 5
    x_ref, shift_ref, scale_ref, g_ref, w_ref = refs[:5]
    cos_ref, sin_ref = refs[5:7] if rope else (None, None)
    q_ref, k_ref, v_ref, ga_ref, u_ref, gs_ref = refs[n_in:n_in + 6]
    kc_ref, vc_ref = refs[n_in + 6:] if emit_caches else (None, None)
    x = x_ref[...]
    nb, tp, d = x.shape
    ms = jnp.mean(x * x, axis=-1, keepdims=True)
    h = x * lax.rsqrt(ms + EPS) * (g_ref[...] * (1.0 + scale_ref[...])) + shift_ref[...]
    proj = jnp.dot(h.reshape(nb * tp, d).astype(BF16), w_ref[...], preferred_element_type=F32)
    da = d_attn
    q = proj[:, 0:da]
    k = proj[:, da:2 * da]
    tok = lambda z: z.reshape(nb, tp, z.shape[-1])
    if rope:
        lane = lax.broadcasted_iota(jnp.int32, (1, da), 1)
        low = (lane % (QK_DIM // 2)) < (QK_DIM // 4)
        cos = cos_ref[...]
        sin = sin_ref[...]

        def rot(z):
            partner = jnp.where(low, pltpu.roll(z, da - QK_DIM // 4, 1), pltpu.roll(z, QK_DIM // 4, 1))
            return tok(z) * cos + tok(partner) * sin

        q = rot(q)
        k = rot(k)
    else:
        q = tok(q)
        k = tok(k)
    q_ref[...] = (q * Q_PRESCALE).astype(q_ref.dtype)
    v = tok(proj[:, 2 * da:3 * da])
    k_ref[...] = k.astype(k_ref.dtype)
    v_ref[...] = v.astype(v_ref.dtype)
    if emit_caches:
        for hd in range(kc_ref.shape[2]):
            kc_ref[:, :, hd, :] = k[:, :, hd * V_DIM:(hd + 1) * V_DIM]
            vc_ref[:, :, hd, :] = v[:, :, hd * V_DIM:(hd + 1) * V_DIM]
    ga_ref[...] = tok(_silu(proj[:, 3 * da:4 * da])).astype(ga_ref.dtype)
    d_ssm = u_ref.shape[0] * u_ref.shape[-1]
    _to_scan_layout(proj[:, 4 * da:4 * da + d_ssm], u_ref, nb)
    gs_ref[...] = tok(_silu(proj[:, 4 * da + d_ssm:])).astype(gs_ref.dtype)


def _inproj(x, shift, scale, norm_pre, w_in_bf, rope_tabs, emit_caches, d_attn, d_ssm):
    b, l, d = x.shape
    tp = TOKEN_TILE // b
    t = SSM_CHUNK
    rope = rope_tabs is not None
    const = lambda a: pl.BlockSpec(a.shape, lambda j: (0,) * a.ndim)
    tok = lambda w: pl.BlockSpec((b, tp, w), lambda j: (0, j, 0))
    g2 = norm_pre.reshape(1, d).astype(F32)
    in_specs = [tok(d), const(shift), const(scale), const(g2), const(w_in_bf)]
    args = [x, shift, scale, g2, w_in_bf]
    if rope:
        in_specs += [pl.BlockSpec((tp, d_attn), lambda j: (j, 0))] * 2
        args += list(rope_tabs)
    scan_rows = tp // t * b
    attn_shape = jax.ShapeDtypeStruct((b, l, d_attn), BF16)
    out_specs = [tok(d_attn), tok(d_attn), tok(d_attn), tok(d_attn),
                 pl.BlockSpec((d_ssm // SCAN_SLAB, t, scan_rows, SCAN_SLAB), lambda j: (0, 0, j, 0)),
                 tok(d_ssm)]
    out_shape = [attn_shape, attn_shape, attn_shape, attn_shape,
                 jax.ShapeDtypeStruct((d_ssm // SCAN_SLAB, t, l // t * b, SCAN_SLAB), F32),
                 jax.ShapeDtypeStruct((b, l, d_ssm), BF16)]
    if emit_caches:
        out_specs += [pl.BlockSpec((b, tp, d_attn // V_DIM, V_DIM), lambda j: (0, j, 0, 0))] * 2
        out_shape += [jax.ShapeDtypeStruct((b, l, d_attn // V_DIM, V_DIM), F32)] * 2
    return pl.pallas_call(
        functools.partial(_inproj_kernel, rope=rope, emit_caches=emit_caches, d_attn=d_attn),
        grid=(l // tp,),
        in_specs=in_specs,
        out_specs=out_specs,
        out_shape=out_shape,
        compiler_params=_params("arbitrary"),
        name="inproj_rope" if rope else "inproj",
    )(*args)


def _rope_tables(l, d_attn):
    rows = l // GRID_W
    row = jnp.repeat(jnp.arange(rows, dtype=F32), GRID_W)
    col = jnp.tile(jnp.arange(GRID_W, dtype=F32), rows)
    n_freq = QK_DIM // 4
    inv = ROPE_THETA ** (-jnp.arange(n_freq, dtype=F32) / n_freq)
    ang_r = row[:, None] * inv
    ang_c = col[:, None] * inv
    cos64 = jnp.concatenate([jnp.cos(ang_r), jnp.cos(ang_r), jnp.cos(ang_c), jnp.cos(ang_c)], axis=-1)
    sin64 = jnp.concatenate([-jnp.sin(ang_r), jnp.sin(ang_r), -jnp.sin(ang_c), jnp.sin(ang_c)], axis=-1)
    reps = d_attn // QK_DIM
    return jnp.tile(cos64, (1, reps)), jnp.tile(sin64, (1, reps))


def _head(ref, h):
    return ref[:, h, :] if len(ref.shape) == 3 else ref[:, h * V_DIM:(h + 1) * V_DIM]


def _attn_kernel(*refs, has_cache, lam_init):
    if has_cache:
        lam_ref, q_ref, k_ref, v_ref, ck_ref, cv_ref, ga_ref, sub_ref, o_ref, k_scr, vt_scr = refs
    else:
        lam_ref, q_ref, k_ref, v_ref, ga_ref, sub_ref, o_ref, k_scr, vt_scr = refs
    n_heads, lk = k_scr.shape[0], k_scr.shape[1]
    ln = k_ref.shape[0]
    lam = lam_ref[0:1, 0:1]
    lane = lax.broadcasted_iota(jnp.int32, (1, V_DIM), 1)
    first = lane < QK_DIM
    dims = (((1,), (1,)), ((), ()))

    @pl.when(pl.program_id(1) == 0)
    def _():
        for h in range(n_heads):
            if has_cache:
                k_scr[h, 0:lk - ln, :] = _head(ck_ref, h).astype(BF16)
                vt_scr[h, 0:V_DIM, 0:lk - ln] = _head(cv_ref, h).astype(F32).T.astype(BF16)
            k_scr[h, lk - ln:lk, :] = _head(k_ref, h).astype(BF16)
            vt_scr[h, 0:V_DIM, lk - ln:lk] = _head(v_ref, h).astype(F32).T.astype(BF16)
            vt_scr[h, V_DIM:, :] = jnp.ones((vt_scr.shape[1] - V_DIM, lk), BF16)

    for h in range(n_heads):
        q = _head(q_ref, h)
        zero = jnp.zeros_like(q)
        k = k_scr[h]
        vt = vt_scr[h]
        st1 = lax.dot_general(k, jnp.where(first, q, zero), dims, preferred_element_type=F32)
        st2 = lax.dot_general(k, jnp.where(first, zero, q), dims, preferred_element_type=F32)
        et1 = jnp.exp2(st1 - jnp.max(st1, axis=0, keepdims=True)).astype(BF16)
        et2 = jnp.exp2(st2 - jnp.max(st2, axis=0, keepdims=True)).astype(BF16)
        o1 = jnp.dot(vt, et1, preferred_element_type=F32)
        o2 = jnp.dot(vt, et2, preferred_element_type=F32)
        r1 = 1.0 / o1[V_DIM:V_DIM + 1, :]
        r2 = lam / o2[V_DIM:V_DIM + 1, :]
        o = (o1[0:V_DIM, :] * r1 - o2[0:V_DIM, :] * r2).T
        ms = jnp.mean(o * o, axis=-1, keepdims=True)
        o = o * lax.rsqrt(ms + EPS) * sub_ref[...] * (1.0 - lam_init)
        o_ref[:, h * V_DIM:(h + 1) * V_DIM] = (o * _head(ga_ref, h).astype(F32)).astype(o_ref.dtype)


def _attention(lam, q, k, v, cache_k, cache_v, ga, subln, lam_init):
    b, l, d_attn = q.shape
    n_heads = d_attn // V_DIM
    tq = min(ATTN_Q_TILE, l)
    nq = l // tq
    has_cache = cache_k is not None
    q_spec = pl.BlockSpec((None, tq, d_attn), lambda i, j: (i, j, 0))
    whole = lambda a: pl.BlockSpec((None,) + a.shape[1:], lambda i, j: (i,) + (0,) * (a.ndim - 1))
    in_specs = [pl.BlockSpec((8, 128), lambda i, j: (0, 0)), q_spec, whole(k), whole(v)]
    args = [lam, q, k, v]
    if has_cache:
        in_specs += [whole(cache_k), whole(cache_v)]
        args += [cache_k, cache_v]
    in_specs += [q_spec, pl.BlockSpec((1, V_DIM), lambda i, j: (0, 0))]
    args += [ga, subln.reshape(1, V_DIM).astype(F32)]
    lk = l + (cache_k.shape[1] if has_cache else 0)
    out = pl.pallas_call(
        functools.partial(_attn_kernel, has_cache=has_cache, lam_init=lam_init),
        grid=(b, nq),
        in_specs=in_specs,
        out_specs=q_spec,
        out_shape=jax.ShapeDtypeStruct((b, l, d_attn), BF16),
        scratch_shapes=[pltpu.VMEM((n_heads, lk, V_DIM), BF16),
                        pltpu.VMEM((n_heads, V_DIM + SUM_ROWS, lk), BF16)],
        compiler_params=_params("arbitrary", "arbitrary"),
        name="attn_cache" if has_cache else "attn",
    )(*args)
    return out


def _ssm_kernel(*refs, batch, n_chunks, with_state):
    if with_state:
        u_ref, toept_ref, wt_ref, vt_ref, a_ref, h0_ref, y_ref, fin_ref, zt_scr, wc_scr, sin_scr, yt_scr = refs
    else:
        u_ref, toept_ref, wt_ref, vt_ref, a_ref, h0_ref, y_ref, zt_scr, wc_scr, sin_scr, yt_scr = refs
    t = u_ref.shape[0]
    gb = zt_scr.shape[0]
    hc = zt_scr.shape[1] // t
    sw = a_ref.shape[-1]
    hw = sw // 2
    for s in range(t):
        at = u_ref[s].T.astype(BF16)
        for g in range(gb):
            zt_scr[g, s * hc:(s + 1) * hc, :] = at[g * hc:(g + 1) * hc, :]
    for g in range(gb):
        wc_scr[g] = jnp.dot(wt_ref[g], zt_scr[g], preferred_element_type=F32).T
    fwd = lax.broadcasted_iota(jnp.int32, (1, sw), 1) < hw

    def step(j, carry):
        rf = pl.ds(pl.multiple_of(j * batch, batch), batch)
        rb = pl.ds(pl.multiple_of((n_chunks - 1 - j) * batch, batch), batch)
        out = []
        for g in range(gb):
            sr, si = carry[g]
            sin_scr[g, rf, 0:hw] = sr[:, 0:hw]
            sin_scr[g, rb, hw:sw] = sr[:, hw:sw]
            sin_scr[g, rf, sw:sw + hw] = si[:, 0:hw]
            sin_scr[g, rb, sw + hw:2 * sw] = si[:, hw:sw]
            wr = jnp.where(fwd, wc_scr[g, rf, 0:sw], wc_scr[g, rb, 0:sw])
            wi = jnp.where(fwd, wc_scr[g, rf, sw:2 * sw], wc_scr[g, rb, sw:2 * sw])
            ar = a_ref[g, 0]
            ai = a_ref[g, 1]
            out.append((ar * sr - ai * si + wr, ar * si + ai * sr + wi))
        return tuple(out)

    fin = lax.fori_loop(0, n_chunks, step, tuple((h0_ref[g, 0], h0_ref[g, 1]) for g in range(gb)))
    for g in range(gb):
        sint = sin_scr[g].T.astype(BF16)
        y2t = (jnp.dot(toept_ref[g], zt_scr[g], preferred_element_type=F32)
               + jnp.dot(vt_ref[g], sint, preferred_element_type=F32))
        for s in range(t):
            yt_scr[s, g * hc:(g + 1) * hc, :] = y2t[s * hc:(s + 1) * hc, :]
        if with_state:
            fin_ref[g, 0] = fin[g][0]
            fin_ref[g, 1] = fin[g][1]
    for s in range(t):
        y_ref[s] = yt_scr[s].T.astype(y_ref.dtype)


def _ssm(u3, ssm_ops, h0, with_state):
    toept, wt, vt, a_t = ssm_ops
    n_slabs, t, rows, cw = u3.shape
    batch = h0.shape[2]
    n_chunks = rows // batch
    g = toept.shape[0]
    gb = g // n_slabs
    slab = pl.BlockSpec((None, t, rows, cw), lambda i: (i, 0, 0, 0))
    blk = lambda a: pl.BlockSpec((gb,) + a.shape[1:], lambda i: (i,) + (0,) * (a.ndim - 1))
    out_specs = [slab]
    out_shape = [jax.ShapeDtypeStruct(u3.shape, F32)]
    if with_state:
        out_specs.append(blk(h0))
        out_shape.append(jax.ShapeDtypeStruct(h0.shape, F32))
    nk = toept.shape[-1]
    outs = pl.pallas_call(
        functools.partial(_ssm_kernel, batch=batch, n_chunks=n_chunks, with_state=with_state),
        grid=(g // gb,),
        in_specs=[slab, blk(toept), blk(wt), blk(vt), blk(a_t), blk(h0)],
        out_specs=out_specs,
        out_shape=out_shape,
        scratch_shapes=[pltpu.VMEM((gb, nk, rows), BF16),
                        pltpu.VMEM((gb, rows, wt.shape[1]), F32),
                        pltpu.VMEM((gb, rows, wt.shape[1]), F32),
                        pltpu.VMEM((t, cw, rows), F32)],
        compiler_params=_params("arbitrary"),
        name="ssm_state" if with_state else "ssm",
    )(u3, toept, wt, vt, a_t, h0)
    return outs


def _state_to_lanes(h0):
    b, nd, nc, g, p = h0.shape
    return h0.astype(F32).transpose(3, 2, 0, 1, 4).reshape(g, nc, b, nd * p)


def _state_from_lanes(fin, nd):
    g, nc, b, w = fin.shape
    return fin.reshape(g, nc, b, nd, w // nd).transpose(2, 3, 1, 0, 4)


def _outproj_kernel(x_ref, a_ref, y_ref, u_ref, gs_ref, gate_ref, dskip_ref,
                    wglu_ref, bglu_ref, wout_ref, g_ref, o_ref):
    nb, tp, d = x_ref.shape
    y = _scan_block(y_ref) + dskip_ref[...] * _scan_block(u_ref)
    t, srows, d_ssm = y.shape
    rows = lambda ref: ref[...].reshape(nb * tp, ref.shape[-1])
    y = y.reshape(t * srows, d_ssm)
    c0 = math.sqrt(2.0 / math.pi)
    hy = 0.5 * y
    ys = hy + hy * jnp.tanh(y * (c0 + (c0 * 0.044715) * (y * y)))
    z = jnp.dot(ys.astype(BF16), wglu_ref[...], preferred_element_type=F32) + bglu_ref[...]
    ys = _from_scan_layout((ys * jax.nn.sigmoid(z)).reshape(t, srows, d_ssm), nb) * rows(gs_ref).astype(F32)
    cat = jnp.concatenate([rows(a_ref), ys.astype(BF16)], axis=-1)
    out = jnp.dot(cat, wout_ref[...], preferred_element_type=F32)
    ms = jnp.mean(out * out, axis=-1, keepdims=True)
    out = out * lax.rsqrt(ms + EPS)
    o_ref[...] = x_ref[...] + (g_ref[...] * gate_ref[...]) * out.reshape(nb, tp, d)


def _outproj(x, a_out, y_ssm, u, gs, gate, d_skip, w_glu_bf, b_glu, w_out_bf, norm_post):
    b, l, d = x.shape
    d_attn = a_out.shape[-1]
    d_ssm = gs.shape[-1]
    tp = TOKEN_TILE // b
    t = SSM_CHUNK
    const = lambda a: pl.BlockSpec(a.shape, lambda j: (0,) * a.ndim)
    tok = lambda w: pl.BlockSpec((b, tp, w), lambda j: (0, j, 0))
    scan = pl.BlockSpec((d_ssm // SCAN_SLAB, t, tp // t * b, SCAN_SLAB), lambda j: (0, 0, j, 0))
    consts = [gate, d_skip.reshape(1, d_ssm).astype(F32), w_glu_bf, b_glu.reshape(1, d_ssm).astype(F32),
              w_out_bf, norm_post.reshape(1, d).astype(F32)]
    return pl.pallas_call(
        _outproj_kernel,
        grid=(l // tp,),
        in_specs=[tok(d), tok(d_attn), scan, scan, tok(d_ssm)] + [const(a) for a in consts],
        out_specs=tok(d),
        out_shape=jax.ShapeDtypeStruct((b, l, d), F32),
        compiler_params=_params("arbitrary"),
        name="outproj",
    )(x, a_out, y_ssm, u, gs, *consts)


def _mixer(x, mod, h0_lanes, cache_k, cache_v, rope_tabs, emit_caches, with_state, lam, lam_init,
           norm_pre, w_in_bf, ssm_ops, subln, d_skip, w_glu_bf, b_glu, w_out_bf, norm_post):
    b, l, d = x.shape
    d_ssm = w_glu_bf.shape[0]
    d_attn = (w_in_bf.shape[1] - 2 * d_ssm) // 4
    shift, scale, gate = mod
    q, k, v, ga, u, gs, *caches = _inproj(x, shift, scale, norm_pre, w_in_bf, rope_tabs, emit_caches,
                                          d_attn, d_ssm)
    a_out = _attention(lam, q, k, v, cache_k, cache_v, ga, subln, lam_init)
    outs = _ssm(u, ssm_ops, h0_lanes, with_state)
    y = _outproj(x, a_out, outs[0], u, gs, gate, d_skip, w_glu_bf, b_glu, w_out_bf, norm_post)
    fin = outs[1] if with_state else None
    return y, caches, fin


def kernel(x_prompt, x_sample, cache_k, cache_v, state_ssm, c, c_ctx, w_ada, b_ada, norm_pre, norm_post, w_in, lambda_qk, subln, ssm_A_re, ssm_A_im, ssm_log_dt, ssm_B_re, ssm_B_im, ssm_C_re, ssm_C_im, ssm_D, w_glu, b_glu, w_out):
    xp, xs = x_prompt, x_sample
    bp, lp, d = xp.shape
    bs, ls, _ = xs.shape
    depth = w_in.shape[0]
    nd = ssm_A_re.shape[1]
    g, p = ssm_A_re.shape[-2:]
    d_ssm = w_glu.shape[-1]
    d_attn = (w_in.shape[-1] - 2 * d_ssm) // 4
    rope_tabs = _rope_tables(ls, d_attn)
    cond_rows = 16
    cond = jnp.zeros((cond_rows, d), F32).at[:bs].set(c.astype(F32)).at[bs].set(c_ctx.astype(F32))
    new_k, new_v, new_s = [], [], []
    for layer in range(depth):
        lam_init = 0.8 - 0.6 * math.exp(-0.3 * layer)
        m = _adaln(cond, w_ada[layer], b_ada[layer])
        mod_s = tuple(m[:bs, i * d:(i + 1) * d].reshape(bs, 1, d) for i in range(3))
        mod_p = tuple(m[bs:bs + 1, i * d:(i + 1) * d].reshape(1, 1, d) for i in range(3))
        ssm_ops, lam = _ssmprep(
            ssm_A_re[layer], ssm_A_im[layer], ssm_log_dt[layer], ssm_B_re[layer], ssm_B_im[layer],
            ssm_C_re[layer], ssm_C_im[layer], lambda_qk[layer], lam_init)
        shared = (lam, lam_init, norm_pre[layer], w_in[layer].astype(BF16), ssm_ops, subln[layer],
                  ssm_D[layer], w_glu[layer].astype(BF16), b_glu[layer], w_out[layer].astype(BF16),
                  norm_post[layer])

        h0_p = jnp.zeros((g, 2, bp, nd * p), F32)
        xp, (k_p, v_p), fin = _mixer(xp, mod_p, h0_p, None, None, None, True, True, *shared)
        new_k.append(k_p)
        new_v.append(v_p)
        new_s.append(_state_from_lanes(fin, nd).astype(xp.dtype))

        h0_s = _state_to_lanes(state_ssm[:, layer])
        xs, _, _ = _mixer(xs, mod_s, h0_s, cache_k[:, layer], cache_v[:, layer], rope_tabs, False, False,
                             *shared)

    return (xp, xs, jnp.stack(new_k, axis=1), jnp.stack(new_v, axis=1), jnp.stack(new_s, axis=1))
```

```python
import functools
import math

import jax
import jax.numpy as jnp
from jax import lax
from jax.experimental import pallas as pl
from jax.experimental.pallas import tpu as pltpu

F32 = jnp.float32
BF16 = jnp.bfloat16

GRID_W = 64
QK_DIM = 64
V_DIM = 2 * QK_DIM
ROPE_THETA = 10000.0
EPS = 1e-6
Q_PRESCALE = math.log2(math.e) * QK_DIM ** -0.5

SUM_ROWS = 16
TOKEN_TILE = 512
ATTN_Q_TILE = 1024
SSM_CHUNK = 16
SCAN_SLAB = 128
VMEM_LIMIT = 48 * 1024 * 1024


def _silu(x):
    return x * jax.nn.sigmoid(x)


def _params(*sem):
    return pltpu.CompilerParams(dimension_semantics=sem, vmem_limit_bytes=VMEM_LIMIT)


def _adaln_kernel(c_ref, w_ref, b_ref, o_ref):
    s = _silu(c_ref[...]).astype(BF16)
    o_ref[...] = jnp.dot(s, w_ref[...].astype(BF16), preferred_element_type=F32) + b_ref[...]


def _adaln(cond, w_ada, b_ada):
    rows, d = cond.shape
    n = w_ada.shape[1]
    bn = 512
    return pl.pallas_call(
        _adaln_kernel,
        grid=(n // bn,),
        in_specs=[pl.BlockSpec((rows, d), lambda j: (0, 0)),
                  pl.BlockSpec((d, bn), lambda j: (0, j)),
                  pl.BlockSpec((1, bn), lambda j: (0, j))],
        out_specs=pl.BlockSpec((rows, bn), lambda j: (0, j)),
        out_shape=jax.ShapeDtypeStruct((rows, n), F32),
        compiler_params=_params("arbitrary"),
        name="adaln",
    )(cond, w_ada, b_ada.reshape(1, n))


def _ssmprep_kernel(are_ref, aim_ref, ldt_ref, bre_ref, bim_ref, cre_ref, cim_ref, lq_ref,
                    toept_ref, wt_ref, vt_ref, a_ref, lam_ref,
                    cplre_scr, cplimn_scr, w_scr, v_scr, toep_scr, *, lam_init, chunk):
    a_re = are_ref[...]
    a_im = aim_ref[...]
    dt = jnp.exp(ldt_ref[...])
    mag = jnp.exp(a_re * dt)
    ab_re = mag * jnp.cos(a_im * dt)
    ab_im = mag * jnp.sin(a_im * dt)
    nr, ni = ab_re - 1.0, ab_im
    den = a_re * a_re + a_im * a_im
    f_re = (nr * a_re + ni * a_im) / den
    f_im = (ni * a_re - nr * a_im) / den
    b_re = bre_ref[...]
    b_im = bim_ref[...]
    bb_re = f_re * b_re - f_im * b_im
    bb_im = f_re * b_im + f_im * b_re
    c_re = cre_ref[...]
    c_im = cim_ref[...]
    hc = b_re.shape[1]
    sw = a_re.shape[-1]
    hw = sw // 2
    lo, hi = slice(0, hw), slice(hw, sw)
    lo_im, hi_im = slice(sw, sw + hw), slice(sw + hw, 2 * sw)
    blk = lambda j: slice(j * hc, (j + 1) * hc)
    last = chunk - 1
    cplre_scr[...] = jnp.zeros_like(cplre_scr)
    cplimn_scr[...] = jnp.zeros_like(cplimn_scr)
    pw_re = jnp.ones_like(ab_re)
    pw_im = jnp.zeros_like(ab_im)
    for tau in range(chunk + 1):
        cp_re = c_re * pw_re - c_im * pw_im
        cp_imn = -(c_re * pw_im + c_im * pw_re)
        if tau < chunk:
            cplre_scr[:, blk(last + tau), lo] = cp_re[:, :, lo]
            cplre_scr[:, blk(last - tau), hi] = cp_re[:, :, hi]
            cplimn_scr[:, blk(last + tau), lo] = cp_imn[:, :, lo]
            cplimn_scr[:, blk(last - tau), hi] = cp_imn[:, :, hi]
            e_re = pw_re * bb_re - pw_im * bb_im
            e_im = pw_re * bb_im + pw_im * bb_re
            w_scr[:, blk(last - tau), lo] = e_re[:, :, lo]
            w_scr[:, blk(tau), hi] = e_re[:, :, hi]
            w_scr[:, blk(last - tau), lo_im] = e_im[:, :, lo]
            w_scr[:, blk(tau), hi_im] = e_im[:, :, hi]
        if tau >= 1:
            v_scr[:, blk(tau - 1), lo] = cp_re[:, :, lo]
            v_scr[:, blk(chunk - tau), hi] = cp_re[:, :, hi]
            v_scr[:, blk(tau - 1), lo_im] = cp_imn[:, :, lo]
            v_scr[:, blk(chunk - tau), hi_im] = cp_imn[:, :, hi]
        if tau < chunk:
            pw_re, pw_im = pw_re * ab_re - pw_im * ab_im, pw_re * ab_im + pw_im * ab_re
    a_ref[:, 0] = pw_re
    a_ref[:, 1] = pw_im
    dims = (((2,), (2,)), ((0,), (0,)))
    taps = (lax.dot_general(bb_re.astype(BF16), cplre_scr[...].astype(BF16), dims,
                            preferred_element_type=F32)
            + lax.dot_general(bb_im.astype(BF16), cplimn_scr[...].astype(BF16), dims,
                              preferred_element_type=F32))
    nk = chunk * hc
    for s in range(chunk):
        toep_scr[:, blk(s), :] = taps[:, :, (last - s) * hc:(last - s) * hc + nk]
    for g in range(toept_ref.shape[0]):
        toept_ref[g] = toep_scr[g].T.astype(toept_ref.dtype)
        wt_ref[g] = w_scr[g].T.astype(wt_ref.dtype)
    vt_ref[...] = v_scr[...].astype(vt_ref.dtype)
    lq = lq_ref[...]
    s01 = jnp.sum(lq[0:1] * lq[1:2], axis=-1, keepdims=True)
    s23 = jnp.sum(lq[2:3] * lq[3:4], axis=-1, keepdims=True)
    lam = jnp.exp(s01) - jnp.exp(s23) + lam_init
    lam_ref[...] = jnp.broadcast_to(lam, lam_ref.shape)


def _ssmprep(a_re, a_im, log_dt, b_re, b_im, c_re, c_im, lq, lam_init):
    nd, g, p = a_re.shape
    hc = b_re.shape[-1]
    t = SSM_CHUNK
    sw = nd * p
    gb = SCAN_SLAB // hc
    row = lambda a: a.astype(F32).transpose(1, 0, 2).reshape(g, 1, sw)
    ldt = jnp.broadcast_to(log_dt[..., None], (nd, g, p))
    bt = lambda a: a.astype(F32).transpose(1, 3, 0, 2).reshape(g, hc, sw)
    ct = lambda a: a.astype(F32).transpose(1, 2, 0, 3).reshape(g, hc, sw)
    sds = jax.ShapeDtypeStruct
    nk = t * hc
    blk = lambda shape: pl.BlockSpec((gb,) + shape[1:], lambda i: (i,) + (0,) * (len(shape) - 1))
    mat = (g, nk, 2 * sw)
    toept, wt, vt, a_t, lam = pl.pallas_call(
        functools.partial(_ssmprep_kernel, lam_init=lam_init, chunk=t),
        grid=(g // gb,),
        in_specs=[blk((g, 1, sw))] * 3 + [blk((g, hc, sw))] * 4 + [pl.BlockSpec(lq.shape, lambda i: (0, 0))],
        out_specs=[blk((g, nk, nk)), blk((g, 2 * sw, nk)), blk(mat), blk((g, 2, 1, sw)),
                   pl.BlockSpec((8, 128), lambda i: (0, 0))],
        out_shape=(sds((g, nk, nk), BF16), sds((g, 2 * sw, nk), BF16), sds(mat, BF16),
                   sds((g, 2, 1, sw), F32), sds((8, 128), F32)),
        scratch_shapes=[pltpu.VMEM((gb, 2 * nk, sw), F32), pltpu.VMEM((gb, 2 * nk, sw), F32),
                        pltpu.VMEM((gb, nk, 2 * sw), F32), pltpu.VMEM((gb, nk, 2 * sw), F32),
                        pltpu.VMEM((gb, nk, nk), F32)],
        compiler_params=_params("arbitrary"),
        name="ssmprep",
    )(row(a_re), row(a_im), row(ldt), bt(b_re), bt(b_im), ct(c_re), ct(c_im), lq.astype(F32))
    return (toept, wt, vt, a_t), lam


def _to_scan_layout(u, o_ref, batch):
    n_slabs, t, _, cw = o_ref.shape
    tp = u.shape[0] // batch
    u_tb = jnp.swapaxes(u.reshape(batch, tp, u.shape[-1]), 0, 1)
    for c in range(tp // t):
        for s in range(t):
            for v in range(n_slabs):
                o_ref[v, s, c * batch:(c + 1) * batch, :] = u_tb[c * t + s][:, v * cw:(v + 1) * cw].astype(o_ref.dtype)


def _scan_block(ref):
    return jnp.concatenate([ref[v] for v in range(ref.shape[0])], axis=-1).astype(F32)


def _from_scan_layout(ref, batch):
    t, rows, d = ref.shape
    tp = rows // batch * t
    u_tb = jnp.stack([ref[s, c * batch:(c + 1) * batch, :] for c in range(tp // t) for s in range(t)], axis=0)
    return jnp.swapaxes(u_tb, 0, 1).reshape(batch * tp, d)


def _inproj_kernel(*refs, rope, emit_caches, d_attn):
    n_in = 7 if rope else 5
    x_ref, shift_ref, scale_ref, g_ref, w_ref = refs[:5]
    cos_ref, sin_ref = refs[5:7] if rope else (None, None)
    q_ref, k_ref, v_ref, ga_ref, u_ref, gs_ref = refs[n_in:n_in + 6]
    kc_ref, vc_ref = refs[n_in + 6:] if emit_caches else (None, None)
    x = x_ref[...]
    nb, tp, d = x.shape
    ms = jnp.mean(x * x, axis=-1, keepdims=True)
    h = x * lax.rsqrt(ms + EPS) * (g_ref[...] * (1.0 + scale_ref[...])) + shift_ref[...]
    proj = jnp.dot(h.reshape(nb * tp, d).astype(BF16), w_ref[...], preferred_element_type=F32)
    da = d_attn
    q = proj[:, 0:da]
    k = proj[:, da:2 * da]
    tok = lambda z: z.reshape(nb, tp, z.shape[-1])
    if rope:
        lane = lax.broadcasted_iota(jnp.int32, (1, da), 1)
        low = (lane % (QK_DIM // 2)) < (QK_DIM // 4)
        cos = cos_ref[...]
        sin = sin_ref[...]

        def rot(z):
            partner = jnp.where(low, pltpu.roll(z, da - QK_DIM // 4, 1), pltpu.roll(z, QK_DIM // 4, 1))
            return tok(z) * cos + tok(partner) * sin

        q = rot(q)
        k = rot(k)
    else:
        q = tok(q)
        k = tok(k)
    q_ref[...] = (q * Q_PRESCALE).astype(q_ref.dtype)
    v = tok(proj[:, 2 * da:3 * da])
    k_ref[...] = k.astype(k_ref.dtype)
    v_ref[...] = v.astype(v_ref.dtype)
    if emit_caches:
        for hd in range(kc_ref.shape[2]):
            kc_ref[:, :, hd, :] = k[:, :, hd * V_DIM:(hd + 1) * V_DIM]
            vc_ref[:, :, hd, :] = v[:, :, hd * V_DIM:(hd + 1) * V_DIM]
    ga_ref[...] = tok(_silu(proj[:, 3 * da:4 * da])).astype(ga_ref.dtype)
    d_ssm = u_ref.shape[0] * u_ref.shape[-1]
    _to_scan_layout(proj[:, 4 * da:4 * da + d_ssm], u_ref, nb)
    gs_ref[...] = tok(_silu(proj[:, 4 * da + d_ssm:])).astype(gs_ref.dtype)


def _inproj(x, shift, scale, norm_pre, w_in_bf, rope_tabs, emit_caches, d_attn, d_ssm):
    b, l, d = x.shape
    tp = TOKEN_TILE // b
    t = SSM_CHUNK
    rope = rope_tabs is not None
    const = lambda a: pl.BlockSpec(a.shape, lambda j: (0,) * a.ndim)
    tok = lambda w: pl.BlockSpec((b, tp, w), lambda j: (0, j, 0))
    g2 = norm_pre.reshape(1, d).astype(F32)
    in_specs = [tok(d), const(shift), const(scale), const(g2), const(w_in_bf)]
    args = [x, shift, scale, g2, w_in_bf]
    if rope:
        in_specs += [pl.BlockSpec((tp, d_attn), lambda j: (j, 0))] * 2
        args += list(rope_tabs)
    scan_rows = tp // t * b
    attn_shape = jax.ShapeDtypeStruct((b, l, d_attn), BF16)
    out_specs = [tok(d_attn), tok(d_attn), tok(d_attn), tok(d_attn),
                 pl.BlockSpec((d_ssm // SCAN_SLAB, t, scan_rows, SCAN_SLAB), lambda j: (0, 0, j, 0)),
                 tok(d_ssm)]
    out_shape = [attn_shape, attn_shape, attn_shape, attn_shape,
                 jax.ShapeDtypeStruct((d_ssm // SCAN_SLAB, t, l // t * b, SCAN_SLAB), BF16),
                 jax.ShapeDtypeStruct((b, l, d_ssm), BF16)]
    if emit_caches:
        out_specs += [pl.BlockSpec((b, tp, d_attn // V_DIM, V_DIM), lambda j: (0, j, 0, 0))] * 2
        out_shape += [jax.ShapeDtypeStruct((b, l, d_attn // V_DIM, V_DIM), F32)] * 2
    return pl.pallas_call(
        functools.partial(_inproj_kernel, rope=rope, emit_caches=emit_caches, d_attn=d_attn),
        grid=(l // tp,),
        in_specs=in_specs,
        out_specs=out_specs,
        out_shape=out_shape,
        compiler_params=_params("arbitrary"),
        name="inproj_rope" if rope else "inproj",
    )(*args)


def _rope_tables(l, d_attn):
    rows = l // GRID_W
    row = jnp.repeat(jnp.arange(rows, dtype=F32), GRID_W)
    col = jnp.tile(jnp.arange(GRID_W, dtype=F32), rows)
    n_freq = QK_DIM // 4
    inv = ROPE_THETA ** (-jnp.arange(n_freq, dtype=F32) / n_freq)
    ang_r = row[:, None] * inv
    ang_c = col[:, None] * inv
    cos64 = jnp.concatenate([jnp.cos(ang_r), jnp.cos(ang_r), jnp.cos(ang_c), jnp.cos(ang_c)], axis=-1)
    sin64 = jnp.concatenate([-jnp.sin(ang_r), jnp.sin(ang_r), -jnp.sin(ang_c), jnp.sin(ang_c)], axis=-1)
    reps = d_attn // QK_DIM
    return jnp.tile(cos64, (1, reps)), jnp.tile(sin64, (1, reps))


def _head(ref, h):
    return ref[:, h, :] if len(ref.shape) == 3 else ref[:, h * V_DIM:(h + 1) * V_DIM]


def _attn_kernel(*refs, has_cache, lam_init):
    if has_cache:
        lam_ref, q_ref, k_ref, v_ref, ck_ref, cv_ref, ga_ref, sub_ref, o_ref, k_scr, vt_scr = refs
    else:
        lam_ref, q_ref, k_ref, v_ref, ga_ref, sub_ref, o_ref, k_scr, vt_scr = refs
    n_heads, lk = k_scr.shape[0], k_scr.shape[1]
    ln = k_ref.shape[0]
    lam = lam_ref[0:1, 0:1]
    lane = lax.broadcasted_iota(jnp.int32, (1, V_DIM), 1)
    first = lane < QK_DIM
    dims = (((1,), (1,)), ((), ()))

    @pl.when(pl.program_id(1) == 0)
    def _():
        for h in range(n_heads):
            if has_cache:
                k_scr[h, 0:lk - ln, :] = _head(ck_ref, h).astype(BF16)
                vt_scr[h, 0:V_DIM, 0:lk - ln] = _head(cv_ref, h).astype(F32).T.astype(BF16)
            k_scr[h, lk - ln:lk, :] = _head(k_ref, h).astype(BF16)
            vt_scr[h, 0:V_DIM, lk - ln:lk] = _head(v_ref, h).astype(F32).T.astype(BF16)
            vt_scr[h, V_DIM:, :] = jnp.ones((vt_scr.shape[1] - V_DIM, lk), BF16)

    def scores(h):
        q = _head(q_ref, h)
        zero = jnp.zeros_like(q)
        k = k_scr[h]
        return (lax.dot_general(k, jnp.where(first, q, zero), dims, preferred_element_type=F32),
                lax.dot_general(k, jnp.where(first, zero, q), dims, preferred_element_type=F32))

    nxt = scores(0)
    for h in range(n_heads):
        st1, st2 = nxt
        if h + 1 < n_heads:
            nxt = scores(h + 1)
        vt = vt_scr[h]
        et1 = jnp.exp2(st1 - jnp.max(st1, axis=0, keepdims=True)).astype(BF16)
        et2 = jnp.exp2(st2 - jnp.max(st2, axis=0, keepdims=True)).astype(BF16)
        o1 = jnp.dot(vt, et1, preferred_element_type=F32)
        o2 = jnp.dot(vt, et2, preferred_element_type=F32)
        r1 = 1.0 / o1[V_DIM:V_DIM + 1, :]
        r2 = lam / o2[V_DIM:V_DIM + 1, :]
        o = (o1[0:V_DIM, :] * r1 - o2[0:V_DIM, :] * r2).T
        ms = jnp.mean(o * o, axis=-1, keepdims=True)
        o = o * lax.rsqrt(ms + EPS) * sub_ref[...] * (1.0 - lam_init)
        o_ref[:, h * V_DIM:(h + 1) * V_DIM] = (o * _head(ga_ref, h).astype(F32)).astype(o_ref.dtype)


def _attention(lam, q, k, v, cache_k, cache_v, ga, subln, lam_init):
    b, l, d_attn = q.shape
    n_heads = d_attn // V_DIM
    tq = min(ATTN_Q_TILE, l)
    nq = l // tq
    has_cache = cache_k is not None
    q_spec = pl.BlockSpec((None, tq, d_attn), lambda i, j: (i, j, 0))
    whole = lambda a: pl.BlockSpec((None,) + a.shape[1:], lambda i, j: (i,) + (0,) * (a.ndim - 1))
    in_specs = [pl.BlockSpec((8, 128), lambda i, j: (0, 0)), q_spec, whole(k), whole(v)]
    args = [lam, q, k, v]
    if has_cache:
        in_specs += [whole(cache_k), whole(cache_v)]
        args += [cache_k, cache_v]
    in_specs += [q_spec, pl.BlockSpec((1, V_DIM), lambda i, j: (0, 0))]
    args += [ga, subln.reshape(1, V_DIM).astype(F32)]
    lk = l + (cache_k.shape[1] if has_cache else 0)
    out = pl.pallas_call(
        functools.partial(_attn_kernel, has_cache=has_cache, lam_init=lam_init),
        grid=(b, nq),
        in_specs=in_specs,
        out_specs=q_spec,
        out_shape=jax.ShapeDtypeStruct((b, l, d_attn), BF16),
        scratch_shapes=[pltpu.VMEM((n_heads, lk, V_DIM), BF16),
                        pltpu.VMEM((n_heads, V_DIM + SUM_ROWS, lk), BF16)],
        compiler_params=_params("arbitrary", "arbitrary"),
        name="attn_cache" if has_cache else "attn",
    )(*args)
    return out


def _ssm_kernel(*refs, batch, n_chunks, with_state):
    if with_state:
        u_ref, toept_ref, wt_ref, vt_ref, a_ref, h0_ref, y_ref, fin_ref, zt_scr, wc_scr, sin_scr, yt_scr = refs
    else:
        u_ref, toept_ref, wt_ref, vt_ref, a_ref, h0_ref, y_ref, zt_scr, wc_scr, sin_scr, yt_scr = refs
    t = u_ref.shape[0]
    gb = zt_scr.shape[0]
    hc = zt_scr.shape[1] // t
    sw = a_ref.shape[-1]
    hw = sw // 2
    for s in range(t):
        at = u_ref[s].T
        for g in range(gb):
            zt_scr[g, s * hc:(s + 1) * hc, :] = at[g * hc:(g + 1) * hc, :]
    for g in range(gb):
        wc_scr[g] = jnp.dot(wt_ref[g], zt_scr[g], preferred_element_type=F32).T
    fwd = lax.broadcasted_iota(jnp.int32, (1, sw), 1) < hw

    def step(j, carry):
        rf = pl.ds(pl.multiple_of(j * batch, batch), batch)
        rb = pl.ds(pl.multiple_of((n_chunks - 1 - j) * batch, batch), batch)
        out = []
        for g in range(gb):
            sr, si = carry[g]
            sin_scr[g, rf, 0:hw] = sr[:, 0:hw]
            sin_scr[g, rb, hw:sw] = sr[:, hw:sw]
            sin_scr[g, rf, sw:sw + hw] = si[:, 0:hw]
            sin_scr[g, rb, sw + hw:2 * sw] = si[:, hw:sw]
            wr = jnp.where(fwd, wc_scr[g, rf, 0:sw], wc_scr[g, rb, 0:sw])
            wi = jnp.where(fwd, wc_scr[g, rf, sw:2 * sw], wc_scr[g, rb, sw:2 * sw])
            ar = a_ref[g, 0]
            ai = a_ref[g, 1]
            out.append((ar * sr - ai * si + wr, ar * si + ai * sr + wi))
        return tuple(out)

    fin = lax.fori_loop(0, n_chunks, step, tuple((h0_ref[g, 0], h0_ref[g, 1]) for g in range(gb)))
    for g in range(gb):
        sint = sin_scr[g].T.astype(BF16)
        y2t = (jnp.dot(toept_ref[g], zt_scr[g], preferred_element_type=F32)
               + jnp.dot(vt_ref[g], sint, preferred_element_type=F32))
        for s in range(t):
            yt_scr[s, g * hc:(g + 1) * hc, :] = y2t[s * hc:(s + 1) * hc, :]
        if with_state:
            fin_ref[g, 0] = fin[g][0]
            fin_ref[g, 1] = fin[g][1]
    for s in range(t):
        y_ref[s] = yt_scr[s].T.astype(y_ref.dtype)


def _ssm(u3, ssm_ops, h0, with_state):
    toept, wt, vt, a_t = ssm_ops
    n_slabs, t, rows, cw = u3.shape
    batch = h0.shape[2]
    n_chunks = rows // batch
    g = toept.shape[0]
    gb = g // n_slabs
    slab = pl.BlockSpec((None, t, rows, cw), lambda i: (i, 0, 0, 0))
    blk = lambda a: pl.BlockSpec((gb,) + a.shape[1:], lambda i: (i,) + (0,) * (a.ndim - 1))
    out_specs = [slab]
    out_shape = [jax.ShapeDtypeStruct(u3.shape, u3.dtype)]
    if with_state:
        out_specs.append(blk(h0))
        out_shape.append(jax.ShapeDtypeStruct(h0.shape, F32))
    nk = toept.shape[-1]
    outs = pl.pallas_call(
        functools.partial(_ssm_kernel, batch=batch, n_chunks=n_chunks, with_state=with_state),
        grid=(g // gb,),
        in_specs=[slab, blk(toept), blk(wt), blk(vt), blk(a_t), blk(h0)],
        out_specs=out_specs,
        out_shape=out_shape,
        scratch_shapes=[pltpu.VMEM((gb, nk, rows), BF16),
                        pltpu.VMEM((gb, rows, wt.shape[1]), F32),
                        pltpu.VMEM((gb, rows, wt.shape[1]), F32),
                        pltpu.VMEM((t, cw, rows), F32)],
        compiler_params=_params("arbitrary"),
        name="ssm_state" if with_state else "ssm",
    )(u3, toept, wt, vt, a_t, h0)
    return outs


def _state_to_lanes(h0):
    b, nd, nc, g, p = h0.shape
    return h0.astype(F32).transpose(3, 2, 0, 1, 4).reshape(g, nc, b, nd * p)


def _state_from_lanes(fin, nd):
    g, nc, b, w = fin.shape
    return fin.reshape(g, nc, b, nd, w // nd).transpose(2, 3, 1, 0, 4)


def _outproj_kernel(x_ref, a_ref, y_ref, u_ref, gs_ref, gate_ref, dskip_ref,
                    wglu_ref, bglu_ref, wout_ref, g_ref, o_ref):
    nb, tp, d = x_ref.shape
    y = _scan_block(y_ref) + dskip_ref[...] * _scan_block(u_ref)
    t, srows, d_ssm = y.shape
    rows = lambda ref: ref[...].reshape(nb * tp, ref.shape[-1])
    y = y.reshape(t * srows, d_ssm)
    c0 = math.sqrt(2.0 / math.pi)
    hy = 0.5 * y
    ys = hy + hy * jnp.tanh(y * (c0 + (c0 * 0.044715) * (y * y)))
    z = jnp.dot(ys.astype(BF16), wglu_ref[...], preferred_element_type=F32) + bglu_ref[...]
    ys = _from_scan_layout((ys * jax.nn.sigmoid(z)).reshape(t, srows, d_ssm), nb) * rows(gs_ref).astype(F32)
    cat = jnp.concatenate([rows(a_ref), ys.astype(BF16)], axis=-1)
    out = jnp.dot(cat, wout_ref[...], preferred_element_type=F32)
    ms = jnp.mean(out * out, axis=-1, keepdims=True)
    out = out * lax.rsqrt(ms + EPS)
    o_ref[...] = x_ref[...] + (g_ref[...] * gate_ref[...]) * out.reshape(nb, tp, d)


def _outproj(x, a_out, y_ssm, u, gs, gate, d_skip, w_glu_bf, b_glu, w_out_bf, norm_post):
    b, l, d = x.shape
    d_attn = a_out.shape[-1]
    d_ssm = gs.shape[-1]
    tp = TOKEN_TILE // b
    t = SSM_CHUNK
    const = lambda a: pl.BlockSpec(a.shape, lambda j: (0,) * a.ndim)
    tok = lambda w: pl.BlockSpec((b, tp, w), lambda j: (0, j, 0))
    scan = pl.BlockSpec((d_ssm // SCAN_SLAB, t, tp // t * b, SCAN_SLAB), lambda j: (0, 0, j, 0))
    consts = [gate, d_skip.reshape(1, d_ssm).astype(F32), w_glu_bf, b_glu.reshape(1, d_ssm).astype(F32),
              w_out_bf, norm_post.reshape(1, d).astype(F32)]
    return pl.pallas_call(
        _outproj_kernel,
        grid=(l // tp,),
        in_specs=[tok(d), tok(d_attn), scan, scan, tok(d_ssm)] + [const(a) for a in consts],
        out_specs=tok(d),
        out_shape=jax.ShapeDtypeStruct((b, l, d), F32),
        compiler_params=_params("arbitrary"),
        name="outproj",
    )(x, a_out, y_ssm, u, gs, *consts)


def _mixer(x, mod, h0_lanes, cache_k, cache_v, rope_tabs, emit_caches, with_state, lam, lam_init,
           norm_pre, w_in_bf, ssm_ops, subln, d_skip, w_glu_bf, b_glu, w_out_bf, norm_post):
    b, l, d = x.shape
    d_ssm = w_glu_bf.shape[0]
    d_attn = (w_in_bf.shape[1] - 2 * d_ssm) // 4
    shift, scale, gate = mod
    q, k, v, ga, u, gs, *caches = _inproj(x, shift, scale, norm_pre, w_in_bf, rope_tabs, emit_caches,
                                          d_attn, d_ssm)
    a_out = _attention(lam, q, k, v, cache_k, cache_v, ga, subln, lam_init)
    outs = _ssm(u, ssm_ops, h0_lanes, with_state)
    y = _outproj(x, a_out, outs[0], u, gs, gate, d_skip, w_glu_bf, b_glu, w_out_bf, norm_post)
    fin = outs[1] if with_state else None
    return y, caches, fin


def kernel(x_prompt, x_sample, cache_k, cache_v, state_ssm, c, c_ctx, w_ada, b_ada, norm_pre, norm_post, w_in, lambda_qk, subln, ssm_A_re, ssm_A_im, ssm_log_dt, ssm_B_re, ssm_B_im, ssm_C_re, ssm_C_im, ssm_D, w_glu, b_glu, w_out):
    xp, xs = x_prompt, x_sample
    bp, lp, d = xp.shape
    bs, ls, _ = xs.shape
    depth = w_in.shape[0]
    nd = ssm_A_re.shape[1]
    g, p = ssm_A_re.shape[-2:]
    d_ssm = w_glu.shape[-1]
    d_attn = (w_in.shape[-1] - 2 * d_ssm) // 4
    rope_tabs = _rope_tables(ls, d_attn)
    cond_rows = 16
    cond = jnp.zeros((cond_rows, d), F32).at[:bs].set(c.astype(F32)).at[bs].set(c_ctx.astype(F32))
    new_k, new_v, new_s = [], [], []
    for layer in range(depth):
        lam_init = 0.8 - 0.6 * math.exp(-0.3 * layer)
        m = _adaln(cond, w_ada[layer], b_ada[layer])
        mod_s = tuple(m[:bs, i * d:(i + 1) * d].reshape(bs, 1, d) for i in range(3))
        mod_p = tuple(m[bs:bs + 1, i * d:(i + 1) * d].reshape(1, 1, d) for i in range(3))
        ssm_ops, lam = _ssmprep(
            ssm_A_re[layer], ssm_A_im[layer], ssm_log_dt[layer], ssm_B_re[layer], ssm_B_im[layer],
            ssm_C_re[layer], ssm_C_im[layer], lambda_qk[layer], lam_init)
        shared = (lam, lam_init, norm_pre[layer], w_in[layer].astype(BF16), ssm_ops, subln[layer],
                  ssm_D[layer], w_glu[layer].astype(BF16), b_glu[layer], w_out[layer].astype(BF16),
                  norm_post[layer])

        h0_p = jnp.zeros((g, 2, bp, nd * p), F32)
        xp, (k_p, v_p), fin = _mixer(xp, mod_p, h0_p, None, None, None, True, True, *shared)
        new_k.append(k_p)
        new_v.append(v_p)
        new_s.append(_state_from_lanes(fin, nd).astype(xp.dtype))

        h0_s = _state_to_lanes(state_ssm[:, layer])
        xs, _, _ = _mixer(xs, mod_s, h0_s, cache_k[:, layer], cache_v[:, layer], rope_tabs, False, False,
                             *shared)

    return (xp, xs, jnp.stack(new_k, axis=1), jnp.stack(new_v, axis=1), jnp.stack(new_s, axis=1))
```

```python
import functools
import math

import jax
import jax.numpy as jnp
from jax import lax
from jax.experimental import pallas as pl
from jax.experimental.pallas import tpu as pltpu

F32 = jnp.float32
BF16 = jnp.bfloat16

GRID_W = 64
QK_DIM = 64
V_DIM = 2 * QK_DIM
ROPE_THETA = 10000.0
EPS = 1e-6
Q_PRESCALE = math.log2(math.e) * QK_DIM ** -0.5

SUM_ROWS = 16
TOKEN_TILE = 1024
ATTN_Q_TILE = 1024
ATTN_Q_SUB = 512
SSM_CHUNK = 16
SCAN_SLAB = 128
VMEM_LIMIT = 48 * 1024 * 1024


def _silu(x):
    return x * jax.nn.sigmoid(x)


def _params(*sem):
    return pltpu.CompilerParams(dimension_semantics=sem, vmem_limit_bytes=VMEM_LIMIT)


def _adaln_kernel(c_ref, w_ref, b_ref, o_ref):
    s = _silu(c_ref[...]).astype(BF16)
    o_ref[...] = jnp.dot(s, w_ref[...].astype(BF16), preferred_element_type=F32) + b_ref[...]


def _adaln(cond, w_ada, b_ada):
    rows, d = cond.shape
    n = w_ada.shape[1]
    bn = 512
    return pl.pallas_call(
        _adaln_kernel,
        grid=(n // bn,),
        in_specs=[pl.BlockSpec((rows, d), lambda j: (0, 0)),
                  pl.BlockSpec((d, bn), lambda j: (0, j)),
                  pl.BlockSpec((1, bn), lambda j: (0, j))],
        out_specs=pl.BlockSpec((rows, bn), lambda j: (0, j)),
        out_shape=jax.ShapeDtypeStruct((rows, n), F32),
        compiler_params=_params("arbitrary"),
        name="adaln",
    )(cond, w_ada, b_ada.reshape(1, n))


def _ssmprep_kernel(are_ref, aim_ref, ldt_ref, bre_ref, bim_ref, cre_ref, cim_ref, lq_ref,
                    toept_ref, wt_ref, vt_ref, a_ref, lam_ref,
                    cplre_scr, cplimn_scr, w_scr, v_scr, toep_scr, *, lam_init, chunk):
    a_re = are_ref[...]
    a_im = aim_ref[...]
    dt = jnp.exp(ldt_ref[...])
    mag = jnp.exp(a_re * dt)
    ab_re = mag * jnp.cos(a_im * dt)
    ab_im = mag * jnp.sin(a_im * dt)
    nr, ni = ab_re - 1.0, ab_im
    den = a_re * a_re + a_im * a_im
    f_re = (nr * a_re + ni * a_im) / den
    f_im = (ni * a_re - nr * a_im) / den
    b_re = bre_ref[...]
    b_im = bim_ref[...]
    bb_re = f_re * b_re - f_im * b_im
    bb_im = f_re * b_im + f_im * b_re
    c_re = cre_ref[...]
    c_im = cim_ref[...]
    hc = b_re.shape[1]
    sw = a_re.shape[-1]
    hw = sw // 2
    lo, hi = slice(0, hw), slice(hw, sw)
    lo_im, hi_im = slice(sw, sw + hw), slice(sw + hw, 2 * sw)
    blk = lambda j: slice(j * hc, (j + 1) * hc)
    last = chunk - 1
    cplre_scr[...] = jnp.zeros_like(cplre_scr)
    cplimn_scr[...] = jnp.zeros_like(cplimn_scr)
    pw_re = jnp.ones_like(ab_re)
    pw_im = jnp.zeros_like(ab_im)
    for tau in range(chunk + 1):
        cp_re = c_re * pw_re - c_im * pw_im
        cp_imn = -(c_re * pw_im + c_im * pw_re)
        if tau < chunk:
            cplre_scr[:, blk(last + tau), lo] = cp_re[:, :, lo]
            cplre_scr[:, blk(last - tau), hi] = cp_re[:, :, hi]
            cplimn_scr[:, blk(last + tau), lo] = cp_imn[:, :, lo]
            cplimn_scr[:, blk(last - tau), hi] = cp_imn[:, :, hi]
            e_re = pw_re * bb_re - pw_im * bb_im
            e_im = pw_re * bb_im + pw_im * bb_re
            w_scr[:, blk(last - tau), lo] = e_re[:, :, lo]
            w_scr[:, blk(tau), hi] = e_re[:, :, hi]
            w_scr[:, blk(last - tau), lo_im] = e_im[:, :, lo]
            w_scr[:, blk(tau), hi_im] = e_im[:, :, hi]
        if tau >= 1:
            v_scr[:, blk(tau - 1), lo] = cp_re[:, :, lo]
            v_scr[:, blk(chunk - tau), hi] = cp_re[:, :, hi]
            v_scr[:, blk(tau - 1), lo_im] = cp_imn[:, :, lo]
            v_scr[:, blk(chunk - tau), hi_im] = cp_imn[:, :, hi]
        if tau < chunk:
            pw_re, pw_im = pw_re * ab_re - pw_im * ab_im, pw_re * ab_im + pw_im * ab_re
    a_ref[:, 0] = pw_re
    a_ref[:, 1] = pw_im
    dims = (((2,), (2,)), ((0,), (0,)))
    taps = (lax.dot_general(bb_re.astype(BF16), cplre_scr[...].astype(BF16), dims,
                            preferred_element_type=F32)
            + lax.dot_general(bb_im.astype(BF16), cplimn_scr[...].astype(BF16), dims,
                              preferred_element_type=F32))
    nk = chunk * hc
    for s in range(chunk):
        toep_scr[:, blk(s), :] = taps[:, :, (last - s) * hc:(last - s) * hc + nk]
    for g in range(toept_ref.shape[0]):
        toept_ref[g] = toep_scr[g].T.astype(toept_ref.dtype)
        wt_ref[g] = w_scr[g].T.astype(wt_ref.dtype)
    vt_ref[...] = v_scr[...].astype(vt_ref.dtype)
    lq = lq_ref[...]
    s01 = jnp.sum(lq[0:1] * lq[1:2], axis=-1, keepdims=True)
    s23 = jnp.sum(lq[2:3] * lq[3:4], axis=-1, keepdims=True)
    lam = jnp.exp(s01) - jnp.exp(s23) + lam_init
    lam_ref[...] = jnp.broadcast_to(lam, lam_ref.shape)


def _ssmprep(a_re, a_im, log_dt, b_re, b_im, c_re, c_im, lq, lam_init):
    nd, g, p = a_re.shape
    hc = b_re.shape[-1]
    t = SSM_CHUNK
    sw = nd * p
    gb = SCAN_SLAB // hc
    row = lambda a: a.astype(F32).transpose(1, 0, 2).reshape(g, 1, sw)
    ldt = jnp.broadcast_to(log_dt[..., None], (nd, g, p))
    bt = lambda a: a.astype(F32).transpose(1, 3, 0, 2).reshape(g, hc, sw)
    ct = lambda a: a.astype(F32).transpose(1, 2, 0, 3).reshape(g, hc, sw)
    sds = jax.ShapeDtypeStruct
    nk = t * hc
    blk = lambda shape: pl.BlockSpec((gb,) + shape[1:], lambda i: (i,) + (0,) * (len(shape) - 1))
    mat = (g, nk, 2 * sw)
    toept, wt, vt, a_t, lam = pl.pallas_call(
        functools.partial(_ssmprep_kernel, lam_init=lam_init, chunk=t),
        grid=(g // gb,),
        in_specs=[blk((g, 1, sw))] * 3 + [blk((g, hc, sw))] * 4 + [pl.BlockSpec(lq.shape, lambda i: (0, 0))],
        out_specs=[blk((g, nk, nk)), blk((g, 2 * sw, nk)), blk(mat), blk((g, 2, 1, sw)),
                   pl.BlockSpec((8, 128), lambda i: (0, 0))],
        out_shape=(sds((g, nk, nk), BF16), sds((g, 2 * sw, nk), BF16), sds(mat, BF16),
                   sds((g, 2, 1, sw), F32), sds((8, 128), F32)),
        scratch_shapes=[pltpu.VMEM((gb, 2 * nk, sw), F32), pltpu.VMEM((gb, 2 * nk, sw), F32),
                        pltpu.VMEM((gb, nk, 2 * sw), F32), pltpu.VMEM((gb, nk, 2 * sw), F32),
                        pltpu.VMEM((gb, nk, nk), F32)],
        compiler_params=_params("arbitrary"),
        name="ssmprep",
    )(row(a_re), row(a_im), row(ldt), bt(b_re), bt(b_im), ct(c_re), ct(c_im), lq.astype(F32))
    return (toept, wt, vt, a_t), lam


def _to_scan_layout(u, o_ref, batch):
    n_slabs, t, _, cw = o_ref.shape
    tp = u.shape[0] // batch
    u_tb = jnp.swapaxes(u.reshape(batch, tp, u.shape[-1]), 0, 1)
    for c in range(tp // t):
        for s in range(t):
            for v in range(n_slabs):
                o_ref[v, s, c * batch:(c + 1) * batch, :] = u_tb[c * t + s][:, v * cw:(v + 1) * cw].astype(o_ref.dtype)


def _scan_block(ref):
    return jnp.concatenate([ref[v] for v in range(ref.shape[0])], axis=-1).astype(F32)


def _from_scan_layout(ref, batch):
    t, rows, d = ref.shape
    tp = rows // batch * t
    u_tb = jnp.stack([ref[s, c * batch:(c + 1) * batch, :] for c in range(tp // t) for s in range(t)], axis=0)
    return jnp.swapaxes(u_tb, 0, 1).reshape(batch * tp, d)


def _inproj_kernel(*refs, rope, emit_caches, d_attn):
    n_in = 7 if rope else 5
    x_ref, shift_ref, scale_ref, g_ref, w_ref = refs[:5]
    cos_ref, sin_ref = refs[5:7] if rope else (None, None)
    q_ref, k_ref, v_ref, ga_ref, u_ref, gs_ref = refs[n_in:n_in + 6]
    kc_ref, vc_ref = refs[n_in + 6:] if emit_caches else (None, None)
    x = x_ref[...]
    nb, tp, d = x.shape
    ms = jnp.mean(x * x, axis=-1, keepdims=True)
    h = x * lax.rsqrt(ms + EPS) * (g_ref[...] * (1.0 + scale_ref[...])) + shift_ref[...]
    hb = h.reshape(nb * tp, d).astype(BF16)
    proj = lambda lo, hi: jnp.dot(hb, w_ref[:, lo:hi], preferred_element_type=F32)
    da = d_attn
    d_ssm = u_ref.shape[0] * u_ref.shape[-1]
    tok = lambda z: z.reshape(nb, tp, z.shape[-1])
    q = proj(0, da)
    k = proj(da, 2 * da)
    if rope:
        lane = lax.broadcasted_iota(jnp.int32, (1, da), 1)
        low = (lane % (QK_DIM // 2)) < (QK_DIM // 4)
        cos = cos_ref[...]
        sin = sin_ref[...]

        def rot(z):
            partner = jnp.where(low, pltpu.roll(z, da - QK_DIM // 4, 1), pltpu.roll(z, QK_DIM // 4, 1))
            return tok(z) * cos + tok(partner) * sin

        q = rot(q)
        k = rot(k)
    else:
        q = tok(q)
        k = tok(k)
    q_ref[...] = (q * Q_PRESCALE).astype(q_ref.dtype)
    k_ref[...] = k.astype(k_ref.dtype)
    if emit_caches:
        for hd in range(kc_ref.shape[2]):
            kc_ref[:, :, hd, :] = k[:, :, hd * V_DIM:(hd + 1) * V_DIM]
    v = tok(proj(2 * da, 3 * da))
    v_ref[...] = v.astype(v_ref.dtype)
    if emit_caches:
        for hd in range(vc_ref.shape[2]):
            vc_ref[:, :, hd, :] = v[:, :, hd * V_DIM:(hd + 1) * V_DIM]
    ga_ref[...] = tok(_silu(proj(3 * da, 4 * da))).astype(ga_ref.dtype)
    gs_ref[...] = tok(_silu(proj(4 * da + d_ssm, 4 * da + 2 * d_ssm))).astype(gs_ref.dtype)
    _to_scan_layout(proj(4 * da, 4 * da + d_ssm), u_ref, nb)


def _inproj(x, shift, scale, norm_pre, w_in_bf, rope_tabs, emit_caches, d_attn, d_ssm):
    b, l, d = x.shape
    tp = TOKEN_TILE // b
    t = SSM_CHUNK
    rope = rope_tabs is not None
    const = lambda a: pl.BlockSpec(a.shape, lambda j: (0,) * a.ndim)
    tok = lambda w: pl.BlockSpec((b, tp, w), lambda j: (0, j, 0))
    g2 = norm_pre.reshape(1, d).astype(F32)
    in_specs = [tok(d), const(shift), const(scale), const(g2), const(w_in_bf)]
    args = [x, shift, scale, g2, w_in_bf]
    if rope:
        in_specs += [pl.BlockSpec((tp, d_attn), lambda j: (j, 0))] * 2
        args += list(rope_tabs)
    scan_rows = tp // t * b
    attn_shape = jax.ShapeDtypeStruct((b, l, d_attn), BF16)
    out_specs = [tok(d_attn), tok(d_attn), tok(d_attn), tok(d_attn),
                 pl.BlockSpec((d_ssm // SCAN_SLAB, t, scan_rows, SCAN_SLAB), lambda j: (0, 0, j, 0)),
                 tok(d_ssm)]
    out_shape = [attn_shape, attn_shape, attn_shape, attn_shape,
                 jax.ShapeDtypeStruct((d_ssm // SCAN_SLAB, t, l // t * b, SCAN_SLAB), BF16),
                 jax.ShapeDtypeStruct((b, l, d_ssm), BF16)]
    if emit_caches:
        out_specs += [pl.BlockSpec((b, tp, d_attn // V_DIM, V_DIM), lambda j: (0, j, 0, 0))] * 2
        out_shape += [jax.ShapeDtypeStruct((b, l, d_attn // V_DIM, V_DIM), F32)] * 2
    return pl.pallas_call(
        functools.partial(_inproj_kernel, rope=rope, emit_caches=emit_caches, d_attn=d_attn),
        grid=(l // tp,),
        in_specs=in_specs,
        out_specs=out_specs,
        out_shape=out_shape,
        compiler_params=_params("arbitrary"),
        name="inproj_rope" if rope else "inproj",
    )(*args)


def _rope_tables(l, d_attn):
    rows = l // GRID_W
    row = jnp.repeat(jnp.arange(rows, dtype=F32), GRID_W)
    col = jnp.tile(jnp.arange(GRID_W, dtype=F32), rows)
    n_freq = QK_DIM // 4
    inv = ROPE_THETA ** (-jnp.arange(n_freq, dtype=F32) / n_freq)
    ang_r = row[:, None] * inv
    ang_c = col[:, None] * inv
    cos64 = jnp.concatenate([jnp.cos(ang_r), jnp.cos(ang_r), jnp.cos(ang_c), jnp.cos(ang_c)], axis=-1)
    sin64 = jnp.concatenate([-jnp.sin(ang_r), jnp.sin(ang_r), -jnp.sin(ang_c), jnp.sin(ang_c)], axis=-1)
    reps = d_attn // QK_DIM
    return jnp.tile(cos64, (1, reps)), jnp.tile(sin64, (1, reps))


def _head(ref, h):
    return ref[:, h, :] if len(ref.shape) == 3 else ref[:, h * V_DIM:(h + 1) * V_DIM]


def _attn_kernel(*refs, has_cache, lam_init):
    if has_cache:
        lam_ref, q_ref, k_ref, v_ref, ck_ref, cv_ref, ga_ref, sub_ref, o_ref, k_scr, vt_scr = refs
    else:
        lam_ref, q_ref, k_ref, v_ref, ga_ref, sub_ref, o_ref, k_scr, vt_scr = refs
    n_heads, lk = k_scr.shape[0], k_scr.shape[1]
    ln = k_ref.shape[0]
    lam = lam_ref[0:1, 0:1]
    lane = lax.broadcasted_iota(jnp.int32, (1, V_DIM), 1)
    first = lane < QK_DIM
    dims = (((1,), (1,)), ((), ()))

    @pl.when(pl.program_id(1) == 0)
    def _():
        for h in range(n_heads):
            if has_cache:
                k_scr[h, 0:lk - ln, :] = _head(ck_ref, h).astype(BF16)
                vt_scr[h, 0:V_DIM, 0:lk - ln] = _head(cv_ref, h).astype(F32).T.astype(BF16)
            k_scr[h, lk - ln:lk, :] = _head(k_ref, h).astype(BF16)
            vt_scr[h, 0:V_DIM, lk - ln:lk] = _head(v_ref, h).astype(F32).T.astype(BF16)
            vt_scr[h, V_DIM:, :] = jnp.ones((vt_scr.shape[1] - V_DIM, lk), BF16)

    tq = q_ref.shape[0]
    sub = min(tq, ATTN_Q_SUB)
    items = [(h, r) for h in range(n_heads) for r in range(0, tq, sub)]

    def scores(item):
        h, r = item
        q = _head(q_ref, h)[r:r + sub, :]
        zero = jnp.zeros_like(q)
        k = k_scr[h]
        return (lax.dot_general(k, jnp.where(first, q, zero), dims, preferred_element_type=F32),
                lax.dot_general(k, jnp.where(first, zero, q), dims, preferred_element_type=F32))

    nxt = scores(items[0])
    for n, (h, r) in enumerate(items):
        st1, st2 = nxt
        if n + 1 < len(items):
            nxt = scores(items[n + 1])
        vt = vt_scr[h]
        et1 = jnp.exp2(st1 - jnp.max(st1, axis=0, keepdims=True)).astype(BF16)
        et2 = jnp.exp2(st2 - jnp.max(st2, axis=0, keepdims=True)).astype(BF16)
        o1 = jnp.dot(vt, et1, preferred_element_type=F32)
        o2 = jnp.dot(vt, et2, preferred_element_type=F32)
        r1 = 1.0 / o1[V_DIM:V_DIM + 1, :]
        r2 = lam / o2[V_DIM:V_DIM + 1, :]
        o = (o1[0:V_DIM, :] * r1 - o2[0:V_DIM, :] * r2).T
        ms = jnp.mean(o * o, axis=-1, keepdims=True)
        o = o * lax.rsqrt(ms + EPS) * sub_ref[...] * (1.0 - lam_init)
        ga = _head(ga_ref, h)[r:r + sub, :].astype(F32)
        o_ref[r:r + sub, h * V_DIM:(h + 1) * V_DIM] = (o * ga).astype(o_ref.dtype)


def _attention(lam, q, k, v, cache_k, cache_v, ga, subln, lam_init):
    b, l, d_attn = q.shape
    n_heads = d_attn // V_DIM
    tq = min(ATTN_Q_TILE, l)
    nq = l // tq
    has_cache = cache_k is not None
    q_spec = pl.BlockSpec((None, tq, d_attn), lambda i, j: (i, j, 0))
    whole = lambda a: pl.BlockSpec((None,) + a.shape[1:], lambda i, j: (i,) + (0,) * (a.ndim - 1))
    in_specs = [pl.BlockSpec((8, 128), lambda i, j: (0, 0)), q_spec, whole(k), whole(v)]
    args = [lam, q, k, v]
    if has_cache:
        in_specs += [whole(cache_k), whole(cache_v)]
        args += [cache_k, cache_v]
    in_specs += [q_spec, pl.BlockSpec((1, V_DIM), lambda i, j: (0, 0))]
    args += [ga, subln.reshape(1, V_DIM).astype(F32)]
    lk = l + (cache_k.shape[1] if has_cache else 0)
    out = pl.pallas_call(
        functools.partial(_attn_kernel, has_cache=has_cache, lam_init=lam_init),
        grid=(b, nq),
        in_specs=in_specs,
        out_specs=q_spec,
        out_shape=jax.ShapeDtypeStruct((b, l, d_attn), BF16),
        scratch_shapes=[pltpu.VMEM((n_heads, lk, V_DIM), BF16),
                        pltpu.VMEM((n_heads, V_DIM + SUM_ROWS, lk), BF16)],
        compiler_params=_params("arbitrary", "arbitrary"),
        name="attn_cache" if has_cache else "attn",
    )(*args)
    return out


def _ssm_kernel(*refs, batch, n_chunks, with_state):
    if with_state:
        u_ref, toept_ref, wt_ref, vt_ref, a_ref, h0_ref, y_ref, fin_ref, zt_scr, wc_scr, sin_scr, yt_scr = refs
    else:
        u_ref, toept_ref, wt_ref, vt_ref, a_ref, h0_ref, y_ref, zt_scr, wc_scr, sin_scr, yt_scr = refs
    t = u_ref.shape[0]
    gb = zt_scr.shape[0]
    hc = zt_scr.shape[1] // t
    sw = a_ref.shape[-1]
    hw = sw // 2
    for s in range(t):
        at = u_ref[s].T
        for g in range(gb):
            zt_scr[g, s * hc:(s + 1) * hc, :] = at[g * hc:(g + 1) * hc, :]
    for g in range(gb):
        wc_scr[g] = jnp.dot(wt_ref[g], zt_scr[g], preferred_element_type=F32).T
    fwd = lax.broadcasted_iota(jnp.int32, (1, sw), 1) < hw

    def step(j, carry):
        rf = pl.ds(pl.multiple_of(j * batch, batch), batch)
        rb = pl.ds(pl.multiple_of((n_chunks - 1 - j) * batch, batch), batch)
        out = []
        for g in range(gb):
            sr, si = carry[g]
            sin_scr[g, rf, 0:hw] = sr[:, 0:hw]
            sin_scr[g, rb, hw:sw] = sr[:, hw:sw]
            sin_scr[g, rf, sw:sw + hw] = si[:, 0:hw]
            sin_scr[g, rb, sw + hw:2 * sw] = si[:, hw:sw]
            wr = jnp.where(fwd, wc_scr[g, rf, 0:sw], wc_scr[g, rb, 0:sw])
            wi = jnp.where(fwd, wc_scr[g, rf, sw:2 * sw], wc_scr[g, rb, sw:2 * sw])
            ar = a_ref[g, 0]
            ai = a_ref[g, 1]
            out.append((ar * sr - ai * si + wr, ar * si + ai * sr + wi))
        return tuple(out)

    fin = lax.fori_loop(0, n_chunks, step, tuple((h0_ref[g, 0], h0_ref[g, 1]) for g in range(gb)))
    for g in range(gb):
        sint = sin_scr[g].T.astype(BF16)
        y2t = (jnp.dot(toept_ref[g], zt_scr[g], preferred_element_type=F32)
               + jnp.dot(vt_ref[g], sint, preferred_element_type=F32))
        for s in range(t):
            yt_scr[s, g * hc:(g + 1) * hc, :] = y2t[s * hc:(s + 1) * hc, :]
        if with_state:
            fin_ref[g, 0] = fin[g][0]
            fin_ref[g, 1] = fin[g][1]
    for s in range(t):
        y_ref[s] = yt_scr[s].T.astype(y_ref.dtype)


def _ssm(u3, ssm_ops, h0, with_state):
    toept, wt, vt, a_t = ssm_ops
    n_slabs, t, rows, cw = u3.shape
    batch = h0.shape[2]
    n_chunks = rows // batch
    g = toept.shape[0]
    gb = g // n_slabs
    slab = pl.BlockSpec((None, t, rows, cw), lambda i: (i, 0, 0, 0))
    blk = lambda a: pl.BlockSpec((gb,) + a.shape[1:], lambda i: (i,) + (0,) * (a.ndim - 1))
    out_specs = [slab]
    out_shape = [jax.ShapeDtypeStruct(u3.shape, u3.dtype)]
    if with_state:
        out_specs.append(blk(h0))
        out_shape.append(jax.ShapeDtypeStruct(h0.shape, F32))
    nk = toept.shape[-1]
    outs = pl.pallas_call(
        functools.partial(_ssm_kernel, batch=batch, n_chunks=n_chunks, with_state=with_state),
        grid=(g // gb,),
        in_specs=[slab, blk(toept), blk(wt), blk(vt), blk(a_t), blk(h0)],
        out_specs=out_specs,
        out_shape=out_shape,
        scratch_shapes=[pltpu.VMEM((gb, nk, rows), BF16),
                        pltpu.VMEM((gb, rows, wt.shape[1]), F32),
                        pltpu.VMEM((gb, rows, wt.shape[1]), F32),
                        pltpu.VMEM((t, cw, rows), F32)],
        compiler_params=_params("arbitrary"),
        name="ssm_state" if with_state else "ssm",
    )(u3, toept, wt, vt, a_t, h0)
    return outs


def _state_to_lanes(h0):
    b, nd, nc, g, p = h0.shape
    return h0.astype(F32).transpose(3, 2, 0, 1, 4).reshape(g, nc, b, nd * p)


def _state_from_lanes(fin, nd):
    g, nc, b, w = fin.shape
    return fin.reshape(g, nc, b, nd, w // nd).transpose(2, 3, 1, 0, 4)


def _outproj_kernel(x_ref, a_ref, y_ref, u_ref, gs_ref, gate_ref, dskip_ref,
                    wglu_ref, bglu_ref, wout_ref, g_ref, o_ref):
    nb, tp, d = x_ref.shape
    y = _scan_block(y_ref) + dskip_ref[...] * _scan_block(u_ref)
    t, srows, d_ssm = y.shape
    rows = lambda ref: ref[...].reshape(nb * tp, ref.shape[-1])
    y = y.reshape(t * srows, d_ssm)
    c0 = math.sqrt(2.0 / math.pi)
    hy = 0.5 * y
    ys = hy + hy * jnp.tanh(y * (c0 + (c0 * 0.044715) * (y * y)))
    z = jnp.dot(ys.astype(BF16), wglu_ref[...], preferred_element_type=F32) + bglu_ref[...]
    ys = _from_scan_layout((ys * jax.nn.sigmoid(z)).reshape(t, srows, d_ssm), nb) * rows(gs_ref).astype(F32)
    cat = jnp.concatenate([rows(a_ref), ys.astype(BF16)], axis=-1)
    out = jnp.dot(cat, wout_ref[...], preferred_element_type=F32)
    ms = jnp.mean(out * out, axis=-1, keepdims=True)
    out = out * lax.rsqrt(ms + EPS)
    o_ref[...] = x_ref[...] + (g_ref[...] * gate_ref[...]) * out.reshape(nb, tp, d)


def _outproj(x, a_out, y_ssm, u, gs, gate, d_skip, w_glu_bf, b_glu, w_out_bf, norm_post):
    b, l, d = x.shape
    d_attn = a_out.shape[-1]
    d_ssm = gs.shape[-1]
    tp = TOKEN_TILE // b
    t = SSM_CHUNK
    const = lambda a: pl.BlockSpec(a.shape, lambda j: (0,) * a.ndim)
    tok = lambda w: pl.BlockSpec((b, tp, w), lambda j: (0, j, 0))
    scan = pl.BlockSpec((d_ssm // SCAN_SLAB, t, tp // t * b, SCAN_SLAB), lambda j: (0, 0, j, 0))
    consts = [gate, d_skip.reshape(1, d_ssm).astype(F32), w_glu_bf, b_glu.reshape(1, d_ssm).astype(F32),
              w_out_bf, norm_post.reshape(1, d).astype(F32)]
    return pl.pallas_call(
        _outproj_kernel,
        grid=(l // tp,),
        in_specs=[tok(d), tok(d_attn), scan, scan, tok(d_ssm)] + [const(a) for a in consts],
        out_specs=tok(d),
        out_shape=jax.ShapeDtypeStruct((b, l, d), F32),
        compiler_params=_params("arbitrary"),
        name="outproj",
    )(x, a_out, y_ssm, u, gs, *consts)


def _mixer(x, mod, h0_lanes, cache_k, cache_v, rope_tabs, emit_caches, with_state, lam, lam_init,
           norm_pre, w_in_bf, ssm_ops, subln, d_skip, w_glu_bf, b_glu, w_out_bf, norm_post):
    b, l, d = x.shape
    d_ssm = w_glu_bf.shape[0]
    d_attn = (w_in_bf.shape[1] - 2 * d_ssm) // 4
    shift, scale, gate = mod
    q, k, v, ga, u, gs, *caches = _inproj(x, shift, scale, norm_pre, w_in_bf, rope_tabs, emit_caches,
                                          d_attn, d_ssm)
    a_out = _attention(lam, q, k, v, cache_k, cache_v, ga, subln, lam_init)
    outs = _ssm(u, ssm_ops, h0_lanes, with_state)
    y = _outproj(x, a_out, outs[0], u, gs, gate, d_skip, w_glu_bf, b_glu, w_out_bf, norm_post)
    fin = outs[1] if with_state else None
    return y, caches, fin


def kernel(x_prompt, x_sample, cache_k, cache_v, state_ssm, c, c_ctx, w_ada, b_ada, norm_pre, norm_post, w_in, lambda_qk, subln, ssm_A_re, ssm_A_im, ssm_log_dt, ssm_B_re, ssm_B_im, ssm_C_re, ssm_C_im, ssm_D, w_glu, b_glu, w_out):
    xp, xs = x_prompt, x_sample
    bp, lp, d = xp.shape
    bs, ls, _ = xs.shape
    depth = w_in.shape[0]
    nd = ssm_A_re.shape[1]
    g, p = ssm_A_re.shape[-2:]
    d_ssm = w_glu.shape[-1]
    d_attn = (w_in.shape[-1] - 2 * d_ssm) // 4
    rope_tabs = _rope_tables(ls, d_attn)
    cond_rows = 16
    cond = jnp.zeros((cond_rows, d), F32).at[:bs].set(c.astype(F32)).at[bs].set(c_ctx.astype(F32))
    new_k, new_v, new_s = [], [], []
    for layer in range(depth):
        lam_init = 0.8 - 0.6 * math.exp(-0.3 * layer)
        m = _adaln(cond, w_ada[layer], b_ada[layer])
        mod_s = tuple(m[:bs, i * d:(i + 1) * d].reshape(bs, 1, d) for i in range(3))
        mod_p = tuple(m[bs:bs + 1, i * d:(i + 1) * d].reshape(1, 1, d) for i in range(3))
        ssm_ops, lam = _ssmprep(
            ssm_A_re[layer], ssm_A_im[layer], ssm_log_dt[layer], ssm_B_re[layer], ssm_B_im[layer],
            ssm_C_re[layer], ssm_C_im[layer], lambda_qk[layer], lam_init)
        shared = (lam, lam_init, norm_pre[layer], w_in[layer].astype(BF16), ssm_ops, subln[layer],
                  ssm_D[layer], w_glu[layer].astype(BF16), b_glu[layer], w_out[layer].astype(BF16),
                  norm_post[layer])

        h0_p = jnp.zeros((g, 2, bp, nd * p), F32)
        xp, (k_p, v_p), fin = _mixer(xp, mod_p, h0_p, None, None, None, True, True, *shared)
        new_k.append(k_p)
        new_v.append(v_p)
        new_s.append(_state_from_lanes(fin, nd).astype(xp.dtype))

        h0_s = _state_to_lanes(state_ssm[:, layer])
        xs, _, _ = _mixer(xs, mod_s, h0_s, cache_k[:, layer], cache_v[:, layer], rope_tabs, False, False,
                             *shared)

    return (xp, xs, jnp.stack(new_k, axis=1), jnp.stack(new_v, axis=1), jnp.stack(new_s, axis=1))
```

```python
import functools
import math

import jax
import jax.numpy as jnp
import numpy as np
from jax import lax
from jax.experimental import pallas as pl
from jax.experimental.pallas import tpu as pltpu

F32 = jnp.float32
BF16 = jnp.bfloat16

GRID_W = 64
QK_DIM = 64
V_DIM = 2 * QK_DIM
ROPE_THETA = 10000.0
EPS = 1e-6
Q_PRESCALE = math.log2(math.e) * QK_DIM ** -0.5

SUM_ROWS = 16
TOKEN_TILE = 1024
ATTN_Q_TILE = 1024
ATTN_Q_SUB = 512
SSM_CHUNK = 16
SCAN_SLAB = 128
VMEM_LIMIT = 48 * 1024 * 1024


def _silu(x):
    return x * jax.nn.sigmoid(x)


def _params(*sem):
    return pltpu.CompilerParams(dimension_semantics=sem, vmem_limit_bytes=VMEM_LIMIT)


def _adaln_kernel(c_ref, w_ref, b_ref, o_ref):
    s = _silu(c_ref[...]).astype(BF16)
    o_ref[...] = jnp.dot(s, w_ref[...].astype(BF16), preferred_element_type=F32) + b_ref[...]


def _adaln(cond, w_ada, b_ada):
    rows, d = cond.shape
    n = w_ada.shape[1]
    bn = 512
    return pl.pallas_call(
        _adaln_kernel,
        grid=(n // bn,),
        in_specs=[pl.BlockSpec((rows, d), lambda j: (0, 0)),
                  pl.BlockSpec((d, bn), lambda j: (0, j)),
                  pl.BlockSpec((1, bn), lambda j: (0, j))],
        out_specs=pl.BlockSpec((rows, bn), lambda j: (0, j)),
        out_shape=jax.ShapeDtypeStruct((rows, n), F32),
        compiler_params=_params("arbitrary"),
        name="adaln",
    )(cond, w_ada, b_ada.reshape(1, n))


def _ssmprep_kernel(are_ref, aim_ref, ldt_ref, bre_ref, bim_ref, cre_ref, cim_ref, lq_ref,
                    toept_ref, wt_ref, vt_ref, a_ref, lam_ref,
                    cplre_scr, cplimn_scr, w_scr, v_scr, toep_scr, *, lam_init, chunk):
    a_re = are_ref[...]
    a_im = aim_ref[...]
    dt = jnp.exp(ldt_ref[...])
    mag = jnp.exp(a_re * dt)
    ab_re = mag * jnp.cos(a_im * dt)
    ab_im = mag * jnp.sin(a_im * dt)
    nr, ni = ab_re - 1.0, ab_im
    den = a_re * a_re + a_im * a_im
    f_re = (nr * a_re + ni * a_im) / den
    f_im = (ni * a_re - nr * a_im) / den
    b_re = bre_ref[...]
    b_im = bim_ref[...]
    bb_re = f_re * b_re - f_im * b_im
    bb_im = f_re * b_im + f_im * b_re
    c_re = cre_ref[...]
    c_im = cim_ref[...]
    hc = b_re.shape[1]
    sw = a_re.shape[-1]
    hw = sw // 2
    lo, hi = slice(0, hw), slice(hw, sw)
    lo_im, hi_im = slice(sw, sw + hw), slice(sw + hw, 2 * sw)
    blk = lambda j: slice(j * hc, (j + 1) * hc)
    last = chunk - 1
    cplre_scr[...] = jnp.zeros_like(cplre_scr)
    cplimn_scr[...] = jnp.zeros_like(cplimn_scr)
    pw_re = jnp.ones_like(ab_re)
    pw_im = jnp.zeros_like(ab_im)
    for tau in range(chunk + 1):
        cp_re = c_re * pw_re - c_im * pw_im
        cp_imn = -(c_re * pw_im + c_im * pw_re)
        if tau < chunk:
            cplre_scr[:, blk(last + tau), lo] = cp_re[:, :, lo]
            cplre_scr[:, blk(last - tau), hi] = cp_re[:, :, hi]
            cplimn_scr[:, blk(last + tau), lo] = cp_imn[:, :, lo]
            cplimn_scr[:, blk(last - tau), hi] = cp_imn[:, :, hi]
            e_re = pw_re * bb_re - pw_im * bb_im
            e_im = pw_re * bb_im + pw_im * bb_re
            w_scr[:, blk(last - tau), lo] = e_re[:, :, lo]
            w_scr[:, blk(tau), hi] = e_re[:, :, hi]
            w_scr[:, blk(last - tau), lo_im] = e_im[:, :, lo]
            w_scr[:, blk(tau), hi_im] = e_im[:, :, hi]
        if tau >= 1:
            v_scr[:, blk(tau - 1), lo] = cp_re[:, :, lo]
            v_scr[:, blk(chunk - tau), hi] = cp_re[:, :, hi]
            v_scr[:, blk(tau - 1), lo_im] = cp_imn[:, :, lo]
            v_scr[:, blk(chunk - tau), hi_im] = cp_imn[:, :, hi]
        if tau < chunk:
            pw_re, pw_im = pw_re * ab_re - pw_im * ab_im, pw_re * ab_im + pw_im * ab_re
    a_ref[:, 0] = pw_re
    a_ref[:, 1] = pw_im
    dims = (((2,), (2,)), ((0,), (0,)))
    taps = (lax.dot_general(bb_re.astype(BF16), cplre_scr[...].astype(BF16), dims,
                            preferred_element_type=F32)
            + lax.dot_general(bb_im.astype(BF16), cplimn_scr[...].astype(BF16), dims,
                              preferred_element_type=F32))
    nk = chunk * hc
    for s in range(chunk):
        toep_scr[:, blk(s), :] = taps[:, :, (last - s) * hc:(last - s) * hc + nk]
    for g in range(toept_ref.shape[0]):
        toept_ref[g] = toep_scr[g].T.astype(toept_ref.dtype)
        wt_ref[g] = w_scr[g].T.astype(wt_ref.dtype)
    vt_ref[...] = v_scr[...].astype(vt_ref.dtype)
    lq = lq_ref[...]
    s01 = jnp.sum(lq[0:1] * lq[1:2], axis=-1, keepdims=True)
    s23 = jnp.sum(lq[2:3] * lq[3:4], axis=-1, keepdims=True)
    lam = jnp.exp(s01) - jnp.exp(s23) + lam_init
    lam_ref[...] = jnp.broadcast_to(lam, lam_ref.shape)


def _ssmprep(a_re, a_im, log_dt, b_re, b_im, c_re, c_im, lq, lam_init):
    nd, g, p = a_re.shape
    hc = b_re.shape[-1]
    t = SSM_CHUNK
    sw = nd * p
    gb = SCAN_SLAB // hc
    row = lambda a: a.astype(F32).transpose(1, 0, 2).reshape(g, 1, sw)
    ldt = jnp.broadcast_to(log_dt[..., None], (nd, g, p))
    bt = lambda a: a.astype(F32).transpose(1, 3, 0, 2).reshape(g, hc, sw)
    ct = lambda a: a.astype(F32).transpose(1, 2, 0, 3).reshape(g, hc, sw)
    sds = jax.ShapeDtypeStruct
    nk = t * hc
    blk = lambda shape: pl.BlockSpec((gb,) + shape[1:], lambda i: (i,) + (0,) * (len(shape) - 1))
    mat = (g, nk, 2 * sw)
    toept, wt, vt, a_t, lam = pl.pallas_call(
        functools.partial(_ssmprep_kernel, lam_init=lam_init, chunk=t),
        grid=(g // gb,),
        in_specs=[blk((g, 1, sw))] * 3 + [blk((g, hc, sw))] * 4 + [pl.BlockSpec(lq.shape, lambda i: (0, 0))],
        out_specs=[blk((g, nk, nk)), blk((g, 2 * sw, nk)), blk(mat), blk((g, 2, 1, sw)),
                   pl.BlockSpec((8, 128), lambda i: (0, 0))],
        out_shape=(sds((g, nk, nk), BF16), sds((g, 2 * sw, nk), BF16), sds(mat, BF16),
                   sds((g, 2, 1, sw), F32), sds((8, 128), F32)),
        scratch_shapes=[pltpu.VMEM((gb, 2 * nk, sw), F32), pltpu.VMEM((gb, 2 * nk, sw), F32),
                        pltpu.VMEM((gb, nk, 2 * sw), F32), pltpu.VMEM((gb, nk, 2 * sw), F32),
                        pltpu.VMEM((gb, nk, nk), F32)],
        compiler_params=_params("arbitrary"),
        name="ssmprep",
    )(row(a_re), row(a_im), row(ldt), bt(b_re), bt(b_im), ct(c_re), ct(c_im), lq.astype(F32))
    return (toept, wt, vt, a_t), lam


def _to_scan_layout(u, o_ref, batch):
    n_slabs, t, _, cw = o_ref.shape
    tp = u.shape[0] // batch
    u_tb = jnp.swapaxes(u.reshape(batch, tp, u.shape[-1]), 0, 1)
    for c in range(tp // t):
        for s in range(t):
            for v in range(n_slabs):
                o_ref[v, s, c * batch:(c + 1) * batch, :] = u_tb[c * t + s][:, v * cw:(v + 1) * cw].astype(o_ref.dtype)


def _scan_block(ref):
    return jnp.concatenate([ref[v] for v in range(ref.shape[0])], axis=-1).astype(F32)


def _from_scan_layout(ref, batch):
    t, rows, d = ref.shape
    tp = rows // batch * t
    u_tb = jnp.stack([ref[s, c * batch:(c + 1) * batch, :] for c in range(tp // t) for s in range(t)], axis=0)
    return jnp.swapaxes(u_tb, 0, 1).reshape(batch * tp, d)


def _inproj_kernel(*refs, rope, emit_caches, d_attn):
    n_in = 7 if rope else 5
    x_ref, shift_ref, scale_ref, g_ref, w_ref = refs[:5]
    cos_ref, sin_ref = refs[5:7] if rope else (None, None)
    q_ref, k_ref, v_ref, ga_ref, u_ref, gs_ref = refs[n_in:n_in + 6]
    kc_ref, vc_ref = refs[n_in + 6:] if emit_caches else (None, None)
    x = x_ref[...]
    nb, tp, d = x.shape
    ms = jnp.mean(x * x, axis=-1, keepdims=True)
    h = x * lax.rsqrt(ms + EPS) * (g_ref[...] * (1.0 + scale_ref[...])) + shift_ref[...]
    hb = h.reshape(nb * tp, d).astype(BF16)
    proj = lambda lo, hi: jnp.dot(hb, w_ref[:, lo:hi], preferred_element_type=F32)
    da = d_attn
    d_ssm = u_ref.shape[0] * u_ref.shape[-1]
    tok = lambda z: z.reshape(nb, tp, z.shape[-1])
    q = proj(0, da)
    k = proj(da, 2 * da)
    if rope:
        lane = lax.broadcasted_iota(jnp.int32, (1, da), 1)
        low = (lane % (QK_DIM // 2)) < (QK_DIM // 4)
        cos = cos_ref[...]
        sin = sin_ref[...]

        def rot(z):
            partner = jnp.where(low, pltpu.roll(z, da - QK_DIM // 4, 1), pltpu.roll(z, QK_DIM // 4, 1))
            return tok(z) * cos + tok(partner) * sin

        q = rot(q)
        k = rot(k)
    else:
        q = tok(q)
        k = tok(k)
    q_ref[...] = (q * Q_PRESCALE).astype(q_ref.dtype)
    k_ref[...] = k.astype(k_ref.dtype)
    if emit_caches:
        for hd in range(kc_ref.shape[2]):
            kc_ref[:, :, hd, :] = k[:, :, hd * V_DIM:(hd + 1) * V_DIM]
    v = tok(proj(2 * da, 3 * da))
    v_ref[...] = v.astype(v_ref.dtype)
    if emit_caches:
        for hd in range(vc_ref.shape[2]):
            vc_ref[:, :, hd, :] = v[:, :, hd * V_DIM:(hd + 1) * V_DIM]
    ga_ref[...] = tok(_silu(proj(3 * da, 4 * da))).astype(ga_ref.dtype)
    gs_ref[...] = tok(_silu(proj(4 * da + d_ssm, 4 * da + 2 * d_ssm))).astype(gs_ref.dtype)
    _to_scan_layout(proj(4 * da, 4 * da + d_ssm), u_ref, nb)


def _inproj(x, shift, scale, norm_pre, w_in_bf, rope_tabs, emit_caches, d_attn, d_ssm):
    b, l, d = x.shape
    tp = TOKEN_TILE // b
    t = SSM_CHUNK
    rope = rope_tabs is not None
    const = lambda a: pl.BlockSpec(a.shape, lambda j: (0,) * a.ndim)
    tok = lambda w: pl.BlockSpec((b, tp, w), lambda j: (0, j, 0))
    g2 = norm_pre.reshape(1, d).astype(F32)
    in_specs = [tok(d), const(shift), const(scale), const(g2), const(w_in_bf)]
    args = [x, shift, scale, g2, w_in_bf]
    if rope:
        in_specs += [pl.BlockSpec((tp, d_attn), lambda j: (j, 0))] * 2
        args += list(rope_tabs)
    scan_rows = tp // t * b
    attn_shape = jax.ShapeDtypeStruct((b, l, d_attn), BF16)
    out_specs = [tok(d_attn), tok(d_attn), tok(d_attn), tok(d_attn),
                 pl.BlockSpec((d_ssm // SCAN_SLAB, t, scan_rows, SCAN_SLAB), lambda j: (0, 0, j, 0)),
                 tok(d_ssm)]
    out_shape = [attn_shape, attn_shape, attn_shape, attn_shape,
                 jax.ShapeDtypeStruct((d_ssm // SCAN_SLAB, t, l // t * b, SCAN_SLAB), BF16),
                 jax.ShapeDtypeStruct((b, l, d_ssm), BF16)]
    if emit_caches:
        out_specs += [pl.BlockSpec((b, tp, d_attn // V_DIM, V_DIM), lambda j: (0, j, 0, 0))] * 2
        out_shape += [jax.ShapeDtypeStruct((b, l, d_attn // V_DIM, V_DIM), F32)] * 2
    return pl.pallas_call(
        functools.partial(_inproj_kernel, rope=rope, emit_caches=emit_caches, d_attn=d_attn),
        grid=(l // tp,),
        in_specs=in_specs,
        out_specs=out_specs,
        out_shape=out_shape,
        compiler_params=_params("arbitrary"),
        name="inproj_rope" if rope else "inproj",
    )(*args)


def _rope_tables(l, d_attn):
    rows = l // GRID_W
    row = np.repeat(np.arange(rows, dtype=np.float64), GRID_W)
    col = np.tile(np.arange(GRID_W, dtype=np.float64), rows)
    n_freq = QK_DIM // 4
    inv = ROPE_THETA ** (-np.arange(n_freq, dtype=np.float64) / n_freq)
    ang_r = row[:, None] * inv
    ang_c = col[:, None] * inv
    cos64 = np.concatenate([np.cos(ang_r), np.cos(ang_r), np.cos(ang_c), np.cos(ang_c)], axis=-1)
    sin64 = np.concatenate([-np.sin(ang_r), np.sin(ang_r), -np.sin(ang_c), np.sin(ang_c)], axis=-1)
    reps = d_attn // QK_DIM
    return (jnp.asarray(np.tile(cos64, (1, reps)), dtype=F32),
            jnp.asarray(np.tile(sin64, (1, reps)), dtype=F32))


def _head(ref, h):
    return ref[:, h, :] if len(ref.shape) == 3 else ref[:, h * V_DIM:(h + 1) * V_DIM]


def _attn_kernel(*refs, has_cache, lam_init):
    if has_cache:
        lam_ref, q_ref, k_ref, v_ref, ck_ref, cv_ref, ga_ref, sub_ref, o_ref, k_scr, vt_scr = refs
    else:
        lam_ref, q_ref, k_ref, v_ref, ga_ref, sub_ref, o_ref, k_scr, vt_scr = refs
    n_heads, lk = k_scr.shape[0], k_scr.shape[1]
    ln = k_ref.shape[0]
    lam = lam_ref[0:1, 0:1]
    lane = lax.broadcasted_iota(jnp.int32, (1, V_DIM), 1)
    first = lane < QK_DIM
    dims = (((1,), (1,)), ((), ()))

    @pl.when(pl.program_id(1) == 0)
    def _():
        for h in range(n_heads):
            if has_cache:
                k_scr[h, 0:lk - ln, :] = _head(ck_ref, h).astype(BF16)
                vt_scr[h, 0:V_DIM, 0:lk - ln] = _head(cv_ref, h).astype(F32).T.astype(BF16)
            k_scr[h, lk - ln:lk, :] = _head(k_ref, h).astype(BF16)
            vt_scr[h, 0:V_DIM, lk - ln:lk] = _head(v_ref, h).astype(F32).T.astype(BF16)
            vt_scr[h, V_DIM:, :] = jnp.ones((vt_scr.shape[1] - V_DIM, lk), BF16)

    tq = q_ref.shape[0]
    sub = min(tq, ATTN_Q_SUB)
    items = [(h, r) for h in range(n_heads) for r in range(0, tq, sub)]

    def scores(item):
        h, r = item
        q = _head(q_ref, h)[r:r + sub, :]
        zero = jnp.zeros_like(q)
        k = k_scr[h]
        return (lax.dot_general(k, jnp.where(first, q, zero), dims, preferred_element_type=F32),
                lax.dot_general(k, jnp.where(first, zero, q), dims, preferred_element_type=F32))

    nxt = scores(items[0])
    for n, (h, r) in enumerate(items):
        st1, st2 = nxt
        if n + 1 < len(items):
            nxt = scores(items[n + 1])
        vt = vt_scr[h]
        et1 = jnp.exp2(st1 - jnp.max(st1, axis=0, keepdims=True)).astype(BF16)
        et2 = jnp.exp2(st2 - jnp.max(st2, axis=0, keepdims=True)).astype(BF16)
        o1 = jnp.dot(vt, et1, preferred_element_type=F32)
        o2 = jnp.dot(vt, et2, preferred_element_type=F32)
        r1 = 1.0 / o1[V_DIM:V_DIM + 1, :]
        r2 = lam / o2[V_DIM:V_DIM + 1, :]
        o = (o1[0:V_DIM, :] * r1 - o2[0:V_DIM, :] * r2).T
        ms = jnp.mean(o * o, axis=-1, keepdims=True)
        o = o * lax.rsqrt(ms + EPS) * sub_ref[...] * (1.0 - lam_init)
        ga = _head(ga_ref, h)[r:r + sub, :].astype(F32)
        o_ref[r:r + sub, h * V_DIM:(h + 1) * V_DIM] = (o * ga).astype(o_ref.dtype)


def _attention(lam, q, k, v, cache_k, cache_v, ga, subln, lam_init):
    b, l, d_attn = q.shape
    n_heads = d_attn // V_DIM
    tq = min(ATTN_Q_TILE, l)
    nq = l // tq
    has_cache = cache_k is not None
    q_spec = pl.BlockSpec((None, tq, d_attn), lambda i, j: (i, j, 0))
    whole = lambda a: pl.BlockSpec((None,) + a.shape[1:], lambda i, j: (i,) + (0,) * (a.ndim - 1))
    in_specs = [pl.BlockSpec((8, 128), lambda i, j: (0, 0)), q_spec, whole(k), whole(v)]
    args = [lam, q, k, v]
    if has_cache:
        in_specs += [whole(cache_k), whole(cache_v)]
        args += [cache_k, cache_v]
    in_specs += [q_spec, pl.BlockSpec((1, V_DIM), lambda i, j: (0, 0))]
    args += [ga, subln.reshape(1, V_DIM).astype(F32)]
    lk = l + (cache_k.shape[1] if has_cache else 0)
    out = pl.pallas_call(
        functools.partial(_attn_kernel, has_cache=has_cache, lam_init=lam_init),
        grid=(b, nq),
        in_specs=in_specs,
        out_specs=q_spec,
        out_shape=jax.ShapeDtypeStruct((b, l, d_attn), BF16),
        scratch_shapes=[pltpu.VMEM((n_heads, lk, V_DIM), BF16),
                        pltpu.VMEM((n_heads, V_DIM + SUM_ROWS, lk), BF16)],
        compiler_params=_params("arbitrary", "arbitrary"),
        name="attn_cache" if has_cache else "attn",
    )(*args)
    return out


def _ssm_kernel(*refs, batch, n_chunks, with_state):
    if with_state:
        u_ref, toept_ref, wt_ref, vt_ref, a_ref, h0_ref, y_ref, fin_ref, zt_scr, wc_scr, sin_scr, yt_scr = refs
    else:
        u_ref, toept_ref, wt_ref, vt_ref, a_ref, h0_ref, y_ref, zt_scr, wc_scr, sin_scr, yt_scr = refs
    t = u_ref.shape[0]
    gb = zt_scr.shape[0]
    hc = zt_scr.shape[1] // t
    sw = a_ref.shape[-1]
    hw = sw // 2
    for s in range(t):
        at = u_ref[s].T
        for g in range(gb):
            zt_scr[g, s * hc:(s + 1) * hc, :] = at[g * hc:(g + 1) * hc, :]
    for g in range(gb):
        wc_scr[g] = jnp.dot(wt_ref[g], zt_scr[g], preferred_element_type=F32).T
    fwd = lax.broadcasted_iota(jnp.int32, (1, sw), 1) < hw

    def step(j, carry):
        rf = pl.ds(pl.multiple_of(j * batch, batch), batch)
        rb = pl.ds(pl.multiple_of((n_chunks - 1 - j) * batch, batch), batch)
        out = []
        for g in range(gb):
            sr, si = carry[g]
            sin_scr[g, rf, 0:hw] = sr[:, 0:hw]
            sin_scr[g, rb, hw:sw] = sr[:, hw:sw]
            sin_scr[g, rf, sw:sw + hw] = si[:, 0:hw]
            sin_scr[g, rb, sw + hw:2 * sw] = si[:, hw:sw]
            wr = jnp.where(fwd, wc_scr[g, rf, 0:sw], wc_scr[g, rb, 0:sw])
            wi = jnp.where(fwd, wc_scr[g, rf, sw:2 * sw], wc_scr[g, rb, sw:2 * sw])
            ar = a_ref[g, 0]
            ai = a_ref[g, 1]
            out.append((ar * sr - ai * si + wr, ar * si + ai * sr + wi))
        return tuple(out)

    fin = lax.fori_loop(0, n_chunks, step, tuple((h0_ref[g, 0], h0_ref[g, 1]) for g in range(gb)))
    for g in range(gb):
        sint = sin_scr[g].T.astype(BF16)
        y2t = (jnp.dot(toept_ref[g], zt_scr[g], preferred_element_type=F32)
               + jnp.dot(vt_ref[g], sint, preferred_element_type=F32))
        for s in range(t):
            yt_scr[s, g * hc:(g + 1) * hc, :] = y2t[s * hc:(s + 1) * hc, :]
        if with_state:
            fin_ref[g, 0] = fin[g][0]
            fin_ref[g, 1] = fin[g][1]
    for s in range(t):
        y_ref[s] = yt_scr[s].T.astype(y_ref.dtype)


def _ssm(u3, ssm_ops, h0, with_state):
    toept, wt, vt, a_t = ssm_ops
    n_slabs, t, rows, cw = u3.shape
    batch = h0.shape[2]
    n_chunks = rows // batch
    g = toept.shape[0]
    gb = g // n_slabs
    slab = pl.BlockSpec((None, t, rows, cw), lambda i: (i, 0, 0, 0))
    blk = lambda a: pl.BlockSpec((gb,) + a.shape[1:], lambda i: (i,) + (0,) * (a.ndim - 1))
    out_specs = [slab]
    out_shape = [jax.ShapeDtypeStruct(u3.shape, u3.dtype)]
    if with_state:
        out_specs.append(blk(h0))
        out_shape.append(jax.ShapeDtypeStruct(h0.shape, F32))
    nk = toept.shape[-1]
    outs = pl.pallas_call(
        functools.partial(_ssm_kernel, batch=batch, n_chunks=n_chunks, with_state=with_state),
        grid=(g // gb,),
        in_specs=[slab, blk(toept), blk(wt), blk(vt), blk(a_t), blk(h0)],
        out_specs=out_specs,
        out_shape=out_shape,
        scratch_shapes=[pltpu.VMEM((gb, nk, rows), BF16),
                        pltpu.VMEM((gb, rows, wt.shape[1]), F32),
                        pltpu.VMEM((gb, rows, wt.shape[1]), F32),
                        pltpu.VMEM((t, cw, rows), F32)],
        compiler_params=_params("arbitrary"),
        name="ssm_state" if with_state else "ssm",
    )(u3, toept, wt, vt, a_t, h0)
    return outs


def _state_to_lanes(h0):
    b, nd, nc, g, p = h0.shape
    return h0.astype(F32).transpose(3, 2, 0, 1, 4).reshape(g, nc, b, nd * p)


def _state_from_lanes(fin, nd):
    g, nc, b, w = fin.shape
    return fin.reshape(g, nc, b, nd, w // nd).transpose(2, 3, 1, 0, 4)


def _outproj_kernel(x_ref, a_ref, y_ref, u_ref, gs_ref, gate_ref, dskip_ref,
                    wglu_ref, bglu_ref, wout_ref, g_ref, o_ref):
    nb, tp, d = x_ref.shape
    y = _scan_block(y_ref) + dskip_ref[...] * _scan_block(u_ref)
    t, srows, d_ssm = y.shape
    rows = lambda ref: ref[...].reshape(nb * tp, ref.shape[-1])
    y = y.reshape(t * srows, d_ssm)
    c0 = math.sqrt(2.0 / math.pi)
    hy = 0.5 * y
    ys = hy + hy * jnp.tanh(y * (c0 + (c0 * 0.044715) * (y * y)))
    z = jnp.dot(ys.astype(BF16), wglu_ref[...], preferred_element_type=F32) + bglu_ref[...]
    ys = _from_scan_layout((ys * jax.nn.sigmoid(z)).reshape(t, srows, d_ssm), nb) * rows(gs_ref).astype(F32)
    cat = jnp.concatenate([rows(a_ref), ys.astype(BF16)], axis=-1)
    out = jnp.dot(cat, wout_ref[...], preferred_element_type=F32)
    ms = jnp.mean(out * out, axis=-1, keepdims=True)
    out = out * lax.rsqrt(ms + EPS)
    o_ref[...] = x_ref[...] + (g_ref[...] * gate_ref[...]) * out.reshape(nb, tp, d)


def _outproj(x, a_out, y_ssm, u, gs, gate, d_skip, w_glu_bf, b_glu, w_out_bf, norm_post):
    b, l, d = x.shape
    d_attn = a_out.shape[-1]
    d_ssm = gs.shape[-1]
    tp = TOKEN_TILE // b
    t = SSM_CHUNK
    const = lambda a: pl.BlockSpec(a.shape, lambda j: (0,) * a.ndim)
    tok = lambda w: pl.BlockSpec((b, tp, w), lambda j: (0, j, 0))
    scan = pl.BlockSpec((d_ssm // SCAN_SLAB, t, tp // t * b, SCAN_SLAB), lambda j: (0, 0, j, 0))
    consts = [gate, d_skip.reshape(1, d_ssm).astype(F32), w_glu_bf, b_glu.reshape(1, d_ssm).astype(F32),
              w_out_bf, norm_post.reshape(1, d).astype(F32)]
    return pl.pallas_call(
        _outproj_kernel,
        grid=(l // tp,),
        in_specs=[tok(d), tok(d_attn), scan, scan, tok(d_ssm)] + [const(a) for a in consts],
        out_specs=tok(d),
        out_shape=jax.ShapeDtypeStruct((b, l, d), F32),
        compiler_params=_params("arbitrary"),
        name="outproj",
    )(x, a_out, y_ssm, u, gs, *consts)


def _mixer(x, mod, h0_lanes, cache_k, cache_v, rope_tabs, emit_caches, with_state, lam, lam_init,
           norm_pre, w_in_bf, ssm_ops, subln, d_skip, w_glu_bf, b_glu, w_out_bf, norm_post):
    b, l, d = x.shape
    d_ssm = w_glu_bf.shape[0]
    d_attn = (w_in_bf.shape[1] - 2 * d_ssm) // 4
    shift, scale, gate = mod
    q, k, v, ga, u, gs, *caches = _inproj(x, shift, scale, norm_pre, w_in_bf, rope_tabs, emit_caches,
                                          d_attn, d_ssm)
    a_out = _attention(lam, q, k, v, cache_k, cache_v, ga, subln, lam_init)
    outs = _ssm(u, ssm_ops, h0_lanes, with_state)
    y = _outproj(x, a_out, outs[0], u, gs, gate, d_skip, w_glu_bf, b_glu, w_out_bf, norm_post)
    fin = outs[1] if with_state else None
    return y, caches, fin


def kernel(x_prompt, x_sample, cache_k, cache_v, state_ssm, c, c_ctx, w_ada, b_ada, norm_pre, norm_post, w_in, lambda_qk, subln, ssm_A_re, ssm_A_im, ssm_log_dt, ssm_B_re, ssm_B_im, ssm_C_re, ssm_C_im, ssm_D, w_glu, b_glu, w_out):
    xp, xs = x_prompt, x_sample
    bp, lp, d = xp.shape
    bs, ls, _ = xs.shape
    depth = w_in.shape[0]
    nd = ssm_A_re.shape[1]
    g, p = ssm_A_re.shape[-2:]
    d_ssm = w_glu.shape[-1]
    d_attn = (w_in.shape[-1] - 2 * d_ssm) // 4
    rope_tabs = _rope_tables(ls, d_attn)
    cond_rows = 16
    cond = jnp.zeros((cond_rows, d), F32).at[:bs].set(c.astype(F32)).at[bs].set(c_ctx.astype(F32))
    new_k, new_v, new_s = [], [], []
    for layer in range(depth):
        lam_init = 0.8 - 0.6 * math.exp(-0.3 * layer)
        m = _adaln(cond, w_ada[layer], b_ada[layer])
        mod_s = tuple(m[:bs, i * d:(i + 1) * d].reshape(bs, 1, d) for i in range(3))
        mod_p = tuple(m[bs:bs + 1, i * d:(i + 1) * d].reshape(1, 1, d) for i in range(3))
        ssm_ops, lam = _ssmprep(
            ssm_A_re[layer], ssm_A_im[layer], ssm_log_dt[layer], ssm_B_re[layer], ssm_B_im[layer],
            ssm_C_re[layer], ssm_C_im[layer], lambda_qk[layer], lam_init)
        shared = (lam, lam_init, norm_pre[layer], w_in[layer].astype(BF16), ssm_ops, subln[layer],
                  ssm_D[layer], w_glu[layer].astype(BF16), b_glu[layer], w_out[layer].astype(BF16),
                  norm_post[layer])

        h0_p = jnp.zeros((g, 2, bp, nd * p), F32)
        xp, (k_p, v_p), fin = _mixer(xp, mod_p, h0_p, None, None, None, True, True, *shared)
        new_k.append(k_p)
        new_v.append(v_p)
        new_s.append(_state_from_lanes(fin, nd).astype(xp.dtype))

        h0_s = _state_to_lanes(state_ssm[:, layer])
        xs, _, _ = _mixer(xs, mod_s, h0_s, cache_k[:, layer], cache_v[:, layer], rope_tabs, False, False,
                             *shared)

    return (xp, xs, jnp.stack(new_k, axis=1), jnp.stack(new_v, axis=1), jnp.stack(new_s, axis=1))
```

```python
import functools
import math

import jax
import jax.numpy as jnp
import numpy as np
from jax import lax
from jax.experimental import pallas as pl
from jax.experimental.pallas import tpu as pltpu

F32 = jnp.float32
BF16 = jnp.bfloat16

GRID_W = 64
QK_DIM = 64
V_DIM = 2 * QK_DIM
ROPE_THETA = 10000.0
EPS = 1e-6
Q_PRESCALE = math.log2(math.e) * QK_DIM ** -0.5

SUM_ROWS = 16
TOKEN_TILE = 1024
ATTN_Q_TILE = 1024
ATTN_Q_SUB = 512
SSM_CHUNK = 16
SCAN_SLAB = 128
VMEM_LIMIT = 48 * 1024 * 1024


def _silu(x):
    return x * jax.nn.sigmoid(x)


def _params(*sem):
    return pltpu.CompilerParams(dimension_semantics=sem, vmem_limit_bytes=VMEM_LIMIT)


def _adaln_kernel(c_ref, w_ref, b_ref, o_ref):
    s = _silu(c_ref[...]).astype(BF16)
    o_ref[...] = jnp.dot(s, w_ref[...].astype(BF16), preferred_element_type=F32) + b_ref[...]


def _adaln(cond, w_ada, b_ada):
    rows, d = cond.shape
    n = w_ada.shape[1]
    bn = 512
    return pl.pallas_call(
        _adaln_kernel,
        grid=(n // bn,),
        in_specs=[pl.BlockSpec((rows, d), lambda j: (0, 0)),
                  pl.BlockSpec((d, bn), lambda j: (0, j)),
                  pl.BlockSpec((1, bn), lambda j: (0, j))],
        out_specs=pl.BlockSpec((rows, bn), lambda j: (0, j)),
        out_shape=jax.ShapeDtypeStruct((rows, n), F32),
        compiler_params=_params("arbitrary"),
        name="adaln",
    )(cond, w_ada, b_ada.reshape(1, n))


def _ssmprep_kernel(are_ref, aim_ref, ldt_ref, bre_ref, bim_ref, cre_ref, cim_ref, lq_ref,
                    toept_ref, wt_ref, vt_ref, a_ref, lam_ref,
                    cplre_scr, cplimn_scr, w_scr, v_scr, toep_scr, *, lam_init, chunk):
    a_re = are_ref[...]
    a_im = aim_ref[...]
    dt = jnp.exp(ldt_ref[...])
    mag = jnp.exp(a_re * dt)
    ab_re = mag * jnp.cos(a_im * dt)
    ab_im = mag * jnp.sin(a_im * dt)
    nr, ni = ab_re - 1.0, ab_im
    den = a_re * a_re + a_im * a_im
    f_re = (nr * a_re + ni * a_im) / den
    f_im = (ni * a_re - nr * a_im) / den
    b_re = bre_ref[...]
    b_im = bim_ref[...]
    bb_re = f_re * b_re - f_im * b_im
    bb_im = f_re * b_im + f_im * b_re
    c_re = cre_ref[...]
    c_im = cim_ref[...]
    hc = b_re.shape[1]
    sw = a_re.shape[-1]
    hw = sw // 2
    lo, hi = slice(0, hw), slice(hw, sw)
    lo_im, hi_im = slice(sw, sw + hw), slice(sw + hw, 2 * sw)
    blk = lambda j: slice(j * hc, (j + 1) * hc)
    last = chunk - 1
    cplre_scr[...] = jnp.zeros_like(cplre_scr)
    cplimn_scr[...] = jnp.zeros_like(cplimn_scr)
    pw_re = jnp.ones_like(ab_re)
    pw_im = jnp.zeros_like(ab_im)
    for tau in range(chunk + 1):
        cp_re = c_re * pw_re - c_im * pw_im
        cp_imn = -(c_re * pw_im + c_im * pw_re)
        if tau < chunk:
            cplre_scr[:, blk(last + tau), lo] = cp_re[:, :, lo]
            cplre_scr[:, blk(last - tau), hi] = cp_re[:, :, hi]
            cplimn_scr[:, blk(last + tau), lo] = cp_imn[:, :, lo]
            cplimn_scr[:, blk(last - tau), hi] = cp_imn[:, :, hi]
            e_re = pw_re * bb_re - pw_im * bb_im
            e_im = pw_re * bb_im + pw_im * bb_re
            w_scr[:, blk(last - tau), lo] = e_re[:, :, lo]
            w_scr[:, blk(tau), hi] = e_re[:, :, hi]
            w_scr[:, blk(last - tau), lo_im] = e_im[:, :, lo]
            w_scr[:, blk(tau), hi_im] = e_im[:, :, hi]
        if tau >= 1:
            v_scr[:, blk(tau - 1), lo] = cp_re[:, :, lo]
            v_scr[:, blk(chunk - tau), hi] = cp_re[:, :, hi]
            v_scr[:, blk(tau - 1), lo_im] = cp_imn[:, :, lo]
            v_scr[:, blk(chunk - tau), hi_im] = cp_imn[:, :, hi]
        if tau < chunk:
            pw_re, pw_im = pw_re * ab_re - pw_im * ab_im, pw_re * ab_im + pw_im * ab_re
    a_ref[:, 0] = pw_re
    a_ref[:, 1] = pw_im
    dims = (((2,), (2,)), ((0,), (0,)))
    taps = (lax.dot_general(bb_re.astype(BF16), cplre_scr[...].astype(BF16), dims,
                            preferred_element_type=F32)
            + lax.dot_general(bb_im.astype(BF16), cplimn_scr[...].astype(BF16), dims,
                              preferred_element_type=F32))
    nk = chunk * hc
    for s in range(chunk):
        toep_scr[:, blk(s), :] = taps[:, :, (last - s) * hc:(last - s) * hc + nk]
    for g in range(toept_ref.shape[0]):
        toept_ref[g] = toep_scr[g].T.astype(toept_ref.dtype)
        wt_ref[g] = w_scr[g].T.astype(wt_ref.dtype)
    vt_ref[...] = v_scr[...].astype(vt_ref.dtype)
    lq = lq_ref[...]
    s01 = jnp.sum(lq[0:1] * lq[1:2], axis=-1, keepdims=True)
    s23 = jnp.sum(lq[2:3] * lq[3:4], axis=-1, keepdims=True)
    lam = jnp.exp(s01) - jnp.exp(s23) + lam_init
    lam_ref[...] = jnp.broadcast_to(lam, lam_ref.shape)


def _ssmprep(a_re, a_im, log_dt, b_re, b_im, c_re, c_im, lq, lam_init):
    nd, g, p = a_re.shape
    hc = b_re.shape[-1]
    t = SSM_CHUNK
    sw = nd * p
    gb = SCAN_SLAB // hc
    row = lambda a: a.astype(F32).transpose(1, 0, 2).reshape(g, 1, sw)
    ldt = jnp.broadcast_to(log_dt[..., None], (nd, g, p))
    bt = lambda a: a.astype(F32).transpose(1, 3, 0, 2).reshape(g, hc, sw)
    ct = lambda a: a.astype(F32).transpose(1, 2, 0, 3).reshape(g, hc, sw)
    sds = jax.ShapeDtypeStruct
    nk = t * hc
    blk = lambda shape: pl.BlockSpec((gb,) + shape[1:], lambda i: (i,) + (0,) * (len(shape) - 1))
    mat = (g, nk, 2 * sw)
    toept, wt, vt, a_t, lam = pl.pallas_call(
        functools.partial(_ssmprep_kernel, lam_init=lam_init, chunk=t),
        grid=(g // gb,),
        in_specs=[blk((g, 1, sw))] * 3 + [blk((g, hc, sw))] * 4 + [pl.BlockSpec(lq.shape, lambda i: (0, 0))],
        out_specs=[blk((g, nk, nk)), blk((g, 2 * sw, nk)), blk(mat), blk((g, 2, 1, sw)),
                   pl.BlockSpec((8, 128), lambda i: (0, 0))],
        out_shape=(sds((g, nk, nk), BF16), sds((g, 2 * sw, nk), BF16), sds(mat, BF16),
                   sds((g, 2, 1, sw), F32), sds((8, 128), F32)),
        scratch_shapes=[pltpu.VMEM((gb, 2 * nk, sw), F32), pltpu.VMEM((gb, 2 * nk, sw), F32),
                        pltpu.VMEM((gb, nk, 2 * sw), F32), pltpu.VMEM((gb, nk, 2 * sw), F32),
                        pltpu.VMEM((gb, nk, nk), F32)],
        compiler_params=_params("arbitrary"),
        name="ssmprep",
    )(row(a_re), row(a_im), row(ldt), bt(b_re), bt(b_im), ct(c_re), ct(c_im), lq.astype(F32))
    return (toept, wt, vt, a_t), lam


def _to_scan_layout(u, o_ref, batch):
    n_slabs, t, _, cw = o_ref.shape
    tp = u.shape[0] // batch
    u_tb = jnp.swapaxes(u.reshape(batch, tp, u.shape[-1]), 0, 1)
    for c in range(tp // t):
        for s in range(t):
            for v in range(n_slabs):
                o_ref[v, s, c * batch:(c + 1) * batch, :] = u_tb[c * t + s][:, v * cw:(v + 1) * cw].astype(o_ref.dtype)


def _scan_block(ref):
    return jnp.concatenate([ref[v] for v in range(ref.shape[0])], axis=-1).astype(F32)


def _from_scan_layout(ref, batch):
    t, rows, d = ref.shape
    tp = rows // batch * t
    u_tb = jnp.stack([ref[s, c * batch:(c + 1) * batch, :] for c in range(tp // t) for s in range(t)], axis=0)
    return jnp.swapaxes(u_tb, 0, 1).reshape(batch * tp, d)


def _inproj_kernel(*refs, rope, emit_caches, d_attn):
    n_in = 7 if rope else 5
    x_ref, shift_ref, scale_ref, g_ref, w_ref = refs[:5]
    cos_ref, sin_ref = refs[5:7] if rope else (None, None)
    q_ref, k_ref, v_ref, ga_ref, u_ref, gs_ref = refs[n_in:n_in + 6]
    kc_ref, vc_ref = refs[n_in + 6:] if emit_caches else (None, None)
    x = x_ref[...]
    nb, tp, d = x.shape
    ms = jnp.mean(x * x, axis=-1, keepdims=True)
    h = x * lax.rsqrt(ms + EPS) * (g_ref[...] * (1.0 + scale_ref[...])) + shift_ref[...]
    hb = h.reshape(nb * tp, d).astype(BF16)
    proj = lambda lo, hi: jnp.dot(hb, w_ref[:, lo:hi], preferred_element_type=F32)
    da = d_attn
    d_ssm = u_ref.shape[0] * u_ref.shape[-1]
    tok = lambda z: z.reshape(nb, tp, z.shape[-1])
    q = proj(0, da)
    k = proj(da, 2 * da)
    if rope:
        lane = lax.broadcasted_iota(jnp.int32, (1, da), 1)
        low = (lane % (QK_DIM // 2)) < (QK_DIM // 4)
        cos = cos_ref[...]
        sin = sin_ref[...]

        def rot(z):
            partner = jnp.where(low, pltpu.roll(z, da - QK_DIM // 4, 1), pltpu.roll(z, QK_DIM // 4, 1))
            return tok(z) * cos + tok(partner) * sin

        q = rot(q)
        k = rot(k)
    else:
        q = tok(q)
        k = tok(k)
    q_ref[...] = (q * Q_PRESCALE).astype(q_ref.dtype)
    k_ref[...] = k.astype(k_ref.dtype)
    if emit_caches:
        for hd in range(kc_ref.shape[2]):
            kc_ref[:, :, hd, :] = k[:, :, hd * V_DIM:(hd + 1) * V_DIM]
    v = tok(proj(2 * da, 3 * da))
    v_ref[...] = v.astype(v_ref.dtype)
    if emit_caches:
        for hd in range(vc_ref.shape[2]):
            vc_ref[:, :, hd, :] = v[:, :, hd * V_DIM:(hd + 1) * V_DIM]
    ga_ref[...] = tok(_silu(proj(3 * da, 4 * da))).astype(ga_ref.dtype)
    gs_ref[...] = tok(_silu(proj(4 * da + d_ssm, 4 * da + 2 * d_ssm))).astype(gs_ref.dtype)
    _to_scan_layout(proj(4 * da, 4 * da + d_ssm), u_ref, nb)


def _inproj(x, shift, scale, norm_pre, w_in_bf, rope_tabs, emit_caches, d_attn, d_ssm):
    b, l, d = x.shape
    tp = TOKEN_TILE // b
    t = SSM_CHUNK
    rope = rope_tabs is not None
    const = lambda a: pl.BlockSpec(a.shape, lambda j: (0,) * a.ndim)
    tok = lambda w: pl.BlockSpec((b, tp, w), lambda j: (0, j, 0))
    g2 = norm_pre.reshape(1, d).astype(F32)
    in_specs = [tok(d), const(shift), const(scale), const(g2), const(w_in_bf)]
    args = [x, shift, scale, g2, w_in_bf]
    if rope:
        in_specs += [pl.BlockSpec((tp, d_attn), lambda j: (j, 0))] * 2
        args += list(rope_tabs)
    scan_rows = tp // t * b
    attn_shape = jax.ShapeDtypeStruct((b, l, d_attn), BF16)
    out_specs = [tok(d_attn), tok(d_attn), tok(d_attn), tok(d_attn),
                 pl.BlockSpec((d_ssm // SCAN_SLAB, t, scan_rows, SCAN_SLAB), lambda j: (0, 0, j, 0)),
                 tok(d_ssm)]
    out_shape = [attn_shape, attn_shape, attn_shape, attn_shape,
                 jax.ShapeDtypeStruct((d_ssm // SCAN_SLAB, t, l // t * b, SCAN_SLAB), BF16),
                 jax.ShapeDtypeStruct((b, l, d_ssm), BF16)]
    if emit_caches:
        out_specs += [pl.BlockSpec((b, tp, d_attn // V_DIM, V_DIM), lambda j: (0, j, 0, 0))] * 2
        out_shape += [jax.ShapeDtypeStruct((b, l, d_attn // V_DIM, V_DIM), F32)] * 2
    return pl.pallas_call(
        functools.partial(_inproj_kernel, rope=rope, emit_caches=emit_caches, d_attn=d_attn),
        grid=(l // tp,),
        in_specs=in_specs,
        out_specs=out_specs,
        out_shape=out_shape,
        compiler_params=_params("arbitrary"),
        name="inproj_rope" if rope else "inproj",
    )(*args)


def _rope_tables(l, d_attn):
    rows = l // GRID_W
    row = np.repeat(np.arange(rows, dtype=np.float64), GRID_W)
    col = np.tile(np.arange(GRID_W, dtype=np.float64), rows)
    n_freq = QK_DIM // 4
    inv = ROPE_THETA ** (-np.arange(n_freq, dtype=np.float64) / n_freq)
    ang_r = row[:, None] * inv
    ang_c = col[:, None] * inv
    cos64 = np.concatenate([np.cos(ang_r), np.cos(ang_r), np.cos(ang_c), np.cos(ang_c)], axis=-1)
    sin64 = np.concatenate([-np.sin(ang_r), np.sin(ang_r), -np.sin(ang_c), np.sin(ang_c)], axis=-1)
    reps = d_attn // QK_DIM
    return (jnp.asarray(np.tile(cos64, (1, reps)), dtype=F32),
            jnp.asarray(np.tile(sin64, (1, reps)), dtype=F32))


def _head(ref, bi, h):
    return ref[bi, :, h, :] if len(ref.shape) == 4 else ref[bi, :, h * V_DIM:(h + 1) * V_DIM]


def _attn_kernel(*refs, has_cache, lam_init):
    if has_cache:
        lam_ref, q_ref, k_ref, v_ref, ck_ref, cv_ref, ga_ref, sub_ref, o_ref, k_scr, vt_scr = refs
    else:
        lam_ref, q_ref, k_ref, v_ref, ga_ref, sub_ref, o_ref, k_scr, vt_scr = refs
    nbk, tq = q_ref.shape[0], q_ref.shape[1]
    n_heads, lk = k_scr.shape[0] // nbk, k_scr.shape[1]
    ln = k_ref.shape[1]
    lam = lam_ref[0:1, 0:1]
    lane = lax.broadcasted_iota(jnp.int32, (1, V_DIM), 1)
    first = lane < QK_DIM
    dims = (((1,), (1,)), ((), ()))
    slot = lambda bi, h: bi * n_heads + h

    @pl.when(pl.program_id(1) == 0)
    def _():
        for bi in range(nbk):
            for h in range(n_heads):
                n = slot(bi, h)
                if has_cache:
                    k_scr[n, 0:lk - ln, :] = _head(ck_ref, bi, h).astype(BF16)
                    vt_scr[n, 0:V_DIM, 0:lk - ln] = _head(cv_ref, bi, h).astype(F32).T.astype(BF16)
                k_scr[n, lk - ln:lk, :] = _head(k_ref, bi, h).astype(BF16)
                vt_scr[n, 0:V_DIM, lk - ln:lk] = _head(v_ref, bi, h).astype(F32).T.astype(BF16)
                vt_scr[n, V_DIM:, :] = jnp.ones((vt_scr.shape[1] - V_DIM, lk), BF16)

    sub = min(tq, ATTN_Q_SUB)
    items = [(bi, h, r) for bi in range(nbk) for h in range(n_heads) for r in range(0, tq, sub)]

    def scores(item):
        bi, h, r = item
        q = _head(q_ref, bi, h)[r:r + sub, :]
        zero = jnp.zeros_like(q)
        k = k_scr[slot(bi, h)]
        return (lax.dot_general(k, jnp.where(first, q, zero), dims, preferred_element_type=F32),
                lax.dot_general(k, jnp.where(first, zero, q), dims, preferred_element_type=F32))

    nxt = scores(items[0])
    for n, (bi, h, r) in enumerate(items):
        st1, st2 = nxt
        if n + 1 < len(items):
            nxt = scores(items[n + 1])
        vt = vt_scr[slot(bi, h)]
        et1 = jnp.exp2(st1 - jnp.max(st1, axis=0, keepdims=True)).astype(BF16)
        et2 = jnp.exp2(st2 - jnp.max(st2, axis=0, keepdims=True)).astype(BF16)
        o1 = jnp.dot(vt, et1, preferred_element_type=F32)
        o2 = jnp.dot(vt, et2, preferred_element_type=F32)
        r1 = 1.0 / o1[V_DIM:V_DIM + 1, :]
        r2 = lam / o2[V_DIM:V_DIM + 1, :]
        o = (o1[0:V_DIM, :] * r1 - o2[0:V_DIM, :] * r2).T
        ms = jnp.mean(o * o, axis=-1, keepdims=True)
        o = o * lax.rsqrt(ms + EPS) * sub_ref[...] * (1.0 - lam_init)
        ga = _head(ga_ref, bi, h)[r:r + sub, :].astype(F32)
        o_ref[bi, r:r + sub, h * V_DIM:(h + 1) * V_DIM] = (o * ga).astype(o_ref.dtype)


def _attention(lam, q, k, v, cache_k, cache_v, ga, subln, lam_init):
    b, l, d_attn = q.shape
    n_heads = d_attn // V_DIM
    tq = min(ATTN_Q_TILE, l)
    nq = l // tq
    nbk = max(1, ATTN_Q_TILE // l)
    has_cache = cache_k is not None
    q_spec = pl.BlockSpec((nbk, tq, d_attn), lambda i, j: (i, j, 0))
    whole = lambda a: pl.BlockSpec((nbk,) + a.shape[1:], lambda i, j: (i,) + (0,) * (a.ndim - 1))
    in_specs = [pl.BlockSpec((8, 128), lambda i, j: (0, 0)), q_spec, whole(k), whole(v)]
    args = [lam, q, k, v]
    if has_cache:
        in_specs += [whole(cache_k), whole(cache_v)]
        args += [cache_k, cache_v]
    in_specs += [q_spec, pl.BlockSpec((1, V_DIM), lambda i, j: (0, 0))]
    args += [ga, subln.reshape(1, V_DIM).astype(F32)]
    lk = l + (cache_k.shape[1] if has_cache else 0)
    out = pl.pallas_call(
        functools.partial(_attn_kernel, has_cache=has_cache, lam_init=lam_init),
        grid=(b // nbk, nq),
        in_specs=in_specs,
        out_specs=q_spec,
        out_shape=jax.ShapeDtypeStruct((b, l, d_attn), BF16),
        scratch_shapes=[pltpu.VMEM((nbk * n_heads, lk, V_DIM), BF16),
                        pltpu.VMEM((nbk * n_heads, V_DIM + SUM_ROWS, lk), BF16)],
        compiler_params=_params("arbitrary", "arbitrary"),
        name="attn_cache" if has_cache else "attn",
    )(*args)
    return out


def _ssm_kernel(*refs, batch, n_chunks, with_state):
    if with_state:
        u_ref, toept_ref, wt_ref, vt_ref, a_ref, h0_ref, y_ref, fin_ref, zt_scr, wc_scr, sin_scr, yt_scr = refs
    else:
        u_ref, toept_ref, wt_ref, vt_ref, a_ref, h0_ref, y_ref, zt_scr, wc_scr, sin_scr, yt_scr = refs
    t = u_ref.shape[0]
    gb = zt_scr.shape[0]
    hc = zt_scr.shape[1] // t
    sw = a_ref.shape[-1]
    hw = sw // 2
    for s in range(t):
        at = u_ref[s].T
        for g in range(gb):
            zt_scr[g, s * hc:(s + 1) * hc, :] = at[g * hc:(g + 1) * hc, :]
    for g in range(gb):
        wc_scr[g] = jnp.dot(wt_ref[g], zt_scr[g], preferred_element_type=F32).astype(BF16).T.astype(F32)
    fwd = lax.broadcasted_iota(jnp.int32, (1, sw), 1) < hw

    def step(j, carry):
        rf = pl.ds(pl.multiple_of(j * batch, batch), batch)
        rb = pl.ds(pl.multiple_of((n_chunks - 1 - j) * batch, batch), batch)
        out = []
        for g in range(gb):
            sr, si = carry[g]
            sin_scr[g, rf, 0:hw] = sr[:, 0:hw]
            sin_scr[g, rb, hw:sw] = sr[:, hw:sw]
            sin_scr[g, rf, sw:sw + hw] = si[:, 0:hw]
            sin_scr[g, rb, sw + hw:2 * sw] = si[:, hw:sw]
            wr = jnp.where(fwd, wc_scr[g, rf, 0:sw], wc_scr[g, rb, 0:sw])
            wi = jnp.where(fwd, wc_scr[g, rf, sw:2 * sw], wc_scr[g, rb, sw:2 * sw])
            ar = a_ref[g, 0]
            ai = a_ref[g, 1]
            out.append((ar * sr - ai * si + wr, ar * si + ai * sr + wi))
        return tuple(out)

    fin = lax.fori_loop(0, n_chunks, step, tuple((h0_ref[g, 0], h0_ref[g, 1]) for g in range(gb)))
    for g in range(gb):
        sint = sin_scr[g].astype(BF16).T
        y2t = (jnp.dot(toept_ref[g], zt_scr[g], preferred_element_type=F32)
               + jnp.dot(vt_ref[g], sint, preferred_element_type=F32))
        for s in range(t):
            yt_scr[s, g * hc:(g + 1) * hc, :] = y2t[s * hc:(s + 1) * hc, :].astype(yt_scr.dtype)
        if with_state:
            fin_ref[g, 0] = fin[g][0]
            fin_ref[g, 1] = fin[g][1]
    for s in range(t):
        y_ref[s] = yt_scr[s].T


def _ssm(u3, ssm_ops, h0, with_state):
    toept, wt, vt, a_t = ssm_ops
    n_slabs, t, rows, cw = u3.shape
    batch = h0.shape[2]
    n_chunks = rows // batch
    g = toept.shape[0]
    gb = g // n_slabs
    slab = pl.BlockSpec((None, t, rows, cw), lambda i: (i, 0, 0, 0))
    blk = lambda a: pl.BlockSpec((gb,) + a.shape[1:], lambda i: (i,) + (0,) * (a.ndim - 1))
    out_specs = [slab]
    out_shape = [jax.ShapeDtypeStruct(u3.shape, u3.dtype)]
    if with_state:
        out_specs.append(blk(h0))
        out_shape.append(jax.ShapeDtypeStruct(h0.shape, F32))
    nk = toept.shape[-1]
    outs = pl.pallas_call(
        functools.partial(_ssm_kernel, batch=batch, n_chunks=n_chunks, with_state=with_state),
        grid=(g // gb,),
        in_specs=[slab, blk(toept), blk(wt), blk(vt), blk(a_t), blk(h0)],
        out_specs=out_specs,
        out_shape=out_shape,
        scratch_shapes=[pltpu.VMEM((gb, nk, rows), BF16),
                        pltpu.VMEM((gb, rows, wt.shape[1]), F32),
                        pltpu.VMEM((gb, rows, wt.shape[1]), F32),
                        pltpu.VMEM((t, cw, rows), u3.dtype)],
        compiler_params=_params("arbitrary"),
        name="ssm_state" if with_state else "ssm",
    )(u3, toept, wt, vt, a_t, h0)
    return outs


def _state_to_lanes(h0):
    b, nd, nc, g, p = h0.shape
    return h0.astype(F32).transpose(3, 2, 0, 1, 4).reshape(g, nc, b, nd * p)


def _state_from_lanes(fin, nd):
    g, nc, b, w = fin.shape
    return fin.reshape(g, nc, b, nd, w // nd).transpose(2, 3, 1, 0, 4)


def _outproj_kernel(x_ref, a_ref, y_ref, u_ref, gs_ref, gate_ref, dskip_ref,
                    wglu_ref, bglu_ref, wout_ref, g_ref, o_ref):
    nb, tp, d = x_ref.shape
    y = _scan_block(y_ref) + dskip_ref[...] * _scan_block(u_ref)
    t, srows, d_ssm = y.shape
    rows = lambda ref: ref[...].reshape(nb * tp, ref.shape[-1])
    y = y.reshape(t * srows, d_ssm)
    c0 = math.sqrt(2.0 / math.pi)
    hy = 0.5 * y
    ys = hy + hy * jnp.tanh(y * (c0 + (c0 * 0.044715) * (y * y)))
    z = jnp.dot(ys.astype(BF16), wglu_ref[...], preferred_element_type=F32) + bglu_ref[...]
    ys = _from_scan_layout((ys * jax.nn.sigmoid(z)).reshape(t, srows, d_ssm), nb) * rows(gs_ref).astype(F32)
    cat = jnp.concatenate([rows(a_ref), ys.astype(BF16)], axis=-1)
    out = jnp.dot(cat, wout_ref[...], preferred_element_type=F32)
    ms = jnp.mean(out * out, axis=-1, keepdims=True)
    out = out * lax.rsqrt(ms + EPS)
    o_ref[...] = x_ref[...] + (g_ref[...] * gate_ref[...]) * out.reshape(nb, tp, d)


def _outproj(x, a_out, y_ssm, u, gs, gate, d_skip, w_glu_bf, b_glu, w_out_bf, norm_post):
    b, l, d = x.shape
    d_attn = a_out.shape[-1]
    d_ssm = gs.shape[-1]
    tp = TOKEN_TILE // b
    t = SSM_CHUNK
    const = lambda a: pl.BlockSpec(a.shape, lambda j: (0,) * a.ndim)
    tok = lambda w: pl.BlockSpec((b, tp, w), lambda j: (0, j, 0))
    scan = pl.BlockSpec((d_ssm // SCAN_SLAB, t, tp // t * b, SCAN_SLAB), lambda j: (0, 0, j, 0))
    consts = [gate, d_skip.reshape(1, d_ssm).astype(F32), w_glu_bf, b_glu.reshape(1, d_ssm).astype(F32),
              w_out_bf, norm_post.reshape(1, d).astype(F32)]
    return pl.pallas_call(
        _outproj_kernel,
        grid=(l // tp,),
        in_specs=[tok(d), tok(d_attn), scan, scan, tok(d_ssm)] + [const(a) for a in consts],
        out_specs=tok(d),
        out_shape=jax.ShapeDtypeStruct((b, l, d), F32),
        compiler_params=_params("arbitrary"),
        name="outproj",
    )(x, a_out, y_ssm, u, gs, *consts)


def _mixer(x, mod, h0_lanes, cache_k, cache_v, rope_tabs, emit_caches, with_state, lam, lam_init,
           norm_pre, w_in_bf, ssm_ops, subln, d_skip, w_glu_bf, b_glu, w_out_bf, norm_post):
    b, l, d = x.shape
    d_ssm = w_glu_bf.shape[0]
    d_attn = (w_in_bf.shape[1] - 2 * d_ssm) // 4
    shift, scale, gate = mod
    q, k, v, ga, u, gs, *caches = _inproj(x, shift, scale, norm_pre, w_in_bf, rope_tabs, emit_caches,
                                          d_attn, d_ssm)
    a_out = _attention(lam, q, k, v, cache_k, cache_v, ga, subln, lam_init)
    outs = _ssm(u, ssm_ops, h0_lanes, with_state)
    y = _outproj(x, a_out, outs[0], u, gs, gate, d_skip, w_glu_bf, b_glu, w_out_bf, norm_post)
    fin = outs[1] if with_state else None
    return y, caches, fin


def kernel(x_prompt, x_sample, cache_k, cache_v, state_ssm, c, c_ctx, w_ada, b_ada, norm_pre, norm_post, w_in, lambda_qk, subln, ssm_A_re, ssm_A_im, ssm_log_dt, ssm_B_re, ssm_B_im, ssm_C_re, ssm_C_im, ssm_D, w_glu, b_glu, w_out):
    xp, xs = x_prompt, x_sample
    bp, lp, d = xp.shape
    bs, ls, _ = xs.shape
    depth = w_in.shape[0]
    nd = ssm_A_re.shape[1]
    g, p = ssm_A_re.shape[-2:]
    d_ssm = w_glu.shape[-1]
    d_attn = (w_in.shape[-1] - 2 * d_ssm) // 4
    rope_tabs = _rope_tables(ls, d_attn)
    cond_rows = 16
    cond = jnp.zeros((cond_rows, d), F32).at[:bs].set(c.astype(F32)).at[bs].set(c_ctx.astype(F32))
    new_k, new_v, new_s = [], [], []
    for layer in range(depth):
        lam_init = 0.8 - 0.6 * math.exp(-0.3 * layer)
        m = _adaln(cond, w_ada[layer], b_ada[layer])
        mod_s = tuple(m[:bs, i * d:(i + 1) * d].reshape(bs, 1, d) for i in range(3))
        mod_p = tuple(m[bs:bs + 1, i * d:(i + 1) * d].reshape(1, 1, d) for i in range(3))
        ssm_ops, lam = _ssmprep(
            ssm_A_re[layer], ssm_A_im[layer], ssm_log_dt[layer], ssm_B_re[layer], ssm_B_im[layer],
            ssm_C_re[layer], ssm_C_im[layer], lambda_qk[layer], lam_init)
        shared = (lam, lam_init, norm_pre[layer], w_in[layer].astype(BF16), ssm_ops, subln[layer],
                  ssm_D[layer], w_glu[layer].astype(BF16), b_glu[layer], w_out[layer].astype(BF16),
                  norm_post[layer])

        h0_p = jnp.zeros((g, 2, bp, nd * p), F32)
        xp, (k_p, v_p), fin = _mixer(xp, mod_p, h0_p, None, None, None, True, True, *shared)
        new_k.append(k_p)
        new_v.append(v_p)
        new_s.append(_state_from_lanes(fin, nd).astype(xp.dtype))

        h0_s = _state_to_lanes(state_ssm[:, layer])
        xs, _, _ = _mixer(xs, mod_s, h0_s, cache_k[:, layer], cache_v[:, layer], rope_tabs, False, False,
                             *shared)

    return (xp, xs, jnp.stack(new_k, axis=1), jnp.stack(new_v, axis=1), jnp.stack(new_s, axis=1))
```

```python
import functools
import math

import jax
import jax.numpy as jnp
import numpy as np
from jax import lax
from jax.experimental import pallas as pl
from jax.experimental.pallas import tpu as pltpu

F32 = jnp.float32
BF16 = jnp.bfloat16

GRID_W = 64
QK_DIM = 64
V_DIM = 2 * QK_DIM
ROPE_THETA = 10000.0
EPS = 1e-6
Q_PRESCALE = math.log2(math.e) * QK_DIM ** -0.5

SUM_ROWS = 16
TOKEN_TILE = 1024
ATTN_Q_TILE = 1024
ATTN_Q_SUB = 512
SSM_CHUNK = 16
SCAN_SLAB = 128
VMEM_LIMIT = 48 * 1024 * 1024


def _silu(x):
    return x * jax.nn.sigmoid(x)


def _params(*sem):
    return pltpu.CompilerParams(dimension_semantics=sem, vmem_limit_bytes=VMEM_LIMIT)


def _adaln_kernel(c_ref, w_ref, b_ref, o_ref):
    s = _silu(c_ref[...]).astype(BF16)
    o_ref[...] = jnp.dot(s, w_ref[...].astype(BF16), preferred_element_type=F32) + b_ref[...]


def _adaln(cond, w_ada, b_ada):
    rows, d = cond.shape
    n = w_ada.shape[1]
    bn = 512
    return pl.pallas_call(
        _adaln_kernel,
        grid=(n // bn,),
        in_specs=[pl.BlockSpec((rows, d), lambda j: (0, 0)),
                  pl.BlockSpec((d, bn), lambda j: (0, j)),
                  pl.BlockSpec((1, bn), lambda j: (0, j))],
        out_specs=pl.BlockSpec((rows, bn), lambda j: (0, j)),
        out_shape=jax.ShapeDtypeStruct((rows, n), F32),
        compiler_params=_params("arbitrary"),
        name="adaln",
    )(cond, w_ada, b_ada.reshape(1, n))


def _ssmprep_kernel(are_ref, aim_ref, ldt_ref, bre_ref, bim_ref, cre_ref, cim_ref, lq_ref,
                    toept_ref, wt_ref, vt_ref, a_ref, lam_ref,
                    cplre_scr, cplimn_scr, w_scr, v_scr, toep_scr, *, lam_init, chunk):
    a_re = are_ref[...]
    a_im = aim_ref[...]
    dt = jnp.exp(ldt_ref[...])
    mag = jnp.exp(a_re * dt)
    ab_re = mag * jnp.cos(a_im * dt)
    ab_im = mag * jnp.sin(a_im * dt)
    nr, ni = ab_re - 1.0, ab_im
    den = a_re * a_re + a_im * a_im
    f_re = (nr * a_re + ni * a_im) / den
    f_im = (ni * a_re - nr * a_im) / den
    b_re = bre_ref[...]
    b_im = bim_ref[...]
    bb_re = f_re * b_re - f_im * b_im
    bb_im = f_re * b_im + f_im * b_re
    c_re = cre_ref[...]
    c_im = cim_ref[...]
    hc = b_re.shape[1]
    sw = a_re.shape[-1]
    hw = sw // 2
    lo, hi = slice(0, hw), slice(hw, sw)
    lo_im, hi_im = slice(sw, sw + hw), slice(sw + hw, 2 * sw)
    blk = lambda j: slice(j * hc, (j + 1) * hc)
    last = chunk - 1
    cplre_scr[...] = jnp.zeros_like(cplre_scr)
    cplimn_scr[...] = jnp.zeros_like(cplimn_scr)
    pw_re = jnp.ones_like(ab_re)
    pw_im = jnp.zeros_like(ab_im)
    for tau in range(chunk + 1):
        cp_re = c_re * pw_re - c_im * pw_im
        cp_imn = -(c_re * pw_im + c_im * pw_re)
        if tau < chunk:
            cplre_scr[:, blk(last + tau), lo] = cp_re[:, :, lo]
            cplre_scr[:, blk(last - tau), hi] = cp_re[:, :, hi]
            cplimn_scr[:, blk(last + tau), lo] = cp_imn[:, :, lo]
            cplimn_scr[:, blk(last - tau), hi] = cp_imn[:, :, hi]
            e_re = pw_re * bb_re - pw_im * bb_im
            e_im = pw_re * bb_im + pw_im * bb_re
            w_scr[:, blk(last - tau), lo] = e_re[:, :, lo]
            w_scr[:, blk(tau), hi] = e_re[:, :, hi]
            w_scr[:, blk(last - tau), lo_im] = e_im[:, :, lo]
            w_scr[:, blk(tau), hi_im] = e_im[:, :, hi]
        if tau >= 1:
            v_scr[:, blk(tau - 1), lo] = cp_re[:, :, lo]
            v_scr[:, blk(chunk - tau), hi] = cp_re[:, :, hi]
            v_scr[:, blk(tau - 1), lo_im] = cp_imn[:, :, lo]
            v_scr[:, blk(chunk - tau), hi_im] = cp_imn[:, :, hi]
        if tau < chunk:
            pw_re, pw_im = pw_re * ab_re - pw_im * ab_im, pw_re * ab_im + pw_im * ab_re
    a_ref[:, 0] = pw_re
    a_ref[:, 1] = pw_im
    dims = (((2,), (2,)), ((0,), (0,)))
    taps = (lax.dot_general(bb_re.astype(BF16), cplre_scr[...].astype(BF16), dims,
                            preferred_element_type=F32)
            + lax.dot_general(bb_im.astype(BF16), cplimn_scr[...].astype(BF16), dims,
                              preferred_element_type=F32))
    nk = chunk * hc
    for s in range(chunk):
        toep_scr[:, blk(s), :] = taps[:, :, (last - s) * hc:(last - s) * hc + nk]
    for g in range(toept_ref.shape[0]):
        toept_ref[g] = toep_scr[g].T.astype(toept_ref.dtype)
        wt_ref[g] = w_scr[g].T.astype(wt_ref.dtype)
    vt_ref[...] = v_scr[...].astype(vt_ref.dtype)
    lq = lq_ref[...]
    s01 = jnp.sum(lq[0:1] * lq[1:2], axis=-1, keepdims=True)
    s23 = jnp.sum(lq[2:3] * lq[3:4], axis=-1, keepdims=True)
    lam = jnp.exp(s01) - jnp.exp(s23) + lam_init
    lam_ref[...] = jnp.broadcast_to(lam, lam_ref.shape)


def _ssmprep(a_re, a_im, log_dt, b_re, b_im, c_re, c_im, lq, lam_init):
    nd, g, p = a_re.shape
    hc = b_re.shape[-1]
    t = SSM_CHUNK
    sw = nd * p
    gb = SCAN_SLAB // hc
    row = lambda a: a.astype(F32).transpose(1, 0, 2).reshape(g, 1, sw)
    ldt = jnp.broadcast_to(log_dt[..., None], (nd, g, p))
    bt = lambda a: a.astype(F32).transpose(1, 3, 0, 2).reshape(g, hc, sw)
    ct = lambda a: a.astype(F32).transpose(1, 2, 0, 3).reshape(g, hc, sw)
    sds = jax.ShapeDtypeStruct
    nk = t * hc
    blk = lambda shape: pl.BlockSpec((gb,) + shape[1:], lambda i: (i,) + (0,) * (len(shape) - 1))
    mat = (g, nk, 2 * sw)
    toept, wt, vt, a_t, lam = pl.pallas_call(
        functools.partial(_ssmprep_kernel, lam_init=lam_init, chunk=t),
        grid=(g // gb,),
        in_specs=[blk((g, 1, sw))] * 3 + [blk((g, hc, sw))] * 4 + [pl.BlockSpec(lq.shape, lambda i: (0, 0))],
        out_specs=[blk((g, nk, nk)), blk((g, 2 * sw, nk)), blk(mat), blk((g, 2, 1, sw)),
                   pl.BlockSpec((8, 128), lambda i: (0, 0))],
        out_shape=(sds((g, nk, nk), BF16), sds((g, 2 * sw, nk), BF16), sds(mat, BF16),
                   sds((g, 2, 1, sw), F32), sds((8, 128), F32)),
        scratch_shapes=[pltpu.VMEM((gb, 2 * nk, sw), F32), pltpu.VMEM((gb, 2 * nk, sw), F32),
                        pltpu.VMEM((gb, nk, 2 * sw), F32), pltpu.VMEM((gb, nk, 2 * sw), F32),
                        pltpu.VMEM((gb, nk, nk), F32)],
        compiler_params=_params("arbitrary"),
        name="ssmprep",
    )(row(a_re), row(a_im), row(ldt), bt(b_re), bt(b_im), ct(c_re), ct(c_im), lq.astype(F32))
    return (toept, wt, vt, a_t), lam


def _to_scan_layout(u, o_ref, batch):
    n_slabs, t, _, cw = o_ref.shape
    tp = u.shape[0] // batch
    u_tb = jnp.swapaxes(u.reshape(batch, tp, u.shape[-1]), 0, 1)
    for c in range(tp // t):
        for s in range(t):
            for v in range(n_slabs):
                o_ref[v, s, c * batch:(c + 1) * batch, :] = u_tb[c * t + s][:, v * cw:(v + 1) * cw].astype(o_ref.dtype)


def _scan_block(ref):
    return jnp.concatenate([ref[v] for v in range(ref.shape[0])], axis=-1).astype(F32)


def _from_scan_layout(ref, batch):
    t, rows, d = ref.shape
    tp = rows // batch * t
    u_tb = jnp.stack([ref[s, c * batch:(c + 1) * batch, :] for c in range(tp // t) for s in range(t)], axis=0)
    return jnp.swapaxes(u_tb, 0, 1).reshape(batch * tp, d)


def _inproj_kernel(*refs, rope, emit_caches, d_attn):
    n_in = 7 if rope else 5
    x_ref, shift_ref, scale_ref, g_ref, w_ref = refs[:5]
    cos_ref, sin_ref = refs[5:7] if rope else (None, None)
    q_ref, k_ref, v_ref, ga_ref, u_ref, gs_ref = refs[n_in:n_in + 6]
    kc_ref, vc_ref = refs[n_in + 6:] if emit_caches else (None, None)
    x = x_ref[...]
    nb, tp, d = x.shape
    ms = jnp.mean(x * x, axis=-1, keepdims=True)
    h = x * lax.rsqrt(ms + EPS) * (g_ref[...] * (1.0 + scale_ref[...])) + shift_ref[...]
    hb = h.reshape(nb * tp, d).astype(BF16)
    proj = lambda lo, hi: jnp.dot(hb, w_ref[:, lo:hi].astype(BF16), preferred_element_type=F32)
    da = d_attn
    d_ssm = u_ref.shape[0] * u_ref.shape[-1]
    tok = lambda z: z.reshape(nb, tp, z.shape[-1])
    q = proj(0, da)
    k = proj(da, 2 * da)
    if rope:
        lane = lax.broadcasted_iota(jnp.int32, (1, da), 1)
        low = (lane % (QK_DIM // 2)) < (QK_DIM // 4)
        cos = cos_ref[...]
        sin = sin_ref[...]

        def rot(z):
            partner = jnp.where(low, pltpu.roll(z, da - QK_DIM // 4, 1), pltpu.roll(z, QK_DIM // 4, 1))
            return tok(z) * cos + tok(partner) * sin

        q = rot(q)
        k = rot(k)
    else:
        q = tok(q)
        k = tok(k)
    q_ref[...] = (q * Q_PRESCALE).astype(q_ref.dtype)
    k_ref[...] = k.astype(k_ref.dtype)
    if emit_caches:
        for hd in range(kc_ref.shape[2]):
            kc_ref[:, :, hd, :] = k[:, :, hd * V_DIM:(hd + 1) * V_DIM]
    v = tok(proj(2 * da, 3 * da))
    v_ref[...] = v.astype(v_ref.dtype)
    if emit_caches:
        for hd in range(vc_ref.shape[2]):
            vc_ref[:, :, hd, :] = v[:, :, hd * V_DIM:(hd + 1) * V_DIM]
    ga_ref[...] = tok(_silu(proj(3 * da, 4 * da))).astype(ga_ref.dtype)
    gs_ref[...] = tok(_silu(proj(4 * da + d_ssm, 4 * da + 2 * d_ssm))).astype(gs_ref.dtype)
    _to_scan_layout(proj(4 * da, 4 * da + d_ssm), u_ref, nb)


def _inproj(x, shift, scale, norm_pre, w_in, rope_tabs, emit_caches, d_attn, d_ssm):
    b, l, d = x.shape
    tp = TOKEN_TILE // b
    t = SSM_CHUNK
    rope = rope_tabs is not None
    const = lambda a: pl.BlockSpec(a.shape, lambda j: (0,) * a.ndim)
    tok = lambda w: pl.BlockSpec((b, tp, w), lambda j: (0, j, 0))
    g2 = norm_pre.reshape(1, d).astype(F32)
    w_spec = pl.BlockSpec(w_in.shape, lambda j: (0, 0), pipeline_mode=pl.Buffered(1))
    in_specs = [tok(d), const(shift), const(scale), const(g2), w_spec]
    args = [x, shift, scale, g2, w_in]
    if rope:
        in_specs += [pl.BlockSpec((tp, d_attn), lambda j: (j, 0))] * 2
        args += list(rope_tabs)
    scan_rows = tp // t * b
    attn_shape = jax.ShapeDtypeStruct((b, l, d_attn), BF16)
    out_specs = [tok(d_attn), tok(d_attn), tok(d_attn), tok(d_attn),
                 pl.BlockSpec((d_ssm // SCAN_SLAB, t, scan_rows, SCAN_SLAB), lambda j: (0, 0, j, 0)),
                 tok(d_ssm)]
    out_shape = [attn_shape, attn_shape, attn_shape, attn_shape,
                 jax.ShapeDtypeStruct((d_ssm // SCAN_SLAB, t, l // t * b, SCAN_SLAB), BF16),
                 jax.ShapeDtypeStruct((b, l, d_ssm), BF16)]
    if emit_caches:
        out_specs += [pl.BlockSpec((b, tp, d_attn // V_DIM, V_DIM), lambda j: (0, j, 0, 0))] * 2
        out_shape += [jax.ShapeDtypeStruct((b, l, d_attn // V_DIM, V_DIM), F32)] * 2
    return pl.pallas_call(
        functools.partial(_inproj_kernel, rope=rope, emit_caches=emit_caches, d_attn=d_attn),
        grid=(l // tp,),
        in_specs=in_specs,
        out_specs=out_specs,
        out_shape=out_shape,
        compiler_params=_params("arbitrary"),
        name="inproj_rope" if rope else "inproj",
    )(*args)


def _rope_tables(l, d_attn):
    rows = l // GRID_W
    row = np.repeat(np.arange(rows, dtype=np.float64), GRID_W)
    col = np.tile(np.arange(GRID_W, dtype=np.float64), rows)
    n_freq = QK_DIM // 4
    inv = ROPE_THETA ** (-np.arange(n_freq, dtype=np.float64) / n_freq)
    ang_r = row[:, None] * inv
    ang_c = col[:, None] * inv
    cos64 = np.concatenate([np.cos(ang_r), np.cos(ang_r), np.cos(ang_c), np.cos(ang_c)], axis=-1)
    sin64 = np.concatenate([-np.sin(ang_r), np.sin(ang_r), -np.sin(ang_c), np.sin(ang_c)], axis=-1)
    reps = d_attn // QK_DIM
    return (jnp.asarray(np.tile(cos64, (1, reps)), dtype=F32),
            jnp.asarray(np.tile(sin64, (1, reps)), dtype=F32))


def _head(ref, bi, h):
    return ref[bi, :, h, :] if len(ref.shape) == 4 else ref[bi, :, h * V_DIM:(h + 1) * V_DIM]


def _attn_kernel(*refs, has_cache, lam_init):
    if has_cache:
        lam_ref, q_ref, k_ref, v_ref, ck_ref, cv_ref, ga_ref, sub_ref, o_ref, k_scr, vt_scr = refs
    else:
        lam_ref, q_ref, k_ref, v_ref, ga_ref, sub_ref, o_ref, k_scr, vt_scr = refs
    nbk, tq = q_ref.shape[0], q_ref.shape[1]
    n_heads, lk = k_scr.shape[0] // nbk, k_scr.shape[1]
    ln = k_ref.shape[1]
    lam = lam_ref[0:1, 0:1]
    lane = lax.broadcasted_iota(jnp.int32, (1, V_DIM), 1)
    first = lane < QK_DIM
    dims = (((1,), (1,)), ((), ()))
    slot = lambda bi, h: bi * n_heads + h

    @pl.when(pl.program_id(1) == 0)
    def _():
        for bi in range(nbk):
            for h in range(n_heads):
                n = slot(bi, h)
                if has_cache:
                    k_scr[n, 0:lk - ln, :] = _head(ck_ref, bi, h).astype(BF16)
                    vt_scr[n, 0:V_DIM, 0:lk - ln] = _head(cv_ref, bi, h).astype(F32).T.astype(BF16)
                k_scr[n, lk - ln:lk, :] = _head(k_ref, bi, h).astype(BF16)
                vt_scr[n, 0:V_DIM, lk - ln:lk] = _head(v_ref, bi, h).astype(F32).T.astype(BF16)
                vt_scr[n, V_DIM:, :] = jnp.ones((vt_scr.shape[1] - V_DIM, lk), BF16)

    sub = min(tq, ATTN_Q_SUB)
    items = [(bi, h, r) for bi in range(nbk) for h in range(n_heads) for r in range(0, tq, sub)]

    def scores(item):
        bi, h, r = item
        q = _head(q_ref, bi, h)[r:r + sub, :]
        zero = jnp.zeros_like(q)
        k = k_scr[slot(bi, h)]
        return (lax.dot_general(k, jnp.where(first, q, zero), dims, preferred_element_type=F32),
                lax.dot_general(k, jnp.where(first, zero, q), dims, preferred_element_type=F32))

    nxt = scores(items[0])
    for n, (bi, h, r) in enumerate(items):
        st1, st2 = nxt
        if n + 1 < len(items):
            nxt = scores(items[n + 1])
        vt = vt_scr[slot(bi, h)]
        et1 = jnp.exp2(st1 - jnp.max(st1, axis=0, keepdims=True)).astype(BF16)
        et2 = jnp.exp2(st2 - jnp.max(st2, axis=0, keepdims=True)).astype(BF16)
        o1 = jnp.dot(vt, et1, preferred_element_type=F32)
        o2 = jnp.dot(vt, et2, preferred_element_type=F32)
        r1 = 1.0 / o1[V_DIM:V_DIM + 1, :]
        r2 = lam / o2[V_DIM:V_DIM + 1, :]
        o = (o1[0:V_DIM, :] * r1 - o2[0:V_DIM, :] * r2).T
        ms = jnp.mean(o * o, axis=-1, keepdims=True)
        o = o * lax.rsqrt(ms + EPS) * sub_ref[...] * (1.0 - lam_init)
        ga = _head(ga_ref, bi, h)[r:r + sub, :].astype(F32)
        o_ref[bi, r:r + sub, h * V_DIM:(h + 1) * V_DIM] = (o * ga).astype(o_ref.dtype)


def _attention(lam, q, k, v, cache_k, cache_v, ga, subln, lam_init):
    b, l, d_attn = q.shape
    n_heads = d_attn // V_DIM
    tq = min(ATTN_Q_TILE, l)
    nq = l // tq
    nbk = max(1, ATTN_Q_TILE // l)
    has_cache = cache_k is not None
    q_spec = pl.BlockSpec((nbk, tq, d_attn), lambda i, j: (i, j, 0))
    whole = lambda a: pl.BlockSpec((nbk,) + a.shape[1:], lambda i, j: (i,) + (0,) * (a.ndim - 1))
    in_specs = [pl.BlockSpec((8, 128), lambda i, j: (0, 0)), q_spec, whole(k), whole(v)]
    args = [lam, q, k, v]
    if has_cache:
        in_specs += [whole(cache_k), whole(cache_v)]
        args += [cache_k, cache_v]
    in_specs += [q_spec, pl.BlockSpec((1, V_DIM), lambda i, j: (0, 0))]
    args += [ga, subln.reshape(1, V_DIM).astype(F32)]
    lk = l + (cache_k.shape[1] if has_cache else 0)
    out = pl.pallas_call(
        functools.partial(_attn_kernel, has_cache=has_cache, lam_init=lam_init),
        grid=(b // nbk, nq),
        in_specs=in_specs,
        out_specs=q_spec,
        out_shape=jax.ShapeDtypeStruct((b, l, d_attn), BF16),
        scratch_shapes=[pltpu.VMEM((nbk * n_heads, lk, V_DIM), BF16),
                        pltpu.VMEM((nbk * n_heads, V_DIM + SUM_ROWS, lk), BF16)],
        compiler_params=_params("arbitrary", "arbitrary"),
        name="attn_cache" if has_cache else "attn",
    )(*args)
    return out


def _ssm_kernel(*refs, batch, n_chunks, with_state):
    if with_state:
        u_ref, toept_ref, wt_ref, vt_ref, a_ref, h0_ref, y_ref, fin_ref, zt_scr, wc_scr, sin_scr, yt_scr = refs
    else:
        u_ref, toept_ref, wt_ref, vt_ref, a_ref, h0_ref, y_ref, zt_scr, wc_scr, sin_scr, yt_scr = refs
    t = u_ref.shape[0]
    gb = zt_scr.shape[0]
    hc = zt_scr.shape[1] // t
    sw = a_ref.shape[-1]
    hw = sw // 2
    for s in range(t):
        at = u_ref[s].T
        for g in range(gb):
            zt_scr[g, s * hc:(s + 1) * hc, :] = at[g * hc:(g + 1) * hc, :]
    for g in range(gb):
        wc_scr[g] = jnp.dot(wt_ref[g], zt_scr[g], preferred_element_type=F32).astype(BF16).T.astype(F32)
    fwd = lax.broadcasted_iota(jnp.int32, (1, sw), 1) < hw

    def step(j, carry):
        rf = pl.ds(pl.multiple_of(j * batch, batch), batch)
        rb = pl.ds(pl.multiple_of((n_chunks - 1 - j) * batch, batch), batch)
        out = []
        for g in range(gb):
            sr, si = carry[g]
            sin_scr[g, rf, 0:hw] = sr[:, 0:hw]
            sin_scr[g, rb, hw:sw] = sr[:, hw:sw]
            sin_scr[g, rf, sw:sw + hw] = si[:, 0:hw]
            sin_scr[g, rb, sw + hw:2 * sw] = si[:, hw:sw]
            wr = jnp.where(fwd, wc_scr[g, rf, 0:sw], wc_scr[g, rb, 0:sw])
            wi = jnp.where(fwd, wc_scr[g, rf, sw:2 * sw], wc_scr[g, rb, sw:2 * sw])
            ar = a_ref[g, 0]
            ai = a_ref[g, 1]
            out.append((ar * sr - ai * si + wr, ar * si + ai * sr + wi))
        return tuple(out)

    fin = lax.fori_loop(0, n_chunks, step, tuple((h0_ref[g, 0], h0_ref[g, 1]) for g in range(gb)))
    for g in range(gb):
        sint = sin_scr[g].astype(BF16).T
        y2t = (jnp.dot(toept_ref[g], zt_scr[g], preferred_element_type=F32)
               + jnp.dot(vt_ref[g], sint, preferred_element_type=F32))
        for s in range(t):
            yt_scr[s, g * hc:(g + 1) * hc, :] = y2t[s * hc:(s + 1) * hc, :].astype(yt_scr.dtype)
        if with_state:
            fin_ref[g, 0] = fin[g][0]
            fin_ref[g, 1] = fin[g][1]
    for s in range(t):
        y_ref[s] = yt_scr[s].T


def _ssm(u3, ssm_ops, h0, with_state):
    toept, wt, vt, a_t = ssm_ops
    n_slabs, t, rows, cw = u3.shape
    batch = h0.shape[2]
    n_chunks = rows // batch
    g = toept.shape[0]
    gb = g // n_slabs
    slab = pl.BlockSpec((None, t, rows, cw), lambda i: (i, 0, 0, 0))
    blk = lambda a: pl.BlockSpec((gb,) + a.shape[1:], lambda i: (i,) + (0,) * (a.ndim - 1))
    out_specs = [slab]
    out_shape = [jax.ShapeDtypeStruct(u3.shape, u3.dtype)]
    if with_state:
        out_specs.append(blk(h0))
        out_shape.append(jax.ShapeDtypeStruct(h0.shape, F32))
    nk = toept.shape[-1]
    outs = pl.pallas_call(
        functools.partial(_ssm_kernel, batch=batch, n_chunks=n_chunks, with_state=with_state),
        grid=(g // gb,),
        in_specs=[slab, blk(toept), blk(wt), blk(vt), blk(a_t), blk(h0)],
        out_specs=out_specs,
        out_shape=out_shape,
        scratch_shapes=[pltpu.VMEM((gb, nk, rows), BF16),
                        pltpu.VMEM((gb, rows, wt.shape[1]), F32),
                        pltpu.VMEM((gb, rows, wt.shape[1]), F32),
                        pltpu.VMEM((t, cw, rows), u3.dtype)],
        compiler_params=_params("arbitrary"),
        name="ssm_state" if with_state else "ssm",
    )(u3, toept, wt, vt, a_t, h0)
    return outs


def _state_to_lanes(h0):
    b, nd, nc, g, p = h0.shape
    return h0.astype(F32).transpose(3, 2, 0, 1, 4).reshape(g, nc, b, nd * p)


def _state_from_lanes(fin, nd):
    g, nc, b, w = fin.shape
    return fin.reshape(g, nc, b, nd, w // nd).transpose(2, 3, 1, 0, 4)


def _outproj_kernel(x_ref, a_ref, y_ref, u_ref, gs_ref, gate_ref, dskip_ref,
                    wglu_ref, bglu_ref, wout_ref, g_ref, o_ref):
    nb, tp, d = x_ref.shape
    y = _scan_block(y_ref) + dskip_ref[...] * _scan_block(u_ref)
    t, srows, d_ssm = y.shape
    rows = lambda ref: ref[...].reshape(nb * tp, ref.shape[-1])
    y = y.reshape(t * srows, d_ssm)
    c0 = math.sqrt(2.0 / math.pi)
    hy = 0.5 * y
    ys = hy + hy * jnp.tanh(y * (c0 + (c0 * 0.044715) * (y * y)))
    z = jnp.dot(ys.astype(BF16), wglu_ref[...], preferred_element_type=F32) + bglu_ref[...]
    ys = _from_scan_layout((ys * jax.nn.sigmoid(z)).reshape(t, srows, d_ssm), nb) * rows(gs_ref).astype(F32)
    cat = jnp.concatenate([rows(a_ref), ys.astype(BF16)], axis=-1)
    out = jnp.dot(cat, wout_ref[...], preferred_element_type=F32)
    ms = jnp.mean(out * out, axis=-1, keepdims=True)
    out = out * lax.rsqrt(ms + EPS)
    o_ref[...] = x_ref[...] + (g_ref[...] * gate_ref[...]) * out.reshape(nb, tp, d)


def _outproj(x, a_out, y_ssm, u, gs, gate, d_skip, w_glu_bf, b_glu, w_out_bf, norm_post):
    b, l, d = x.shape
    d_attn = a_out.shape[-1]
    d_ssm = gs.shape[-1]
    tp = TOKEN_TILE // b
    t = SSM_CHUNK
    const = lambda a: pl.BlockSpec(a.shape, lambda j: (0,) * a.ndim)
    tok = lambda w: pl.BlockSpec((b, tp, w), lambda j: (0, j, 0))
    scan = pl.BlockSpec((d_ssm // SCAN_SLAB, t, tp // t * b, SCAN_SLAB), lambda j: (0, 0, j, 0))
    consts = [gate, d_skip.reshape(1, d_ssm).astype(F32), w_glu_bf, b_glu.reshape(1, d_ssm).astype(F32),
              w_out_bf, norm_post.reshape(1, d).astype(F32)]
    return pl.pallas_call(
        _outproj_kernel,
        grid=(l // tp,),
        in_specs=[tok(d), tok(d_attn), scan, scan, tok(d_ssm)] + [const(a) for a in consts],
        out_specs=tok(d),
        out_shape=jax.ShapeDtypeStruct((b, l, d), F32),
        compiler_params=_params("arbitrary"),
        name="outproj",
    )(x, a_out, y_ssm, u, gs, *consts)


def _mixer(x, mod, h0_lanes, cache_k, cache_v, rope_tabs, emit_caches, with_state, lam, lam_init,
           norm_pre, w_in, ssm_ops, subln, d_skip, w_glu_bf, b_glu, w_out_bf, norm_post):
    b, l, d = x.shape
    d_ssm = w_glu_bf.shape[0]
    d_attn = (w_in.shape[1] - 2 * d_ssm) // 4
    shift, scale, gate = mod
    q, k, v, ga, u, gs, *caches = _inproj(x, shift, scale, norm_pre, w_in, rope_tabs, emit_caches,
                                          d_attn, d_ssm)
    a_out = _attention(lam, q, k, v, cache_k, cache_v, ga, subln, lam_init)
    outs = _ssm(u, ssm_ops, h0_lanes, with_state)
    y = _outproj(x, a_out, outs[0], u, gs, gate, d_skip, w_glu_bf, b_glu, w_out_bf, norm_post)
    fin = outs[1] if with_state else None
    return y, caches, fin


def kernel(x_prompt, x_sample, cache_k, cache_v, state_ssm, c, c_ctx, w_ada, b_ada, norm_pre, norm_post, w_in, lambda_qk, subln, ssm_A_re, ssm_A_im, ssm_log_dt, ssm_B_re, ssm_B_im, ssm_C_re, ssm_C_im, ssm_D, w_glu, b_glu, w_out):
    xp, xs = x_prompt, x_sample
    bp, lp, d = xp.shape
    bs, ls, _ = xs.shape
    depth = w_in.shape[0]
    nd = ssm_A_re.shape[1]
    g, p = ssm_A_re.shape[-2:]
    d_ssm = w_glu.shape[-1]
    d_attn = (w_in.shape[-1] - 2 * d_ssm) // 4
    rope_tabs = _rope_tables(ls, d_attn)
    cond_rows = 16
    cond = jnp.zeros((cond_rows, d), F32).at[:bs].set(c.astype(F32)).at[bs].set(c_ctx.astype(F32))
    new_k, new_v, new_s = [], [], []
    for layer in range(depth):
        lam_init = 0.8 - 0.6 * math.exp(-0.3 * layer)
        m = _adaln(cond, w_ada[layer], b_ada[layer])
        mod_s = tuple(m[:bs, i * d:(i + 1) * d].reshape(bs, 1, d) for i in range(3))
        mod_p = tuple(m[bs:bs + 1, i * d:(i + 1) * d].reshape(1, 1, d) for i in range(3))
        ssm_ops, lam = _ssmprep(
            ssm_A_re[layer], ssm_A_im[layer], ssm_log_dt[layer], ssm_B_re[layer], ssm_B_im[layer],
            ssm_C_re[layer], ssm_C_im[layer], lambda_qk[layer], lam_init)
        shared = (lam, lam_init, norm_pre[layer], w_in[layer], ssm_ops, subln[layer],
                  ssm_D[layer], w_glu[layer].astype(BF16), b_glu[layer], w_out[layer].astype(BF16),
                  norm_post[layer])

        h0_p = jnp.zeros((g, 2, bp, nd * p), F32)
        xp, (k_p, v_p), fin = _mixer(xp, mod_p, h0_p, None, None, None, True, True, *shared)
        new_k.append(k_p)
        new_v.append(v_p)
        new_s.append(_state_from_lanes(fin, nd).astype(xp.dtype))

        h0_s = _state_to_lanes(state_ssm[:, layer])
        xs, _, _ = _mixer(xs, mod_s, h0_s, cache_k[:, layer], cache_v[:, layer], rope_tabs, False, False,
                             *shared)

    return (xp, xs, jnp.stack(new_k, axis=1), jnp.stack(new_v, axis=1), jnp.stack(new_s, axis=1))
```

```python
import functools
import math

import jax
import jax.numpy as jnp
import numpy as np
from jax import lax
from jax.experimental import pallas as pl
from jax.experimental.pallas import tpu as pltpu

F32 = jnp.float32
BF16 = jnp.bfloat16

GRID_W = 64
QK_DIM = 64
V_DIM = 2 * QK_DIM
ROPE_THETA = 10000.0
EPS = 1e-6
Q_PRESCALE = math.log2(math.e) * QK_DIM ** -0.5

SUM_ROWS = 16
ADALN_K_BLOCK = 256
TOKEN_TILE = 1024
ATTN_Q_TILE = 1024
ATTN_Q_SUB = 512
SSM_CHUNK = 16
SCAN_SLAB = 128
VMEM_LIMIT = 48 * 1024 * 1024


def _silu(x):
    return x * jax.nn.sigmoid(x)


def _params(*sem):
    return pltpu.CompilerParams(dimension_semantics=sem, vmem_limit_bytes=VMEM_LIMIT)


def _adaln_kernel(c_ref, w_ref, b_ref, o_ref):
    @pl.when(pl.program_id(0) == 0)
    def _():
        o_ref[...] = jnp.broadcast_to(b_ref[...], o_ref.shape)

    s = _silu(c_ref[...]).astype(BF16)
    o_ref[...] += jnp.dot(s, w_ref[...].astype(BF16), preferred_element_type=F32)


def _adaln(cond, w_ada, b_ada):
    rows, d = cond.shape
    n = w_ada.shape[1]
    bk = ADALN_K_BLOCK
    return pl.pallas_call(
        _adaln_kernel,
        grid=(d // bk,),
        in_specs=[pl.BlockSpec((rows, bk), lambda j: (0, j)),
                  pl.BlockSpec((bk, n), lambda j: (j, 0)),
                  pl.BlockSpec((1, n), lambda j: (0, 0))],
        out_specs=pl.BlockSpec((rows, n), lambda j: (0, 0)),
        out_shape=jax.ShapeDtypeStruct((rows, n), F32),
        compiler_params=_params("arbitrary"),
        name="adaln",
    )(cond, w_ada, b_ada.reshape(1, n))


def _ssmprep_kernel(are_ref, aim_ref, ldt_ref, bre_ref, bim_ref, cre_ref, cim_ref, lq_ref,
                    toept_ref, wt_ref, vt_ref, a_ref, lam_ref,
                    cplre_scr, cplimn_scr, w_scr, v_scr, toep_scr, *, lam_init, chunk):
    a_re = are_ref[...]
    a_im = aim_ref[...]
    dt = jnp.exp(ldt_ref[...])
    mag = jnp.exp(a_re * dt)
    ab_re = mag * jnp.cos(a_im * dt)
    ab_im = mag * jnp.sin(a_im * dt)
    nr, ni = ab_re - 1.0, ab_im
    den = a_re * a_re + a_im * a_im
    f_re = (nr * a_re + ni * a_im) / den
    f_im = (ni * a_re - nr * a_im) / den
    b_re = bre_ref[...]
    b_im = bim_ref[...]
    bb_re = f_re * b_re - f_im * b_im
    bb_im = f_re * b_im + f_im * b_re
    c_re = cre_ref[...]
    c_im = cim_ref[...]
    hc = b_re.shape[1]
    sw = a_re.shape[-1]
    hw = sw // 2
    lo, hi = slice(0, hw), slice(hw, sw)
    lo_im, hi_im = slice(sw, sw + hw), slice(sw + hw, 2 * sw)
    blk = lambda j: slice(j * hc, (j + 1) * hc)
    last = chunk - 1
    cplre_scr[...] = jnp.zeros_like(cplre_scr)
    cplimn_scr[...] = jnp.zeros_like(cplimn_scr)
    pw_re = jnp.ones_like(ab_re)
    pw_im = jnp.zeros_like(ab_im)
    for tau in range(chunk + 1):
        cp_re = c_re * pw_re - c_im * pw_im
        cp_imn = -(c_re * pw_im + c_im * pw_re)
        if tau < chunk:
            cplre_scr[:, blk(last + tau), lo] = cp_re[:, :, lo]
            cplre_scr[:, blk(last - tau), hi] = cp_re[:, :, hi]
            cplimn_scr[:, blk(last + tau), lo] = cp_imn[:, :, lo]
            cplimn_scr[:, blk(last - tau), hi] = cp_imn[:, :, hi]
            e_re = pw_re * bb_re - pw_im * bb_im
            e_im = pw_re * bb_im + pw_im * bb_re
            w_scr[:, blk(last - tau), lo] = e_re[:, :, lo]
            w_scr[:, blk(tau), hi] = e_re[:, :, hi]
            w_scr[:, blk(last - tau), lo_im] = e_im[:, :, lo]
            w_scr[:, blk(tau), hi_im] = e_im[:, :, hi]
        if tau >= 1:
            v_scr[:, blk(tau - 1), lo] = cp_re[:, :, lo]
            v_scr[:, blk(chunk - tau), hi] = cp_re[:, :, hi]
            v_scr[:, blk(tau - 1), lo_im] = cp_imn[:, :, lo]
            v_scr[:, blk(chunk - tau), hi_im] = cp_imn[:, :, hi]
        if tau < chunk:
            pw_re, pw_im = pw_re * ab_re - pw_im * ab_im, pw_re * ab_im + pw_im * ab_re
    a_ref[:, 0] = pw_re
    a_ref[:, 1] = pw_im
    dims = (((2,), (2,)), ((0,), (0,)))
    taps = (lax.dot_general(bb_re.astype(BF16), cplre_scr[...].astype(BF16), dims,
                            preferred_element_type=F32)
            + lax.dot_general(bb_im.astype(BF16), cplimn_scr[...].astype(BF16), dims,
                              preferred_element_type=F32))
    nk = chunk * hc
    for s in range(chunk):
        toep_scr[:, blk(s), :] = taps[:, :, (last - s) * hc:(last - s) * hc + nk]
    for g in range(toept_ref.shape[0]):
        toept_ref[g] = toep_scr[g].T.astype(toept_ref.dtype)
        wt_ref[g] = w_scr[g].T.astype(wt_ref.dtype)
    vt_ref[...] = v_scr[...].astype(vt_ref.dtype)
    lq = lq_ref[...]
    s01 = jnp.sum(lq[0:1] * lq[1:2], axis=-1, keepdims=True)
    s23 = jnp.sum(lq[2:3] * lq[3:4], axis=-1, keepdims=True)
    lam = jnp.exp(s01) - jnp.exp(s23) + lam_init
    lam_ref[...] = jnp.broadcast_to(lam, lam_ref.shape)


def _ssmprep(a_re, a_im, log_dt, b_re, b_im, c_re, c_im, lq, lam_init):
    nd, g, p = a_re.shape
    hc = b_re.shape[-1]
    t = SSM_CHUNK
    sw = nd * p
    gb = SCAN_SLAB // hc
    row = lambda a: a.astype(F32).transpose(1, 0, 2).reshape(g, 1, sw)
    ldt = jnp.broadcast_to(log_dt[..., None], (nd, g, p))
    bt = lambda a: a.astype(F32).transpose(1, 3, 0, 2).reshape(g, hc, sw)
    ct = lambda a: a.astype(F32).transpose(1, 2, 0, 3).reshape(g, hc, sw)
    sds = jax.ShapeDtypeStruct
    nk = t * hc
    blk = lambda shape: pl.BlockSpec((gb,) + shape[1:], lambda i: (i,) + (0,) * (len(shape) - 1))
    mat = (g, nk, 2 * sw)
    toept, wt, vt, a_t, lam = pl.pallas_call(
        functools.partial(_ssmprep_kernel, lam_init=lam_init, chunk=t),
        grid=(g // gb,),
        in_specs=[blk((g, 1, sw))] * 3 + [blk((g, hc, sw))] * 4 + [pl.BlockSpec(lq.shape, lambda i: (0, 0))],
        out_specs=[blk((g, nk, nk)), blk((g, 2 * sw, nk)), blk(mat), blk((g, 2, 1, sw)),
                   pl.BlockSpec((8, 128), lambda i: (0, 0))],
        out_shape=(sds((g, nk, nk), BF16), sds((g, 2 * sw, nk), BF16), sds(mat, BF16),
                   sds((g, 2, 1, sw), F32), sds((8, 128), F32)),
        scratch_shapes=[pltpu.VMEM((gb, 2 * nk, sw), F32), pltpu.VMEM((gb, 2 * nk, sw), F32),
                        pltpu.VMEM((gb, nk, 2 * sw), F32), pltpu.VMEM((gb, nk, 2 * sw), F32),
                        pltpu.VMEM((gb, nk, nk), F32)],
        compiler_params=_params("arbitrary"),
        name="ssmprep",
    )(row(a_re), row(a_im), row(ldt), bt(b_re), bt(b_im), ct(c_re), ct(c_im), lq.astype(F32))
    return (toept, wt, vt, a_t), lam


def _to_scan_layout(u, o_ref, batch):
    n_slabs, t, _, cw = o_ref.shape
    tp = u.shape[0] // batch
    u_tb = jnp.swapaxes(u.reshape(batch, tp, u.shape[-1]), 0, 1)
    for c in range(tp // t):
        for s in range(t):
            for v in range(n_slabs):
                o_ref[v, s, c * batch:(c + 1) * batch, :] = u_tb[c * t + s][:, v * cw:(v + 1) * cw].astype(o_ref.dtype)


def _scan_block(ref):
    return jnp.concatenate([ref[v] for v in range(ref.shape[0])], axis=-1).astype(F32)


def _from_scan_layout(ref, batch):
    t, rows, d = ref.shape
    tp = rows // batch * t
    u_tb = jnp.stack([ref[s, c * batch:(c + 1) * batch, :] for c in range(tp // t) for s in range(t)], axis=0)
    return jnp.swapaxes(u_tb, 0, 1).reshape(batch * tp, d)


def _inproj_kernel(*refs, rope, emit_caches, d_attn):
    n_in = 7 if rope else 5
    x_ref, shift_ref, scale_ref, g_ref, w_ref = refs[:5]
    cos_ref, sin_ref = refs[5:7] if rope else (None, None)
    q_ref, k_ref, v_ref, ga_ref, u_ref, gs_ref = refs[n_in:n_in + 6]
    kc_ref, vc_ref = refs[n_in + 6:] if emit_caches else (None, None)
    x = x_ref[...]
    nb, tp, d = x.shape
    ms = jnp.mean(x * x, axis=-1, keepdims=True)
    h = x * lax.rsqrt(ms + EPS) * (g_ref[...] * (1.0 + scale_ref[...])) + shift_ref[...]
    hb = h.reshape(nb * tp, d).astype(BF16)
    proj = lambda lo, hi: jnp.dot(hb, w_ref[:, lo:hi].astype(BF16), preferred_element_type=F32)
    da = d_attn
    d_ssm = u_ref.shape[0] * u_ref.shape[-1]
    tok = lambda z: z.reshape(nb, tp, z.shape[-1])
    q = proj(0, da)
    k = proj(da, 2 * da)
    if rope:
        lane = lax.broadcasted_iota(jnp.int32, (1, da), 1)
        low = (lane % (QK_DIM // 2)) < (QK_DIM // 4)
        cos = cos_ref[...]
        sin = sin_ref[...]

        def rot(z):
            partner = jnp.where(low, pltpu.roll(z, da - QK_DIM // 4, 1), pltpu.roll(z, QK_DIM // 4, 1))
            return tok(z) * cos + tok(partner) * sin

        q = rot(q)
        k = rot(k)
    else:
        q = tok(q)
        k = tok(k)
    q_ref[...] = (q * Q_PRESCALE).astype(q_ref.dtype)
    k_ref[...] = k.astype(k_ref.dtype)
    n_heads = da // V_DIM
    if emit_caches:
        for hd in range(n_heads):
            kc_ref[:, pl.ds(hd, tp, stride=n_heads), :] = k[:, :, hd * V_DIM:(hd + 1) * V_DIM]
    v = tok(proj(2 * da, 3 * da))
    v_ref[...] = v.astype(v_ref.dtype)
    if emit_caches:
        for hd in range(n_heads):
            vc_ref[:, pl.ds(hd, tp, stride=n_heads), :] = v[:, :, hd * V_DIM:(hd + 1) * V_DIM]
    ga_ref[...] = tok(_silu(proj(3 * da, 4 * da))).astype(ga_ref.dtype)
    gs_ref[...] = tok(_silu(proj(4 * da + d_ssm, 4 * da + 2 * d_ssm))).astype(gs_ref.dtype)
    _to_scan_layout(proj(4 * da, 4 * da + d_ssm), u_ref, nb)


def _inproj(x, shift, scale, norm_pre, w_in, rope_tabs, emit_caches, d_attn, d_ssm):
    b, l, d = x.shape
    tp = TOKEN_TILE // b
    t = SSM_CHUNK
    rope = rope_tabs is not None
    const = lambda a: pl.BlockSpec(a.shape, lambda j: (0,) * a.ndim)
    tok = lambda w: pl.BlockSpec((b, tp, w), lambda j: (0, j, 0))
    g2 = norm_pre.reshape(1, d).astype(F32)
    w_spec = pl.BlockSpec(w_in.shape, lambda j: (0, 0), pipeline_mode=pl.Buffered(1))
    in_specs = [tok(d), const(shift), const(scale), const(g2), w_spec]
    args = [x, shift, scale, g2, w_in]
    if rope:
        in_specs += [pl.BlockSpec((tp, d_attn), lambda j: (j, 0))] * 2
        args += list(rope_tabs)
    scan_rows = tp // t * b
    attn_shape = jax.ShapeDtypeStruct((b, l, d_attn), BF16)
    out_specs = [tok(d_attn), tok(d_attn), tok(d_attn), tok(d_attn),
                 pl.BlockSpec((d_ssm // SCAN_SLAB, t, scan_rows, SCAN_SLAB), lambda j: (0, 0, j, 0)),
                 tok(d_ssm)]
    out_shape = [attn_shape, attn_shape, attn_shape, attn_shape,
                 jax.ShapeDtypeStruct((d_ssm // SCAN_SLAB, t, l // t * b, SCAN_SLAB), BF16),
                 jax.ShapeDtypeStruct((b, l, d_ssm), BF16)]
    if emit_caches:
        n_heads = d_attn // V_DIM
        out_specs += [pl.BlockSpec((b, tp * n_heads, V_DIM), lambda j: (0, j, 0))] * 2
        out_shape += [jax.ShapeDtypeStruct((b, l * n_heads, V_DIM), F32)] * 2
    return pl.pallas_call(
        functools.partial(_inproj_kernel, rope=rope, emit_caches=emit_caches, d_attn=d_attn),
        grid=(l // tp,),
        in_specs=in_specs,
        out_specs=out_specs,
        out_shape=out_shape,
        compiler_params=_params("arbitrary"),
        name="inproj_rope" if rope else "inproj",
    )(*args)


def _rope_tables(l, d_attn):
    rows = l // GRID_W
    row = np.repeat(np.arange(rows, dtype=np.float64), GRID_W)
    col = np.tile(np.arange(GRID_W, dtype=np.float64), rows)
    n_freq = QK_DIM // 4
    inv = ROPE_THETA ** (-np.arange(n_freq, dtype=np.float64) / n_freq)
    ang_r = row[:, None] * inv
    ang_c = col[:, None] * inv
    cos64 = np.concatenate([np.cos(ang_r), np.cos(ang_r), np.cos(ang_c), np.cos(ang_c)], axis=-1)
    sin64 = np.concatenate([-np.sin(ang_r), np.sin(ang_r), -np.sin(ang_c), np.sin(ang_c)], axis=-1)
    reps = d_attn // QK_DIM
    return (jnp.asarray(np.tile(cos64, (1, reps)), dtype=F32),
            jnp.asarray(np.tile(sin64, (1, reps)), dtype=F32))


def _head(ref, bi, h, n_heads):
    if ref.shape[-1] == V_DIM:
        return ref[bi, pl.ds(h, ref.shape[1] // n_heads, stride=n_heads), :]
    return ref[bi, :, h * V_DIM:(h + 1) * V_DIM]


def _attn_kernel(*refs, has_cache, lam_init):
    if has_cache:
        lam_ref, q_ref, k_ref, v_ref, ck_ref, cv_ref, ga_ref, sub_ref, o_ref, k_scr, vt_scr = refs
    else:
        lam_ref, q_ref, k_ref, v_ref, ga_ref, sub_ref, o_ref, k_scr, vt_scr = refs
    nbk, tq = q_ref.shape[0], q_ref.shape[1]
    n_heads, lk = k_scr.shape[0] // nbk, k_scr.shape[1]
    ln = k_ref.shape[1]
    lam = lam_ref[0:1, 0:1]
    lane = lax.broadcasted_iota(jnp.int32, (1, V_DIM), 1)
    first = lane < QK_DIM
    dims = (((1,), (1,)), ((), ()))
    slot = lambda bi, h: bi * n_heads + h

    @pl.when(pl.program_id(1) == 0)
    def _():
        for bi in range(nbk):
            for h in range(n_heads):
                n = slot(bi, h)
                if has_cache:
                    k_scr[n, 0:lk - ln, :] = _head(ck_ref, bi, h, n_heads).astype(BF16)
                    vt_scr[n, 0:V_DIM, 0:lk - ln] = _head(cv_ref, bi, h, n_heads).astype(F32).T.astype(BF16)
                k_scr[n, lk - ln:lk, :] = _head(k_ref, bi, h, n_heads).astype(BF16)
                vt_scr[n, 0:V_DIM, lk - ln:lk] = _head(v_ref, bi, h, n_heads).astype(F32).T.astype(BF16)
                vt_scr[n, V_DIM:, :] = jnp.ones((vt_scr.shape[1] - V_DIM, lk), BF16)

    sub = min(tq, ATTN_Q_SUB)
    items = [(bi, h, r) for bi in range(nbk) for h in range(n_heads) for r in range(0, tq, sub)]

    def scores(item):
        bi, h, r = item
        q = _head(q_ref, bi, h, n_heads)[r:r + sub, :]
        zero = jnp.zeros_like(q)
        k = k_scr[slot(bi, h)]
        return (lax.dot_general(k, jnp.where(first, q, zero), dims, preferred_element_type=F32),
                lax.dot_general(k, jnp.where(first, zero, q), dims, preferred_element_type=F32))

    nxt = scores(items[0])
    for n, (bi, h, r) in enumerate(items):
        st1, st2 = nxt
        if n + 1 < len(items):
            nxt = scores(items[n + 1])
        vt = vt_scr[slot(bi, h)]
        et1 = jnp.exp2(st1 - jnp.max(st1, axis=0, keepdims=True)).astype(BF16)
        et2 = jnp.exp2(st2 - jnp.max(st2, axis=0, keepdims=True)).astype(BF16)
        o1 = jnp.dot(vt, et1, preferred_element_type=F32)
        o2 = jnp.dot(vt, et2, preferred_element_type=F32)
        r1 = 1.0 / o1[V_DIM:V_DIM + 1, :]
        r2 = lam / o2[V_DIM:V_DIM + 1, :]
        o = (o1[0:V_DIM, :] * r1 - o2[0:V_DIM, :] * r2).T
        ms = jnp.mean(o * o, axis=-1, keepdims=True)
        o = o * lax.rsqrt(ms + EPS) * sub_ref[...] * (1.0 - lam_init)
        ga = _head(ga_ref, bi, h, n_heads)[r:r + sub, :].astype(F32)
        o_ref[bi, r:r + sub, h * V_DIM:(h + 1) * V_DIM] = (o * ga).astype(o_ref.dtype)


def _attention(lam, q, k, v, cache_k, cache_v, ga, subln, lam_init):
    b, l, d_attn = q.shape
    n_heads = d_attn // V_DIM
    tq = min(ATTN_Q_TILE, l)
    nq = l // tq
    nbk = max(1, ATTN_Q_TILE // l)
    has_cache = cache_k is not None
    q_spec = pl.BlockSpec((nbk, tq, d_attn), lambda i, j: (i, j, 0))
    whole = lambda a: pl.BlockSpec((nbk,) + a.shape[1:], lambda i, j: (i,) + (0,) * (a.ndim - 1))
    in_specs = [pl.BlockSpec((8, 128), lambda i, j: (0, 0)), q_spec, whole(k), whole(v)]
    args = [lam, q, k, v]
    if has_cache:
        in_specs += [whole(cache_k), whole(cache_v)]
        args += [cache_k, cache_v]
    in_specs += [q_spec, pl.BlockSpec((1, V_DIM), lambda i, j: (0, 0))]
    args += [ga, subln.reshape(1, V_DIM).astype(F32)]
    lk = l + (cache_k.shape[1] // n_heads if has_cache else 0)
    out = pl.pallas_call(
        functools.partial(_attn_kernel, has_cache=has_cache, lam_init=lam_init),
        grid=(b // nbk, nq),
        in_specs=in_specs,
        out_specs=q_spec,
        out_shape=jax.ShapeDtypeStruct((b, l, d_attn), BF16),
        scratch_shapes=[pltpu.VMEM((nbk * n_heads, lk, V_DIM), BF16),
                        pltpu.VMEM((nbk * n_heads, V_DIM + SUM_ROWS, lk), BF16)],
        compiler_params=_params("arbitrary", "arbitrary"),
        name="attn_cache" if has_cache else "attn",
    )(*args)
    return out


def _ssm_kernel(*refs, batch, n_chunks, with_state):
    if with_state:
        u_ref, toept_ref, wt_ref, vt_ref, a_ref, h0_ref, y_ref, fin_ref, zt_scr, wc_scr, sin_scr, yt_scr = refs
    else:
        u_ref, toept_ref, wt_ref, vt_ref, a_ref, h0_ref, y_ref, zt_scr, wc_scr, sin_scr, yt_scr = refs
    t = u_ref.shape[0]
    gb = zt_scr.shape[0]
    hc = zt_scr.shape[1] // t
    sw = a_ref.shape[-1]
    hw = sw // 2
    for s in range(t):
        at = u_ref[s].T
        for g in range(gb):
            zt_scr[g, s * hc:(s + 1) * hc, :] = at[g * hc:(g + 1) * hc, :]
    for g in range(gb):
        wc_scr[g] = jnp.dot(wt_ref[g], zt_scr[g], preferred_element_type=F32).astype(BF16).T.astype(F32)
    fwd = lax.broadcasted_iota(jnp.int32, (1, sw), 1) < hw

    def step(j, carry):
        rf = pl.ds(pl.multiple_of(j * batch, batch), batch)
        rb = pl.ds(pl.multiple_of((n_chunks - 1 - j) * batch, batch), batch)
        out = []
        for g in range(gb):
            sr, si = carry[g]
            sin_scr[g, rf, 0:hw] = sr[:, 0:hw]
            sin_scr[g, rb, hw:sw] = sr[:, hw:sw]
            sin_scr[g, rf, sw:sw + hw] = si[:, 0:hw]
            sin_scr[g, rb, sw + hw:2 * sw] = si[:, hw:sw]
            wr = jnp.where(fwd, wc_scr[g, rf, 0:sw], wc_scr[g, rb, 0:sw])
            wi = jnp.where(fwd, wc_scr[g, rf, sw:2 * sw], wc_scr[g, rb, sw:2 * sw])
            ar = a_ref[g, 0]
            ai = a_ref[g, 1]
            out.append((ar * sr - ai * si + wr, ar * si + ai * sr + wi))
        return tuple(out)

    fin = lax.fori_loop(0, n_chunks, step, tuple((h0_ref[g, 0], h0_ref[g, 1]) for g in range(gb)))
    for g in range(gb):
        sint = sin_scr[g].astype(BF16).T
        y2t = (jnp.dot(toept_ref[g], zt_scr[g], preferred_element_type=F32)
               + jnp.dot(vt_ref[g], sint, preferred_element_type=F32))
        for s in range(t):
            yt_scr[s, g * hc:(g + 1) * hc, :] = y2t[s * hc:(s + 1) * hc, :].astype(yt_scr.dtype)
        if with_state:
            fin_ref[g, 0] = fin[g][0]
            fin_ref[g, 1] = fin[g][1]
    for s in range(t):
        y_ref[s] = yt_scr[s].T


def _ssm(u3, ssm_ops, h0, with_state):
    toept, wt, vt, a_t = ssm_ops
    n_slabs, t, rows, cw = u3.shape
    batch = h0.shape[2]
    n_chunks = rows // batch
    g = toept.shape[0]
    gb = g // n_slabs
    slab = pl.BlockSpec((None, t, rows, cw), lambda i: (i, 0, 0, 0))
    blk = lambda a: pl.BlockSpec((gb,) + a.shape[1:], lambda i: (i,) + (0,) * (a.ndim - 1))
    out_specs = [slab]
    out_shape = [jax.ShapeDtypeStruct(u3.shape, u3.dtype)]
    if with_state:
        out_specs.append(blk(h0))
        out_shape.append(jax.ShapeDtypeStruct(h0.shape, F32))
    nk = toept.shape[-1]
    outs = pl.pallas_call(
        functools.partial(_ssm_kernel, batch=batch, n_chunks=n_chunks, with_state=with_state),
        grid=(g // gb,),
        in_specs=[slab, blk(toept), blk(wt), blk(vt), blk(a_t), blk(h0)],
        out_specs=out_specs,
        out_shape=out_shape,
        scratch_shapes=[pltpu.VMEM((gb, nk, rows), BF16),
                        pltpu.VMEM((gb, rows, wt.shape[1]), F32),
                        pltpu.VMEM((gb, rows, wt.shape[1]), F32),
                        pltpu.VMEM((t, cw, rows), u3.dtype)],
        compiler_params=_params("arbitrary"),
        name="ssm_state" if with_state else "ssm",
    )(u3, toept, wt, vt, a_t, h0)
    return outs


def _state_to_lanes(h0):
    b, nd, nc, g, p = h0.shape
    return h0.astype(F32).transpose(3, 2, 0, 1, 4).reshape(g, nc, b, nd * p)


def _state_from_lanes(fin, nd):
    g, nc, b, w = fin.shape
    return fin.reshape(g, nc, b, nd, w // nd).transpose(2, 3, 1, 0, 4)


def _outproj_kernel(x_ref, a_ref, y_ref, u_ref, gs_ref, gate_ref, dskip_ref,
                    wglu_ref, bglu_ref, wout_ref, g_ref, o_ref):
    nb, tp, d = x_ref.shape
    y = _scan_block(y_ref) + dskip_ref[...] * _scan_block(u_ref)
    t, srows, d_ssm = y.shape
    rows = lambda ref: ref[...].reshape(nb * tp, ref.shape[-1])
    y = y.reshape(t * srows, d_ssm)
    c0 = math.sqrt(2.0 / math.pi)
    hy = 0.5 * y
    ys = hy + hy * jnp.tanh(y * (c0 + (c0 * 0.044715) * (y * y)))
    z = jnp.dot(ys.astype(BF16), wglu_ref[...], preferred_element_type=F32) + bglu_ref[...]
    ys = _from_scan_layout((ys * jax.nn.sigmoid(z)).reshape(t, srows, d_ssm), nb) * rows(gs_ref).astype(F32)
    cat = jnp.concatenate([rows(a_ref), ys.astype(BF16)], axis=-1)
    out = jnp.dot(cat, wout_ref[...], preferred_element_type=F32)
    ms = jnp.mean(out * out, axis=-1, keepdims=True)
    out = out * lax.rsqrt(ms + EPS)
    o_ref[...] = x_ref[...] + (g_ref[...] * gate_ref[...]) * out.reshape(nb, tp, d)


def _outproj(x, a_out, y_ssm, u, gs, gate, d_skip, w_glu_bf, b_glu, w_out_bf, norm_post):
    b, l, d = x.shape
    d_attn = a_out.shape[-1]
    d_ssm = gs.shape[-1]
    tp = TOKEN_TILE // b
    t = SSM_CHUNK
    const = lambda a: pl.BlockSpec(a.shape, lambda j: (0,) * a.ndim)
    tok = lambda w: pl.BlockSpec((b, tp, w), lambda j: (0, j, 0))
    scan = pl.BlockSpec((d_ssm // SCAN_SLAB, t, tp // t * b, SCAN_SLAB), lambda j: (0, 0, j, 0))
    consts = [gate, d_skip.reshape(1, d_ssm).astype(F32), w_glu_bf, b_glu.reshape(1, d_ssm).astype(F32),
              w_out_bf, norm_post.reshape(1, d).astype(F32)]
    return pl.pallas_call(
        _outproj_kernel,
        grid=(l // tp,),
        in_specs=[tok(d), tok(d_attn), scan, scan, tok(d_ssm)] + [const(a) for a in consts],
        out_specs=tok(d),
        out_shape=jax.ShapeDtypeStruct((b, l, d), F32),
        compiler_params=_params("arbitrary"),
        name="outproj",
    )(x, a_out, y_ssm, u, gs, *consts)


def _mixer(x, mod, h0_lanes, cache_k, cache_v, rope_tabs, emit_caches, with_state, lam, lam_init,
           norm_pre, w_in, ssm_ops, subln, d_skip, w_glu_bf, b_glu, w_out_bf, norm_post):
    b, l, d = x.shape
    d_ssm = w_glu_bf.shape[0]
    d_attn = (w_in.shape[1] - 2 * d_ssm) // 4
    shift, scale, gate = mod
    q, k, v, ga, u, gs, *caches = _inproj(x, shift, scale, norm_pre, w_in, rope_tabs, emit_caches,
                                          d_attn, d_ssm)
    a_out = _attention(lam, q, k, v, cache_k, cache_v, ga, subln, lam_init)
    outs = _ssm(u, ssm_ops, h0_lanes, with_state)
    y = _outproj(x, a_out, outs[0], u, gs, gate, d_skip, w_glu_bf, b_glu, w_out_bf, norm_post)
    fin = outs[1] if with_state else None
    return y, caches, fin


def kernel(x_prompt, x_sample, cache_k, cache_v, state_ssm, c, c_ctx, w_ada, b_ada, norm_pre, norm_post, w_in, lambda_qk, subln, ssm_A_re, ssm_A_im, ssm_log_dt, ssm_B_re, ssm_B_im, ssm_C_re, ssm_C_im, ssm_D, w_glu, b_glu, w_out):
    xp, xs = x_prompt, x_sample
    bp, lp, d = xp.shape
    bs, ls, _ = xs.shape
    depth = w_in.shape[0]
    nd = ssm_A_re.shape[1]
    g, p = ssm_A_re.shape[-2:]
    d_ssm = w_glu.shape[-1]
    d_attn = (w_in.shape[-1] - 2 * d_ssm) // 4
    rope_tabs = _rope_tables(ls, d_attn)
    cond_rows = 16
    cond = jnp.zeros((cond_rows, d), F32).at[:bs].set(c.astype(F32)).at[bs].set(c_ctx.astype(F32))
    new_k, new_v, new_s = [], [], []
    for layer in range(depth):
        lam_init = 0.8 - 0.6 * math.exp(-0.3 * layer)
        m = _adaln(cond, w_ada[layer], b_ada[layer])
        mod_s = tuple(m[:bs, i * d:(i + 1) * d].reshape(bs, 1, d) for i in range(3))
        mod_p = tuple(m[bs:bs + 1, i * d:(i + 1) * d].reshape(1, 1, d) for i in range(3))
        ssm_ops, lam = _ssmprep(
            ssm_A_re[layer], ssm_A_im[layer], ssm_log_dt[layer], ssm_B_re[layer], ssm_B_im[layer],
            ssm_C_re[layer], ssm_C_im[layer], lambda_qk[layer], lam_init)
        shared = (lam, lam_init, norm_pre[layer], w_in[layer], ssm_ops, subln[layer],
                  ssm_D[layer], w_glu[layer].astype(BF16), b_glu[layer], w_out[layer].astype(BF16),
                  norm_post[layer])

        h0_p = jnp.zeros((g, 2, bp, nd * p), F32)
        xp, (k_p, v_p), fin = _mixer(xp, mod_p, h0_p, None, None, None, True, True, *shared)
        new_k.append(k_p.reshape(bp, lp, d_attn // V_DIM, V_DIM))
        new_v.append(v_p.reshape(bp, lp, d_attn // V_DIM, V_DIM))
        new_s.append(_state_from_lanes(fin, nd).astype(xp.dtype))

        h0_s = _state_to_lanes(state_ssm[:, layer])
        ck = cache_k[:, layer].reshape(bs, -1, V_DIM)
        cv = cache_v[:, layer].reshape(bs, -1, V_DIM)
        xs, _, _ = _mixer(xs, mod_s, h0_s, ck, cv, rope_tabs, False, False, *shared)

    return (xp, xs, jnp.stack(new_k, axis=1), jnp.stack(new_v, axis=1), jnp.stack(new_s, axis=1))
```

```python
import functools
import math

import jax
import jax.numpy as jnp
import numpy as np
from jax import lax
from jax.experimental import pallas as pl
from jax.experimental.pallas import tpu as pltpu

F32 = jnp.float32
BF16 = jnp.bfloat16

GRID_W = 64
QK_DIM = 64
V_DIM = 2 * QK_DIM
ROPE_THETA = 10000.0
EPS = 1e-6
Q_PRESCALE = math.log2(math.e) * QK_DIM ** -0.5

VREG_TILE = (8, 128)
SUM_ROWS = 16
COND_ROWS = 16
ADALN_K_BLOCK = 256
TOKEN_TILE = 1024
ATTN_Q_TILE = 1024
ATTN_Q_SUB = 512
SSM_CHUNK = 16
SCAN_SLAB = 128
VMEM_LIMIT = 48 * 1024 * 1024


def _silu(x):
    return x * jax.nn.sigmoid(x)


def _params(*sem):
    return pltpu.CompilerParams(dimension_semantics=sem, vmem_limit_bytes=VMEM_LIMIT)


def _adaln_kernel(c_ref, w_ref, b_ref, o_ref):
    @pl.when(pl.program_id(0) == 0)
    def _():
        o_ref[...] = jnp.broadcast_to(b_ref[...], o_ref.shape)

    s = _silu(c_ref[...]).astype(BF16)
    o_ref[...] += jnp.dot(s, w_ref[...].astype(BF16), preferred_element_type=F32)


def _adaln(cond, w_ada, b_ada):
    rows, d = cond.shape
    n = w_ada.shape[1]
    bk = ADALN_K_BLOCK
    return pl.pallas_call(
        _adaln_kernel,
        grid=(d // bk,),
        in_specs=[pl.BlockSpec((rows, bk), lambda j: (0, j)),
                  pl.BlockSpec((bk, n), lambda j: (j, 0)),
                  pl.BlockSpec((1, n), lambda j: (0, 0))],
        out_specs=pl.BlockSpec((rows, n), lambda j: (0, 0)),
        out_shape=jax.ShapeDtypeStruct((rows, n), F32),
        compiler_params=_params("arbitrary"),
        name="adaln",
    )(cond, w_ada, b_ada.reshape(1, n))


def _ssmprep_kernel(are_ref, aim_ref, ldt_ref, bre_ref, bim_ref, cre_ref, cim_ref, lq_ref,
                    toept_ref, wt_ref, vt_ref, a_ref, lam_ref,
                    cplre_scr, cplimn_scr, w_scr, v_scr, toep_scr, *, lam_init, chunk):
    a_re = are_ref[...]
    a_im = aim_ref[...]
    dt = jnp.exp(ldt_ref[...])
    mag = jnp.exp(a_re * dt)
    ab_re = mag * jnp.cos(a_im * dt)
    ab_im = mag * jnp.sin(a_im * dt)
    nr, ni = ab_re - 1.0, ab_im
    den = a_re * a_re + a_im * a_im
    f_re = (nr * a_re + ni * a_im) / den
    f_im = (ni * a_re - nr * a_im) / den
    b_re = bre_ref[...]
    b_im = bim_ref[...]
    bb_re = f_re * b_re - f_im * b_im
    bb_im = f_re * b_im + f_im * b_re
    c_re = cre_ref[...]
    c_im = cim_ref[...]
    hc = b_re.shape[1]
    sw = a_re.shape[-1]
    hw = sw // 2
    lo, hi = slice(0, hw), slice(hw, sw)
    lo_im, hi_im = slice(sw, sw + hw), slice(sw + hw, 2 * sw)
    blk = lambda j: slice(j * hc, (j + 1) * hc)
    last = chunk - 1
    cplre_scr[...] = jnp.zeros_like(cplre_scr)
    cplimn_scr[...] = jnp.zeros_like(cplimn_scr)
    pw_re = jnp.ones_like(ab_re)
    pw_im = jnp.zeros_like(ab_im)
    for tau in range(chunk + 1):
        cp_re = c_re * pw_re - c_im * pw_im
        cp_imn = -(c_re * pw_im + c_im * pw_re)
        if tau < chunk:
            cplre_scr[:, blk(last + tau), lo] = cp_re[:, :, lo]
            cplre_scr[:, blk(last - tau), hi] = cp_re[:, :, hi]
            cplimn_scr[:, blk(last + tau), lo] = cp_imn[:, :, lo]
            cplimn_scr[:, blk(last - tau), hi] = cp_imn[:, :, hi]
            e_re = pw_re * bb_re - pw_im * bb_im
            e_im = pw_re * bb_im + pw_im * bb_re
            w_scr[:, blk(last - tau), lo] = e_re[:, :, lo]
            w_scr[:, blk(tau), hi] = e_re[:, :, hi]
            w_scr[:, blk(last - tau), lo_im] = e_im[:, :, lo]
            w_scr[:, blk(tau), hi_im] = e_im[:, :, hi]
        if tau >= 1:
            v_scr[:, blk(tau - 1), lo] = cp_re[:, :, lo]
            v_scr[:, blk(chunk - tau), hi] = cp_re[:, :, hi]
            v_scr[:, blk(tau - 1), lo_im] = cp_imn[:, :, lo]
            v_scr[:, blk(chunk - tau), hi_im] = cp_imn[:, :, hi]
        if tau < chunk:
            pw_re, pw_im = pw_re * ab_re - pw_im * ab_im, pw_re * ab_im + pw_im * ab_re
    a_ref[:, 0] = pw_re
    a_ref[:, 1] = pw_im
    dims = (((2,), (2,)), ((0,), (0,)))
    taps = (lax.dot_general(bb_re.astype(BF16), cplre_scr[...].astype(BF16), dims,
                            preferred_element_type=F32)
            + lax.dot_general(bb_im.astype(BF16), cplimn_scr[...].astype(BF16), dims,
                              preferred_element_type=F32))
    nk = chunk * hc
    for s in range(chunk):
        toep_scr[:, blk(s), :] = taps[:, :, (last - s) * hc:(last - s) * hc + nk]
    for g in range(toept_ref.shape[0]):
        toept_ref[g] = toep_scr[g].astype(toept_ref.dtype).T
        wt_ref[g] = w_scr[g].astype(wt_ref.dtype).T
    vt_ref[...] = v_scr[...].astype(vt_ref.dtype)
    lq = lq_ref[...]
    s01 = jnp.sum(lq[0:1] * lq[1:2], axis=-1, keepdims=True)
    s23 = jnp.sum(lq[2:3] * lq[3:4], axis=-1, keepdims=True)
    lam = jnp.exp(s01) - jnp.exp(s23) + lam_init
    lam_ref[...] = jnp.broadcast_to(lam, lam_ref.shape)


def _ssmprep(a_re, a_im, log_dt, b_re, b_im, c_re, c_im, lq, lam_init):
    nd, g, p = a_re.shape
    hc = b_re.shape[-1]
    t = SSM_CHUNK
    sw = nd * p
    gb = SCAN_SLAB // hc
    row = lambda a: a.astype(F32).transpose(1, 0, 2).reshape(g, 1, sw)
    ldt = jnp.broadcast_to(log_dt[..., None], (nd, g, p))
    bt = lambda a: a.astype(F32).transpose(1, 3, 0, 2).reshape(g, hc, sw)
    ct = lambda a: a.astype(F32).transpose(1, 2, 0, 3).reshape(g, hc, sw)
    sds = jax.ShapeDtypeStruct
    nk = t * hc
    blk = lambda shape: pl.BlockSpec((gb,) + shape[1:], lambda i: (i,) + (0,) * (len(shape) - 1))
    mat = (g, nk, 2 * sw)
    toept, wt, vt, a_t, lam = pl.pallas_call(
        functools.partial(_ssmprep_kernel, lam_init=lam_init, chunk=t),
        grid=(g // gb,),
        in_specs=[blk((g, 1, sw))] * 3 + [blk((g, hc, sw))] * 4 + [pl.BlockSpec(lq.shape, lambda i: (0, 0))],
        out_specs=[blk((g, nk, nk)), blk((g, 2 * sw, nk)), blk(mat), blk((g, 2, 1, sw)),
                   pl.BlockSpec(VREG_TILE, lambda i: (0, 0))],
        out_shape=(sds((g, nk, nk), BF16), sds((g, 2 * sw, nk), BF16), sds(mat, BF16),
                   sds((g, 2, 1, sw), F32), sds(VREG_TILE, F32)),
        scratch_shapes=[pltpu.VMEM((gb, 2 * nk, sw), F32), pltpu.VMEM((gb, 2 * nk, sw), F32),
                        pltpu.VMEM((gb, nk, 2 * sw), F32), pltpu.VMEM((gb, nk, 2 * sw), F32),
                        pltpu.VMEM((gb, nk, nk), F32)],
        compiler_params=_params("arbitrary"),
        name="ssmprep",
    )(row(a_re), row(a_im), row(ldt), bt(b_re), bt(b_im), ct(c_re), ct(c_im), lq.astype(F32))
    return (toept, wt, vt, a_t), lam


def _to_scan_layout(u, o_ref, batch):
    n_slabs, t, _, cw = o_ref.shape
    tp = u.shape[0] // batch
    u_tb = jnp.swapaxes(u.reshape(batch, tp, u.shape[-1]), 0, 1)
    for c in range(tp // t):
        for s in range(t):
            for v in range(n_slabs):
                o_ref[v, s, c * batch:(c + 1) * batch, :] = u_tb[c * t + s][:, v * cw:(v + 1) * cw].astype(o_ref.dtype)


def _scan_block(ref):
    return jnp.concatenate([ref[v] for v in range(ref.shape[0])], axis=-1).astype(F32)


def _from_scan_layout(ref, batch):
    t, rows, d = ref.shape
    tp = rows // batch * t
    u_tb = jnp.stack([ref[s, c * batch:(c + 1) * batch, :] for c in range(tp // t) for s in range(t)], axis=0)
    return jnp.swapaxes(u_tb, 0, 1).reshape(batch * tp, d)


def _inproj_kernel(*refs, rope, emit_caches, d_attn):
    n_in = 7 if rope else 5
    x_ref, shift_ref, scale_ref, g_ref, w_ref = refs[:5]
    cos_ref, sin_ref = refs[5:7] if rope else (None, None)
    q_ref, k_ref, v_ref, ga_ref, u_ref, gs_ref = refs[n_in:n_in + 6]
    kc_ref, vc_ref = refs[n_in + 6:] if emit_caches else (None, None)
    x = x_ref[...]
    nb, tp, d = x.shape
    ms = jnp.mean(x * x, axis=-1, keepdims=True)
    h = x * lax.rsqrt(ms + EPS) * (g_ref[...] * (1.0 + scale_ref[...])) + shift_ref[...]
    hb = h.reshape(nb * tp, d).astype(BF16)
    proj = lambda lo, hi: jnp.dot(hb, w_ref[:, lo:hi].astype(BF16), preferred_element_type=F32)
    da = d_attn
    d_ssm = u_ref.shape[0] * u_ref.shape[-1]
    tok = lambda z: z.reshape(nb, tp, z.shape[-1])
    q = proj(0, da)
    k = proj(da, 2 * da)
    if rope:
        lane = lax.broadcasted_iota(jnp.int32, (1, da), 1)
        low = (lane % (QK_DIM // 2)) < (QK_DIM // 4)
        cos = cos_ref[...]
        sin = sin_ref[...]

        def rot(z):
            partner = jnp.where(low, pltpu.roll(z, da - QK_DIM // 4, 1), pltpu.roll(z, QK_DIM // 4, 1))
            return tok(z) * cos + tok(partner) * sin

        q = rot(q)
        k = rot(k)
    else:
        q = tok(q)
        k = tok(k)
    q_ref[...] = (q * Q_PRESCALE).astype(q_ref.dtype)
    k_ref[...] = k.astype(k_ref.dtype)
    n_heads = da // V_DIM
    if emit_caches:
        for hd in range(n_heads):
            kc_ref[:, pl.ds(hd, tp, stride=n_heads), :] = k[:, :, hd * V_DIM:(hd + 1) * V_DIM]
    v = tok(proj(2 * da, 3 * da))
    v_ref[...] = v.astype(v_ref.dtype)
    if emit_caches:
        for hd in range(n_heads):
            vc_ref[:, pl.ds(hd, tp, stride=n_heads), :] = v[:, :, hd * V_DIM:(hd + 1) * V_DIM]
    ga_ref[...] = tok(_silu(proj(3 * da, 4 * da))).astype(ga_ref.dtype)
    gs_ref[...] = tok(_silu(proj(4 * da + d_ssm, 4 * da + 2 * d_ssm))).astype(gs_ref.dtype)
    _to_scan_layout(proj(4 * da, 4 * da + d_ssm), u_ref, nb)


def _inproj(x, shift, scale, norm_pre, w_in, rope_tabs, emit_caches, d_attn, d_ssm):
    b, l, d = x.shape
    tp = TOKEN_TILE // b
    t = SSM_CHUNK
    rope = rope_tabs is not None
    const = lambda a: pl.BlockSpec(a.shape, lambda j: (0,) * a.ndim)
    tok = lambda w: pl.BlockSpec((b, tp, w), lambda j: (0, j, 0))
    g2 = norm_pre.reshape(1, d).astype(F32)
    w_spec = pl.BlockSpec(w_in.shape, lambda j: (0, 0), pipeline_mode=pl.Buffered(1))
    in_specs = [tok(d), const(shift), const(scale), const(g2), w_spec]
    args = [x, shift, scale, g2, w_in]
    if rope:
        in_specs += [pl.BlockSpec((tp, d_attn), lambda j: (j, 0))] * 2
        args += list(rope_tabs)
    scan_rows = tp // t * b
    attn_shape = jax.ShapeDtypeStruct((b, l, d_attn), BF16)
    out_specs = [tok(d_attn), tok(d_attn), tok(d_attn), tok(d_attn),
                 pl.BlockSpec((d_ssm // SCAN_SLAB, t, scan_rows, SCAN_SLAB), lambda j: (0, 0, j, 0)),
                 tok(d_ssm)]
    out_shape = [attn_shape, attn_shape, attn_shape, attn_shape,
                 jax.ShapeDtypeStruct((d_ssm // SCAN_SLAB, t, l // t * b, SCAN_SLAB), BF16),
                 jax.ShapeDtypeStruct((b, l, d_ssm), BF16)]
    if emit_caches:
        n_heads = d_attn // V_DIM
        out_specs += [pl.BlockSpec((b, tp * n_heads, V_DIM), lambda j: (0, j, 0))] * 2
        out_shape += [jax.ShapeDtypeStruct((b, l * n_heads, V_DIM), F32)] * 2
    return pl.pallas_call(
        functools.partial(_inproj_kernel, rope=rope, emit_caches=emit_caches, d_attn=d_attn),
        grid=(l // tp,),
        in_specs=in_specs,
        out_specs=out_specs,
        out_shape=out_shape,
        compiler_params=_params("arbitrary"),
        name="inproj_rope" if rope else "inproj",
    )(*args)


def _rope_tables(l, d_attn):
    rows = l // GRID_W
    row = np.repeat(np.arange(rows, dtype=np.float64), GRID_W)
    col = np.tile(np.arange(GRID_W, dtype=np.float64), rows)
    n_freq = QK_DIM // 4
    inv = ROPE_THETA ** (-np.arange(n_freq, dtype=np.float64) / n_freq)
    ang_r = row[:, None] * inv
    ang_c = col[:, None] * inv
    cos64 = np.concatenate([np.cos(ang_r), np.cos(ang_r), np.cos(ang_c), np.cos(ang_c)], axis=-1)
    sin64 = np.concatenate([-np.sin(ang_r), np.sin(ang_r), -np.sin(ang_c), np.sin(ang_c)], axis=-1)
    reps = d_attn // QK_DIM
    return (jnp.asarray(np.tile(cos64, (1, reps)), dtype=F32),
            jnp.asarray(np.tile(sin64, (1, reps)), dtype=F32))


def _head(ref, bi, h, n_heads):
    if ref.shape[-1] == V_DIM:
        return ref[bi, pl.ds(h, ref.shape[1] // n_heads, stride=n_heads), :]
    return ref[bi, :, h * V_DIM:(h + 1) * V_DIM]


def _attn_kernel(*refs, has_cache, lam_init):
    if has_cache:
        lam_ref, q_ref, k_ref, v_ref, ck_ref, cv_ref, ga_ref, sub_ref, o_ref, k_scr, vt_scr = refs
    else:
        lam_ref, q_ref, k_ref, v_ref, ga_ref, sub_ref, o_ref, k_scr, vt_scr = refs
    nbk, tq = q_ref.shape[0], q_ref.shape[1]
    n_heads, lk = k_scr.shape[0] // nbk, k_scr.shape[1]
    ln = k_ref.shape[1]
    lam = lam_ref[0:1, 0:1]
    lane = lax.broadcasted_iota(jnp.int32, (1, V_DIM), 1)
    first = lane < QK_DIM
    dims = (((1,), (1,)), ((), ()))
    slot = lambda bi, h: bi * n_heads + h

    @pl.when(pl.program_id(1) == 0)
    def _():
        for bi in range(nbk):
            for h in range(n_heads):
                n = slot(bi, h)
                if has_cache:
                    k_scr[n, 0:lk - ln, :] = _head(ck_ref, bi, h, n_heads).astype(BF16)
                    vt_scr[n, 0:V_DIM, 0:lk - ln] = _head(cv_ref, bi, h, n_heads).astype(F32).T.astype(BF16)
                k_scr[n, lk - ln:lk, :] = _head(k_ref, bi, h, n_heads).astype(BF16)
                vt_scr[n, 0:V_DIM, lk - ln:lk] = _head(v_ref, bi, h, n_heads).astype(F32).T.astype(BF16)
                vt_scr[n, V_DIM:, :] = jnp.ones((vt_scr.shape[1] - V_DIM, lk), BF16)

    sub = min(tq, ATTN_Q_SUB)
    items = [(bi, h, r) for bi in range(nbk) for h in range(n_heads) for r in range(0, tq, sub)]

    def scores(item):
        bi, h, r = item
        q = _head(q_ref, bi, h, n_heads)[r:r + sub, :]
        zero = jnp.zeros_like(q)
        k = k_scr[slot(bi, h)]
        return (lax.dot_general(k, jnp.where(first, q, zero), dims, preferred_element_type=F32),
                lax.dot_general(k, jnp.where(first, zero, q), dims, preferred_element_type=F32))

    nxt = scores(items[0])
    for n, (bi, h, r) in enumerate(items):
        st1, st2 = nxt
        if n + 1 < len(items):
            nxt = scores(items[n + 1])
        vt = vt_scr[slot(bi, h)]
        et1 = jnp.exp2(st1 - jnp.max(st1, axis=0, keepdims=True)).astype(BF16)
        et2 = jnp.exp2(st2 - jnp.max(st2, axis=0, keepdims=True)).astype(BF16)
        o1 = jnp.dot(vt, et1, preferred_element_type=F32)
        o2 = jnp.dot(vt, et2, preferred_element_type=F32)
        r1 = 1.0 / o1[V_DIM:V_DIM + 1, :]
        r2 = lam / o2[V_DIM:V_DIM + 1, :]
        o = (o1[0:V_DIM, :] * r1 - o2[0:V_DIM, :] * r2).T
        ms = jnp.mean(o * o, axis=-1, keepdims=True)
        o = o * lax.rsqrt(ms + EPS) * sub_ref[...] * (1.0 - lam_init)
        ga = _head(ga_ref, bi, h, n_heads)[r:r + sub, :].astype(F32)
        o_ref[bi, r:r + sub, h * V_DIM:(h + 1) * V_DIM] = (o * ga).astype(o_ref.dtype)


def _attention(lam, q, k, v, cache_k, cache_v, ga, subln, lam_init):
    b, l, d_attn = q.shape
    n_heads = d_attn // V_DIM
    tq = min(ATTN_Q_TILE, l)
    nq = l // tq
    nbk = max(1, ATTN_Q_TILE // l)
    has_cache = cache_k is not None
    q_spec = pl.BlockSpec((nbk, tq, d_attn), lambda i, j: (i, j, 0))
    whole = lambda a: pl.BlockSpec((nbk,) + a.shape[1:], lambda i, j: (i,) + (0,) * (a.ndim - 1))
    in_specs = [pl.BlockSpec(VREG_TILE, lambda i, j: (0, 0)), q_spec, whole(k), whole(v)]
    args = [lam, q, k, v]
    if has_cache:
        in_specs += [whole(cache_k), whole(cache_v)]
        args += [cache_k, cache_v]
    in_specs += [q_spec, pl.BlockSpec((1, V_DIM), lambda i, j: (0, 0))]
    args += [ga, subln.reshape(1, V_DIM).astype(F32)]
    lk = l + (cache_k.shape[1] // n_heads if has_cache else 0)
    out = pl.pallas_call(
        functools.partial(_attn_kernel, has_cache=has_cache, lam_init=lam_init),
        grid=(b // nbk, nq),
        in_specs=in_specs,
        out_specs=q_spec,
        out_shape=jax.ShapeDtypeStruct((b, l, d_attn), BF16),
        scratch_shapes=[pltpu.VMEM((nbk * n_heads, lk, V_DIM), BF16),
                        pltpu.VMEM((nbk * n_heads, V_DIM + SUM_ROWS, lk), BF16)],
        compiler_params=_params("arbitrary", "arbitrary"),
        name="attn_cache" if has_cache else "attn",
    )(*args)
    return out


def _ssm_kernel(*refs, batch, n_chunks, with_state):
    if with_state:
        u_ref, toept_ref, wt_ref, vt_ref, a_ref, h0_ref, y_ref, fin_ref, zt_scr, wc_scr, sin_scr, yt_scr = refs
    else:
        u_ref, toept_ref, wt_ref, vt_ref, a_ref, h0_ref, y_ref, zt_scr, wc_scr, sin_scr, yt_scr = refs
    t = u_ref.shape[0]
    gb = zt_scr.shape[0]
    hc = zt_scr.shape[1] // t
    sw = a_ref.shape[-1]
    hw = sw // 2
    for s in range(t):
        at = u_ref[s].T
        for g in range(gb):
            zt_scr[g, s * hc:(s + 1) * hc, :] = at[g * hc:(g + 1) * hc, :]
    for g in range(gb):
        wc_scr[g] = jnp.dot(wt_ref[g], zt_scr[g], preferred_element_type=F32).astype(BF16).T.astype(F32)
    fwd = lax.broadcasted_iota(jnp.int32, (1, sw), 1) < hw

    def step(j, carry):
        rf = pl.ds(pl.multiple_of(j * batch, batch), batch)
        rb = pl.ds(pl.multiple_of((n_chunks - 1 - j) * batch, batch), batch)
        out = []
        for g in range(gb):
            sr, si = carry[g]
            sin_scr[g, rf, 0:hw] = sr[:, 0:hw]
            sin_scr[g, rb, hw:sw] = sr[:, hw:sw]
            sin_scr[g, rf, sw:sw + hw] = si[:, 0:hw]
            sin_scr[g, rb, sw + hw:2 * sw] = si[:, hw:sw]
            wr = jnp.where(fwd, wc_scr[g, rf, 0:sw], wc_scr[g, rb, 0:sw])
            wi = jnp.where(fwd, wc_scr[g, rf, sw:2 * sw], wc_scr[g, rb, sw:2 * sw])
            ar = a_ref[g, 0]
            ai = a_ref[g, 1]
            out.append((ar * sr - ai * si + wr, ar * si + ai * sr + wi))
        return tuple(out)

    fin = lax.fori_loop(0, n_chunks, step, tuple((h0_ref[g, 0], h0_ref[g, 1]) for g in range(gb)))
    for g in range(gb):
        sint = sin_scr[g].astype(BF16).T
        y2t = (jnp.dot(toept_ref[g], zt_scr[g], preferred_element_type=F32)
               + jnp.dot(vt_ref[g], sint, preferred_element_type=F32))
        for s in range(t):
            yt_scr[s, g * hc:(g + 1) * hc, :] = y2t[s * hc:(s + 1) * hc, :].astype(yt_scr.dtype)
        if with_state:
            fin_ref[g, 0] = fin[g][0]
            fin_ref[g, 1] = fin[g][1]
    for s in range(t):
        y_ref[s] = yt_scr[s].T


def _ssm(u3, ssm_ops, h0, with_state):
    toept, wt, vt, a_t = ssm_ops
    n_slabs, t, rows, cw = u3.shape
    batch = h0.shape[2]
    n_chunks = rows // batch
    g = toept.shape[0]
    gb = g // n_slabs
    slab = pl.BlockSpec((None, t, rows, cw), lambda i: (i, 0, 0, 0))
    blk = lambda a: pl.BlockSpec((gb,) + a.shape[1:], lambda i: (i,) + (0,) * (a.ndim - 1))
    out_specs = [slab]
    out_shape = [jax.ShapeDtypeStruct(u3.shape, u3.dtype)]
    if with_state:
        out_specs.append(blk(h0))
        out_shape.append(jax.ShapeDtypeStruct(h0.shape, F32))
    nk = toept.shape[-1]
    outs = pl.pallas_call(
        functools.partial(_ssm_kernel, batch=batch, n_chunks=n_chunks, with_state=with_state),
        grid=(g // gb,),
        in_specs=[slab, blk(toept), blk(wt), blk(vt), blk(a_t), blk(h0)],
        out_specs=out_specs,
        out_shape=out_shape,
        scratch_shapes=[pltpu.VMEM((gb, nk, rows), BF16),
                        pltpu.VMEM((gb, rows, wt.shape[1]), F32),
                        pltpu.VMEM((gb, rows, wt.shape[1]), F32),
                        pltpu.VMEM((t, cw, rows), u3.dtype)],
        compiler_params=_params("arbitrary"),
        name="ssm_state" if with_state else "ssm",
    )(u3, toept, wt, vt, a_t, h0)
    return outs


def _state_to_lanes(h0):
    b, nd, nc, g, p = h0.shape
    return h0.astype(F32).transpose(3, 2, 0, 1, 4).reshape(g, nc, b, nd * p)


def _state_from_lanes(fin, nd):
    g, nc, b, w = fin.shape
    return fin.reshape(g, nc, b, nd, w // nd).transpose(2, 3, 1, 0, 4)


def _outproj_kernel(x_ref, a_ref, y_ref, u_ref, gs_ref, gate_ref, dskip_ref,
                    wglu_ref, bglu_ref, wout_ref, g_ref, o_ref):
    nb, tp, d = x_ref.shape
    y = _scan_block(y_ref) + dskip_ref[...] * _scan_block(u_ref)
    t, srows, d_ssm = y.shape
    rows = lambda ref: ref[...].reshape(nb * tp, ref.shape[-1])
    y = y.reshape(t * srows, d_ssm)
    c0 = math.sqrt(2.0 / math.pi)
    hy = 0.5 * y
    ys = hy + hy * jnp.tanh(y * (c0 + (c0 * 0.044715) * (y * y)))
    z = jnp.dot(ys.astype(BF16), wglu_ref[...], preferred_element_type=F32) + bglu_ref[...]
    ys = _from_scan_layout((ys * jax.nn.sigmoid(z)).reshape(t, srows, d_ssm), nb) * rows(gs_ref).astype(F32)
    cat = jnp.concatenate([rows(a_ref), ys.astype(BF16)], axis=-1)
    out = jnp.dot(cat, wout_ref[...], preferred_element_type=F32)
    ms = jnp.mean(out * out, axis=-1, keepdims=True)
    out = out * lax.rsqrt(ms + EPS)
    o_ref[...] = x_ref[...] + (g_ref[...] * gate_ref[...]) * out.reshape(nb, tp, d)


def _outproj(x, a_out, y_ssm, u, gs, gate, d_skip, w_glu_bf, b_glu, w_out_bf, norm_post):
    b, l, d = x.shape
    d_attn = a_out.shape[-1]
    d_ssm = gs.shape[-1]
    tp = TOKEN_TILE // b
    t = SSM_CHUNK
    const = lambda a: pl.BlockSpec(a.shape, lambda j: (0,) * a.ndim)
    tok = lambda w: pl.BlockSpec((b, tp, w), lambda j: (0, j, 0))
    scan = pl.BlockSpec((d_ssm // SCAN_SLAB, t, tp // t * b, SCAN_SLAB), lambda j: (0, 0, j, 0))
    consts = [gate, d_skip.reshape(1, d_ssm).astype(F32), w_glu_bf, b_glu.reshape(1, d_ssm).astype(F32),
              w_out_bf, norm_post.reshape(1, d).astype(F32)]
    return pl.pallas_call(
        _outproj_kernel,
        grid=(l // tp,),
        in_specs=[tok(d), tok(d_attn), scan, scan, tok(d_ssm)] + [const(a) for a in consts],
        out_specs=tok(d),
        out_shape=jax.ShapeDtypeStruct((b, l, d), F32),
        compiler_params=_params("arbitrary"),
        name="outproj",
    )(x, a_out, y_ssm, u, gs, *consts)


def _mixer(x, mod, h0_lanes, cache_k, cache_v, rope_tabs, emit_caches, with_state, lam, lam_init,
           norm_pre, w_in, ssm_ops, subln, d_skip, w_glu_bf, b_glu, w_out_bf, norm_post):
    b, l, d = x.shape
    d_ssm = w_glu_bf.shape[0]
    d_attn = (w_in.shape[1] - 2 * d_ssm) // 4
    shift, scale, gate = mod
    q, k, v, ga, u, gs, *caches = _inproj(x, shift, scale, norm_pre, w_in, rope_tabs, emit_caches,
                                          d_attn, d_ssm)
    a_out = _attention(lam, q, k, v, cache_k, cache_v, ga, subln, lam_init)
    outs = _ssm(u, ssm_ops, h0_lanes, with_state)
    y = _outproj(x, a_out, outs[0], u, gs, gate, d_skip, w_glu_bf, b_glu, w_out_bf, norm_post)
    fin = outs[1] if with_state else None
    return y, caches, fin


def kernel(x_prompt, x_sample, cache_k, cache_v, state_ssm, c, c_ctx, w_ada, b_ada, norm_pre, norm_post, w_in, lambda_qk, subln, ssm_A_re, ssm_A_im, ssm_log_dt, ssm_B_re, ssm_B_im, ssm_C_re, ssm_C_im, ssm_D, w_glu, b_glu, w_out):
    xp, xs = x_prompt, x_sample
    bp, lp, d = xp.shape
    bs, ls, _ = xs.shape
    depth = w_in.shape[0]
    nd = ssm_A_re.shape[1]
    g, p = ssm_A_re.shape[-2:]
    d_ssm = w_glu.shape[-1]
    d_attn = (w_in.shape[-1] - 2 * d_ssm) // 4
    rope_tabs = _rope_tables(ls, d_attn)
    cond = jnp.zeros((COND_ROWS, d), F32).at[:bs].set(c.astype(F32)).at[bs].set(c_ctx.astype(F32))
    new_k, new_v, new_s = [], [], []
    for layer in range(depth):
        lam_init = 0.8 - 0.6 * math.exp(-0.3 * layer)
        m = _adaln(cond, w_ada[layer], b_ada[layer])
        mod_s = tuple(m[:bs, i * d:(i + 1) * d].reshape(bs, 1, d) for i in range(3))
        mod_p = tuple(m[bs:bs + 1, i * d:(i + 1) * d].reshape(1, 1, d) for i in range(3))
        ssm_ops, lam = _ssmprep(
            ssm_A_re[layer], ssm_A_im[layer], ssm_log_dt[layer], ssm_B_re[layer], ssm_B_im[layer],
            ssm_C_re[layer], ssm_C_im[layer], lambda_qk[layer], lam_init)
        shared = (lam, lam_init, norm_pre[layer], w_in[layer], ssm_ops, subln[layer],
                  ssm_D[layer], w_glu[layer].astype(BF16), b_glu[layer], w_out[layer].astype(BF16),
                  norm_post[layer])

        h0_p = jnp.zeros((g, 2, bp, nd * p), F32)
        xp, (k_p, v_p), fin = _mixer(xp, mod_p, h0_p, None, None, None, True, True, *shared)
        new_k.append(k_p.reshape(bp, lp, d_attn // V_DIM, V_DIM))
        new_v.append(v_p.reshape(bp, lp, d_attn // V_DIM, V_DIM))
        new_s.append(_state_from_lanes(fin, nd).astype(xp.dtype))

        h0_s = _state_to_lanes(state_ssm[:, layer])
        ck = cache_k[:, layer].reshape(bs, -1, V_DIM)
        cv = cache_v[:, layer].reshape(bs, -1, V_DIM)
        xs, _, _ = _mixer(xs, mod_s, h0_s, ck, cv, rope_tabs, False, False, *shared)

    return (xp, xs, jnp.stack(new_k, axis=1), jnp.stack(new_v, axis=1), jnp.stack(new_s, axis=1))
```

```python
import functools
import math

import jax
import jax.numpy as jnp
import numpy as np
from jax import lax
from jax.experimental import pallas as pl
from jax.experimental.pallas import tpu as pltpu

F32 = jnp.float32
BF16 = jnp.bfloat16

GRID_W = 64
QK_DIM = 64
V_DIM = 2 * QK_DIM
ROPE_THETA = 10000.0
EPS = 1e-6
Q_PRESCALE = math.log2(math.e) * QK_DIM ** -0.5

VREG_TILE = (8, 128)
SUM_ROWS = 16
COND_ROWS = 16
ADALN_K_BLOCK = 256
TOKEN_TILE = 1024
ATTN_Q_TILE = 1024
ATTN_Q_SUB = 512
ATTN_KEY_BLOCK = 256
SSM_CHUNK = 16
SCAN_SLAB = 128
VMEM_LIMIT = 56000 * 1024


def _silu(x):
    return x * jax.nn.sigmoid(x)


def _params(*sem):
    return pltpu.CompilerParams(dimension_semantics=sem, vmem_limit_bytes=VMEM_LIMIT)


def _adaln_kernel(c_ref, w_ref, b_ref, o_ref):
    @pl.when(pl.program_id(0) == 0)
    def _():
        o_ref[...] = jnp.broadcast_to(b_ref[...], o_ref.shape)

    s = _silu(c_ref[...]).astype(BF16)
    o_ref[...] += jnp.dot(s, w_ref[...].astype(BF16), preferred_element_type=F32)


def _adaln(cond, w_ada, b_ada):
    rows, d = cond.shape
    n = w_ada.shape[1]
    bk = ADALN_K_BLOCK
    return pl.pallas_call(
        _adaln_kernel,
        grid=(d // bk,),
        in_specs=[pl.BlockSpec((rows, bk), lambda j: (0, j)),
                  pl.BlockSpec((bk, n), lambda j: (j, 0)),
                  pl.BlockSpec((1, n), lambda j: (0, 0))],
        out_specs=pl.BlockSpec((rows, n), lambda j: (0, 0)),
        out_shape=jax.ShapeDtypeStruct((rows, n), F32),
        compiler_params=_params("arbitrary"),
        name="adaln",
    )(cond, w_ada, b_ada.reshape(1, n))


def _ssmprep_kernel(are_ref, aim_ref, ldt_ref, bre_ref, bim_ref, cre_ref, cim_ref, lq_ref,
                    toept_ref, wt_ref, vt_ref, a_ref, lam_ref,
                    cplre_scr, cplimn_scr, w_scr, v_scr, toep_scr, *, lam_init, chunk):
    a_re = are_ref[...]
    a_im = aim_ref[...]
    dt = jnp.exp(ldt_ref[...])
    mag = jnp.exp(a_re * dt)
    ab_re = mag * jnp.cos(a_im * dt)
    ab_im = mag * jnp.sin(a_im * dt)
    nr, ni = ab_re - 1.0, ab_im
    den = a_re * a_re + a_im * a_im
    f_re = (nr * a_re + ni * a_im) / den
    f_im = (ni * a_re - nr * a_im) / den
    b_re = bre_ref[...]
    b_im = bim_ref[...]
    bb_re = f_re * b_re - f_im * b_im
    bb_im = f_re * b_im + f_im * b_re
    c_re = cre_ref[...]
    c_im = cim_ref[...]
    hc = b_re.shape[1]
    sw = a_re.shape[-1]
    hw = sw // 2
    lo, hi = slice(0, hw), slice(hw, sw)
    lo_im, hi_im = slice(sw, sw + hw), slice(sw + hw, 2 * sw)
    blk = lambda j: slice(j * hc, (j + 1) * hc)
    last = chunk - 1
    cplre_scr[...] = jnp.zeros_like(cplre_scr)
    cplimn_scr[...] = jnp.zeros_like(cplimn_scr)
    pw_re = jnp.ones_like(ab_re)
    pw_im = jnp.zeros_like(ab_im)
    for tau in range(chunk + 1):
        cp_re = c_re * pw_re - c_im * pw_im
        cp_imn = -(c_re * pw_im + c_im * pw_re)
        if tau < chunk:
            cplre_scr[:, blk(last + tau), lo] = cp_re[:, :, lo]
            cplre_scr[:, blk(last - tau), hi] = cp_re[:, :, hi]
            cplimn_scr[:, blk(last + tau), lo] = cp_imn[:, :, lo]
            cplimn_scr[:, blk(last - tau), hi] = cp_imn[:, :, hi]
            e_re = pw_re * bb_re - pw_im * bb_im
            e_im = pw_re * bb_im + pw_im * bb_re
            w_scr[:, blk(last - tau), lo] = e_re[:, :, lo]
            w_scr[:, blk(tau), hi] = e_re[:, :, hi]
            w_scr[:, blk(last - tau), lo_im] = e_im[:, :, lo]
            w_scr[:, blk(tau), hi_im] = e_im[:, :, hi]
        if tau >= 1:
            v_scr[:, blk(tau - 1), lo] = cp_re[:, :, lo]
            v_scr[:, blk(chunk - tau), hi] = cp_re[:, :, hi]
            v_scr[:, blk(tau - 1), lo_im] = cp_imn[:, :, lo]
            v_scr[:, blk(chunk - tau), hi_im] = cp_imn[:, :, hi]
        if tau < chunk:
            pw_re, pw_im = pw_re * ab_re - pw_im * ab_im, pw_re * ab_im + pw_im * ab_re
    a_ref[:, 0] = pw_re
    a_ref[:, 1] = pw_im
    dims = (((2,), (2,)), ((0,), (0,)))
    taps = (lax.dot_general(bb_re.astype(BF16), cplre_scr[...].astype(BF16), dims,
                            preferred_element_type=F32)
            + lax.dot_general(bb_im.astype(BF16), cplimn_scr[...].astype(BF16), dims,
                              preferred_element_type=F32))
    nk = chunk * hc
    for s in range(chunk):
        toep_scr[:, blk(s), :] = taps[:, :, (last - s) * hc:(last - s) * hc + nk]
    for g in range(toept_ref.shape[0]):
        toept_ref[g] = toep_scr[g].astype(toept_ref.dtype).T
        wt_ref[g] = w_scr[g].astype(wt_ref.dtype).T
    vt_ref[...] = v_scr[...].astype(vt_ref.dtype)
    lq = lq_ref[...]
    s01 = jnp.sum(lq[0:1] * lq[1:2], axis=-1, keepdims=True)
    s23 = jnp.sum(lq[2:3] * lq[3:4], axis=-1, keepdims=True)
    lam = jnp.exp(s01) - jnp.exp(s23) + lam_init
    lam_ref[...] = jnp.broadcast_to(lam, lam_ref.shape)


def _ssmprep(a_re, a_im, log_dt, b_re, b_im, c_re, c_im, lq, lam_init):
    nd, g, p = a_re.shape
    hc = b_re.shape[-1]
    t = SSM_CHUNK
    sw = nd * p
    gb = SCAN_SLAB // hc
    row = lambda a: a.astype(F32).transpose(1, 0, 2).reshape(g, 1, sw)
    ldt = jnp.broadcast_to(log_dt[..., None], (nd, g, p))
    bt = lambda a: a.astype(F32).transpose(1, 3, 0, 2).reshape(g, hc, sw)
    ct = lambda a: a.astype(F32).transpose(1, 2, 0, 3).reshape(g, hc, sw)
    sds = jax.ShapeDtypeStruct
    nk = t * hc
    blk = lambda shape: pl.BlockSpec((gb,) + shape[1:], lambda i: (i,) + (0,) * (len(shape) - 1))
    mat = (g, nk, 2 * sw)
    toept, wt, vt, a_t, lam = pl.pallas_call(
        functools.partial(_ssmprep_kernel, lam_init=lam_init, chunk=t),
        grid=(g // gb,),
        in_specs=[blk((g, 1, sw))] * 3 + [blk((g, hc, sw))] * 4 + [pl.BlockSpec(lq.shape, lambda i: (0, 0))],
        out_specs=[blk((g, nk, nk)), blk((g, 2 * sw, nk)), blk(mat), blk((g, 2, 1, sw)),
                   pl.BlockSpec(VREG_TILE, lambda i: (0, 0))],
        out_shape=(sds((g, nk, nk), BF16), sds((g, 2 * sw, nk), BF16), sds(mat, BF16),
                   sds((g, 2, 1, sw), F32), sds(VREG_TILE, F32)),
        scratch_shapes=[pltpu.VMEM((gb, 2 * nk, sw), F32), pltpu.VMEM((gb, 2 * nk, sw), F32),
                        pltpu.VMEM((gb, nk, 2 * sw), F32), pltpu.VMEM((gb, nk, 2 * sw), F32),
                        pltpu.VMEM((gb, nk, nk), F32)],
        compiler_params=_params("arbitrary"),
        name="ssmprep",
    )(row(a_re), row(a_im), row(ldt), bt(b_re), bt(b_im), ct(c_re), ct(c_im), lq.astype(F32))
    return (toept, wt, vt, a_t), lam


def _to_scan_layout(u, o_ref, batch):
    n_slabs, t, _, cw = o_ref.shape
    tp = u.shape[0] // batch
    u_tb = jnp.swapaxes(u.reshape(batch, tp, u.shape[-1]), 0, 1)
    for c in range(tp // t):
        for s in range(t):
            for v in range(n_slabs):
                o_ref[v, s, c * batch:(c + 1) * batch, :] = u_tb[c * t + s][:, v * cw:(v + 1) * cw].astype(o_ref.dtype)


def _scan_block(ref):
    return jnp.concatenate([ref[v] for v in range(ref.shape[0])], axis=-1).astype(F32)


def _from_scan_layout(ref, batch):
    t, rows, d = ref.shape
    tp = rows // batch * t
    u_tb = jnp.stack([ref[s, c * batch:(c + 1) * batch, :] for c in range(tp // t) for s in range(t)], axis=0)
    return jnp.swapaxes(u_tb, 0, 1).reshape(batch * tp, d)


def _inproj_kernel(*refs, rope, emit_caches, d_attn):
    n_in = 7 if rope else 5
    x_ref, shift_ref, scale_ref, g_ref, w_ref = refs[:5]
    cos_ref, sin_ref = refs[5:7] if rope else (None, None)
    q_ref, k_ref, v_ref, ga_ref, u_ref, gs_ref = refs[n_in:n_in + 6]
    kc_ref, vc_ref = refs[n_in + 6:] if emit_caches else (None, None)
    x = x_ref[...]
    nb, tp, d = x.shape
    ms = jnp.mean(x * x, axis=-1, keepdims=True)
    h = x * lax.rsqrt(ms + EPS) * (g_ref[...] * (1.0 + scale_ref[...])) + shift_ref[...]
    hb = h.reshape(nb * tp, d).astype(BF16)
    proj = lambda lo, hi: jnp.dot(hb, w_ref[:, lo:hi].astype(BF16), preferred_element_type=F32)
    da = d_attn
    d_ssm = u_ref.shape[0] * u_ref.shape[-1]
    tok = lambda z: z.reshape(nb, tp, z.shape[-1])
    q = proj(0, da)
    k = proj(da, 2 * da)
    if rope:
        lane = lax.broadcasted_iota(jnp.int32, (1, da), 1)
        low = (lane % (QK_DIM // 2)) < (QK_DIM // 4)
        cos = cos_ref[...]
        sin = sin_ref[...]

        def rot(z):
            partner = jnp.where(low, pltpu.roll(z, da - QK_DIM // 4, 1), pltpu.roll(z, QK_DIM // 4, 1))
            return tok(z) * cos + tok(partner) * sin

        q = rot(q)
        k = rot(k)
    else:
        q = tok(q)
        k = tok(k)
    q_ref[...] = (q * Q_PRESCALE).astype(q_ref.dtype)
    k_ref[...] = k.astype(k_ref.dtype)
    n_heads = da // V_DIM
    if emit_caches:
        for hd in range(n_heads):
            kc_ref[:, pl.ds(hd, tp, stride=n_heads), :] = k[:, :, hd * V_DIM:(hd + 1) * V_DIM]
    v = tok(proj(2 * da, 3 * da))
    v_ref[...] = v.astype(v_ref.dtype)
    if emit_caches:
        for hd in range(n_heads):
            vc_ref[:, pl.ds(hd, tp, stride=n_heads), :] = v[:, :, hd * V_DIM:(hd + 1) * V_DIM]
    ga_ref[...] = tok(_silu(proj(3 * da, 4 * da))).astype(ga_ref.dtype)
    gs_ref[...] = tok(_silu(proj(4 * da + d_ssm, 4 * da + 2 * d_ssm))).astype(gs_ref.dtype)
    _to_scan_layout(proj(4 * da, 4 * da + d_ssm), u_ref, nb)


def _inproj(x, shift, scale, norm_pre, w_in, rope_tabs, emit_caches, d_attn, d_ssm):
    b, l, d = x.shape
    tp = TOKEN_TILE // b
    t = SSM_CHUNK
    rope = rope_tabs is not None
    const = lambda a: pl.BlockSpec(a.shape, lambda j: (0,) * a.ndim)
    tok = lambda w: pl.BlockSpec((b, tp, w), lambda j: (0, j, 0))
    g2 = norm_pre.reshape(1, d).astype(F32)
    w_spec = pl.BlockSpec(w_in.shape, lambda j: (0, 0), pipeline_mode=pl.Buffered(1))
    in_specs = [tok(d), const(shift), const(scale), const(g2), w_spec]
    args = [x, shift, scale, g2, w_in]
    if rope:
        in_specs += [pl.BlockSpec((tp, d_attn), lambda j: (j, 0))] * 2
        args += list(rope_tabs)
    scan_rows = tp // t * b
    attn_shape = jax.ShapeDtypeStruct((b, l, d_attn), BF16)
    out_specs = [tok(d_attn), tok(d_attn), tok(d_attn), tok(d_attn),
                 pl.BlockSpec((d_ssm // SCAN_SLAB, t, scan_rows, SCAN_SLAB), lambda j: (0, 0, j, 0)),
                 tok(d_ssm)]
    out_shape = [attn_shape, attn_shape, attn_shape, attn_shape,
                 jax.ShapeDtypeStruct((d_ssm // SCAN_SLAB, t, l // t * b, SCAN_SLAB), BF16),
                 jax.ShapeDtypeStruct((b, l, d_ssm), BF16)]
    if emit_caches:
        n_heads = d_attn // V_DIM
        out_specs += [pl.BlockSpec((b, tp * n_heads, V_DIM), lambda j: (0, j, 0))] * 2
        out_shape += [jax.ShapeDtypeStruct((b, l * n_heads, V_DIM), F32)] * 2
    return pl.pallas_call(
        functools.partial(_inproj_kernel, rope=rope, emit_caches=emit_caches, d_attn=d_attn),
        grid=(l // tp,),
        in_specs=in_specs,
        out_specs=out_specs,
        out_shape=out_shape,
        compiler_params=_params("arbitrary"),
        name="inproj_rope" if rope else "inproj",
    )(*args)


def _rope_tables(l, d_attn):
    rows = l // GRID_W
    row = np.repeat(np.arange(rows, dtype=np.float64), GRID_W)
    col = np.tile(np.arange(GRID_W, dtype=np.float64), rows)
    n_freq = QK_DIM // 4
    inv = ROPE_THETA ** (-np.arange(n_freq, dtype=np.float64) / n_freq)
    ang_r = row[:, None] * inv
    ang_c = col[:, None] * inv
    cos64 = np.concatenate([np.cos(ang_r), np.cos(ang_r), np.cos(ang_c), np.cos(ang_c)], axis=-1)
    sin64 = np.concatenate([-np.sin(ang_r), np.sin(ang_r), -np.sin(ang_c), np.sin(ang_c)], axis=-1)
    reps = d_attn // QK_DIM
    return (jnp.asarray(np.tile(cos64, (1, reps)), dtype=F32),
            jnp.asarray(np.tile(sin64, (1, reps)), dtype=F32))


def _head(ref, bi, h, n_heads):
    if ref.shape[-1] == V_DIM:
        return ref[bi, pl.ds(h, ref.shape[1] // n_heads, stride=n_heads), :]
    return ref[bi, :, h * V_DIM:(h + 1) * V_DIM]


def _attn_kernel(*refs, has_cache, lam_init):
    if has_cache:
        lam_ref, q_ref, k_ref, v_ref, ck_ref, cv_ref, ga_ref, sub_ref, o_ref, k_scr, vt_scr = refs
    else:
        lam_ref, q_ref, k_ref, v_ref, ga_ref, sub_ref, o_ref, k_scr, vt_scr = refs
    nbk, tq = q_ref.shape[0], q_ref.shape[1]
    n_heads, lk = k_scr.shape[0] // nbk, k_scr.shape[1]
    ln = k_ref.shape[1]
    lam = lam_ref[0:1, 0:1]
    lane = lax.broadcasted_iota(jnp.int32, (1, V_DIM), 1)
    first = lane < QK_DIM
    dims = (((1,), (1,)), ((), ()))
    slot = lambda bi, h: bi * n_heads + h

    @pl.when(pl.program_id(1) == 0)
    def _():
        for bi in range(nbk):
            for h in range(n_heads):
                n = slot(bi, h)
                if has_cache:
                    k_scr[n, 0:lk - ln, :] = _head(ck_ref, bi, h, n_heads).astype(BF16)
                    vt_scr[n, 0:V_DIM, 0:lk - ln] = _head(cv_ref, bi, h, n_heads).astype(F32).T.astype(BF16)
                k_scr[n, lk - ln:lk, :] = _head(k_ref, bi, h, n_heads).astype(BF16)
                vt_scr[n, 0:V_DIM, lk - ln:lk] = _head(v_ref, bi, h, n_heads).astype(F32).T.astype(BF16)
                vt_scr[n, V_DIM:, :] = jnp.ones((vt_scr.shape[1] - V_DIM, lk), BF16)

    sub = min(tq, ATTN_Q_SUB)
    items = [(bi, h, r) for bi in range(nbk) for h in range(n_heads) for r in range(0, tq, sub)]

    def scores(item):
        bi, h, r = item
        q = _head(q_ref, bi, h, n_heads)[r:r + sub, :]
        zero = jnp.zeros_like(q)
        k = k_scr[slot(bi, h)]
        return (lax.dot_general(k, jnp.where(first, q, zero), dims, preferred_element_type=F32),
                lax.dot_general(k, jnp.where(first, zero, q), dims, preferred_element_type=F32))

    nxt = scores(items[0])
    for n, (bi, h, r) in enumerate(items):
        st1, st2 = nxt
        if n + 1 < len(items):
            nxt = scores(items[n + 1])
        vt = vt_scr[slot(bi, h)]
        m1 = jnp.max(st1, axis=0, keepdims=True)
        m2 = jnp.max(st2, axis=0, keepdims=True)
        o1 = o2 = None
        for lo in range(0, lk, ATTN_KEY_BLOCK):
            hi = min(lk, lo + ATTN_KEY_BLOCK)
            e1 = jnp.exp2(st1[lo:hi, :] - m1).astype(BF16)
            e2 = jnp.exp2(st2[lo:hi, :] - m2).astype(BF16)
            p1 = jnp.dot(vt[:, lo:hi], e1, preferred_element_type=F32)
            p2 = jnp.dot(vt[:, lo:hi], e2, preferred_element_type=F32)
            o1 = p1 if o1 is None else o1 + p1
            o2 = p2 if o2 is None else o2 + p2
        r1 = 1.0 / o1[V_DIM:V_DIM + 1, :]
        r2 = lam / o2[V_DIM:V_DIM + 1, :]
        o = (o1[0:V_DIM, :] * r1 - o2[0:V_DIM, :] * r2).T
        ms = jnp.mean(o * o, axis=-1, keepdims=True)
        o = o * lax.rsqrt(ms + EPS) * sub_ref[...] * (1.0 - lam_init)
        ga = _head(ga_ref, bi, h, n_heads)[r:r + sub, :].astype(F32)
        o_ref[bi, r:r + sub, h * V_DIM:(h + 1) * V_DIM] = (o * ga).astype(o_ref.dtype)


def _attention(lam, q, k, v, cache_k, cache_v, ga, subln, lam_init):
    b, l, d_attn = q.shape
    n_heads = d_attn // V_DIM
    tq = min(ATTN_Q_TILE, l)
    nq = l // tq
    nbk = max(1, ATTN_Q_TILE // l)
    has_cache = cache_k is not None
    q_spec = pl.BlockSpec((nbk, tq, d_attn), lambda i, j: (i, j, 0))
    whole = lambda a: pl.BlockSpec((nbk,) + a.shape[1:], lambda i, j: (i,) + (0,) * (a.ndim - 1))
    in_specs = [pl.BlockSpec(VREG_TILE, lambda i, j: (0, 0)), q_spec, whole(k), whole(v)]
    args = [lam, q, k, v]
    if has_cache:
        in_specs += [whole(cache_k), whole(cache_v)]
        args += [cache_k, cache_v]
    in_specs += [q_spec, pl.BlockSpec((1, V_DIM), lambda i, j: (0, 0))]
    args += [ga, subln.reshape(1, V_DIM).astype(F32)]
    lk = l + (cache_k.shape[1] // n_heads if has_cache else 0)
    out = pl.pallas_call(
        functools.partial(_attn_kernel, has_cache=has_cache, lam_init=lam_init),
        grid=(b // nbk, nq),
        in_specs=in_specs,
        out_specs=q_spec,
        out_shape=jax.ShapeDtypeStruct((b, l, d_attn), BF16),
        scratch_shapes=[pltpu.VMEM((nbk * n_heads, lk, V_DIM), BF16),
                        pltpu.VMEM((nbk * n_heads, V_DIM + SUM_ROWS, lk), BF16)],
        compiler_params=_params("arbitrary", "arbitrary"),
        name="attn_cache" if has_cache else "attn",
    )(*args)
    return out


def _ssm_kernel(*refs, batch, n_chunks, with_state):
    if with_state:
        u_ref, toept_ref, wt_ref, vt_ref, a_ref, h0_ref, y_ref, fin_ref, zt_scr, wc_scr, sin_scr, yt_scr = refs
    else:
        u_ref, toept_ref, wt_ref, vt_ref, a_ref, h0_ref, y_ref, zt_scr, wc_scr, sin_scr, yt_scr = refs
    t = u_ref.shape[0]
    gb = zt_scr.shape[0]
    hc = zt_scr.shape[1] // t
    sw = a_ref.shape[-1]
    hw = sw // 2
    for s in range(t):
        at = u_ref[s].T
        for g in range(gb):
            zt_scr[g, s * hc:(s + 1) * hc, :] = at[g * hc:(g + 1) * hc, :]
    for g in range(gb):
        wc_scr[g] = jnp.dot(wt_ref[g], zt_scr[g], preferred_element_type=F32).astype(BF16).T.astype(F32)
    fwd = lax.broadcasted_iota(jnp.int32, (1, sw), 1) < hw

    def step(j, carry):
        rf = pl.ds(pl.multiple_of(j * batch, batch), batch)
        rb = pl.ds(pl.multiple_of((n_chunks - 1 - j) * batch, batch), batch)
        out = []
        for g in range(gb):
            sr, si = carry[g]
            sin_scr[g, rf, 0:hw] = sr[:, 0:hw]
            sin_scr[g, rb, hw:sw] = sr[:, hw:sw]
            sin_scr[g, rf, sw:sw + hw] = si[:, 0:hw]
            sin_scr[g, rb, sw + hw:2 * sw] = si[:, hw:sw]
            wr = jnp.where(fwd, wc_scr[g, rf, 0:sw], wc_scr[g, rb, 0:sw])
            wi = jnp.where(fwd, wc_scr[g, rf, sw:2 * sw], wc_scr[g, rb, sw:2 * sw])
            ar = a_ref[g, 0]
            ai = a_ref[g, 1]
            out.append((ar * sr - ai * si + wr, ar * si + ai * sr + wi))
        return tuple(out)

    fin = lax.fori_loop(0, n_chunks, step, tuple((h0_ref[g, 0], h0_ref[g, 1]) for g in range(gb)))
    for g in range(gb):
        sint = sin_scr[g].astype(BF16).T
        y2t = (jnp.dot(toept_ref[g], zt_scr[g], preferred_element_type=F32)
               + jnp.dot(vt_ref[g], sint, preferred_element_type=F32))
        for s in range(t):
            yt_scr[s, g * hc:(g + 1) * hc, :] = y2t[s * hc:(s + 1) * hc, :].astype(yt_scr.dtype)
        if with_state:
            fin_ref[g, 0] = fin[g][0]
            fin_ref[g, 1] = fin[g][1]
    for s in range(t):
        y_ref[s] = yt_scr[s].T


def _ssm(u3, ssm_ops, h0, with_state):
    toept, wt, vt, a_t = ssm_ops
    n_slabs, t, rows, cw = u3.shape
    batch = h0.shape[2]
    n_chunks = rows // batch
    g = toept.shape[0]
    gb = g // n_slabs
    slab = pl.BlockSpec((None, t, rows, cw), lambda i: (i, 0, 0, 0))
    blk = lambda a: pl.BlockSpec((gb,) + a.shape[1:], lambda i: (i,) + (0,) * (a.ndim - 1))
    out_specs = [slab]
    out_shape = [jax.ShapeDtypeStruct(u3.shape, u3.dtype)]
    if with_state:
        out_specs.append(blk(h0))
        out_shape.append(jax.ShapeDtypeStruct(h0.shape, F32))
    nk = toept.shape[-1]
    outs = pl.pallas_call(
        functools.partial(_ssm_kernel, batch=batch, n_chunks=n_chunks, with_state=with_state),
        grid=(g // gb,),
        in_specs=[slab, blk(toept), blk(wt), blk(vt), blk(a_t), blk(h0)],
        out_specs=out_specs,
        out_shape=out_shape,
        scratch_shapes=[pltpu.VMEM((gb, nk, rows), BF16),
                        pltpu.VMEM((gb, rows, wt.shape[1]), F32),
                        pltpu.VMEM((gb, rows, wt.shape[1]), F32),
                        pltpu.VMEM((t, cw, rows), u3.dtype)],
        compiler_params=_params("arbitrary"),
        name="ssm_state" if with_state else "ssm",
    )(u3, toept, wt, vt, a_t, h0)
    return outs


def _state_to_lanes(h0):
    b, nd, nc, g, p = h0.shape
    return h0.astype(F32).transpose(3, 2, 0, 1, 4).reshape(g, nc, b, nd * p)


def _state_from_lanes(fin, nd):
    g, nc, b, w = fin.shape
    return fin.reshape(g, nc, b, nd, w // nd).transpose(2, 3, 1, 0, 4)


def _outproj_kernel(x_ref, a_ref, y_ref, u_ref, gs_ref, gate_ref, dskip_ref,
                    wglu_ref, bglu_ref, wout_ref, g_ref, o_ref):
    nb, tp, d = x_ref.shape
    y = _scan_block(y_ref) + dskip_ref[...] * _scan_block(u_ref)
    t, srows, d_ssm = y.shape
    rows = lambda ref: ref[...].reshape(nb * tp, ref.shape[-1])
    y = y.reshape(t * srows, d_ssm)
    c0 = math.sqrt(2.0 / math.pi)
    hy = 0.5 * y
    ys = hy + hy * jnp.tanh(y * (c0 + (c0 * 0.044715) * (y * y)))
    z = jnp.dot(ys.astype(BF16), wglu_ref[...], preferred_element_type=F32) + bglu_ref[...]
    ys = _from_scan_layout((ys * jax.nn.sigmoid(z)).reshape(t, srows, d_ssm), nb) * rows(gs_ref).astype(F32)
    cat = jnp.concatenate([rows(a_ref), ys.astype(BF16)], axis=-1)
    out = jnp.dot(cat, wout_ref[...], preferred_element_type=F32)
    ms = jnp.mean(out * out, axis=-1, keepdims=True)
    out = out * lax.rsqrt(ms + EPS)
    o_ref[...] = x_ref[...] + (g_ref[...] * gate_ref[...]) * out.reshape(nb, tp, d)


def _outproj(x, a_out, y_ssm, u, gs, gate, d_skip, w_glu_bf, b_glu, w_out_bf, norm_post):
    b, l, d = x.shape
    d_attn = a_out.shape[-1]
    d_ssm = gs.shape[-1]
    tp = TOKEN_TILE // b
    t = SSM_CHUNK
    const = lambda a: pl.BlockSpec(a.shape, lambda j: (0,) * a.ndim)
    tok = lambda w: pl.BlockSpec((b, tp, w), lambda j: (0, j, 0))
    scan = pl.BlockSpec((d_ssm // SCAN_SLAB, t, tp // t * b, SCAN_SLAB), lambda j: (0, 0, j, 0))
    consts = [gate, d_skip.reshape(1, d_ssm).astype(F32), w_glu_bf, b_glu.reshape(1, d_ssm).astype(F32),
              w_out_bf, norm_post.reshape(1, d).astype(F32)]
    return pl.pallas_call(
        _outproj_kernel,
        grid=(l // tp,),
        in_specs=[tok(d), tok(d_attn), scan, scan, tok(d_ssm)] + [const(a) for a in consts],
        out_specs=tok(d),
        out_shape=jax.ShapeDtypeStruct((b, l, d), F32),
        compiler_params=_params("arbitrary"),
        name="outproj",
    )(x, a_out, y_ssm, u, gs, *consts)


def _mixer(x, mod, h0_lanes, cache_k, cache_v, rope_tabs, emit_caches, with_state, lam, lam_init,
           norm_pre, w_in, ssm_ops, subln, d_skip, w_glu_bf, b_glu, w_out_bf, norm_post):
    b, l, d = x.shape
    d_ssm = w_glu_bf.shape[0]
    d_attn = (w_in.shape[1] - 2 * d_ssm) // 4
    shift, scale, gate = mod
    q, k, v, ga, u, gs, *caches = _inproj(x, shift, scale, norm_pre, w_in, rope_tabs, emit_caches,
                                          d_attn, d_ssm)
    a_out = _attention(lam, q, k, v, cache_k, cache_v, ga, subln, lam_init)
    outs = _ssm(u, ssm_ops, h0_lanes, with_state)
    y = _outproj(x, a_out, outs[0], u, gs, gate, d_skip, w_glu_bf, b_glu, w_out_bf, norm_post)
    fin = outs[1] if with_state else None
    return y, caches, fin


def kernel(x_prompt, x_sample, cache_k, cache_v, state_ssm, c, c_ctx, w_ada, b_ada, norm_pre, norm_post, w_in, lambda_qk, subln, ssm_A_re, ssm_A_im, ssm_log_dt, ssm_B_re, ssm_B_im, ssm_C_re, ssm_C_im, ssm_D, w_glu, b_glu, w_out):
    xp, xs = x_prompt, x_sample
    bp, lp, d = xp.shape
    bs, ls, _ = xs.shape
    depth = w_in.shape[0]
    nd = ssm_A_re.shape[1]
    g, p = ssm_A_re.shape[-2:]
    d_ssm = w_glu.shape[-1]
    d_attn = (w_in.shape[-1] - 2 * d_ssm) // 4
    assert bs + 1 <= COND_ROWS and nd == 2
    assert d_ssm % SCAN_SLAB == 0 and SCAN_SLAB % (d_ssm // g) == 0
    assert all(TOKEN_TILE % b == 0 and (TOKEN_TILE // b) % SSM_CHUNK == 0 and l % (TOKEN_TILE // b) == 0
               for b, l in ((bp, lp), (bs, ls)))
    assert all(l % min(ATTN_Q_TILE, l) == 0 and min(ATTN_Q_TILE, l) % min(ATTN_Q_SUB, l) == 0
               and b % max(1, ATTN_Q_TILE // l) == 0 for b, l in ((bp, lp), (bs, ls)))
    rope_tabs = _rope_tables(ls, d_attn)
    cond = jnp.zeros((COND_ROWS, d), F32).at[:bs].set(c.astype(F32)).at[bs].set(c_ctx.astype(F32))
    new_k, new_v, new_s = [], [], []
    for layer in range(depth):
        lam_init = 0.8 - 0.6 * math.exp(-0.3 * layer)
        m = _adaln(cond, w_ada[layer], b_ada[layer])
        mod_s = tuple(m[:bs, i * d:(i + 1) * d].reshape(bs, 1, d) for i in range(3))
        mod_p = tuple(m[bs:bs + 1, i * d:(i + 1) * d].reshape(1, 1, d) for i in range(3))
        ssm_ops, lam = _ssmprep(
            ssm_A_re[layer], ssm_A_im[layer], ssm_log_dt[layer], ssm_B_re[layer], ssm_B_im[layer],
            ssm_C_re[layer], ssm_C_im[layer], lambda_qk[layer], lam_init)
        shared = (lam, lam_init, norm_pre[layer], w_in[layer], ssm_ops, subln[layer],
                  ssm_D[layer], w_glu[layer].astype(BF16), b_glu[layer], w_out[layer].astype(BF16),
                  norm_post[layer])

        h0_p = jnp.zeros((g, 2, bp, nd * p), F32)
        xp, (k_p, v_p), fin = _mixer(xp, mod_p, h0_p, None, None, None, True, True, *shared)
        new_k.append(k_p.reshape(bp, lp, d_attn // V_DIM, V_DIM))
        new_v.append(v_p.reshape(bp, lp, d_attn // V_DIM, V_DIM))
        new_s.append(_state_from_lanes(fin, nd).astype(xp.dtype))

        h0_s = _state_to_lanes(state_ssm[:, layer])
        ck = cache_k[:, layer].reshape(bs, -1, V_DIM)
        cv = cache_v[:, layer].reshape(bs, -1, V_DIM)
        xs, _, _ = _mixer(xs, mod_s, h0_s, ck, cv, rope_tabs, False, False, *shared)

    return (xp, xs, jnp.stack(new_k, axis=1), jnp.stack(new_v, axis=1), jnp.stack(new_s, axis=1))
```

```python
import functools
import math

import jax
import jax.numpy as jnp
import numpy as np
from jax import lax
from jax.experimental import pallas as pl
from jax.experimental.pallas import tpu as pltpu

F32 = jnp.float32
BF16 = jnp.bfloat16

GRID_W = 64
QK_DIM = 64
V_DIM = 2 * QK_DIM
ROPE_THETA = 10000.0
EPS = 1e-6
Q_PRESCALE = math.log2(math.e) * QK_DIM ** -0.5

VREG_TILE = (8, 128)
SUM_ROWS = 16
COND_ROWS = 16
ADALN_K_BLOCK = 256
TOKEN_TILE = 1024
ATTN_Q_TILE = 1024
ATTN_Q_SUB = 512
ATTN_KEY_BLOCK = 256
SSM_CHUNK = 16
SCAN_SLAB = 128
VMEM_LIMIT = 48 * 1024 * 1024


def _silu(x):
    return x * jax.nn.sigmoid(x)


def _params(*sem, vmem_mib):
    return pltpu.CompilerParams(dimension_semantics=sem, vmem_limit_bytes=vmem_mib * 1024 * 1024)


def _adaln_kernel(c_ref, w_ref, b_ref, o_ref):
    @pl.when(pl.program_id(0) == 0)
    def _():
        o_ref[...] = jnp.broadcast_to(b_ref[...], o_ref.shape)

    s = _silu(c_ref[...]).astype(BF16)
    o_ref[...] += jnp.dot(s, w_ref[...].astype(BF16), preferred_element_type=F32)


def _adaln(cond, w_ada, b_ada):
    rows, d = cond.shape
    n = w_ada.shape[1]
    bk = ADALN_K_BLOCK
    return pl.pallas_call(
        _adaln_kernel,
        grid=(d // bk,),
        in_specs=[pl.BlockSpec((rows, bk), lambda j: (0, j)),
                  pl.BlockSpec((bk, n), lambda j: (j, 0)),
                  pl.BlockSpec((1, n), lambda j: (0, 0))],
        out_specs=pl.BlockSpec((rows, n), lambda j: (0, 0)),
        out_shape=jax.ShapeDtypeStruct((rows, n), F32),
        compiler_params=_params("arbitrary", vmem_mib=16),
        name="adaln",
    )(cond, w_ada, b_ada.reshape(1, n))


def _ssmprep_kernel(are_ref, aim_ref, ldt_ref, bre_ref, bim_ref, cre_ref, cim_ref, lq_ref,
                    toept_ref, wt_ref, vt_ref, a_ref, lam_ref,
                    cplre_scr, cplimn_scr, w_scr, v_scr, toep_scr, *, lam_init, chunk):
    a_re = are_ref[...]
    a_im = aim_ref[...]
    dt = jnp.exp(ldt_ref[...])
    mag = jnp.exp(a_re * dt)
    ab_re = mag * jnp.cos(a_im * dt)
    ab_im = mag * jnp.sin(a_im * dt)
    nr, ni = ab_re - 1.0, ab_im
    den = a_re * a_re + a_im * a_im
    f_re = (nr * a_re + ni * a_im) / den
    f_im = (ni * a_re - nr * a_im) / den
    b_re = bre_ref[...]
    b_im = bim_ref[...]
    bb_re = f_re * b_re - f_im * b_im
    bb_im = f_re * b_im + f_im * b_re
    c_re = cre_ref[...]
    c_im = cim_ref[...]
    hc = b_re.shape[1]
    sw = a_re.shape[-1]
    hw = sw // 2
    lo, hi = slice(0, hw), slice(hw, sw)
    lo_im, hi_im = slice(sw, sw + hw), slice(sw + hw, 2 * sw)
    blk = lambda j: slice(j * hc, (j + 1) * hc)
    last = chunk - 1
    cplre_scr[...] = jnp.zeros_like(cplre_scr)
    cplimn_scr[...] = jnp.zeros_like(cplimn_scr)
    pw_re = jnp.ones_like(ab_re)
    pw_im = jnp.zeros_like(ab_im)
    for tau in range(chunk + 1):
        cp_re = c_re * pw_re - c_im * pw_im
        cp_imn = -(c_re * pw_im + c_im * pw_re)
        if tau < chunk:
            cplre_scr[:, blk(last + tau), lo] = cp_re[:, :, lo]
            cplre_scr[:, blk(last - tau), hi] = cp_re[:, :, hi]
            cplimn_scr[:, blk(last + tau), lo] = cp_imn[:, :, lo]
            cplimn_scr[:, blk(last - tau), hi] = cp_imn[:, :, hi]
            e_re = pw_re * bb_re - pw_im * bb_im
            e_im = pw_re * bb_im + pw_im * bb_re
            w_scr[:, blk(last - tau), lo] = e_re[:, :, lo]
            w_scr[:, blk(tau), hi] = e_re[:, :, hi]
            w_scr[:, blk(last - tau), lo_im] = e_im[:, :, lo]
            w_scr[:, blk(tau), hi_im] = e_im[:, :, hi]
        if tau >= 1:
            v_scr[:, blk(tau - 1), lo] = cp_re[:, :, lo]
            v_scr[:, blk(chunk - tau), hi] = cp_re[:, :, hi]
            v_scr[:, blk(tau - 1), lo_im] = cp_imn[:, :, lo]
            v_scr[:, blk(chunk - tau), hi_im] = cp_imn[:, :, hi]
        if tau < chunk:
            pw_re, pw_im = pw_re * ab_re - pw_im * ab_im, pw_re * ab_im + pw_im * ab_re
    a_ref[:, 0] = pw_re
    a_ref[:, 1] = pw_im
    dims = (((2,), (2,)), ((0,), (0,)))
    taps = (lax.dot_general(bb_re.astype(BF16), cplre_scr[...].astype(BF16), dims,
                            preferred_element_type=F32)
            + lax.dot_general(bb_im.astype(BF16), cplimn_scr[...].astype(BF16), dims,
                              preferred_element_type=F32))
    nk = chunk * hc
    for s in range(chunk):
        toep_scr[:, blk(s), :] = taps[:, :, (last - s) * hc:(last - s) * hc + nk]
    for g in range(toept_ref.shape[0]):
        toept_ref[g] = toep_scr[g].astype(toept_ref.dtype).T
        wt_ref[g] = w_scr[g].astype(wt_ref.dtype).T
    vt_ref[...] = v_scr[...].astype(vt_ref.dtype)
    lq = lq_ref[...]
    s01 = jnp.sum(lq[0:1] * lq[1:2], axis=-1, keepdims=True)
    s23 = jnp.sum(lq[2:3] * lq[3:4], axis=-1, keepdims=True)
    lam = jnp.exp(s01) - jnp.exp(s23) + lam_init
    lam_ref[...] = jnp.broadcast_to(lam, lam_ref.shape)


def _ssmprep(a_re, a_im, log_dt, b_re, b_im, c_re, c_im, lq, lam_init):
    nd, g, p = a_re.shape
    hc = b_re.shape[-1]
    t = SSM_CHUNK
    sw = nd * p
    gb = SCAN_SLAB // hc
    row = lambda a: a.astype(F32).transpose(1, 0, 2).reshape(g, 1, sw)
    ldt = jnp.broadcast_to(log_dt[..., None], (nd, g, p))
    bt = lambda a: a.astype(F32).transpose(1, 3, 0, 2).reshape(g, hc, sw)
    ct = lambda a: a.astype(F32).transpose(1, 2, 0, 3).reshape(g, hc, sw)
    sds = jax.ShapeDtypeStruct
    nk = t * hc
    blk = lambda shape: pl.BlockSpec((gb,) + shape[1:], lambda i: (i,) + (0,) * (len(shape) - 1))
    mat = (g, nk, 2 * sw)
    toept, wt, vt, a_t, lam = pl.pallas_call(
        functools.partial(_ssmprep_kernel, lam_init=lam_init, chunk=t),
        grid=(g // gb,),
        in_specs=[blk((g, 1, sw))] * 3 + [blk((g, hc, sw))] * 4 + [pl.BlockSpec(lq.shape, lambda i: (0, 0))],
        out_specs=[blk((g, nk, nk)), blk((g, 2 * sw, nk)), blk(mat), blk((g, 2, 1, sw)),
                   pl.BlockSpec(VREG_TILE, lambda i: (0, 0))],
        out_shape=(sds((g, nk, nk), BF16), sds((g, 2 * sw, nk), BF16), sds(mat, BF16),
                   sds((g, 2, 1, sw), F32), sds(VREG_TILE, F32)),
        scratch_shapes=[pltpu.VMEM((gb, 2 * nk, sw), F32), pltpu.VMEM((gb, 2 * nk, sw), F32),
                        pltpu.VMEM((gb, nk, 2 * sw), F32), pltpu.VMEM((gb, nk, 2 * sw), F32),
                        pltpu.VMEM((gb, nk, nk), F32)],
        compiler_params=_params("arbitrary", vmem_mib=24),
        name="ssmprep",
    )(row(a_re), row(a_im), row(ldt), bt(b_re), bt(b_im), ct(c_re), ct(c_im), lq.astype(F32))
    return (toept, wt, vt, a_t), lam


def _to_scan_layout(u, o_ref, batch):
    n_slabs, t, _, cw = o_ref.shape
    tp = u.shape[0] // batch
    u_tb = jnp.swapaxes(u.reshape(batch, tp, u.shape[-1]), 0, 1)
    for c in range(tp // t):
        for s in range(t):
            for v in range(n_slabs):
                o_ref[v, s, c * batch:(c + 1) * batch, :] = u_tb[c * t + s][:, v * cw:(v + 1) * cw].astype(o_ref.dtype)


def _scan_block(ref):
    return jnp.concatenate([ref[v] for v in range(ref.shape[0])], axis=-1).astype(F32)


def _from_scan_layout(ref, batch):
    t, rows, d = ref.shape
    tp = rows // batch * t
    u_tb = jnp.stack([ref[s, c * batch:(c + 1) * batch, :] for c in range(tp // t) for s in range(t)], axis=0)
    return jnp.swapaxes(u_tb, 0, 1).reshape(batch * tp, d)


def _inproj_kernel(*refs, rope, emit_caches, d_attn):
    n_in = 7 if rope else 5
    x_ref, shift_ref, scale_ref, g_ref, w_ref = refs[:5]
    cos_ref, sin_ref = refs[5:7] if rope else (None, None)
    q_ref, k_ref, v_ref, ga_ref, u_ref, gs_ref = refs[n_in:n_in + 6]
    kc_ref, vc_ref = refs[n_in + 6:] if emit_caches else (None, None)
    x = x_ref[...]
    nb, tp, d = x.shape
    ms = jnp.mean(x * x, axis=-1, keepdims=True)
    h = x * lax.rsqrt(ms + EPS) * (g_ref[...] * (1.0 + scale_ref[...])) + shift_ref[...]
    hb = h.reshape(nb * tp, d).astype(BF16)
    proj = lambda lo, hi: jnp.dot(hb, w_ref[:, lo:hi].astype(BF16), preferred_element_type=F32)
    da = d_attn
    d_ssm = u_ref.shape[0] * u_ref.shape[-1]
    tok = lambda z: z.reshape(nb, tp, z.shape[-1])
    q = proj(0, da)
    k = proj(da, 2 * da)
    if rope:
        lane = lax.broadcasted_iota(jnp.int32, (1, da), 1)
        low = (lane % (QK_DIM // 2)) < (QK_DIM // 4)
        cos = cos_ref[...]
        sin = sin_ref[...]

        def rot(z):
            partner = jnp.where(low, pltpu.roll(z, da - QK_DIM // 4, 1), pltpu.roll(z, QK_DIM // 4, 1))
            return tok(z) * cos + tok(partner) * sin

        q = rot(q)
        k = rot(k)
    else:
        q = tok(q)
        k = tok(k)
    q_ref[...] = (q * Q_PRESCALE).astype(q_ref.dtype)
    k_ref[...] = k.astype(k_ref.dtype)
    n_heads = da // V_DIM
    if emit_caches:
        for hd in range(n_heads):
            kc_ref[:, pl.ds(hd, tp, stride=n_heads), :] = k[:, :, hd * V_DIM:(hd + 1) * V_DIM]
    v = tok(proj(2 * da, 3 * da))
    v_ref[...] = v.astype(v_ref.dtype)
    if emit_caches:
        for hd in range(n_heads):
            vc_ref[:, pl.ds(hd, tp, stride=n_heads), :] = v[:, :, hd * V_DIM:(hd + 1) * V_DIM]
    ga_ref[...] = tok(_silu(proj(3 * da, 4 * da))).astype(ga_ref.dtype)
    gs_ref[...] = tok(_silu(proj(4 * da + d_ssm, 4 * da + 2 * d_ssm))).astype(gs_ref.dtype)
    _to_scan_layout(proj(4 * da, 4 * da + d_ssm), u_ref, nb)


def _inproj(x, shift, scale, norm_pre, w_in, rope_tabs, emit_caches, d_attn, d_ssm):
    b, l, d = x.shape
    tp = TOKEN_TILE // b
    t = SSM_CHUNK
    rope = rope_tabs is not None
    const = lambda a: pl.BlockSpec(a.shape, lambda j: (0,) * a.ndim)
    tok = lambda w: pl.BlockSpec((b, tp, w), lambda j: (0, j, 0))
    g2 = norm_pre.reshape(1, d).astype(F32)
    w_spec = pl.BlockSpec(w_in.shape, lambda j: (0, 0), pipeline_mode=pl.Buffered(1))
    in_specs = [tok(d), const(shift), const(scale), const(g2), w_spec]
    args = [x, shift, scale, g2, w_in]
    if rope:
        in_specs += [pl.BlockSpec((tp, d_attn), lambda j: (j, 0))] * 2
        args += list(rope_tabs)
    scan_rows = tp // t * b
    attn_shape = jax.ShapeDtypeStruct((b, l, d_attn), BF16)
    out_specs = [tok(d_attn), tok(d_attn), tok(d_attn), tok(d_attn),
                 pl.BlockSpec((d_ssm // SCAN_SLAB, t, scan_rows, SCAN_SLAB), lambda j: (0, 0, j, 0)),
                 tok(d_ssm)]
    out_shape = [attn_shape, attn_shape, attn_shape, attn_shape,
                 jax.ShapeDtypeStruct((d_ssm // SCAN_SLAB, t, l // t * b, SCAN_SLAB), BF16),
                 jax.ShapeDtypeStruct((b, l, d_ssm), BF16)]
    if emit_caches:
        n_heads = d_attn // V_DIM
        out_specs += [pl.BlockSpec((b, tp * n_heads, V_DIM), lambda j: (0, j, 0))] * 2
        out_shape += [jax.ShapeDtypeStruct((b, l * n_heads, V_DIM), F32)] * 2
    return pl.pallas_call(
        functools.partial(_inproj_kernel, rope=rope, emit_caches=emit_caches, d_attn=d_attn),
        grid=(l // tp,),
        in_specs=in_specs,
        out_specs=out_specs,
        out_shape=out_shape,
        compiler_params=_params("arbitrary", vmem_mib=48),
        name="inproj_rope" if rope else "inproj",
    )(*args)


def _rope_tables(l, d_attn):
    rows = l // GRID_W
    row = np.repeat(np.arange(rows, dtype=np.float64), GRID_W)
    col = np.tile(np.arange(GRID_W, dtype=np.float64), rows)
    n_freq = QK_DIM // 4
    inv = ROPE_THETA ** (-np.arange(n_freq, dtype=np.float64) / n_freq)
    ang_r = row[:, None] * inv
    ang_c = col[:, None] * inv
    cos64 = np.concatenate([np.cos(ang_r), np.cos(ang_r), np.cos(ang_c), np.cos(ang_c)], axis=-1)
    sin64 = np.concatenate([-np.sin(ang_r), np.sin(ang_r), -np.sin(ang_c), np.sin(ang_c)], axis=-1)
    reps = d_attn // QK_DIM
    return (jnp.asarray(np.tile(cos64, (1, reps)), dtype=F32),
            jnp.asarray(np.tile(sin64, (1, reps)), dtype=F32))


def _head(ref, bi, h, n_heads):
    if ref.shape[-1] == V_DIM:
        return ref[bi, pl.ds(h, ref.shape[1] // n_heads, stride=n_heads), :]
    return ref[bi, :, h * V_DIM:(h + 1) * V_DIM]


def _attn_kernel(*refs, has_cache, lam_init):
    if has_cache:
        lam_ref, q_ref, k_ref, v_ref, ck_ref, cv_ref, ga_ref, sub_ref, o_ref, k_scr, vt_scr = refs
    else:
        lam_ref, q_ref, k_ref, v_ref, ga_ref, sub_ref, o_ref, k_scr, vt_scr = refs
    nbk, tq = q_ref.shape[0], q_ref.shape[1]
    n_heads, lk = k_scr.shape[0] // nbk, k_scr.shape[1]
    ln = k_ref.shape[1]
    lam = lam_ref[0:1, 0:1]
    lane = lax.broadcasted_iota(jnp.int32, (1, V_DIM), 1)
    first = lane < QK_DIM
    dims = (((1,), (1,)), ((), ()))
    slot = lambda bi, h: bi * n_heads + h

    @pl.when(pl.program_id(1) == 0)
    def _():
        for bi in range(nbk):
            for h in range(n_heads):
                n = slot(bi, h)
                if has_cache:
                    k_scr[n, 0:lk - ln, :] = _head(ck_ref, bi, h, n_heads).astype(BF16)
                    vt_scr[n, 0:V_DIM, 0:lk - ln] = _head(cv_ref, bi, h, n_heads).astype(F32).T.astype(BF16)
                k_scr[n, lk - ln:lk, :] = _head(k_ref, bi, h, n_heads).astype(BF16)
                vt_scr[n, 0:V_DIM, lk - ln:lk] = _head(v_ref, bi, h, n_heads).astype(F32).T.astype(BF16)
                vt_scr[n, V_DIM:, :] = jnp.ones((vt_scr.shape[1] - V_DIM, lk), BF16)

    sub = min(tq, ATTN_Q_SUB)
    items = [(bi, h, r) for bi in range(nbk) for h in range(n_heads) for r in range(0, tq, sub)]

    def scores(item):
        bi, h, r = item
        q = _head(q_ref, bi, h, n_heads)[r:r + sub, :]
        zero = jnp.zeros_like(q)
        k = k_scr[slot(bi, h)]
        return (lax.dot_general(k, jnp.where(first, q, zero), dims, preferred_element_type=F32),
                lax.dot_general(k, jnp.where(first, zero, q), dims, preferred_element_type=F32))

    nxt = scores(items[0])
    for n, (bi, h, r) in enumerate(items):
        st1, st2 = nxt
        if n + 1 < len(items):
            nxt = scores(items[n + 1])
        vt = vt_scr[slot(bi, h)]
        m1 = jnp.max(st1, axis=0, keepdims=True)
        m2 = jnp.max(st2, axis=0, keepdims=True)
        o1 = o2 = None
        for lo in range(0, lk, ATTN_KEY_BLOCK):
            hi = min(lk, lo + ATTN_KEY_BLOCK)
            e1 = jnp.exp2(st1[lo:hi, :] - m1).astype(BF16)
            e2 = jnp.exp2(st2[lo:hi, :] - m2).astype(BF16)
            p1 = jnp.dot(vt[:, lo:hi], e1, preferred_element_type=F32)
            p2 = jnp.dot(vt[:, lo:hi], e2, preferred_element_type=F32)
            o1 = p1 if o1 is None else o1 + p1
            o2 = p2 if o2 is None else o2 + p2
        r1 = 1.0 / o1[V_DIM:V_DIM + 1, :]
        r2 = lam / o2[V_DIM:V_DIM + 1, :]
        o = (o1[0:V_DIM, :] * r1 - o2[0:V_DIM, :] * r2).T
        ms = jnp.mean(o * o, axis=-1, keepdims=True)
        o = o * lax.rsqrt(ms + EPS) * sub_ref[...] * (1.0 - lam_init)
        ga = _head(ga_ref, bi, h, n_heads)[r:r + sub, :].astype(F32)
        o_ref[bi, r:r + sub, h * V_DIM:(h + 1) * V_DIM] = (o * ga).astype(o_ref.dtype)


def _attention(lam, q, k, v, cache_k, cache_v, ga, subln, lam_init):
    b, l, d_attn = q.shape
    n_heads = d_attn // V_DIM
    tq = min(ATTN_Q_TILE, l)
    nq = l // tq
    nbk = max(1, ATTN_Q_TILE // l)
    has_cache = cache_k is not None
    q_spec = pl.BlockSpec((nbk, tq, d_attn), lambda i, j: (i, j, 0))
    whole = lambda a: pl.BlockSpec((nbk,) + a.shape[1:], lambda i, j: (i,) + (0,) * (a.ndim - 1))
    in_specs = [pl.BlockSpec(VREG_TILE, lambda i, j: (0, 0)), q_spec, whole(k), whole(v)]
    args = [lam, q, k, v]
    if has_cache:
        in_specs += [whole(cache_k), whole(cache_v)]
        args += [cache_k, cache_v]
    in_specs += [q_spec, pl.BlockSpec((1, V_DIM), lambda i, j: (0, 0))]
    args += [ga, subln.reshape(1, V_DIM).astype(F32)]
    lk = l + (cache_k.shape[1] // n_heads if has_cache else 0)
    out = pl.pallas_call(
        functools.partial(_attn_kernel, has_cache=has_cache, lam_init=lam_init),
        grid=(b // nbk, nq),
        in_specs=in_specs,
        out_specs=q_spec,
        out_shape=jax.ShapeDtypeStruct((b, l, d_attn), BF16),
        scratch_shapes=[pltpu.VMEM((nbk * n_heads, lk, V_DIM), BF16),
                        pltpu.VMEM((nbk * n_heads, V_DIM + SUM_ROWS, lk), BF16)],
        compiler_params=_params("arbitrary", "arbitrary", vmem_mib=32),
        name="attn_cache" if has_cache else "attn",
    )(*args)
    return out


def _ssm_kernel(*refs, batch, n_chunks, with_state):
    if with_state:
        u_ref, toept_ref, wt_ref, vt_ref, a_ref, h0_ref, y_ref, fin_ref, zt_scr, wc_scr, sin_scr, yt_scr = refs
    else:
        u_ref, toept_ref, wt_ref, vt_ref, a_ref, h0_ref, y_ref, zt_scr, wc_scr, sin_scr, yt_scr = refs
    t = u_ref.shape[0]
    gb = zt_scr.shape[0]
    hc = zt_scr.shape[1] // t
    sw = a_ref.shape[-1]
    hw = sw // 2
    for s in range(t):
        at = u_ref[s].T
        for g in range(gb):
            zt_scr[g, s * hc:(s + 1) * hc, :] = at[g * hc:(g + 1) * hc, :]
    for g in range(gb):
        wc_scr[g] = jnp.dot(wt_ref[g], zt_scr[g], preferred_element_type=F32).astype(BF16).T.astype(F32)
    fwd = lax.broadcasted_iota(jnp.int32, (1, sw), 1) < hw

    def step(j, carry):
        rf = pl.ds(pl.multiple_of(j * batch, batch), batch)
        rb = pl.ds(pl.multiple_of((n_chunks - 1 - j) * batch, batch), batch)
        out = []
        for g in range(gb):
            sr, si = carry[g]
            sin_scr[g, rf, 0:hw] = sr[:, 0:hw]
            sin_scr[g, rb, hw:sw] = sr[:, hw:sw]
            sin_scr[g, rf, sw:sw + hw] = si[:, 0:hw]
            sin_scr[g, rb, sw + hw:2 * sw] = si[:, hw:sw]
            wr = jnp.where(fwd, wc_scr[g, rf, 0:sw], wc_scr[g, rb, 0:sw])
            wi = jnp.where(fwd, wc_scr[g, rf, sw:2 * sw], wc_scr[g, rb, sw:2 * sw])
            ar = a_ref[g, 0]
            ai = a_ref[g, 1]
            out.append((ar * sr - ai * si + wr, ar * si + ai * sr + wi))
        return tuple(out)

    fin = lax.fori_loop(0, n_chunks, step, tuple((h0_ref[g, 0], h0_ref[g, 1]) for g in range(gb)))
    for g in range(gb):
        sint = sin_scr[g].astype(BF16).T
        y2t = (jnp.dot(toept_ref[g], zt_scr[g], preferred_element_type=F32)
               + jnp.dot(vt_ref[g], sint, preferred_element_type=F32))
        for s in range(t):
            yt_scr[s, g * hc:(g + 1) * hc, :] = y2t[s * hc:(s + 1) * hc, :].astype(yt_scr.dtype)
        if with_state:
            fin_ref[g, 0] = fin[g][0]
            fin_ref[g, 1] = fin[g][1]
    for s in range(t):
        y_ref[s] = yt_scr[s].T


def _ssm(u3, ssm_ops, h0, with_state):
    toept, wt, vt, a_t = ssm_ops
    n_slabs, t, rows, cw = u3.shape
    batch = h0.shape[2]
    n_chunks = rows // batch
    g = toept.shape[0]
    gb = g // n_slabs
    slab = pl.BlockSpec((None, t, rows, cw), lambda i: (i, 0, 0, 0))
    blk = lambda a: pl.BlockSpec((gb,) + a.shape[1:], lambda i: (i,) + (0,) * (a.ndim - 1))
    out_specs = [slab]
    out_shape = [jax.ShapeDtypeStruct(u3.shape, u3.dtype)]
    if with_state:
        out_specs.append(blk(h0))
        out_shape.append(jax.ShapeDtypeStruct(h0.shape, F32))
    nk = toept.shape[-1]
    outs = pl.pallas_call(
        functools.partial(_ssm_kernel, batch=batch, n_chunks=n_chunks, with_state=with_state),
        grid=(g // gb,),
        in_specs=[slab, blk(toept), blk(wt), blk(vt), blk(a_t), blk(h0)],
        out_specs=out_specs,
        out_shape=out_shape,
        scratch_shapes=[pltpu.VMEM((gb, nk, rows), BF16),
                        pltpu.VMEM((gb, rows, wt.shape[1]), F32),
                        pltpu.VMEM((gb, rows, wt.shape[1]), F32),
                        pltpu.VMEM((t, cw, rows), u3.dtype)],
        compiler_params=_params("arbitrary", vmem_mib=36),
        name="ssm_state" if with_state else "ssm",
    )(u3, toept, wt, vt, a_t, h0)
    return outs


def _state_to_lanes(h0):
    b, nd, nc, g, p = h0.shape
    return h0.astype(F32).transpose(3, 2, 0, 1, 4).reshape(g, nc, b, nd * p)


def _state_from_lanes(fin, nd):
    g, nc, b, w = fin.shape
    return fin.reshape(g, nc, b, nd, w // nd).transpose(2, 3, 1, 0, 4)


def _outproj_kernel(x_ref, a_ref, y_ref, u_ref, gs_ref, gate_ref, dskip_ref,
                    wglu_ref, bglu_ref, wout_ref, g_ref, o_ref):
    nb, tp, d = x_ref.shape
    y = _scan_block(y_ref) + dskip_ref[...] * _scan_block(u_ref)
    t, srows, d_ssm = y.shape
    rows = lambda ref: ref[...].reshape(nb * tp, ref.shape[-1])
    y = y.reshape(t * srows, d_ssm)
    c0 = math.sqrt(2.0 / math.pi)
    hy = 0.5 * y
    ys = hy + hy * jnp.tanh(y * (c0 + (c0 * 0.044715) * (y * y)))
    z = jnp.dot(ys.astype(BF16), wglu_ref[...], preferred_element_type=F32) + bglu_ref[...]
    ys = _from_scan_layout((ys * jax.nn.sigmoid(z)).reshape(t, srows, d_ssm), nb) * rows(gs_ref).astype(F32)
    cat = jnp.concatenate([rows(a_ref), ys.astype(BF16)], axis=-1)
    out = jnp.dot(cat, wout_ref[...], preferred_element_type=F32)
    ms = jnp.mean(out * out, axis=-1, keepdims=True)
    out = out * lax.rsqrt(ms + EPS)
    o_ref[...] = x_ref[...] + (g_ref[...] * gate_ref[...]) * out.reshape(nb, tp, d)


def _outproj(x, a_out, y_ssm, u, gs, gate, d_skip, w_glu_bf, b_glu, w_out_bf, norm_post):
    b, l, d = x.shape
    d_attn = a_out.shape[-1]
    d_ssm = gs.shape[-1]
    tp = TOKEN_TILE // b
    t = SSM_CHUNK
    const = lambda a: pl.BlockSpec(a.shape, lambda j: (0,) * a.ndim)
    tok = lambda w: pl.BlockSpec((b, tp, w), lambda j: (0, j, 0))
    scan = pl.BlockSpec((d_ssm // SCAN_SLAB, t, tp // t * b, SCAN_SLAB), lambda j: (0, 0, j, 0))
    consts = [gate, d_skip.reshape(1, d_ssm).astype(F32), w_glu_bf, b_glu.reshape(1, d_ssm).astype(F32),
              w_out_bf, norm_post.reshape(1, d).astype(F32)]
    return pl.pallas_call(
        _outproj_kernel,
        grid=(l // tp,),
        in_specs=[tok(d), tok(d_attn), scan, scan, tok(d_ssm)] + [const(a) for a in consts],
        out_specs=tok(d),
        out_shape=jax.ShapeDtypeStruct((b, l, d), F32),
        compiler_params=_params("arbitrary", vmem_mib=40),
        name="outproj",
    )(x, a_out, y_ssm, u, gs, *consts)


def _mixer(x, mod, h0_lanes, cache_k, cache_v, rope_tabs, emit_caches, with_state, lam, lam_init,
           norm_pre, w_in, ssm_ops, subln, d_skip, w_glu_bf, b_glu, w_out_bf, norm_post):
    b, l, d = x.shape
    d_ssm = w_glu_bf.shape[0]
    d_attn = (w_in.shape[1] - 2 * d_ssm) // 4
    shift, scale, gate = mod
    q, k, v, ga, u, gs, *caches = _inproj(x, shift, scale, norm_pre, w_in, rope_tabs, emit_caches,
                                          d_attn, d_ssm)
    a_out = _attention(lam, q, k, v, cache_k, cache_v, ga, subln, lam_init)
    outs = _ssm(u, ssm_ops, h0_lanes, with_state)
    y = _outproj(x, a_out, outs[0], u, gs, gate, d_skip, w_glu_bf, b_glu, w_out_bf, norm_post)
    fin = outs[1] if with_state else None
    return y, caches, fin


def kernel(x_prompt, x_sample, cache_k, cache_v, state_ssm, c, c_ctx, w_ada, b_ada, norm_pre, norm_post, w_in, lambda_qk, subln, ssm_A_re, ssm_A_im, ssm_log_dt, ssm_B_re, ssm_B_im, ssm_C_re, ssm_C_im, ssm_D, w_glu, b_glu, w_out):
    xp, xs = x_prompt, x_sample
    bp, lp, d = xp.shape
    bs, ls, _ = xs.shape
    depth = w_in.shape[0]
    nd = ssm_A_re.shape[1]
    g, p = ssm_A_re.shape[-2:]
    d_ssm = w_glu.shape[-1]
    d_attn = (w_in.shape[-1] - 2 * d_ssm) // 4
    assert bs + 1 <= COND_ROWS and nd == 2
    assert d_ssm % SCAN_SLAB == 0 and SCAN_SLAB % (d_ssm // g) == 0
    assert all(TOKEN_TILE % b == 0 and (TOKEN_TILE // b) % SSM_CHUNK == 0 and l % (TOKEN_TILE // b) == 0
               for b, l in ((bp, lp), (bs, ls)))
    assert all(l % min(ATTN_Q_TILE, l) == 0 and min(ATTN_Q_TILE, l) % min(ATTN_Q_SUB, l) == 0
               and b % max(1, ATTN_Q_TILE // l) == 0 for b, l in ((bp, lp), (bs, ls)))
    rope_tabs = _rope_tables(ls, d_attn)
    cond = jnp.zeros((COND_ROWS, d), F32).at[:bs].set(c.astype(F32)).at[bs].set(c_ctx.astype(F32))
    new_k, new_v, new_s = [], [], []
    for layer in range(depth):
        lam_init = 0.8 - 0.6 * math.exp(-0.3 * layer)
        m = _adaln(cond, w_ada[layer], b_ada[layer])
        mod_s = tuple(m[:bs, i * d:(i + 1) * d].reshape(bs, 1, d) for i in range(3))
        mod_p = tuple(m[bs:bs + 1, i * d:(i + 1) * d].reshape(1, 1, d) for i in range(3))
        ssm_ops, lam = _ssmprep(
            ssm_A_re[layer], ssm_A_im[layer], ssm_log_dt[layer], ssm_B_re[layer], ssm_B_im[layer],
            ssm_C_re[layer], ssm_C_im[layer], lambda_qk[layer], lam_init)
        shared = (lam, lam_init, norm_pre[layer], w_in[layer], ssm_ops, subln[layer],
                  ssm_D[layer], w_glu[layer].astype(BF16), b_glu[layer], w_out[layer].astype(BF16),
                  norm_post[layer])

        h0_p = jnp.zeros((g, 2, bp, nd * p), F32)
        xp, (k_p, v_p), fin = _mixer(xp, mod_p, h0_p, None, None, None, True, True, *shared)
        new_k.append(k_p.reshape(bp, lp, d_attn // V_DIM, V_DIM))
        new_v.append(v_p.reshape(bp, lp, d_attn // V_DIM, V_DIM))
        new_s.append(_state_from_lanes(fin, nd).astype(xp.dtype))

        h0_s = _state_to_lanes(state_ssm[:, layer])
        ck = cache_k[:, layer].reshape(bs, -1, V_DIM)
        cv = cache_v[:, layer].reshape(bs, -1, V_DIM)
        xs, _, _ = _mixer(xs, mod_s, h0_s, ck, cv, rope_tabs, False, False, *shared)

    return (xp, xs, jnp.stack(new_k, axis=1), jnp.stack(new_v, axis=1), jnp.stack(new_s, axis=1))
```

```python
import functools
import math

import jax
import jax.numpy as jnp
import numpy as np
from jax import lax
from jax.experimental import pallas as pl
from jax.experimental.pallas import tpu as pltpu

F32 = jnp.float32
BF16 = jnp.bfloat16

GRID_W = 64
QK_DIM = 64
V_DIM = 2 * QK_DIM
ROPE_THETA = 10000.0
EPS = 1e-6
Q_PRESCALE = math.log2(math.e) * QK_DIM ** -0.5

VREG_TILE = (8, 128)
SUM_ROWS = 16
COND_ROWS = 16
ADALN_K_BLOCK = 256
TOKEN_TILE = 1024
ATTN_Q_TILE = 2048
ATTN_Q_SUB = 512
ATTN_KEY_BLOCK = 256
SSM_CHUNK = 16
SCAN_SLAB = 128
VMEM_LIMIT = 48 * 1024 * 1024


def _silu(x):
    return x * jax.nn.sigmoid(x)


def _params(*sem):
    return pltpu.CompilerParams(dimension_semantics=sem, vmem_limit_bytes=VMEM_LIMIT)


def _adaln_kernel(c_ref, w_ref, b_ref, o_ref):
    @pl.when(pl.program_id(0) == 0)
    def _():
        o_ref[...] = jnp.broadcast_to(b_ref[...], o_ref.shape)

    s = _silu(c_ref[...]).astype(BF16)
    o_ref[...] += jnp.dot(s, w_ref[...].astype(BF16), preferred_element_type=F32)


def _adaln(cond, w_ada, b_ada):
    rows, d = cond.shape
    n = w_ada.shape[1]
    bk = ADALN_K_BLOCK
    return pl.pallas_call(
        _adaln_kernel,
        grid=(d // bk,),
        in_specs=[pl.BlockSpec((rows, bk), lambda j: (0, j)),
                  pl.BlockSpec((bk, n), lambda j: (j, 0)),
                  pl.BlockSpec((1, n), lambda j: (0, 0))],
        out_specs=pl.BlockSpec((rows, n), lambda j: (0, 0)),
        out_shape=jax.ShapeDtypeStruct((rows, n), F32),
        compiler_params=_params("arbitrary"),
        name="adaln",
    )(cond, w_ada, b_ada.reshape(1, n))


def _ssmprep_kernel(are_ref, aim_ref, ldt_ref, bre_ref, bim_ref, cre_ref, cim_ref, lq_ref,
                    toept_ref, wt_ref, vt_ref, a_ref, lam_ref,
                    cplre_scr, cplimn_scr, w_scr, v_scr, toep_scr, *, lam_init, chunk):
    a_re = are_ref[...]
    a_im = aim_ref[...]
    dt = jnp.exp(ldt_ref[...])
    mag = jnp.exp(a_re * dt)
    ab_re = mag * jnp.cos(a_im * dt)
    ab_im = mag * jnp.sin(a_im * dt)
    nr, ni = ab_re - 1.0, ab_im
    den = a_re * a_re + a_im * a_im
    f_re = (nr * a_re + ni * a_im) / den
    f_im = (ni * a_re - nr * a_im) / den
    b_re = bre_ref[...]
    b_im = bim_ref[...]
    bb_re = f_re * b_re - f_im * b_im
    bb_im = f_re * b_im + f_im * b_re
    c_re = cre_ref[...]
    c_im = cim_ref[...]
    hc = b_re.shape[1]
    sw = a_re.shape[-1]
    hw = sw // 2
    lo, hi = slice(0, hw), slice(hw, sw)
    lo_im, hi_im = slice(sw, sw + hw), slice(sw + hw, 2 * sw)
    blk = lambda j: slice(j * hc, (j + 1) * hc)
    last = chunk - 1
    cplre_scr[...] = jnp.zeros_like(cplre_scr)
    cplimn_scr[...] = jnp.zeros_like(cplimn_scr)
    pw_re = jnp.ones_like(ab_re)
    pw_im = jnp.zeros_like(ab_im)
    for tau in range(chunk + 1):
        cp_re = c_re * pw_re - c_im * pw_im
        cp_imn = -(c_re * pw_im + c_im * pw_re)
        if tau < chunk:
            cplre_scr[:, blk(last + tau), lo] = cp_re[:, :, lo]
            cplre_scr[:, blk(last - tau), hi] = cp_re[:, :, hi]
            cplimn_scr[:, blk(last + tau), lo] = cp_imn[:, :, lo]
            cplimn_scr[:, blk(last - tau), hi] = cp_imn[:, :, hi]
            e_re = pw_re * bb_re - pw_im * bb_im
            e_im = pw_re * bb_im + pw_im * bb_re
            w_scr[:, blk(last - tau), lo] = e_re[:, :, lo]
            w_scr[:, blk(tau), hi] = e_re[:, :, hi]
            w_scr[:, blk(last - tau), lo_im] = e_im[:, :, lo]
            w_scr[:, blk(tau), hi_im] = e_im[:, :, hi]
        if tau >= 1:
            v_scr[:, blk(tau - 1), lo] = cp_re[:, :, lo]
            v_scr[:, blk(chunk - tau), hi] = cp_re[:, :, hi]
            v_scr[:, blk(tau - 1), lo_im] = cp_imn[:, :, lo]
            v_scr[:, blk(chunk - tau), hi_im] = cp_imn[:, :, hi]
        if tau < chunk:
            pw_re, pw_im = pw_re * ab_re - pw_im * ab_im, pw_re * ab_im + pw_im * ab_re
    a_ref[:, 0] = pw_re
    a_ref[:, 1] = pw_im
    dims = (((2,), (2,)), ((0,), (0,)))
    taps = (lax.dot_general(bb_re.astype(BF16), cplre_scr[...].astype(BF16), dims,
                            preferred_element_type=F32)
            + lax.dot_general(bb_im.astype(BF16), cplimn_scr[...].astype(BF16), dims,
                              preferred_element_type=F32))
    nk = chunk * hc
    for s in range(chunk):
        toep_scr[:, blk(s), :] = taps[:, :, (last - s) * hc:(last - s) * hc + nk]
    for g in range(toept_ref.shape[0]):
        toept_ref[g] = toep_scr[g].astype(toept_ref.dtype).T
        wt_ref[g] = w_scr[g].astype(wt_ref.dtype).T
    vt_ref[...] = v_scr[...].astype(vt_ref.dtype)
    lq = lq_ref[...]
    s01 = jnp.sum(lq[0:1] * lq[1:2], axis=-1, keepdims=True)
    s23 = jnp.sum(lq[2:3] * lq[3:4], axis=-1, keepdims=True)
    lam = jnp.exp(s01) - jnp.exp(s23) + lam_init
    lam_ref[...] = jnp.broadcast_to(lam, lam_ref.shape)


def _ssmprep(a_re, a_im, log_dt, b_re, b_im, c_re, c_im, lq, lam_init):
    nd, g, p = a_re.shape
    hc = b_re.shape[-1]
    t = SSM_CHUNK
    sw = nd * p
    gb = SCAN_SLAB // hc
    row = lambda a: a.astype(F32).transpose(1, 0, 2).reshape(g, 1, sw)
    ldt = jnp.broadcast_to(log_dt[..., None], (nd, g, p))
    bt = lambda a: a.astype(F32).transpose(1, 3, 0, 2).reshape(g, hc, sw)
    ct = lambda a: a.astype(F32).transpose(1, 2, 0, 3).reshape(g, hc, sw)
    sds = jax.ShapeDtypeStruct
    nk = t * hc
    blk = lambda shape: pl.BlockSpec((gb,) + shape[1:], lambda i: (i,) + (0,) * (len(shape) - 1))
    mat = (g, nk, 2 * sw)
    toept, wt, vt, a_t, lam = pl.pallas_call(
        functools.partial(_ssmprep_kernel, lam_init=lam_init, chunk=t),
        grid=(g // gb,),
        in_specs=[blk((g, 1, sw))] * 3 + [blk((g, hc, sw))] * 4 + [pl.BlockSpec(lq.shape, lambda i: (0, 0))],
        out_specs=[blk((g, nk, nk)), blk((g, 2 * sw, nk)), blk(mat), blk((g, 2, 1, sw)),
                   pl.BlockSpec(VREG_TILE, lambda i: (0, 0))],
        out_shape=(sds((g, nk, nk), BF16), sds((g, 2 * sw, nk), BF16), sds(mat, BF16),
                   sds((g, 2, 1, sw), F32), sds(VREG_TILE, F32)),
        scratch_shapes=[pltpu.VMEM((gb, 2 * nk, sw), F32), pltpu.VMEM((gb, 2 * nk, sw), F32),
                        pltpu.VMEM((gb, nk, 2 * sw), F32), pltpu.VMEM((gb, nk, 2 * sw), F32),
                        pltpu.VMEM((gb, nk, nk), F32)],
        compiler_params=_params("arbitrary"),
        name="ssmprep",
    )(row(a_re), row(a_im), row(ldt), bt(b_re), bt(b_im), ct(c_re), ct(c_im), lq.astype(F32))
    return (toept, wt, vt, a_t), lam


def _to_scan_layout(u, o_ref, batch):
    n_slabs, t, _, cw = o_ref.shape
    tp = u.shape[0] // batch
    u_tb = jnp.swapaxes(u.reshape(batch, tp, u.shape[-1]), 0, 1)
    for c in range(tp // t):
        for s in range(t):
            for v in range(n_slabs):
                o_ref[v, s, c * batch:(c + 1) * batch, :] = u_tb[c * t + s][:, v * cw:(v + 1) * cw].astype(o_ref.dtype)


def _scan_block(ref):
    return jnp.concatenate([ref[v] for v in range(ref.shape[0])], axis=-1).astype(F32)


def _from_scan_layout(ref, batch):
    t, rows, d = ref.shape
    tp = rows // batch * t
    u_tb = jnp.stack([ref[s, c * batch:(c + 1) * batch, :] for c in range(tp // t) for s in range(t)], axis=0)
    return jnp.swapaxes(u_tb, 0, 1).reshape(batch * tp, d)


def _inproj_kernel(*refs, rope, emit_caches, d_attn):
    n_in = 7 if rope else 5
    x_ref, shift_ref, scale_ref, g_ref, w_ref = refs[:5]
    cos_ref, sin_ref = refs[5:7] if rope else (None, None)
    q_ref, k_ref, v_ref, ga_ref, u_ref, gs_ref = refs[n_in:n_in + 6]
    kc_ref, vc_ref = refs[n_in + 6:] if emit_caches else (None, None)
    x = x_ref[...]
    nb, tp, d = x.shape
    ms = jnp.mean(x * x, axis=-1, keepdims=True)
    h = x * lax.rsqrt(ms + EPS) * (g_ref[...] * (1.0 + scale_ref[...])) + shift_ref[...]
    hb = h.reshape(nb * tp, d).astype(BF16)
    proj = lambda lo, hi: jnp.dot(hb, w_ref[:, lo:hi].astype(BF16), preferred_element_type=F32)
    da = d_attn
    d_ssm = u_ref.shape[0] * u_ref.shape[-1]
    tok = lambda z: z.reshape(nb, tp, z.shape[-1])
    q = proj(0, da)
    k = proj(da, 2 * da)
    if rope:
        lane = lax.broadcasted_iota(jnp.int32, (1, da), 1)
        low = (lane % (QK_DIM // 2)) < (QK_DIM // 4)
        cos = cos_ref[...]
        sin = sin_ref[...]

        def rot(z):
            partner = jnp.where(low, pltpu.roll(z, da - QK_DIM // 4, 1), pltpu.roll(z, QK_DIM // 4, 1))
            return tok(z) * cos + tok(partner) * sin

        q = rot(q)
        k = rot(k)
    else:
        q = tok(q)
        k = tok(k)
    q_ref[...] = (q * Q_PRESCALE).astype(q_ref.dtype)
    k_ref[...] = k.astype(k_ref.dtype)
    n_heads = da // V_DIM
    if emit_caches:
        for hd in range(n_heads):
            kc_ref[:, pl.ds(hd, tp, stride=n_heads), :] = k[:, :, hd * V_DIM:(hd + 1) * V_DIM]
    v = tok(proj(2 * da, 3 * da))
    v_ref[...] = v.astype(v_ref.dtype)
    if emit_caches:
        for hd in range(n_heads):
            vc_ref[:, pl.ds(hd, tp, stride=n_heads), :] = v[:, :, hd * V_DIM:(hd + 1) * V_DIM]
    ga_ref[...] = tok(_silu(proj(3 * da, 4 * da))).astype(ga_ref.dtype)
    gs_ref[...] = tok(_silu(proj(4 * da + d_ssm, 4 * da + 2 * d_ssm))).astype(gs_ref.dtype)
    _to_scan_layout(proj(4 * da, 4 * da + d_ssm), u_ref, nb)


def _inproj(x, shift, scale, norm_pre, w_in, rope_tabs, emit_caches, d_attn, d_ssm):
    b, l, d = x.shape
    tp = TOKEN_TILE // b
    t = SSM_CHUNK
    rope = rope_tabs is not None
    const = lambda a: pl.BlockSpec(a.shape, lambda j: (0,) * a.ndim)
    tok = lambda w: pl.BlockSpec((b, tp, w), lambda j: (0, j, 0))
    g2 = norm_pre.reshape(1, d).astype(F32)
    w_spec = pl.BlockSpec(w_in.shape, lambda j: (0, 0), pipeline_mode=pl.Buffered(1))
    in_specs = [tok(d), const(shift), const(scale), const(g2), w_spec]
    args = [x, shift, scale, g2, w_in]
    if rope:
        in_specs += [pl.BlockSpec((tp, d_attn), lambda j: (j, 0))] * 2
        args += list(rope_tabs)
    scan_rows = tp // t * b
    attn_shape = jax.ShapeDtypeStruct((b, l, d_attn), BF16)
    out_specs = [tok(d_attn), tok(d_attn), tok(d_attn), tok(d_attn),
                 pl.BlockSpec((d_ssm // SCAN_SLAB, t, scan_rows, SCAN_SLAB), lambda j: (0, 0, j, 0)),
                 tok(d_ssm)]
    out_shape = [attn_shape, attn_shape, attn_shape, attn_shape,
                 jax.ShapeDtypeStruct((d_ssm // SCAN_SLAB, t, l // t * b, SCAN_SLAB), BF16),
                 jax.ShapeDtypeStruct((b, l, d_ssm), BF16)]
    if emit_caches:
        n_heads = d_attn // V_DIM
        out_specs += [pl.BlockSpec((b, tp * n_heads, V_DIM), lambda j: (0, j, 0))] * 2
        out_shape += [jax.ShapeDtypeStruct((b, l * n_heads, V_DIM), F32)] * 2
    return pl.pallas_call(
        functools.partial(_inproj_kernel, rope=rope, emit_caches=emit_caches, d_attn=d_attn),
        grid=(l // tp,),
        in_specs=in_specs,
        out_specs=out_specs,
        out_shape=out_shape,
        compiler_params=_params("arbitrary"),
        name="inproj_rope" if rope else "inproj",
    )(*args)


def _rope_tables(l, d_attn):
    rows = l // GRID_W
    row = np.repeat(np.arange(rows, dtype=np.float64), GRID_W)
    col = np.tile(np.arange(GRID_W, dtype=np.float64), rows)
    n_freq = QK_DIM // 4
    inv = ROPE_THETA ** (-np.arange(n_freq, dtype=np.float64) / n_freq)
    ang_r = row[:, None] * inv
    ang_c = col[:, None] * inv
    cos64 = np.concatenate([np.cos(ang_r), np.cos(ang_r), np.cos(ang_c), np.cos(ang_c)], axis=-1)
    sin64 = np.concatenate([-np.sin(ang_r), np.sin(ang_r), -np.sin(ang_c), np.sin(ang_c)], axis=-1)
    reps = d_attn // QK_DIM
    return (jnp.asarray(np.tile(cos64, (1, reps)), dtype=F32),
            jnp.asarray(np.tile(sin64, (1, reps)), dtype=F32))


def _head(ref, bi, h, n_heads):
    if ref.shape[-1] == V_DIM:
        return ref[bi, pl.ds(h, ref.shape[1] // n_heads, stride=n_heads), :]
    return ref[bi, :, h * V_DIM:(h + 1) * V_DIM]


def _attn_kernel(*refs, has_cache, lam_init):
    if has_cache:
        lam_ref, q_ref, k_ref, v_ref, ck_ref, cv_ref, ga_ref, sub_ref, o_ref, k_scr, vt_scr = refs
    else:
        lam_ref, q_ref, k_ref, v_ref, ga_ref, sub_ref, o_ref, k_scr, vt_scr = refs
    nbk, tq = q_ref.shape[0], q_ref.shape[1]
    n_heads, lk = k_scr.shape[0] // nbk, k_scr.shape[1]
    ln = k_ref.shape[1]
    lam = lam_ref[0:1, 0:1]
    lane = lax.broadcasted_iota(jnp.int32, (1, V_DIM), 1)
    first = lane < QK_DIM
    dims = (((1,), (1,)), ((), ()))
    slot = lambda bi, h: bi * n_heads + h

    @pl.when(pl.program_id(1) == 0)
    def _():
        for bi in range(nbk):
            for h in range(n_heads):
                n = slot(bi, h)
                if has_cache:
                    k_scr[n, 0:lk - ln, :] = _head(ck_ref, bi, h, n_heads).astype(BF16)
                    vt_scr[n, 0:V_DIM, 0:lk - ln] = _head(cv_ref, bi, h, n_heads).astype(F32).T.astype(BF16)
                k_scr[n, lk - ln:lk, :] = _head(k_ref, bi, h, n_heads).astype(BF16)
                vt_scr[n, 0:V_DIM, lk - ln:lk] = _head(v_ref, bi, h, n_heads).astype(F32).T.astype(BF16)
                vt_scr[n, V_DIM:, :] = jnp.ones((vt_scr.shape[1] - V_DIM, lk), BF16)

    sub = min(tq, ATTN_Q_SUB)
    items = [(bi, h, r) for bi in range(nbk) for h in range(n_heads) for r in range(0, tq, sub)]

    def scores(item):
        bi, h, r = item
        q = _head(q_ref, bi, h, n_heads)[r:r + sub, :]
        zero = jnp.zeros_like(q)
        k = k_scr[slot(bi, h)]
        return (lax.dot_general(k, jnp.where(first, q, zero), dims, preferred_element_type=F32),
                lax.dot_general(k, jnp.where(first, zero, q), dims, preferred_element_type=F32))

    nxt = scores(items[0])
    for n, (bi, h, r) in enumerate(items):
        st1, st2 = nxt
        if n + 1 < len(items):
            nxt = scores(items[n + 1])
        vt = vt_scr[slot(bi, h)]
        m1 = jnp.max(st1, axis=0, keepdims=True)
        m2 = jnp.max(st2, axis=0, keepdims=True)
        o1 = o2 = None
        for lo in range(0, lk, ATTN_KEY_BLOCK):
            hi = min(lk, lo + ATTN_KEY_BLOCK)
            e1 = jnp.exp2(st1[lo:hi, :] - m1).astype(BF16)
            e2 = jnp.exp2(st2[lo:hi, :] - m2).astype(BF16)
            p1 = jnp.dot(vt[:, lo:hi], e1, preferred_element_type=F32)
            p2 = jnp.dot(vt[:, lo:hi], e2, preferred_element_type=F32)
            o1 = p1 if o1 is None else o1 + p1
            o2 = p2 if o2 is None else o2 + p2
        r1 = 1.0 / o1[V_DIM:V_DIM + 1, :]
        r2 = lam / o2[V_DIM:V_DIM + 1, :]
        o = (o1[0:V_DIM, :] * r1 - o2[0:V_DIM, :] * r2).T
        ms = jnp.mean(o * o, axis=-1, keepdims=True)
        o = o * lax.rsqrt(ms + EPS) * sub_ref[...] * (1.0 - lam_init)
        ga = _head(ga_ref, bi, h, n_heads)[r:r + sub, :].astype(F32)
        o_ref[bi, r:r + sub, h * V_DIM:(h + 1) * V_DIM] = (o * ga).astype(o_ref.dtype)


def _attention(lam, q, k, v, cache_k, cache_v, ga, subln, lam_init):
    b, l, d_attn = q.shape
    n_heads = d_attn // V_DIM
    tq = min(ATTN_Q_TILE, l)
    nq = l // tq
    nbk = max(1, ATTN_Q_TILE // l)
    has_cache = cache_k is not None
    q_spec = pl.BlockSpec((nbk, tq, d_attn), lambda i, j: (i, j, 0))
    whole = lambda a: pl.BlockSpec((nbk,) + a.shape[1:], lambda i, j: (i,) + (0,) * (a.ndim - 1))
    in_specs = [pl.BlockSpec(VREG_TILE, lambda i, j: (0, 0)), q_spec, whole(k), whole(v)]
    args = [lam, q, k, v]
    if has_cache:
        in_specs += [whole(cache_k), whole(cache_v)]
        args += [cache_k, cache_v]
    in_specs += [q_spec, pl.BlockSpec((1, V_DIM), lambda i, j: (0, 0))]
    args += [ga, subln.reshape(1, V_DIM).astype(F32)]
    lk = l + (cache_k.shape[1] // n_heads if has_cache else 0)
    out = pl.pallas_call(
        functools.partial(_attn_kernel, has_cache=has_cache, lam_init=lam_init),
        grid=(b // nbk, nq),
        in_specs=in_specs,
        out_specs=q_spec,
        out_shape=jax.ShapeDtypeStruct((b, l, d_attn), BF16),
        scratch_shapes=[pltpu.VMEM((nbk * n_heads, lk, V_DIM), BF16),
                        pltpu.VMEM((nbk * n_heads, V_DIM + SUM_ROWS, lk), BF16)],
        compiler_params=_params("arbitrary", "arbitrary"),
        name="attn_cache" if has_cache else "attn",
    )(*args)
    return out


def _ssm_kernel(*refs, batch, n_chunks, with_state):
    if with_state:
        u_ref, toept_ref, wt_ref, vt_ref, a_ref, h0_ref, y_ref, fin_ref, zt_scr, wc_scr, sin_scr, yt_scr = refs
    else:
        u_ref, toept_ref, wt_ref, vt_ref, a_ref, h0_ref, y_ref, zt_scr, wc_scr, sin_scr, yt_scr = refs
    t = u_ref.shape[0]
    gb = zt_scr.shape[0]
    hc = zt_scr.shape[1] // t
    sw = a_ref.shape[-1]
    hw = sw // 2
    for s in range(t):
        at = u_ref[s].T
        for g in range(gb):
            zt_scr[g, s * hc:(s + 1) * hc, :] = at[g * hc:(g + 1) * hc, :]
    for g in range(gb):
        wc_scr[g] = jnp.dot(wt_ref[g], zt_scr[g], preferred_element_type=F32).astype(BF16).T.astype(F32)
    fwd = lax.broadcasted_iota(jnp.int32, (1, sw), 1) < hw

    def step(j, carry):
        rf = pl.ds(pl.multiple_of(j * batch, batch), batch)
        rb = pl.ds(pl.multiple_of((n_chunks - 1 - j) * batch, batch), batch)
        out = []
        for g in range(gb):
            sr, si = carry[g]
            sin_scr[g, rf, 0:hw] = sr[:, 0:hw]
            sin_scr[g, rb, hw:sw] = sr[:, hw:sw]
            sin_scr[g, rf, sw:sw + hw] = si[:, 0:hw]
            sin_scr[g, rb, sw + hw:2 * sw] = si[:, hw:sw]
            wr = jnp.where(fwd, wc_scr[g, rf, 0:sw], wc_scr[g, rb, 0:sw])
            wi = jnp.where(fwd, wc_scr[g, rf, sw:2 * sw], wc_scr[g, rb, sw:2 * sw])
            ar = a_ref[g, 0]
            ai = a_ref[g, 1]
            out.append((ar * sr - ai * si + wr, ar * si + ai * sr + wi))
        return tuple(out)

    fin = lax.fori_loop(0, n_chunks, step, tuple((h0_ref[g, 0], h0_ref[g, 1]) for g in range(gb)))
    for g in range(gb):
        sint = sin_scr[g].astype(BF16).T
        y2t = (jnp.dot(toept_ref[g], zt_scr[g], preferred_element_type=F32)
               + jnp.dot(vt_ref[g], sint, preferred_element_type=F32))
        for s in range(t):
            yt_scr[s, g * hc:(g + 1) * hc, :] = y2t[s * hc:(s + 1) * hc, :].astype(yt_scr.dtype)
        if with_state:
            fin_ref[g, 0] = fin[g][0]
            fin_ref[g, 1] = fin[g][1]
    for s in range(t):
        y_ref[s] = yt_scr[s].T


def _ssm(u3, ssm_ops, h0, with_state):
    toept, wt, vt, a_t = ssm_ops
    n_slabs, t, rows, cw = u3.shape
    batch = h0.shape[2]
    n_chunks = rows // batch
    g = toept.shape[0]
    gb = g // n_slabs
    slab = pl.BlockSpec((None, t, rows, cw), lambda i: (i, 0, 0, 0))
    blk = lambda a: pl.BlockSpec((gb,) + a.shape[1:], lambda i: (i,) + (0,) * (a.ndim - 1))
    out_specs = [slab]
    out_shape = [jax.ShapeDtypeStruct(u3.shape, u3.dtype)]
    if with_state:
        out_specs.append(blk(h0))
        out_shape.append(jax.ShapeDtypeStruct(h0.shape, F32))
    nk = toept.shape[-1]
    outs = pl.pallas_call(
        functools.partial(_ssm_kernel, batch=batch, n_chunks=n_chunks, with_state=with_state),
        grid=(g // gb,),
        in_specs=[slab, blk(toept), blk(wt), blk(vt), blk(a_t), blk(h0)],
        out_specs=out_specs,
        out_shape=out_shape,
        scratch_shapes=[pltpu.VMEM((gb, nk, rows), BF16),
                        pltpu.VMEM((gb, rows, wt.shape[1]), F32),
                        pltpu.VMEM((gb, rows, wt.shape[1]), F32),
                        pltpu.VMEM((t, cw, rows), u3.dtype)],
        compiler_params=_params("arbitrary"),
        name="ssm_state" if with_state else "ssm",
    )(u3, toept, wt, vt, a_t, h0)
    return outs


def _state_to_lanes(h0):
    b, nd, nc, g, p = h0.shape
    return h0.astype(F32).transpose(3, 2, 0, 1, 4).reshape(g, nc, b, nd * p)


def _state_from_lanes(fin, nd):
    g, nc, b, w = fin.shape
    return fin.reshape(g, nc, b, nd, w // nd).transpose(2, 3, 1, 0, 4)


def _outproj_kernel(x_ref, a_ref, y_ref, u_ref, gs_ref, gate_ref, dskip_ref,
                    wglu_ref, bglu_ref, wout_ref, g_ref, o_ref):
    nb, tp, d = x_ref.shape
    y = _scan_block(y_ref) + dskip_ref[...] * _scan_block(u_ref)
    t, srows, d_ssm = y.shape
    rows = lambda ref: ref[...].reshape(nb * tp, ref.shape[-1])
    y = y.reshape(t * srows, d_ssm)
    c0 = math.sqrt(2.0 / math.pi)
    hy = 0.5 * y
    ys = hy + hy * jnp.tanh(y * (c0 + (c0 * 0.044715) * (y * y)))
    z = jnp.dot(ys.astype(BF16), wglu_ref[...], preferred_element_type=F32) + bglu_ref[...]
    ys = _from_scan_layout((ys * jax.nn.sigmoid(z)).reshape(t, srows, d_ssm), nb) * rows(gs_ref).astype(F32)
    cat = jnp.concatenate([rows(a_ref), ys.astype(BF16)], axis=-1)
    out = jnp.dot(cat, wout_ref[...], preferred_element_type=F32)
    ms = jnp.mean(out * out, axis=-1, keepdims=True)
    out = out * lax.rsqrt(ms + EPS)
    o_ref[...] = x_ref[...] + (g_ref[...] * gate_ref[...]) * out.reshape(nb, tp, d)


def _outproj(x, a_out, y_ssm, u, gs, gate, d_skip, w_glu_bf, b_glu, w_out_bf, norm_post):
    b, l, d = x.shape
    d_attn = a_out.shape[-1]
    d_ssm = gs.shape[-1]
    tp = TOKEN_TILE // b
    t = SSM_CHUNK
    const = lambda a: pl.BlockSpec(a.shape, lambda j: (0,) * a.ndim)
    tok = lambda w: pl.BlockSpec((b, tp, w), lambda j: (0, j, 0))
    scan = pl.BlockSpec((d_ssm // SCAN_SLAB, t, tp // t * b, SCAN_SLAB), lambda j: (0, 0, j, 0))
    consts = [gate, d_skip.reshape(1, d_ssm).astype(F32), w_glu_bf, b_glu.reshape(1, d_ssm).astype(F32),
              w_out_bf, norm_post.reshape(1, d).astype(F32)]
    return pl.pallas_call(
        _outproj_kernel,
        grid=(l // tp,),
        in_specs=[tok(d), tok(d_attn), scan, scan, tok(d_ssm)] + [const(a) for a in consts],
        out_specs=tok(d),
        out_shape=jax.ShapeDtypeStruct((b, l, d), F32),
        compiler_params=_params("arbitrary"),
        name="outproj",
    )(x, a_out, y_ssm, u, gs, *consts)


def _mixer(x, mod, h0_lanes, cache_k, cache_v, rope_tabs, emit_caches, with_state, lam, lam_init,
           norm_pre, w_in, ssm_ops, subln, d_skip, w_glu_bf, b_glu, w_out_bf, norm_post):
    b, l, d = x.shape
    d_ssm = w_glu_bf.shape[0]
    d_attn = (w_in.shape[1] - 2 * d_ssm) // 4
    shift, scale, gate = mod
    q, k, v, ga, u, gs, *caches = _inproj(x, shift, scale, norm_pre, w_in, rope_tabs, emit_caches,
                                          d_attn, d_ssm)
    a_out = _attention(lam, q, k, v, cache_k, cache_v, ga, subln, lam_init)
    outs = _ssm(u, ssm_ops, h0_lanes, with_state)
    y = _outproj(x, a_out, outs[0], u, gs, gate, d_skip, w_glu_bf, b_glu, w_out_bf, norm_post)
    fin = outs[1] if with_state else None
    return y, caches, fin


def kernel(x_prompt, x_sample, cache_k, cache_v, state_ssm, c, c_ctx, w_ada, b_ada, norm_pre, norm_post, w_in, lambda_qk, subln, ssm_A_re, ssm_A_im, ssm_log_dt, ssm_B_re, ssm_B_im, ssm_C_re, ssm_C_im, ssm_D, w_glu, b_glu, w_out):
    xp, xs = x_prompt, x_sample
    bp, lp, d = xp.shape
    bs, ls, _ = xs.shape
    depth = w_in.shape[0]
    nd = ssm_A_re.shape[1]
    g, p = ssm_A_re.shape[-2:]
    d_ssm = w_glu.shape[-1]
    d_attn = (w_in.shape[-1] - 2 * d_ssm) // 4
    assert bs + 1 <= COND_ROWS and nd == 2
    assert d_ssm % SCAN_SLAB == 0 and SCAN_SLAB % (d_ssm // g) == 0
    assert all(TOKEN_TILE % b == 0 and (TOKEN_TILE // b) % SSM_CHUNK == 0 and l % (TOKEN_TILE // b) == 0
               for b, l in ((bp, lp), (bs, ls)))
    assert all(l % min(ATTN_Q_TILE, l) == 0 and min(ATTN_Q_TILE, l) % min(ATTN_Q_SUB, l) == 0
               and b % max(1, ATTN_Q_TILE // l) == 0 for b, l in ((bp, lp), (bs, ls)))
    rope_tabs = _rope_tables(ls, d_attn)
    cond = jnp.zeros((COND_ROWS, d), F32).at[:bs].set(c.astype(F32)).at[bs].set(c_ctx.astype(F32))
    new_k, new_v, new_s = [], [], []
    for layer in range(depth):
        lam_init = 0.8 - 0.6 * math.exp(-0.3 * layer)
        m = _adaln(cond, w_ada[layer], b_ada[layer])
        mod_s = tuple(m[:bs, i * d:(i + 1) * d].reshape(bs, 1, d) for i in range(3))
        mod_p = tuple(m[bs:bs + 1, i * d:(i + 1) * d].reshape(1, 1, d) for i in range(3))
        ssm_ops, lam = _ssmprep(
            ssm_A_re[layer], ssm_A_im[layer], ssm_log_dt[layer], ssm_B_re[layer], ssm_B_im[layer],
            ssm_C_re[layer], ssm_C_im[layer], lambda_qk[layer], lam_init)
        shared = (lam, lam_init, norm_pre[layer], w_in[layer], ssm_ops, subln[layer],
                  ssm_D[layer], w_glu[layer].astype(BF16), b_glu[layer], w_out[layer].astype(BF16),
                  norm_post[layer])

        h0_p = jnp.zeros((g, 2, bp, nd * p), F32)
        xp, (k_p, v_p), fin = _mixer(xp, mod_p, h0_p, None, None, None, True, True, *shared)
        new_k.append(k_p.reshape(bp, lp, d_attn // V_DIM, V_DIM))
        new_v.append(v_p.reshape(bp, lp, d_attn // V_DIM, V_DIM))
        new_s.append(_state_from_lanes(fin, nd).astype(xp.dtype))

        h0_s = _state_to_lanes(state_ssm[:, layer])
        ck = cache_k[:, layer].reshape(bs, -1, V_DIM)
        cv = cache_v[:, layer].reshape(bs, -1, V_DIM)
        xs, _, _ = _mixer(xs, mod_s, h0_s, ck, cv, rope_tabs, False, False, *shared)

    return (xp, xs, jnp.stack(new_k, axis=1), jnp.stack(new_v, axis=1), jnp.stack(new_s, axis=1))
```

```python
import functools
import math

import jax
import jax.numpy as jnp
import numpy as np
from jax import lax
from jax.experimental import pallas as pl
from jax.experimental.pallas import tpu as pltpu

F32 = jnp.float32
BF16 = jnp.bfloat16

GRID_W = 64
QK_DIM = 64
V_DIM = 2 * QK_DIM
ROPE_THETA = 10000.0
EPS = 1e-6
Q_PRESCALE = math.log2(math.e) * QK_DIM ** -0.5

VREG_TILE = (8, 128)
SUM_ROWS = 16
COND_ROWS = 16
ADALN_K_BLOCK = 256
TOKEN_TILE = 1024
ATTN_Q_TILE = 1024
ATTN_Q_SUB = 512
ATTN_KEY_BLOCK = 256
SSM_CHUNK = 16
SCAN_SLAB = 128
VMEM_LIMIT = 48 * 1024 * 1024


def _silu(x):
    return x * jax.nn.sigmoid(x)


def _params(*sem):
    return pltpu.CompilerParams(dimension_semantics=sem, vmem_limit_bytes=VMEM_LIMIT)


def _adaln_kernel(c_ref, w_ref, b_ref, o_ref):
    @pl.when(pl.program_id(0) == 0)
    def _():
        o_ref[...] = jnp.broadcast_to(b_ref[...], o_ref.shape)

    s = _silu(c_ref[...]).astype(BF16)
    o_ref[...] += jnp.dot(s, w_ref[...].astype(BF16), preferred_element_type=F32)


def _adaln(cond, w_ada, b_ada):
    rows, d = cond.shape
    n = w_ada.shape[1]
    bk = ADALN_K_BLOCK
    return pl.pallas_call(
        _adaln_kernel,
        grid=(d // bk,),
        in_specs=[pl.BlockSpec((rows, bk), lambda j: (0, j)),
                  pl.BlockSpec((bk, n), lambda j: (j, 0)),
                  pl.BlockSpec((1, n), lambda j: (0, 0))],
        out_specs=pl.BlockSpec((rows, n), lambda j: (0, 0)),
        out_shape=jax.ShapeDtypeStruct((rows, n), F32),
        compiler_params=_params("arbitrary"),
        name="adaln",
    )(cond, w_ada, b_ada.reshape(1, n))


def _ssmprep_kernel(are_ref, aim_ref, ldt_ref, bre_ref, bim_ref, cre_ref, cim_ref, lq_ref,
                    toept_ref, wt_ref, vt_ref, a_ref, lam_ref,
                    cplre_scr, cplimn_scr, w_scr, v_scr, toep_scr, *, lam_init, chunk):
    a_re = are_ref[...]
    a_im = aim_ref[...]
    dt = jnp.exp(ldt_ref[...])
    mag = jnp.exp(a_re * dt)
    ab_re = mag * jnp.cos(a_im * dt)
    ab_im = mag * jnp.sin(a_im * dt)
    nr, ni = ab_re - 1.0, ab_im
    den = a_re * a_re + a_im * a_im
    f_re = (nr * a_re + ni * a_im) / den
    f_im = (ni * a_re - nr * a_im) / den
    b_re = bre_ref[...]
    b_im = bim_ref[...]
    bb_re = f_re * b_re - f_im * b_im
    bb_im = f_re * b_im + f_im * b_re
    c_re = cre_ref[...]
    c_im = cim_ref[...]
    hc = b_re.shape[1]
    sw = a_re.shape[-1]
    hw = sw // 2
    lo, hi = slice(0, hw), slice(hw, sw)
    lo_im, hi_im = slice(sw, sw + hw), slice(sw + hw, 2 * sw)
    blk = lambda j: slice(j * hc, (j + 1) * hc)
    last = chunk - 1
    cplre_scr[...] = jnp.zeros_like(cplre_scr)
    cplimn_scr[...] = jnp.zeros_like(cplimn_scr)
    pw_re = jnp.ones_like(ab_re)
    pw_im = jnp.zeros_like(ab_im)
    for tau in range(chunk + 1):
        cp_re = c_re * pw_re - c_im * pw_im
        cp_imn = -(c_re * pw_im + c_im * pw_re)
        if tau < chunk:
            cplre_scr[:, blk(last + tau), lo] = cp_re[:, :, lo]
            cplre_scr[:, blk(last - tau), hi] = cp_re[:, :, hi]
            cplimn_scr[:, blk(last + tau), lo] = cp_imn[:, :, lo]
            cplimn_scr[:, blk(last - tau), hi] = cp_imn[:, :, hi]
            e_re = pw_re * bb_re - pw_im * bb_im
            e_im = pw_re * bb_im + pw_im * bb_re
            w_scr[:, blk(last - tau), lo] = e_re[:, :, lo]
            w_scr[:, blk(tau), hi] = e_re[:, :, hi]
            w_scr[:, blk(last - tau), lo_im] = e_im[:, :, lo]
            w_scr[:, blk(tau), hi_im] = e_im[:, :, hi]
        if tau >= 1:
            v_scr[:, blk(tau - 1), lo] = cp_re[:, :, lo]
            v_scr[:, blk(chunk - tau), hi] = cp_re[:, :, hi]
            v_scr[:, blk(tau - 1), lo_im] = cp_imn[:, :, lo]
            v_scr[:, blk(chunk - tau), hi_im] = cp_imn[:, :, hi]
        if tau < chunk:
            pw_re, pw_im = pw_re * ab_re - pw_im * ab_im, pw_re * ab_im + pw_im * ab_re
    a_ref[:, 0] = pw_re
    a_ref[:, 1] = pw_im
    dims = (((2,), (2,)), ((0,), (0,)))
    taps = (lax.dot_general(bb_re.astype(BF16), cplre_scr[...].astype(BF16), dims,
                            preferred_element_type=F32)
            + lax.dot_general(bb_im.astype(BF16), cplimn_scr[...].astype(BF16), dims,
                              preferred_element_type=F32))
    nk = chunk * hc
    for s in range(chunk):
        toep_scr[:, blk(s), :] = taps[:, :, (last - s) * hc:(last - s) * hc + nk]
    for g in range(toept_ref.shape[0]):
        toept_ref[g] = toep_scr[g].astype(toept_ref.dtype).T
        wt_ref[g] = w_scr[g].astype(wt_ref.dtype).T
    vt_ref[...] = v_scr[...].astype(vt_ref.dtype)
    lq = lq_ref[...]
    s01 = jnp.sum(lq[0:1] * lq[1:2], axis=-1, keepdims=True)
    s23 = jnp.sum(lq[2:3] * lq[3:4], axis=-1, keepdims=True)
    lam = jnp.exp(s01) - jnp.exp(s23) + lam_init
    lam_ref[...] = jnp.broadcast_to(lam, lam_ref.shape)


def _ssmprep(a_re, a_im, log_dt, b_re, b_im, c_re, c_im, lq, lam_init):
    nd, g, p = a_re.shape
    hc = b_re.shape[-1]
    t = SSM_CHUNK
    sw = nd * p
    gb = SCAN_SLAB // hc
    row = lambda a: a.astype(F32).transpose(1, 0, 2).reshape(g, 1, sw)
    ldt = jnp.broadcast_to(log_dt[..., None], (nd, g, p))
    bt = lambda a: a.astype(F32).transpose(1, 3, 0, 2).reshape(g, hc, sw)
    ct = lambda a: a.astype(F32).transpose(1, 2, 0, 3).reshape(g, hc, sw)
    sds = jax.ShapeDtypeStruct
    nk = t * hc
    blk = lambda shape: pl.BlockSpec((gb,) + shape[1:], lambda i: (i,) + (0,) * (len(shape) - 1))
    mat = (g, nk, 2 * sw)
    toept, wt, vt, a_t, lam = pl.pallas_call(
        functools.partial(_ssmprep_kernel, lam_init=lam_init, chunk=t),
        grid=(g // gb,),
        in_specs=[blk((g, 1, sw))] * 3 + [blk((g, hc, sw))] * 4 + [pl.BlockSpec(lq.shape, lambda i: (0, 0))],
        out_specs=[blk((g, nk, nk)), blk((g, 2 * sw, nk)), blk(mat), blk((g, 2, 1, sw)),
                   pl.BlockSpec(VREG_TILE, lambda i: (0, 0))],
        out_shape=(sds((g, nk, nk), BF16), sds((g, 2 * sw, nk), BF16), sds(mat, BF16),
                   sds((g, 2, 1, sw), F32), sds(VREG_TILE, F32)),
        scratch_shapes=[pltpu.VMEM((gb, 2 * nk, sw), F32), pltpu.VMEM((gb, 2 * nk, sw), F32),
                        pltpu.VMEM((gb, nk, 2 * sw), F32), pltpu.VMEM((gb, nk, 2 * sw), F32),
                        pltpu.VMEM((gb, nk, nk), F32)],
        compiler_params=_params("arbitrary"),
        name="ssmprep",
    )(row(a_re), row(a_im), row(ldt), bt(b_re), bt(b_im), ct(c_re), ct(c_im), lq.astype(F32))
    return (toept, wt, vt, a_t), lam


def _to_scan_layout(u, o_ref, batch):
    n_slabs, t, _, cw = o_ref.shape
    tp = u.shape[0] // batch
    u_tb = jnp.swapaxes(u.reshape(batch, tp, u.shape[-1]), 0, 1)
    for c in range(tp // t):
        for s in range(t):
            for v in range(n_slabs):
                o_ref[v, s, c * batch:(c + 1) * batch, :] = u_tb[c * t + s][:, v * cw:(v + 1) * cw].astype(o_ref.dtype)


def _scan_block(ref):
    return jnp.concatenate([ref[v] for v in range(ref.shape[0])], axis=-1).astype(F32)


def _from_scan_layout(ref, batch):
    t, rows, d = ref.shape
    tp = rows // batch * t
    u_tb = jnp.stack([ref[s, c * batch:(c + 1) * batch, :] for c in range(tp // t) for s in range(t)], axis=0)
    return jnp.swapaxes(u_tb, 0, 1).reshape(batch * tp, d)


def _inproj_kernel(*refs, rope, emit_caches, d_attn):
    n_in = 7 if rope else 5
    x_ref, shift_ref, scale_ref, g_ref, w_ref = refs[:5]
    cos_ref, sin_ref = refs[5:7] if rope else (None, None)
    q_ref, k_ref, v_ref, ga_ref, u_ref, gs_ref = refs[n_in:n_in + 6]
    kc_ref, vc_ref = refs[n_in + 6:] if emit_caches else (None, None)
    x = x_ref[...]
    nb, tp, d = x.shape
    ms = jnp.mean(x * x, axis=-1, keepdims=True)
    h = x * lax.rsqrt(ms + EPS) * (g_ref[...] * (1.0 + scale_ref[...])) + shift_ref[...]
    hb = h.reshape(nb * tp, d).astype(BF16)
    proj = lambda lo, hi: jnp.dot(hb, w_ref[:, lo:hi].astype(BF16), preferred_element_type=F32)
    da = d_attn
    d_ssm = u_ref.shape[0] * u_ref.shape[-1]
    tok = lambda z: z.reshape(nb, tp, z.shape[-1])
    q = proj(0, da)
    k = proj(da, 2 * da)
    if rope:
        lane = lax.broadcasted_iota(jnp.int32, (1, da), 1)
        low = (lane % (QK_DIM // 2)) < (QK_DIM // 4)
        cos = cos_ref[...]
        sin = sin_ref[...]

        def rot(z):
            partner = jnp.where(low, pltpu.roll(z, da - QK_DIM // 4, 1), pltpu.roll(z, QK_DIM // 4, 1))
            return tok(z) * cos + tok(partner) * sin

        q = rot(q)
        k = rot(k)
    else:
        q = tok(q)
        k = tok(k)
    q_ref[...] = (q * Q_PRESCALE).astype(q_ref.dtype)
    k_ref[...] = k.astype(k_ref.dtype)
    n_heads = da // V_DIM
    if emit_caches:
        for hd in range(n_heads):
            kc_ref[:, pl.ds(hd, tp, stride=n_heads), :] = k[:, :, hd * V_DIM:(hd + 1) * V_DIM]
    v = tok(proj(2 * da, 3 * da))
    v_ref[...] = v.astype(v_ref.dtype)
    if emit_caches:
        for hd in range(n_heads):
            vc_ref[:, pl.ds(hd, tp, stride=n_heads), :] = v[:, :, hd * V_DIM:(hd + 1) * V_DIM]
    ga_ref[...] = tok(_silu(proj(3 * da, 4 * da))).astype(ga_ref.dtype)
    gs_ref[...] = tok(_silu(proj(4 * da + d_ssm, 4 * da + 2 * d_ssm))).astype(gs_ref.dtype)
    _to_scan_layout(proj(4 * da, 4 * da + d_ssm), u_ref, nb)


def _inproj(x, shift, scale, norm_pre, w_in, rope_tabs, emit_caches, d_attn, d_ssm):
    b, l, d = x.shape
    tp = TOKEN_TILE // b
    t = SSM_CHUNK
    rope = rope_tabs is not None
    const = lambda a: pl.BlockSpec(a.shape, lambda j: (0,) * a.ndim)
    tok = lambda w: pl.BlockSpec((b, tp, w), lambda j: (0, j, 0))
    g2 = norm_pre.reshape(1, d).astype(F32)
    w_spec = pl.BlockSpec(w_in.shape, lambda j: (0, 0), pipeline_mode=pl.Buffered(1))
    in_specs = [tok(d), const(shift), const(scale), const(g2), w_spec]
    args = [x, shift, scale, g2, w_in]
    if rope:
        in_specs += [pl.BlockSpec((tp, d_attn), lambda j: (j, 0))] * 2
        args += list(rope_tabs)
    scan_rows = tp // t * b
    attn_shape = jax.ShapeDtypeStruct((b, l, d_attn), BF16)
    out_specs = [tok(d_attn), tok(d_attn), tok(d_attn), tok(d_attn),
                 pl.BlockSpec((d_ssm // SCAN_SLAB, t, scan_rows, SCAN_SLAB), lambda j: (0, 0, j, 0)),
                 tok(d_ssm)]
    out_shape = [attn_shape, attn_shape, attn_shape, attn_shape,
                 jax.ShapeDtypeStruct((d_ssm // SCAN_SLAB, t, l // t * b, SCAN_SLAB), BF16),
                 jax.ShapeDtypeStruct((b, l, d_ssm), BF16)]
    if emit_caches:
        n_heads = d_attn // V_DIM
        out_specs += [pl.BlockSpec((b, tp * n_heads, V_DIM), lambda j: (0, j, 0))] * 2
        out_shape += [jax.ShapeDtypeStruct((b, l * n_heads, V_DIM), F32)] * 2
    return pl.pallas_call(
        functools.partial(_inproj_kernel, rope=rope, emit_caches=emit_caches, d_attn=d_attn),
        grid=(l // tp,),
        in_specs=in_specs,
        out_specs=out_specs,
        out_shape=out_shape,
        compiler_params=_params("arbitrary"),
        name="inproj_rope" if rope else "inproj",
    )(*args)


def _rope_tables(l, d_attn):
    rows = l // GRID_W
    row = np.repeat(np.arange(rows, dtype=np.float64), GRID_W)
    col = np.tile(np.arange(GRID_W, dtype=np.float64), rows)
    n_freq = QK_DIM // 4
    inv = ROPE_THETA ** (-np.arange(n_freq, dtype=np.float64) / n_freq)
    ang_r = row[:, None] * inv
    ang_c = col[:, None] * inv
    cos64 = np.concatenate([np.cos(ang_r), np.cos(ang_r), np.cos(ang_c), np.cos(ang_c)], axis=-1)
    sin64 = np.concatenate([-np.sin(ang_r), np.sin(ang_r), -np.sin(ang_c), np.sin(ang_c)], axis=-1)
    reps = d_attn // QK_DIM
    return (jnp.asarray(np.tile(cos64, (1, reps)), dtype=F32),
            jnp.asarray(np.tile(sin64, (1, reps)), dtype=F32))


def _head(ref, bi, h, n_heads):
    if ref.shape[-1] == V_DIM:
        return ref[bi, pl.ds(h, ref.shape[1] // n_heads, stride=n_heads), :]
    return ref[bi, :, h * V_DIM:(h + 1) * V_DIM]


def _attn_kernel(*refs, has_cache, lam_init):
    if has_cache:
        lam_ref, q_ref, k_ref, v_ref, ck_ref, cv_ref, ga_ref, sub_ref, o_ref, k_scr, vt_scr = refs
    else:
        lam_ref, q_ref, k_ref, v_ref, ga_ref, sub_ref, o_ref, k_scr, vt_scr = refs
    nbk, tq = q_ref.shape[0], q_ref.shape[1]
    n_heads, lk = k_scr.shape[0] // nbk, k_scr.shape[1]
    ln = k_ref.shape[1]
    lam = lam_ref[0:1, 0:1]
    lane = lax.broadcasted_iota(jnp.int32, (1, V_DIM), 1)
    first = lane < QK_DIM
    dims = (((1,), (1,)), ((), ()))
    slot = lambda bi, h: bi * n_heads + h

    @pl.when(pl.program_id(1) == 0)
    def _():
        for bi in range(nbk):
            for h in range(n_heads):
                n = slot(bi, h)
                if has_cache:
                    k_scr[n, 0:lk - ln, :] = _head(ck_ref, bi, h, n_heads).astype(BF16)
                    vt_scr[n, 0:V_DIM, 0:lk - ln] = _head(cv_ref, bi, h, n_heads).astype(F32).T.astype(BF16)
                k_scr[n, lk - ln:lk, :] = _head(k_ref, bi, h, n_heads).astype(BF16)
                vt_scr[n, 0:V_DIM, lk - ln:lk] = _head(v_ref, bi, h, n_heads).astype(F32).T.astype(BF16)
                vt_scr[n, V_DIM:, :] = jnp.ones((vt_scr.shape[1] - V_DIM, lk), BF16)

    sub = min(tq, ATTN_Q_SUB)
    items = [(bi, h, r) for bi in range(nbk) for h in range(n_heads) for r in range(0, tq, sub)]

    def scores(item):
        bi, h, r = item
        q = _head(q_ref, bi, h, n_heads)[r:r + sub, :]
        zero = jnp.zeros_like(q)
        k = k_scr[slot(bi, h)]
        return (lax.dot_general(k, jnp.where(first, q, zero), dims, preferred_element_type=F32),
                lax.dot_general(k, jnp.where(first, zero, q), dims, preferred_element_type=F32))

    nxt = scores(items[0])
    for n, (bi, h, r) in enumerate(items):
        st1, st2 = nxt
        if n + 1 < len(items):
            nxt = scores(items[n + 1])
        vt = vt_scr[slot(bi, h)]
        m1 = jnp.max(st1, axis=0, keepdims=True)
        m2 = jnp.max(st2, axis=0, keepdims=True)
        o1 = o2 = None
        for lo in range(0, lk, ATTN_KEY_BLOCK):
            hi = min(lk, lo + ATTN_KEY_BLOCK)
            e1 = jnp.exp2(st1[lo:hi, :] - m1).astype(BF16)
            e2 = jnp.exp2(st2[lo:hi, :] - m2).astype(BF16)
            p1 = jnp.dot(vt[:, lo:hi], e1, preferred_element_type=F32)
            p2 = jnp.dot(vt[:, lo:hi], e2, preferred_element_type=F32)
            o1 = p1 if o1 is None else o1 + p1
            o2 = p2 if o2 is None else o2 + p2
        r1 = 1.0 / o1[V_DIM:V_DIM + 1, :]
        r2 = lam / o2[V_DIM:V_DIM + 1, :]
        o = (o1[0:V_DIM, :] * r1 - o2[0:V_DIM, :] * r2).T
        ms = jnp.mean(o * o, axis=-1, keepdims=True)
        o = o * lax.rsqrt(ms + EPS) * sub_ref[...] * (1.0 - lam_init)
        ga = _head(ga_ref, bi, h, n_heads)[r:r + sub, :].astype(F32)
        o_ref[bi, r:r + sub, h * V_DIM:(h + 1) * V_DIM] = (o * ga).astype(o_ref.dtype)


def _attention(lam, q, k, v, cache_k, cache_v, ga, subln, lam_init):
    b, l, d_attn = q.shape
    n_heads = d_attn // V_DIM
    tq = min(ATTN_Q_TILE, l)
    nq = l // tq
    nbk = max(1, ATTN_Q_TILE // l)
    has_cache = cache_k is not None
    q_spec = pl.BlockSpec((nbk, tq, d_attn), lambda i, j: (i, j, 0))
    whole = lambda a: pl.BlockSpec((nbk,) + a.shape[1:], lambda i, j: (i,) + (0,) * (a.ndim - 1))
    in_specs = [pl.BlockSpec(VREG_TILE, lambda i, j: (0, 0)), q_spec, whole(k), whole(v)]
    args = [lam, q, k, v]
    if has_cache:
        in_specs += [whole(cache_k), whole(cache_v)]
        args += [cache_k, cache_v]
    in_specs += [q_spec, pl.BlockSpec((1, V_DIM), lambda i, j: (0, 0))]
    args += [ga, subln.reshape(1, V_DIM).astype(F32)]
    lk = l + (cache_k.shape[1] // n_heads if has_cache else 0)
    out = pl.pallas_call(
        functools.partial(_attn_kernel, has_cache=has_cache, lam_init=lam_init),
        grid=(b // nbk, nq),
        in_specs=in_specs,
        out_specs=q_spec,
        out_shape=jax.ShapeDtypeStruct((b, l, d_attn), BF16),
        scratch_shapes=[pltpu.VMEM((nbk * n_heads, lk, V_DIM), BF16),
                        pltpu.VMEM((nbk * n_heads, V_DIM + SUM_ROWS, lk), BF16)],
        compiler_params=_params("arbitrary", "arbitrary"),
        name="attn_cache" if has_cache else "attn",
    )(*args)
    return out


def _ssm_kernel(*refs, batch, n_chunks, with_state):
    if with_state:
        u_ref, toept_ref, wt_ref, vt_ref, a_ref, h0_ref, y_ref, fin_ref, zt_scr, wc_scr, sin_scr, yt_scr = refs
    else:
        u_ref, toept_ref, wt_ref, vt_ref, a_ref, h0_ref, y_ref, zt_scr, wc_scr, sin_scr, yt_scr = refs
    t = u_ref.shape[0]
    gb = zt_scr.shape[0]
    hc = zt_scr.shape[1] // t
    sw = a_ref.shape[-1]
    hw = sw // 2
    for s in range(t):
        at = u_ref[s].T
        for g in range(gb):
            zt_scr[g, s * hc:(s + 1) * hc, :] = at[g * hc:(g + 1) * hc, :]
    for g in range(gb):
        wc_scr[g] = jnp.dot(wt_ref[g], zt_scr[g], preferred_element_type=F32).astype(BF16).T.astype(F32)
    fwd = lax.broadcasted_iota(jnp.int32, (1, sw), 1) < hw

    def step(j, carry):
        rf = pl.ds(pl.multiple_of(j * batch, batch), batch)
        rb = pl.ds(pl.multiple_of((n_chunks - 1 - j) * batch, batch), batch)
        out = []
        for g in range(gb):
            sr, si = carry[g]
            sin_scr[g, rf, 0:hw] = sr[:, 0:hw]
            sin_scr[g, rb, hw:sw] = sr[:, hw:sw]
            sin_scr[g, rf, sw:sw + hw] = si[:, 0:hw]
            sin_scr[g, rb, sw + hw:2 * sw] = si[:, hw:sw]
            wr = jnp.where(fwd, wc_scr[g, rf, 0:sw], wc_scr[g, rb, 0:sw])
            wi = jnp.where(fwd, wc_scr[g, rf, sw:2 * sw], wc_scr[g, rb, sw:2 * sw])
            ar = a_ref[g, 0]
            ai = a_ref[g, 1]
            out.append((ar * sr - ai * si + wr, ar * si + ai * sr + wi))
        return tuple(out)

    fin = lax.fori_loop(0, n_chunks, step, tuple((h0_ref[g, 0], h0_ref[g, 1]) for g in range(gb)),
                        unroll=True)
    for g in range(gb):
        sint = sin_scr[g].astype(BF16).T
        y2t = (jnp.dot(toept_ref[g], zt_scr[g], preferred_element_type=F32)
               + jnp.dot(vt_ref[g], sint, preferred_element_type=F32))
        for s in range(t):
            yt_scr[s, g * hc:(g + 1) * hc, :] = y2t[s * hc:(s + 1) * hc, :].astype(yt_scr.dtype)
        if with_state:
            fin_ref[g, 0] = fin[g][0]
            fin_ref[g, 1] = fin[g][1]
    for s in range(t):
        y_ref[s] = yt_scr[s].T


def _ssm(u3, ssm_ops, h0, with_state):
    toept, wt, vt, a_t = ssm_ops
    n_slabs, t, rows, cw = u3.shape
    batch = h0.shape[2]
    n_chunks = rows // batch
    g = toept.shape[0]
    gb = g // n_slabs
    slab = pl.BlockSpec((None, t, rows, cw), lambda i: (i, 0, 0, 0))
    blk = lambda a: pl.BlockSpec((gb,) + a.shape[1:], lambda i: (i,) + (0,) * (a.ndim - 1))
    out_specs = [slab]
    out_shape = [jax.ShapeDtypeStruct(u3.shape, u3.dtype)]
    if with_state:
        out_specs.append(blk(h0))
        out_shape.append(jax.ShapeDtypeStruct(h0.shape, F32))
    nk = toept.shape[-1]
    outs = pl.pallas_call(
        functools.partial(_ssm_kernel, batch=batch, n_chunks=n_chunks, with_state=with_state),
        grid=(g // gb,),
        in_specs=[slab, blk(toept), blk(wt), blk(vt), blk(a_t), blk(h0)],
        out_specs=out_specs,
        out_shape=out_shape,
        scratch_shapes=[pltpu.VMEM((gb, nk, rows), BF16),
                        pltpu.VMEM((gb, rows, wt.shape[1]), F32),
                        pltpu.VMEM((gb, rows, wt.shape[1]), F32),
                        pltpu.VMEM((t, cw, rows), u3.dtype)],
        compiler_params=_params("arbitrary"),
        name="ssm_state" if with_state else "ssm",
    )(u3, toept, wt, vt, a_t, h0)
    return outs


def _state_to_lanes(h0):
    b, nd, nc, g, p = h0.shape
    return h0.astype(F32).transpose(3, 2, 0, 1, 4).reshape(g, nc, b, nd * p)


def _state_from_lanes(fin, nd):
    g, nc, b, w = fin.shape
    return fin.reshape(g, nc, b, nd, w // nd).transpose(2, 3, 1, 0, 4)


def _outproj_kernel(x_ref, a_ref, y_ref, u_ref, gs_ref, gate_ref, dskip_ref,
                    wglu_ref, bglu_ref, wout_ref, g_ref, o_ref):
    nb, tp, d = x_ref.shape
    y = _scan_block(y_ref) + dskip_ref[...] * _scan_block(u_ref)
    t, srows, d_ssm = y.shape
    rows = lambda ref: ref[...].reshape(nb * tp, ref.shape[-1])
    y = y.reshape(t * srows, d_ssm)
    c0 = math.sqrt(2.0 / math.pi)
    hy = 0.5 * y
    ys = hy + hy * jnp.tanh(y * (c0 + (c0 * 0.044715) * (y * y)))
    z = jnp.dot(ys.astype(BF16), wglu_ref[...], preferred_element_type=F32) + bglu_ref[...]
    ys = _from_scan_layout((ys * jax.nn.sigmoid(z)).reshape(t, srows, d_ssm), nb) * rows(gs_ref).astype(F32)
    cat = jnp.concatenate([rows(a_ref), ys.astype(BF16)], axis=-1)
    out = jnp.dot(cat, wout_ref[...], preferred_element_type=F32)
    ms = jnp.mean(out * out, axis=-1, keepdims=True)
    out = out * lax.rsqrt(ms + EPS)
    o_ref[...] = x_ref[...] + (g_ref[...] * gate_ref[...]) * out.reshape(nb, tp, d)


def _outproj(x, a_out, y_ssm, u, gs, gate, d_skip, w_glu_bf, b_glu, w_out_bf, norm_post):
    b, l, d = x.shape
    d_attn = a_out.shape[-1]
    d_ssm = gs.shape[-1]
    tp = TOKEN_TILE // b
    t = SSM_CHUNK
    const = lambda a: pl.BlockSpec(a.shape, lambda j: (0,) * a.ndim)
    tok = lambda w: pl.BlockSpec((b, tp, w), lambda j: (0, j, 0))
    scan = pl.BlockSpec((d_ssm // SCAN_SLAB, t, tp // t * b, SCAN_SLAB), lambda j: (0, 0, j, 0))
    consts = [gate, d_skip.reshape(1, d_ssm).astype(F32), w_glu_bf, b_glu.reshape(1, d_ssm).astype(F32),
              w_out_bf, norm_post.reshape(1, d).astype(F32)]
    return pl.pallas_call(
        _outproj_kernel,
        grid=(l // tp,),
        in_specs=[tok(d), tok(d_attn), scan, scan, tok(d_ssm)] + [const(a) for a in consts],
        out_specs=tok(d),
        out_shape=jax.ShapeDtypeStruct((b, l, d), F32),
        compiler_params=_params("arbitrary"),
        name="outproj",
    )(x, a_out, y_ssm, u, gs, *consts)


def _mixer(x, mod, h0_lanes, cache_k, cache_v, rope_tabs, emit_caches, with_state, lam, lam_init,
           norm_pre, w_in, ssm_ops, subln, d_skip, w_glu_bf, b_glu, w_out_bf, norm_post):
    b, l, d = x.shape
    d_ssm = w_glu_bf.shape[0]
    d_attn = (w_in.shape[1] - 2 * d_ssm) // 4
    shift, scale, gate = mod
    q, k, v, ga, u, gs, *caches = _inproj(x, shift, scale, norm_pre, w_in, rope_tabs, emit_caches,
                                          d_attn, d_ssm)
    a_out = _attention(lam, q, k, v, cache_k, cache_v, ga, subln, lam_init)
    outs = _ssm(u, ssm_ops, h0_lanes, with_state)
    y = _outproj(x, a_out, outs[0], u, gs, gate, d_skip, w_glu_bf, b_glu, w_out_bf, norm_post)
    fin = outs[1] if with_state else None
    return y, caches, fin


def kernel(x_prompt, x_sample, cache_k, cache_v, state_ssm, c, c_ctx, w_ada, b_ada, norm_pre, norm_post, w_in, lambda_qk, subln, ssm_A_re, ssm_A_im, ssm_log_dt, ssm_B_re, ssm_B_im, ssm_C_re, ssm_C_im, ssm_D, w_glu, b_glu, w_out):
    xp, xs = x_prompt, x_sample
    bp, lp, d = xp.shape
    bs, ls, _ = xs.shape
    depth = w_in.shape[0]
    nd = ssm_A_re.shape[1]
    g, p = ssm_A_re.shape[-2:]
    d_ssm = w_glu.shape[-1]
    d_attn = (w_in.shape[-1] - 2 * d_ssm) // 4
    assert bs + 1 <= COND_ROWS and nd == 2
    assert d_ssm % SCAN_SLAB == 0 and SCAN_SLAB % (d_ssm // g) == 0
    assert all(TOKEN_TILE % b == 0 and (TOKEN_TILE // b) % SSM_CHUNK == 0 and l % (TOKEN_TILE // b) == 0
               for b, l in ((bp, lp), (bs, ls)))
    assert all(l % min(ATTN_Q_TILE, l) == 0 and min(ATTN_Q_TILE, l) % min(ATTN_Q_SUB, l) == 0
               and b % max(1, ATTN_Q_TILE // l) == 0 for b, l in ((bp, lp), (bs, ls)))
    rope_tabs = _rope_tables(ls, d_attn)
    cond = jnp.zeros((COND_ROWS, d), F32).at[:bs].set(c.astype(F32)).at[bs].set(c_ctx.astype(F32))
    new_k, new_v, new_s = [], [], []
    for layer in range(depth):
        lam_init = 0.8 - 0.6 * math.exp(-0.3 * layer)
        m = _adaln(cond, w_ada[layer], b_ada[layer])
        mod_s = tuple(m[:bs, i * d:(i + 1) * d].reshape(bs, 1, d) for i in range(3))
        mod_p = tuple(m[bs:bs + 1, i * d:(i + 1) * d].reshape(1, 1, d) for i in range(3))
        ssm_ops, lam = _ssmprep(
            ssm_A_re[layer], ssm_A_im[layer], ssm_log_dt[layer], ssm_B_re[layer], ssm_B_im[layer],
            ssm_C_re[layer], ssm_C_im[layer], lambda_qk[layer], lam_init)
        shared = (lam, lam_init, norm_pre[layer], w_in[layer], ssm_ops, subln[layer],
                  ssm_D[layer], w_glu[layer].astype(BF16), b_glu[layer], w_out[layer].astype(BF16),
                  norm_post[layer])

        h0_p = jnp.zeros((g, 2, bp, nd * p), F32)
        xp, (k_p, v_p), fin = _mixer(xp, mod_p, h0_p, None, None, None, True, True, *shared)
        new_k.append(k_p.reshape(bp, lp, d_attn // V_DIM, V_DIM))
        new_v.append(v_p.reshape(bp, lp, d_attn // V_DIM, V_DIM))
        new_s.append(_state_from_lanes(fin, nd).astype(xp.dtype))

        h0_s = _state_to_lanes(state_ssm[:, layer])
        ck = cache_k[:, layer].reshape(bs, -1, V_DIM)
        cv = cache_v[:, layer].reshape(bs, -1, V_DIM)
        xs, _, _ = _mixer(xs, mod_s, h0_s, ck, cv, rope_tabs, False, False, *shared)

    return (xp, xs, jnp.stack(new_k, axis=1), jnp.stack(new_v, axis=1), jnp.stack(new_s, axis=1))
```

```python
import functools
import math

import jax
import jax.numpy as jnp
import numpy as np
from jax import lax
from jax.experimental import pallas as pl
from jax.experimental.pallas import tpu as pltpu

F32 = jnp.float32
BF16 = jnp.bfloat16

GRID_W = 64
QK_DIM = 64
V_DIM = 2 * QK_DIM
ROPE_THETA = 10000.0
EPS = 1e-6
Q_PRESCALE = math.log2(math.e) * QK_DIM ** -0.5

VREG_TILE = (8, 128)
SUM_ROWS = 16
COND_ROWS = 16
TOKEN_TILE = 1024
ATTN_Q_TILE = 1024
ATTN_Q_SUB = 512
ATTN_KEY_BLOCK = 256
SSM_CHUNK = 16
SCAN_SLAB = 128
VMEM_LIMIT = 48 * 1024 * 1024


def _silu(x):
    return x * jax.nn.sigmoid(x)


def _params(*sem):
    return pltpu.CompilerParams(dimension_semantics=sem, vmem_limit_bytes=VMEM_LIMIT)


def _adaln_kernel(c_ref, w_ref, b_ref, o_ref):
    @pl.when(pl.program_id(0) == 0)
    def _():
        o_ref[...] = jnp.broadcast_to(b_ref[...], o_ref.shape)

    s = _silu(c_ref[...]).astype(BF16)
    o_ref[...] += jnp.dot(s, w_ref[...].astype(BF16), preferred_element_type=F32)


def _ssmprep_kernel(are_ref, aim_ref, ldt_ref, bre_ref, bim_ref, cre_ref, cim_ref, lq_ref,
                    toept_ref, wt_ref, vt_ref, a_ref, lam_ref,
                    cplre_scr, cplimn_scr, w_scr, v_scr, toep_scr, *, lam_init, chunk):
    a_re = are_ref[...]
    a_im = aim_ref[...]
    dt = jnp.exp(ldt_ref[...])
    mag = jnp.exp(a_re * dt)
    ab_re = mag * jnp.cos(a_im * dt)
    ab_im = mag * jnp.sin(a_im * dt)
    nr, ni = ab_re - 1.0, ab_im
    den = a_re * a_re + a_im * a_im
    f_re = (nr * a_re + ni * a_im) / den
    f_im = (ni * a_re - nr * a_im) / den
    b_re = bre_ref[...]
    b_im = bim_ref[...]
    bb_re = f_re * b_re - f_im * b_im
    bb_im = f_re * b_im + f_im * b_re
    c_re = cre_ref[...]
    c_im = cim_ref[...]
    hc = b_re.shape[1]
    sw = a_re.shape[-1]
    hw = sw // 2
    lo, hi = slice(0, hw), slice(hw, sw)
    lo_im, hi_im = slice(sw, sw + hw), slice(sw + hw, 2 * sw)
    blk = lambda j: slice(j * hc, (j + 1) * hc)
    last = chunk - 1
    cplre_scr[...] = jnp.zeros_like(cplre_scr)
    cplimn_scr[...] = jnp.zeros_like(cplimn_scr)
    pw_re = jnp.ones_like(ab_re)
    pw_im = jnp.zeros_like(ab_im)
    for tau in range(chunk + 1):
        cp_re = c_re * pw_re - c_im * pw_im
        cp_imn = -(c_re * pw_im + c_im * pw_re)
        if tau < chunk:
            cplre_scr[:, blk(last + tau), lo] = cp_re[:, :, lo]
            cplre_scr[:, blk(last - tau), hi] = cp_re[:, :, hi]
            cplimn_scr[:, blk(last + tau), lo] = cp_imn[:, :, lo]
            cplimn_scr[:, blk(last - tau), hi] = cp_imn[:, :, hi]
            e_re = pw_re * bb_re - pw_im * bb_im
            e_im = pw_re * bb_im + pw_im * bb_re
            w_scr[:, blk(last - tau), lo] = e_re[:, :, lo]
            w_scr[:, blk(tau), hi] = e_re[:, :, hi]
            w_scr[:, blk(last - tau), lo_im] = e_im[:, :, lo]
            w_scr[:, blk(tau), hi_im] = e_im[:, :, hi]
        if tau >= 1:
            v_scr[:, blk(tau - 1), lo] = cp_re[:, :, lo]
            v_scr[:, blk(chunk - tau), hi] = cp_re[:, :, hi]
            v_scr[:, blk(tau - 1), lo_im] = cp_imn[:, :, lo]
            v_scr[:, blk(chunk - tau), hi_im] = cp_imn[:, :, hi]
        if tau < chunk:
            pw_re, pw_im = pw_re * ab_re - pw_im * ab_im, pw_re * ab_im + pw_im * ab_re
    a_ref[:, 0] = pw_re
    a_ref[:, 1] = pw_im
    dims = (((2,), (2,)), ((0,), (0,)))
    taps = (lax.dot_general(bb_re.astype(BF16), cplre_scr[...].astype(BF16), dims,
                            preferred_element_type=F32)
            + lax.dot_general(bb_im.astype(BF16), cplimn_scr[...].astype(BF16), dims,
                              preferred_element_type=F32))
    nk = chunk * hc
    for s in range(chunk):
        toep_scr[:, blk(s), :] = taps[:, :, (last - s) * hc:(last - s) * hc + nk]
    for g in range(toept_ref.shape[0]):
        toept_ref[g] = toep_scr[g].astype(toept_ref.dtype).T
        wt_ref[g] = w_scr[g].astype(wt_ref.dtype).T
    vt_ref[...] = v_scr[...].astype(vt_ref.dtype)
    lq = lq_ref[...]
    s01 = jnp.sum(lq[0:1] * lq[1:2], axis=-1, keepdims=True)
    s23 = jnp.sum(lq[2:3] * lq[3:4], axis=-1, keepdims=True)
    lam = jnp.exp(s01) - jnp.exp(s23) + lam_init
    lam_ref[...] = jnp.broadcast_to(lam, lam_ref.shape)


def _prep_kernel(*refs, lam_init, chunk):
    c_ref, wada_ref, bada_ref = refs[0:3]
    prep_in = refs[3:11]
    mod_ref = refs[11]
    prep_out = refs[12:17]
    scratch = refs[17:]
    _adaln_kernel(c_ref, wada_ref, bada_ref, mod_ref)
    _ssmprep_kernel(*prep_in, *prep_out, *scratch, lam_init=lam_init, chunk=chunk)


def _prep(cond, w_ada, b_ada, a_re, a_im, log_dt, b_re, b_im, c_re, c_im, lq, lam_init):
    rows, d = cond.shape
    n = w_ada.shape[1]
    nd, g, p = a_re.shape
    hc = b_re.shape[-1]
    t = SSM_CHUNK
    sw = nd * p
    gb = SCAN_SLAB // hc
    steps = g // gb
    bk = d // steps
    row = lambda a: a.astype(F32).transpose(1, 0, 2).reshape(g, 1, sw)
    ldt = jnp.broadcast_to(log_dt[..., None], (nd, g, p))
    bt = lambda a: a.astype(F32).transpose(1, 3, 0, 2).reshape(g, hc, sw)
    ct = lambda a: a.astype(F32).transpose(1, 2, 0, 3).reshape(g, hc, sw)
    sds = jax.ShapeDtypeStruct
    nk = t * hc
    blk = lambda shape: pl.BlockSpec((gb,) + shape[1:], lambda i: (i,) + (0,) * (len(shape) - 1))
    mat = (g, nk, 2 * sw)
    mod, toept, wt, vt, a_t, lam = pl.pallas_call(
        functools.partial(_prep_kernel, lam_init=lam_init, chunk=t),
        grid=(steps,),
        in_specs=[pl.BlockSpec((rows, bk), lambda i: (0, i)),
                  pl.BlockSpec((bk, n), lambda i: (i, 0)),
                  pl.BlockSpec((1, n), lambda i: (0, 0))]
                 + [blk((g, 1, sw))] * 3 + [blk((g, hc, sw))] * 4 + [pl.BlockSpec(lq.shape, lambda i: (0, 0))],
        out_specs=[pl.BlockSpec((rows, n), lambda i: (0, 0)),
                   blk((g, nk, nk)), blk((g, 2 * sw, nk)), blk(mat), blk((g, 2, 1, sw)),
                   pl.BlockSpec(VREG_TILE, lambda i: (0, 0))],
        out_shape=(sds((rows, n), F32),
                   sds((g, nk, nk), BF16), sds((g, 2 * sw, nk), BF16), sds(mat, BF16),
                   sds((g, 2, 1, sw), F32), sds(VREG_TILE, F32)),
        scratch_shapes=[pltpu.VMEM((gb, 2 * nk, sw), F32), pltpu.VMEM((gb, 2 * nk, sw), F32),
                        pltpu.VMEM((gb, nk, 2 * sw), F32), pltpu.VMEM((gb, nk, 2 * sw), F32),
                        pltpu.VMEM((gb, nk, nk), F32)],
        compiler_params=_params("arbitrary"),
        name="prep",
    )(cond, w_ada, b_ada.reshape(1, n),
      row(a_re), row(a_im), row(ldt), bt(b_re), bt(b_im), ct(c_re), ct(c_im), lq.astype(F32))
    return mod, (toept, wt, vt, a_t), lam


def _to_scan_layout(u, o_ref, batch):
    n_slabs, t, _, cw = o_ref.shape
    tp = u.shape[0] // batch
    u_tb = jnp.swapaxes(u.reshape(batch, tp, u.shape[-1]), 0, 1)
    for c in range(tp // t):
        for s in range(t):
            for v in range(n_slabs):
                o_ref[v, s, c * batch:(c + 1) * batch, :] = u_tb[c * t + s][:, v * cw:(v + 1) * cw].astype(o_ref.dtype)


def _scan_block(ref):
    return jnp.concatenate([ref[v] for v in range(ref.shape[0])], axis=-1).astype(F32)


def _from_scan_layout(ref, batch):
    t, rows, d = ref.shape
    tp = rows // batch * t
    u_tb = jnp.stack([ref[s, c * batch:(c + 1) * batch, :] for c in range(tp // t) for s in range(t)], axis=0)
    return jnp.swapaxes(u_tb, 0, 1).reshape(batch * tp, d)


def _inproj_kernel(*refs, rope, emit_caches, d_attn):
    n_in = 7 if rope else 5
    x_ref, shift_ref, scale_ref, g_ref, w_ref = refs[:5]
    cos_ref, sin_ref = refs[5:7] if rope else (None, None)
    q_ref, k_ref, v_ref, ga_ref, u_ref, gs_ref = refs[n_in:n_in + 6]
    kc_ref, vc_ref = refs[n_in + 6:] if emit_caches else (None, None)
    x = x_ref[...]
    nb, tp, d = x.shape
    ms = jnp.mean(x * x, axis=-1, keepdims=True)
    h = x * lax.rsqrt(ms + EPS) * (g_ref[...] * (1.0 + scale_ref[...])) + shift_ref[...]
    hb = h.reshape(nb * tp, d).astype(BF16)
    proj = lambda lo, hi: jnp.dot(hb, w_ref[:, lo:hi].astype(BF16), preferred_element_type=F32)
    da = d_attn
    d_ssm = u_ref.shape[0] * u_ref.shape[-1]
    tok = lambda z: z.reshape(nb, tp, z.shape[-1])
    q = proj(0, da)
    k = proj(da, 2 * da)
    if rope:
        lane = lax.broadcasted_iota(jnp.int32, (1, da), 1)
        low = (lane % (QK_DIM // 2)) < (QK_DIM // 4)
        cos = cos_ref[...]
        sin = sin_ref[...]

        def rot(z):
            partner = jnp.where(low, pltpu.roll(z, da - QK_DIM // 4, 1), pltpu.roll(z, QK_DIM // 4, 1))
            return tok(z) * cos + tok(partner) * sin

        q = rot(q)
        k = rot(k)
    else:
        q = tok(q)
        k = tok(k)
    q_ref[...] = (q * Q_PRESCALE).astype(q_ref.dtype)
    k_ref[...] = k.astype(k_ref.dtype)
    n_heads = da // V_DIM
    if emit_caches:
        for hd in range(n_heads):
            kc_ref[:, pl.ds(hd, tp, stride=n_heads), :] = k[:, :, hd * V_DIM:(hd + 1) * V_DIM]
    v = tok(proj(2 * da, 3 * da))
    v_ref[...] = v.astype(v_ref.dtype)
    if emit_caches:
        for hd in range(n_heads):
            vc_ref[:, pl.ds(hd, tp, stride=n_heads), :] = v[:, :, hd * V_DIM:(hd + 1) * V_DIM]
    ga_ref[...] = tok(_silu(proj(3 * da, 4 * da))).astype(ga_ref.dtype)
    gs_ref[...] = tok(_silu(proj(4 * da + d_ssm, 4 * da + 2 * d_ssm))).astype(gs_ref.dtype)
    _to_scan_layout(proj(4 * da, 4 * da + d_ssm), u_ref, nb)


def _inproj(x, shift, scale, norm_pre, w_in, rope_tabs, emit_caches, d_attn, d_ssm):
    b, l, d = x.shape
    tp = TOKEN_TILE // b
    t = SSM_CHUNK
    rope = rope_tabs is not None
    const = lambda a: pl.BlockSpec(a.shape, lambda j: (0,) * a.ndim)
    tok = lambda w: pl.BlockSpec((b, tp, w), lambda j: (0, j, 0))
    g2 = norm_pre.reshape(1, d).astype(F32)
    w_spec = pl.BlockSpec(w_in.shape, lambda j: (0, 0), pipeline_mode=pl.Buffered(1))
    in_specs = [tok(d), const(shift), const(scale), const(g2), w_spec]
    args = [x, shift, scale, g2, w_in]
    if rope:
        in_specs += [pl.BlockSpec((tp, d_attn), lambda j: (j, 0))] * 2
        args += list(rope_tabs)
    scan_rows = tp // t * b
    attn_shape = jax.ShapeDtypeStruct((b, l, d_attn), BF16)
    out_specs = [tok(d_attn), tok(d_attn), tok(d_attn), tok(d_attn),
                 pl.BlockSpec((d_ssm // SCAN_SLAB, t, scan_rows, SCAN_SLAB), lambda j: (0, 0, j, 0)),
                 tok(d_ssm)]
    out_shape = [attn_shape, attn_shape, attn_shape, attn_shape,
                 jax.ShapeDtypeStruct((d_ssm // SCAN_SLAB, t, l // t * b, SCAN_SLAB), BF16),
                 jax.ShapeDtypeStruct((b, l, d_ssm), BF16)]
    if emit_caches:
        n_heads = d_attn // V_DIM
        out_specs += [pl.BlockSpec((b, tp * n_heads, V_DIM), lambda j: (0, j, 0))] * 2
        out_shape += [jax.ShapeDtypeStruct((b, l * n_heads, V_DIM), F32)] * 2
    return pl.pallas_call(
        functools.partial(_inproj_kernel, rope=rope, emit_caches=emit_caches, d_attn=d_attn),
        grid=(l // tp,),
        in_specs=in_specs,
        out_specs=out_specs,
        out_shape=out_shape,
        compiler_params=_params("arbitrary"),
        name="inproj_rope" if rope else "inproj",
    )(*args)


def _rope_tables(l, d_attn):
    rows = l // GRID_W
    row = np.repeat(np.arange(rows, dtype=np.float64), GRID_W)
    col = np.tile(np.arange(GRID_W, dtype=np.float64), rows)
    n_freq = QK_DIM // 4
    inv = ROPE_THETA ** (-np.arange(n_freq, dtype=np.float64) / n_freq)
    ang_r = row[:, None] * inv
    ang_c = col[:, None] * inv
    cos64 = np.concatenate([np.cos(ang_r), np.cos(ang_r), np.cos(ang_c), np.cos(ang_c)], axis=-1)
    sin64 = np.concatenate([-np.sin(ang_r), np.sin(ang_r), -np.sin(ang_c), np.sin(ang_c)], axis=-1)
    reps = d_attn // QK_DIM
    return (jnp.asarray(np.tile(cos64, (1, reps)), dtype=F32),
            jnp.asarray(np.tile(sin64, (1, reps)), dtype=F32))


def _head(ref, bi, h, n_heads):
    if ref.shape[-1] == V_DIM:
        return ref[bi, pl.ds(h, ref.shape[1] // n_heads, stride=n_heads), :]
    return ref[bi, :, h * V_DIM:(h + 1) * V_DIM]


def _attn_kernel(*refs, has_cache, lam_init):
    if has_cache:
        lam_ref, q_ref, k_ref, v_ref, ck_ref, cv_ref, ga_ref, sub_ref, o_ref, k_scr, vt_scr = refs
    else:
        lam_ref, q_ref, k_ref, v_ref, ga_ref, sub_ref, o_ref, k_scr, vt_scr = refs
    nbk, tq = q_ref.shape[0], q_ref.shape[1]
    n_heads, lk = k_scr.shape[0] // nbk, k_scr.shape[1]
    ln = k_ref.shape[1]
    lam = lam_ref[0:1, 0:1]
    lane = lax.broadcasted_iota(jnp.int32, (1, V_DIM), 1)
    first = lane < QK_DIM
    dims = (((1,), (1,)), ((), ()))
    slot = lambda bi, h: bi * n_heads + h

    @pl.when(pl.program_id(1) == 0)
    def _():
        for bi in range(nbk):
            for h in range(n_heads):
                n = slot(bi, h)
                if has_cache:
                    k_scr[n, 0:lk - ln, :] = _head(ck_ref, bi, h, n_heads).astype(BF16)
                    vt_scr[n, 0:V_DIM, 0:lk - ln] = _head(cv_ref, bi, h, n_heads).astype(F32).T.astype(BF16)
                k_scr[n, lk - ln:lk, :] = _head(k_ref, bi, h, n_heads).astype(BF16)
                vt_scr[n, 0:V_DIM, lk - ln:lk] = _head(v_ref, bi, h, n_heads).astype(F32).T.astype(BF16)
                vt_scr[n, V_DIM:, :] = jnp.ones((vt_scr.shape[1] - V_DIM, lk), BF16)

    sub = min(tq, ATTN_Q_SUB)
    items = [(bi, h, r) for bi in range(nbk) for h in range(n_heads) for r in range(0, tq, sub)]

    def scores(item):
        bi, h, r = item
        q = _head(q_ref, bi, h, n_heads)[r:r + sub, :]
        zero = jnp.zeros_like(q)
        k = k_scr[slot(bi, h)]
        return (lax.dot_general(k, jnp.where(first, q, zero), dims, preferred_element_type=F32),
                lax.dot_general(k, jnp.where(first, zero, q), dims, preferred_element_type=F32))

    nxt = scores(items[0])
    for n, (bi, h, r) in enumerate(items):
        st1, st2 = nxt
        if n + 1 < len(items):
            nxt = scores(items[n + 1])
        vt = vt_scr[slot(bi, h)]
        m1 = jnp.max(st1, axis=0, keepdims=True)
        m2 = jnp.max(st2, axis=0, keepdims=True)
        o1 = o2 = None
        for lo in range(0, lk, ATTN_KEY_BLOCK):
            hi = min(lk, lo + ATTN_KEY_BLOCK)
            e1 = jnp.exp2(st1[lo:hi, :] - m1).astype(BF16)
            e2 = jnp.exp2(st2[lo:hi, :] - m2).astype(BF16)
            p1 = jnp.dot(vt[:, lo:hi], e1, preferred_element_type=F32)
            p2 = jnp.dot(vt[:, lo:hi], e2, preferred_element_type=F32)
            o1 = p1 if o1 is None else o1 + p1
            o2 = p2 if o2 is None else o2 + p2
        r1 = 1.0 / o1[V_DIM:V_DIM + 1, :]
        r2 = lam / o2[V_DIM:V_DIM + 1, :]
        o = (o1[0:V_DIM, :] * r1 - o2[0:V_DIM, :] * r2).T
        ms = jnp.mean(o * o, axis=-1, keepdims=True)
        o = o * lax.rsqrt(ms + EPS) * sub_ref[...] * (1.0 - lam_init)
        ga = _head(ga_ref, bi, h, n_heads)[r:r + sub, :].astype(F32)
        o_ref[bi, r:r + sub, h * V_DIM:(h + 1) * V_DIM] = (o * ga).astype(o_ref.dtype)


def _attention(lam, q, k, v, cache_k, cache_v, ga, subln, lam_init):
    b, l, d_attn = q.shape
    n_heads = d_attn // V_DIM
    tq = min(ATTN_Q_TILE, l)
    nq = l // tq
    nbk = max(1, ATTN_Q_TILE // l)
    has_cache = cache_k is not None
    q_spec = pl.BlockSpec((nbk, tq, d_attn), lambda i, j: (i, j, 0))
    whole = lambda a: pl.BlockSpec((nbk,) + a.shape[1:], lambda i, j: (i,) + (0,) * (a.ndim - 1))
    in_specs = [pl.BlockSpec(VREG_TILE, lambda i, j: (0, 0)), q_spec, whole(k), whole(v)]
    args = [lam, q, k, v]
    if has_cache:
        in_specs += [whole(cache_k), whole(cache_v)]
        args += [cache_k, cache_v]
    in_specs += [q_spec, pl.BlockSpec((1, V_DIM), lambda i, j: (0, 0))]
    args += [ga, subln.reshape(1, V_DIM).astype(F32)]
    lk = l + (cache_k.shape[1] // n_heads if has_cache else 0)
    out = pl.pallas_call(
        functools.partial(_attn_kernel, has_cache=has_cache, lam_init=lam_init),
        grid=(b // nbk, nq),
        in_specs=in_specs,
        out_specs=q_spec,
        out_shape=jax.ShapeDtypeStruct((b, l, d_attn), BF16),
        scratch_shapes=[pltpu.VMEM((nbk * n_heads, lk, V_DIM), BF16),
                        pltpu.VMEM((nbk * n_heads, V_DIM + SUM_ROWS, lk), BF16)],
        compiler_params=_params("arbitrary", "arbitrary"),
        name="attn_cache" if has_cache else "attn",
    )(*args)
    return out


def _ssm_kernel(*refs, batch, n_chunks, with_state):
    if with_state:
        u_ref, toept_ref, wt_ref, vt_ref, a_ref, h0_ref, y_ref, fin_ref, zt_scr, wc_scr, sin_scr, yt_scr = refs
    else:
        u_ref, toept_ref, wt_ref, vt_ref, a_ref, h0_ref, y_ref, zt_scr, wc_scr, sin_scr, yt_scr = refs
    t = u_ref.shape[0]
    gb = zt_scr.shape[0]
    hc = zt_scr.shape[1] // t
    sw = a_ref.shape[-1]
    hw = sw // 2
    for s in range(t):
        at = u_ref[s].T
        for g in range(gb):
            zt_scr[g, s * hc:(s + 1) * hc, :] = at[g * hc:(g + 1) * hc, :]
    for g in range(gb):
        wc_scr[g] = jnp.dot(wt_ref[g], zt_scr[g], preferred_element_type=F32).astype(BF16).T.astype(F32)
    fwd = lax.broadcasted_iota(jnp.int32, (1, sw), 1) < hw

    def step(j, carry):
        rf = pl.ds(pl.multiple_of(j * batch, batch), batch)
        rb = pl.ds(pl.multiple_of((n_chunks - 1 - j) * batch, batch), batch)
        out = []
        for g in range(gb):
            sr, si = carry[g]
            sin_scr[g, rf, 0:hw] = sr[:, 0:hw]
            sin_scr[g, rb, hw:sw] = sr[:, hw:sw]
            sin_scr[g, rf, sw:sw + hw] = si[:, 0:hw]
            sin_scr[g, rb, sw + hw:2 * sw] = si[:, hw:sw]
            wr = jnp.where(fwd, wc_scr[g, rf, 0:sw], wc_scr[g, rb, 0:sw])
            wi = jnp.where(fwd, wc_scr[g, rf, sw:2 * sw], wc_scr[g, rb, sw:2 * sw])
            ar = a_ref[g, 0]
            ai = a_ref[g, 1]
            out.append((ar * sr - ai * si + wr, ar * si + ai * sr + wi))
        return tuple(out)

    fin = lax.fori_loop(0, n_chunks, step, tuple((h0_ref[g, 0], h0_ref[g, 1]) for g in range(gb)),
                        unroll=True)
    for g in range(gb):
        sint = sin_scr[g].astype(BF16).T
        y2t = (jnp.dot(toept_ref[g], zt_scr[g], preferred_element_type=F32)
               + jnp.dot(vt_ref[g], sint, preferred_element_type=F32))
        for s in range(t):
            yt_scr[s, g * hc:(g + 1) * hc, :] = y2t[s * hc:(s + 1) * hc, :].astype(yt_scr.dtype)
        if with_state:
            fin_ref[g, 0] = fin[g][0]
            fin_ref[g, 1] = fin[g][1]
    for s in range(t):
        y_ref[s] = yt_scr[s].T


def _ssm(u3, ssm_ops, h0, with_state):
    toept, wt, vt, a_t = ssm_ops
    n_slabs, t, rows, cw = u3.shape
    batch = h0.shape[2]
    n_chunks = rows // batch
    g = toept.shape[0]
    gb = g // n_slabs
    slab = pl.BlockSpec((None, t, rows, cw), lambda i: (i, 0, 0, 0))
    blk = lambda a: pl.BlockSpec((gb,) + a.shape[1:], lambda i: (i,) + (0,) * (a.ndim - 1))
    out_specs = [slab]
    out_shape = [jax.ShapeDtypeStruct(u3.shape, u3.dtype)]
    if with_state:
        out_specs.append(blk(h0))
        out_shape.append(jax.ShapeDtypeStruct(h0.shape, F32))
    nk = toept.shape[-1]
    outs = pl.pallas_call(
        functools.partial(_ssm_kernel, batch=batch, n_chunks=n_chunks, with_state=with_state),
        grid=(g // gb,),
        in_specs=[slab, blk(toept), blk(wt), blk(vt), blk(a_t), blk(h0)],
        out_specs=out_specs,
        out_shape=out_shape,
        scratch_shapes=[pltpu.VMEM((gb, nk, rows), BF16),
                        pltpu.VMEM((gb, rows, wt.shape[1]), F32),
                        pltpu.VMEM((gb, rows, wt.shape[1]), F32),
                        pltpu.VMEM((t, cw, rows), u3.dtype)],
        compiler_params=_params("arbitrary"),
        name="ssm_state" if with_state else "ssm",
    )(u3, toept, wt, vt, a_t, h0)
    return outs


def _state_to_lanes(h0):
    b, nd, nc, g, p = h0.shape
    return h0.astype(F32).transpose(3, 2, 0, 1, 4).reshape(g, nc, b, nd * p)


def _state_from_lanes(fin, nd):
    g, nc, b, w = fin.shape
    return fin.reshape(g, nc, b, nd, w // nd).transpose(2, 3, 1, 0, 4)


def _outproj_kernel(x_ref, a_ref, y_ref, u_ref, gs_ref, gate_ref, dskip_ref,
                    wglu_ref, bglu_ref, wout_ref, g_ref, o_ref):
    nb, tp, d = x_ref.shape
    y = _scan_block(y_ref) + dskip_ref[...] * _scan_block(u_ref)
    t, srows, d_ssm = y.shape
    rows = lambda ref: ref[...].reshape(nb * tp, ref.shape[-1])
    y = y.reshape(t * srows, d_ssm)
    c0 = math.sqrt(2.0 / math.pi)
    hy = 0.5 * y
    ys = hy + hy * jnp.tanh(y * (c0 + (c0 * 0.044715) * (y * y)))
    z = jnp.dot(ys.astype(BF16), wglu_ref[...], preferred_element_type=F32) + bglu_ref[...]
    ys = _from_scan_layout((ys * jax.nn.sigmoid(z)).reshape(t, srows, d_ssm), nb) * rows(gs_ref).astype(F32)
    cat = jnp.concatenate([rows(a_ref), ys.astype(BF16)], axis=-1)
    out = jnp.dot(cat, wout_ref[...], preferred_element_type=F32)
    ms = jnp.mean(out * out, axis=-1, keepdims=True)
    out = out * lax.rsqrt(ms + EPS)
    o_ref[...] = x_ref[...] + (g_ref[...] * gate_ref[...]) * out.reshape(nb, tp, d)


def _outproj(x, a_out, y_ssm, u, gs, gate, d_skip, w_glu_bf, b_glu, w_out_bf, norm_post):
    b, l, d = x.shape
    d_attn = a_out.shape[-1]
    d_ssm = gs.shape[-1]
    tp = TOKEN_TILE // b
    t = SSM_CHUNK
    const = lambda a: pl.BlockSpec(a.shape, lambda j: (0,) * a.ndim)
    tok = lambda w: pl.BlockSpec((b, tp, w), lambda j: (0, j, 0))
    scan = pl.BlockSpec((d_ssm // SCAN_SLAB, t, tp // t * b, SCAN_SLAB), lambda j: (0, 0, j, 0))
    consts = [gate, d_skip.reshape(1, d_ssm).astype(F32), w_glu_bf, b_glu.reshape(1, d_ssm).astype(F32),
              w_out_bf, norm_post.reshape(1, d).astype(F32)]
    return pl.pallas_call(
        _outproj_kernel,
        grid=(l // tp,),
        in_specs=[tok(d), tok(d_attn), scan, scan, tok(d_ssm)] + [const(a) for a in consts],
        out_specs=tok(d),
        out_shape=jax.ShapeDtypeStruct((b, l, d), F32),
        compiler_params=_params("arbitrary"),
        name="outproj",
    )(x, a_out, y_ssm, u, gs, *consts)


def _mixer(x, mod, h0_lanes, cache_k, cache_v, rope_tabs, emit_caches, with_state, lam, lam_init,
           norm_pre, w_in, ssm_ops, subln, d_skip, w_glu_bf, b_glu, w_out_bf, norm_post):
    b, l, d = x.shape
    d_ssm = w_glu_bf.shape[0]
    d_attn = (w_in.shape[1] - 2 * d_ssm) // 4
    shift, scale, gate = mod
    q, k, v, ga, u, gs, *caches = _inproj(x, shift, scale, norm_pre, w_in, rope_tabs, emit_caches,
                                          d_attn, d_ssm)
    a_out = _attention(lam, q, k, v, cache_k, cache_v, ga, subln, lam_init)
    outs = _ssm(u, ssm_ops, h0_lanes, with_state)
    y = _outproj(x, a_out, outs[0], u, gs, gate, d_skip, w_glu_bf, b_glu, w_out_bf, norm_post)
    fin = outs[1] if with_state else None
    return y, caches, fin


def kernel(x_prompt, x_sample, cache_k, cache_v, state_ssm, c, c_ctx, w_ada, b_ada, norm_pre, norm_post, w_in, lambda_qk, subln, ssm_A_re, ssm_A_im, ssm_log_dt, ssm_B_re, ssm_B_im, ssm_C_re, ssm_C_im, ssm_D, w_glu, b_glu, w_out):
    xp, xs = x_prompt, x_sample
    bp, lp, d = xp.shape
    bs, ls, _ = xs.shape
    depth = w_in.shape[0]
    nd = ssm_A_re.shape[1]
    g, p = ssm_A_re.shape[-2:]
    d_ssm = w_glu.shape[-1]
    d_attn = (w_in.shape[-1] - 2 * d_ssm) // 4
    assert bs + 1 <= COND_ROWS and nd == 2
    assert d_ssm % SCAN_SLAB == 0 and SCAN_SLAB % (d_ssm // g) == 0
    assert all(TOKEN_TILE % b == 0 and (TOKEN_TILE // b) % SSM_CHUNK == 0 and l % (TOKEN_TILE // b) == 0
               for b, l in ((bp, lp), (bs, ls)))
    assert all(l % min(ATTN_Q_TILE, l) == 0 and min(ATTN_Q_TILE, l) % min(ATTN_Q_SUB, l) == 0
               and b % max(1, ATTN_Q_TILE // l) == 0 for b, l in ((bp, lp), (bs, ls)))
    rope_tabs = _rope_tables(ls, d_attn)
    cond = jnp.zeros((COND_ROWS, d), F32).at[:bs].set(c.astype(F32)).at[bs].set(c_ctx.astype(F32))
    new_k, new_v, new_s = [], [], []
    for layer in range(depth):
        lam_init = 0.8 - 0.6 * math.exp(-0.3 * layer)
        m, ssm_ops, lam = _prep(
            cond, w_ada[layer], b_ada[layer],
            ssm_A_re[layer], ssm_A_im[layer], ssm_log_dt[layer], ssm_B_re[layer], ssm_B_im[layer],
            ssm_C_re[layer], ssm_C_im[layer], lambda_qk[layer], lam_init)
        mod_s = tuple(m[:bs, i * d:(i + 1) * d].reshape(bs, 1, d) for i in range(3))
        mod_p = tuple(m[bs:bs + 1, i * d:(i + 1) * d].reshape(1, 1, d) for i in range(3))
        shared = (lam, lam_init, norm_pre[layer], w_in[layer], ssm_ops, subln[layer],
                  ssm_D[layer], w_glu[layer].astype(BF16), b_glu[layer], w_out[layer].astype(BF16),
                  norm_post[layer])

        h0_p = jnp.zeros((g, 2, bp, nd * p), F32)
        xp, (k_p, v_p), fin = _mixer(xp, mod_p, h0_p, None, None, None, True, True, *shared)
        new_k.append(k_p.reshape(bp, lp, d_attn // V_DIM, V_DIM))
        new_v.append(v_p.reshape(bp, lp, d_attn // V_DIM, V_DIM))
        new_s.append(_state_from_lanes(fin, nd).astype(xp.dtype))

        h0_s = _state_to_lanes(state_ssm[:, layer])
        ck = cache_k[:, layer].reshape(bs, -1, V_DIM)
        cv = cache_v[:, layer].reshape(bs, -1, V_DIM)
        xs, _, _ = _mixer(xs, mod_s, h0_s, ck, cv, rope_tabs, False, False, *shared)

    return (xp, xs, jnp.stack(new_k, axis=1), jnp.stack(new_v, axis=1), jnp.stack(new_s, axis=1))
```

```python
import functools
import math

import jax
import jax.numpy as jnp
import numpy as np
from jax import lax
from jax.experimental import pallas as pl
from jax.experimental.pallas import tpu as pltpu

F32 = jnp.float32
BF16 = jnp.bfloat16

GRID_W = 64
QK_DIM = 64
V_DIM = 2 * QK_DIM
ROPE_THETA = 10000.0
EPS = 1e-6
Q_PRESCALE = math.log2(math.e) * QK_DIM ** -0.5

VREG_TILE = (8, 128)
SUM_ROWS = 16
COND_ROWS = 16
TOKEN_TILE = 1024
ATTN_Q_TILE = 1024
ATTN_Q_SUB = 512
ATTN_KEY_BLOCK = 256
SSM_CHUNK = 16
SCAN_SLAB = 128
VMEM_LIMIT = 48 * 1024 * 1024


def _silu(x):
    return x * jax.nn.sigmoid(x)


def _params(*sem):
    return pltpu.CompilerParams(dimension_semantics=sem, vmem_limit_bytes=VMEM_LIMIT)


def _adaln_kernel(c_ref, w_ref, b_ref, o_ref):
    @pl.when(pl.program_id(0) == 0)
    def _():
        o_ref[...] = jnp.broadcast_to(b_ref[...], o_ref.shape)

    s = _silu(c_ref[...]).astype(BF16)
    o_ref[...] += jnp.dot(s, w_ref[...].astype(BF16), preferred_element_type=F32)


def _ssmprep_kernel(are_ref, aim_ref, ldt_ref, bre_ref, bim_ref, cre_ref, cim_ref, lq_ref,
                    toept_ref, wt_ref, vt_ref, a_ref, lam_ref,
                    cplre_scr, cplimn_scr, w_scr, v_scr, toep_scr, *, lam_init, chunk):
    a_re = are_ref[...]
    a_im = aim_ref[...]
    dt = jnp.exp(ldt_ref[...])
    mag = jnp.exp(a_re * dt)
    ab_re = mag * jnp.cos(a_im * dt)
    ab_im = mag * jnp.sin(a_im * dt)
    nr, ni = ab_re - 1.0, ab_im
    den = a_re * a_re + a_im * a_im
    f_re = (nr * a_re + ni * a_im) / den
    f_im = (ni * a_re - nr * a_im) / den
    b_re = bre_ref[...]
    b_im = bim_ref[...]
    bb_re = f_re * b_re - f_im * b_im
    bb_im = f_re * b_im + f_im * b_re
    c_re = cre_ref[...]
    c_im = cim_ref[...]
    hc = b_re.shape[1]
    sw = a_re.shape[-1]
    hw = sw // 2
    lo, hi = slice(0, hw), slice(hw, sw)
    lo_im, hi_im = slice(sw, sw + hw), slice(sw + hw, 2 * sw)
    blk = lambda j: slice(j * hc, (j + 1) * hc)
    last = chunk - 1
    cplre_scr[...] = jnp.zeros_like(cplre_scr)
    cplimn_scr[...] = jnp.zeros_like(cplimn_scr)
    pw_re = jnp.ones_like(ab_re)
    pw_im = jnp.zeros_like(ab_im)
    for tau in range(chunk + 1):
        cp_re = c_re * pw_re - c_im * pw_im
        cp_imn = -(c_re * pw_im + c_im * pw_re)
        if tau < chunk:
            cplre_scr[:, blk(last + tau), lo] = cp_re[:, :, lo]
            cplre_scr[:, blk(last - tau), hi] = cp_re[:, :, hi]
            cplimn_scr[:, blk(last + tau), lo] = cp_imn[:, :, lo]
            cplimn_scr[:, blk(last - tau), hi] = cp_imn[:, :, hi]
            e_re = pw_re * bb_re - pw_im * bb_im
            e_im = pw_re * bb_im + pw_im * bb_re
            w_scr[:, blk(last - tau), lo] = e_re[:, :, lo]
            w_scr[:, blk(tau), hi] = e_re[:, :, hi]
            w_scr[:, blk(last - tau), lo_im] = e_im[:, :, lo]
            w_scr[:, blk(tau), hi_im] = e_im[:, :, hi]
        if tau >= 1:
            v_scr[:, blk(tau - 1), lo] = cp_re[:, :, lo]
            v_scr[:, blk(chunk - tau), hi] = cp_re[:, :, hi]
            v_scr[:, blk(tau - 1), lo_im] = cp_imn[:, :, lo]
            v_scr[:, blk(chunk - tau), hi_im] = cp_imn[:, :, hi]
        if tau < chunk:
            pw_re, pw_im = pw_re * ab_re - pw_im * ab_im, pw_re * ab_im + pw_im * ab_re
    a_ref[:, 0] = pw_re
    a_ref[:, 1] = pw_im
    dims = (((2,), (2,)), ((0,), (0,)))
    taps = (lax.dot_general(bb_re.astype(BF16), cplre_scr[...].astype(BF16), dims,
                            preferred_element_type=F32)
            + lax.dot_general(bb_im.astype(BF16), cplimn_scr[...].astype(BF16), dims,
                              preferred_element_type=F32))
    nk = chunk * hc
    for s in range(chunk):
        toep_scr[:, blk(s), :] = taps[:, :, (last - s) * hc:(last - s) * hc + nk]
    for g in range(toept_ref.shape[0]):
        toept_ref[g] = toep_scr[g].astype(toept_ref.dtype).T
        wt_ref[g] = w_scr[g].astype(wt_ref.dtype).T
    vt_ref[...] = v_scr[...].astype(vt_ref.dtype)
    lq = lq_ref[...]
    s01 = jnp.sum(lq[0:1] * lq[1:2], axis=-1, keepdims=True)
    s23 = jnp.sum(lq[2:3] * lq[3:4], axis=-1, keepdims=True)
    lam = jnp.exp(s01) - jnp.exp(s23) + lam_init
    lam_ref[...] = jnp.broadcast_to(lam, lam_ref.shape)


def _prep_kernel(*refs, lam_init, chunk):
    c_ref, wada_ref, bada_ref = refs[0:3]
    prep_in = refs[3:11]
    mod_ref = refs[11]
    prep_out = refs[12:17]
    scratch = refs[17:]
    _adaln_kernel(c_ref, wada_ref, bada_ref, mod_ref)
    _ssmprep_kernel(*prep_in, *prep_out, *scratch, lam_init=lam_init, chunk=chunk)


def _prep(cond, w_ada, b_ada, a_re, a_im, log_dt, b_re, b_im, c_re, c_im, lq, lam_init):
    rows, d = cond.shape
    n = w_ada.shape[1]
    nd, g, p = a_re.shape
    hc = b_re.shape[-1]
    t = SSM_CHUNK
    sw = nd * p
    gb = SCAN_SLAB // hc
    steps = g // gb
    bk = d // steps
    row = lambda a: a.astype(F32).transpose(1, 0, 2).reshape(g, 1, sw)
    ldt = jnp.broadcast_to(log_dt[..., None], (nd, g, p))
    bt = lambda a: a.astype(F32).transpose(1, 3, 0, 2).reshape(g, hc, sw)
    ct = lambda a: a.astype(F32).transpose(1, 2, 0, 3).reshape(g, hc, sw)
    sds = jax.ShapeDtypeStruct
    nk = t * hc
    blk = lambda shape: pl.BlockSpec((gb,) + shape[1:], lambda i: (i,) + (0,) * (len(shape) - 1))
    mat = (g, nk, 2 * sw)
    mod, toept, wt, vt, a_t, lam = pl.pallas_call(
        functools.partial(_prep_kernel, lam_init=lam_init, chunk=t),
        grid=(steps,),
        in_specs=[pl.BlockSpec((rows, bk), lambda i: (0, i)),
                  pl.BlockSpec((bk, n), lambda i: (i, 0)),
                  pl.BlockSpec((1, n), lambda i: (0, 0))]
                 + [blk((g, 1, sw))] * 3 + [blk((g, hc, sw))] * 4 + [pl.BlockSpec(lq.shape, lambda i: (0, 0))],
        out_specs=[pl.BlockSpec((rows, n), lambda i: (0, 0)),
                   blk((g, nk, nk)), blk((g, 2 * sw, nk)), blk(mat), blk((g, 2, 1, sw)),
                   pl.BlockSpec(VREG_TILE, lambda i: (0, 0))],
        out_shape=(sds((rows, n), F32),
                   sds((g, nk, nk), BF16), sds((g, 2 * sw, nk), BF16), sds(mat, BF16),
                   sds((g, 2, 1, sw), F32), sds(VREG_TILE, F32)),
        scratch_shapes=[pltpu.VMEM((gb, 2 * nk, sw), F32), pltpu.VMEM((gb, 2 * nk, sw), F32),
                        pltpu.VMEM((gb, nk, 2 * sw), F32), pltpu.VMEM((gb, nk, 2 * sw), F32),
                        pltpu.VMEM((gb, nk, nk), F32)],
        compiler_params=_params("arbitrary"),
        name="prep",
    )(cond, w_ada, b_ada.reshape(1, n),
      row(a_re), row(a_im), row(ldt), bt(b_re), bt(b_im), ct(c_re), ct(c_im), lq.astype(F32))
    return mod, (toept, wt, vt, a_t), lam


def _to_scan_layout(u, o_ref, batch):
    n_slabs, t, _, cw = o_ref.shape
    tp = u.shape[0] // batch
    u_tb = jnp.swapaxes(u.reshape(batch, tp, u.shape[-1]), 0, 1)
    for c in range(tp // t):
        for s in range(t):
            for v in range(n_slabs):
                o_ref[v, s, c * batch:(c + 1) * batch, :] = u_tb[c * t + s][:, v * cw:(v + 1) * cw].astype(o_ref.dtype)


def _scan_block(ref):
    return jnp.concatenate([ref[v] for v in range(ref.shape[0])], axis=-1).astype(F32)


def _from_scan_layout(ref, batch):
    t, rows, d = ref.shape
    tp = rows // batch * t
    u_tb = jnp.stack([ref[s, c * batch:(c + 1) * batch, :] for c in range(tp // t) for s in range(t)], axis=0)
    return jnp.swapaxes(u_tb, 0, 1).reshape(batch * tp, d)


def _inproj_kernel(*refs, rope, emit_caches, d_attn):
    n_in = 7 if rope else 5
    x_ref, shift_ref, scale_ref, g_ref, w_ref = refs[:5]
    cos_ref, sin_ref = refs[5:7] if rope else (None, None)
    q_ref, k_ref, v_ref, ga_ref, u_ref, gs_ref = refs[n_in:n_in + 6]
    kc_ref, vc_ref = refs[n_in + 6:] if emit_caches else (None, None)
    x = x_ref[...]
    nb, tp, d = x.shape
    ms = jnp.mean(x * x, axis=-1, keepdims=True)
    h = x * lax.rsqrt(ms + EPS) * (g_ref[...] * (1.0 + scale_ref[...])) + shift_ref[...]
    hb = h.reshape(nb * tp, d).astype(BF16)
    proj = lambda lo, hi: jnp.dot(hb, w_ref[:, lo:hi].astype(BF16), preferred_element_type=F32)
    da = d_attn
    d_ssm = u_ref.shape[0] * u_ref.shape[-1]
    tok = lambda z: z.reshape(nb, tp, z.shape[-1])
    q = proj(0, da)
    k = proj(da, 2 * da)
    if rope:
        lane = lax.broadcasted_iota(jnp.int32, (1, da), 1)
        low = (lane % (QK_DIM // 2)) < (QK_DIM // 4)
        cos = cos_ref[...]
        sin = sin_ref[...]

        def rot(z):
            partner = jnp.where(low, pltpu.roll(z, da - QK_DIM // 4, 1), pltpu.roll(z, QK_DIM // 4, 1))
            return tok(z) * cos + tok(partner) * sin

        q = rot(q)
        k = rot(k)
    else:
        q = tok(q)
        k = tok(k)
    q_ref[...] = (q * Q_PRESCALE).astype(q_ref.dtype)
    k_ref[...] = k.astype(k_ref.dtype)
    n_heads = da // V_DIM
    if emit_caches:
        for hd in range(n_heads):
            kc_ref[:, pl.ds(hd, tp, stride=n_heads), :] = k[:, :, hd * V_DIM:(hd + 1) * V_DIM]
    v = tok(proj(2 * da, 3 * da))
    v_ref[...] = v.astype(v_ref.dtype)
    if emit_caches:
        for hd in range(n_heads):
            vc_ref[:, pl.ds(hd, tp, stride=n_heads), :] = v[:, :, hd * V_DIM:(hd + 1) * V_DIM]
    ga_ref[...] = tok(_silu(proj(3 * da, 4 * da))).astype(ga_ref.dtype)
    gs_ref[...] = tok(_silu(proj(4 * da + d_ssm, 4 * da + 2 * d_ssm))).astype(gs_ref.dtype)
    _to_scan_layout(proj(4 * da, 4 * da + d_ssm), u_ref, nb)


def _inproj(x, shift, scale, norm_pre, w_in, rope_tabs, emit_caches, d_attn, d_ssm):
    b, l, d = x.shape
    tp = TOKEN_TILE // b
    t = SSM_CHUNK
    rope = rope_tabs is not None
    const = lambda a: pl.BlockSpec(a.shape, lambda j: (0,) * a.ndim)
    tok = lambda w: pl.BlockSpec((b, tp, w), lambda j: (0, j, 0))
    g2 = norm_pre.reshape(1, d).astype(F32)
    w_spec = pl.BlockSpec(w_in.shape, lambda j: (0, 0), pipeline_mode=pl.Buffered(1))
    in_specs = [tok(d), const(shift), const(scale), const(g2), w_spec]
    args = [x, shift, scale, g2, w_in]
    if rope:
        in_specs += [pl.BlockSpec((tp, d_attn), lambda j: (j, 0))] * 2
        args += list(rope_tabs)
    scan_rows = tp // t * b
    attn_shape = jax.ShapeDtypeStruct((b, l, d_attn), BF16)
    out_specs = [tok(d_attn), tok(d_attn), tok(d_attn), tok(d_attn),
                 pl.BlockSpec((d_ssm // SCAN_SLAB, t, scan_rows, SCAN_SLAB), lambda j: (0, 0, j, 0)),
                 tok(d_ssm)]
    out_shape = [attn_shape, attn_shape, attn_shape, attn_shape,
                 jax.ShapeDtypeStruct((d_ssm // SCAN_SLAB, t, l // t * b, SCAN_SLAB), BF16),
                 jax.ShapeDtypeStruct((b, l, d_ssm), BF16)]
    if emit_caches:
        n_heads = d_attn // V_DIM
        out_specs += [pl.BlockSpec((b, tp * n_heads, V_DIM), lambda j: (0, j, 0))] * 2
        out_shape += [jax.ShapeDtypeStruct((b, l * n_heads, V_DIM), F32)] * 2
    return pl.pallas_call(
        functools.partial(_inproj_kernel, rope=rope, emit_caches=emit_caches, d_attn=d_attn),
        grid=(l // tp,),
        in_specs=in_specs,
        out_specs=out_specs,
        out_shape=out_shape,
        compiler_params=_params("arbitrary"),
        name="inproj_rope" if rope else "inproj",
    )(*args)


def _rope_tables(l, d_attn):
    rows = l // GRID_W
    row = np.repeat(np.arange(rows, dtype=np.float64), GRID_W)
    col = np.tile(np.arange(GRID_W, dtype=np.float64), rows)
    n_freq = QK_DIM // 4
    inv = ROPE_THETA ** (-np.arange(n_freq, dtype=np.float64) / n_freq)
    ang_r = row[:, None] * inv
    ang_c = col[:, None] * inv
    cos64 = np.concatenate([np.cos(ang_r), np.cos(ang_r), np.cos(ang_c), np.cos(ang_c)], axis=-1)
    sin64 = np.concatenate([-np.sin(ang_r), np.sin(ang_r), -np.sin(ang_c), np.sin(ang_c)], axis=-1)
    reps = d_attn // QK_DIM
    return (jnp.asarray(np.tile(cos64, (1, reps)), dtype=F32),
            jnp.asarray(np.tile(sin64, (1, reps)), dtype=F32))


def _head(ref, bi, h, n_heads):
    if ref.shape[-1] == V_DIM:
        return ref[bi, pl.ds(h, ref.shape[1] // n_heads, stride=n_heads), :]
    return ref[bi, :, h * V_DIM:(h + 1) * V_DIM]


def _attn_kernel(*refs, has_cache, lam_init):
    if has_cache:
        lam_ref, q_ref, k_ref, v_ref, ck_ref, cv_ref, ga_ref, sub_ref, o_ref, k_scr, vt_scr = refs
    else:
        lam_ref, q_ref, k_ref, v_ref, ga_ref, sub_ref, o_ref, k_scr, vt_scr = refs
    nbk, tq = q_ref.shape[0], q_ref.shape[1]
    n_heads, lk = k_scr.shape[0] // nbk, k_scr.shape[1]
    ln = k_ref.shape[1]
    lam = lam_ref[0:1, 0:1]
    lane = lax.broadcasted_iota(jnp.int32, (1, V_DIM), 1)
    first = lane < QK_DIM
    dims = (((1,), (1,)), ((), ()))
    slot = lambda bi, h: bi * n_heads + h

    @pl.when(pl.program_id(1) == 0)
    def _():
        for bi in range(nbk):
            for h in range(n_heads):
                n = slot(bi, h)
                if has_cache:
                    k_scr[n, 0:lk - ln, :] = _head(ck_ref, bi, h, n_heads).astype(BF16)
                    vt_scr[n, 0:V_DIM, 0:lk - ln] = _head(cv_ref, bi, h, n_heads).astype(F32).T.astype(BF16)
                k_scr[n, lk - ln:lk, :] = _head(k_ref, bi, h, n_heads).astype(BF16)
                vt_scr[n, 0:V_DIM, lk - ln:lk] = _head(v_ref, bi, h, n_heads).astype(F32).T.astype(BF16)
                vt_scr[n, V_DIM:, :] = jnp.ones((vt_scr.shape[1] - V_DIM, lk), BF16)

    sub = min(tq, ATTN_Q_SUB)
    items = [(bi, h, r) for bi in range(nbk) for h in range(n_heads) for r in range(0, tq, sub)]

    def scores(item):
        bi, h, r = item
        q = _head(q_ref, bi, h, n_heads)[r:r + sub, :]
        zero = jnp.zeros_like(q)
        k = k_scr[slot(bi, h)]
        return (lax.dot_general(k, jnp.where(first, q, zero), dims, preferred_element_type=F32),
                lax.dot_general(k, jnp.where(first, zero, q), dims, preferred_element_type=F32))

    nxt = scores(items[0])
    for n, (bi, h, r) in enumerate(items):
        st1, st2 = nxt
        if n + 1 < len(items):
            nxt = scores(items[n + 1])
        vt = vt_scr[slot(bi, h)]
        m1 = jnp.max(st1, axis=0, keepdims=True)
        m2 = jnp.max(st2, axis=0, keepdims=True)
        o1 = o2 = None
        for lo in range(0, lk, ATTN_KEY_BLOCK):
            hi = min(lk, lo + ATTN_KEY_BLOCK)
            e1 = jnp.exp2(st1[lo:hi, :] - m1).astype(BF16)
            e2 = jnp.exp2(st2[lo:hi, :] - m2).astype(BF16)
            p1 = jnp.dot(vt[:, lo:hi], e1, preferred_element_type=F32)
            p2 = jnp.dot(vt[:, lo:hi], e2, preferred_element_type=F32)
            o1 = p1 if o1 is None else o1 + p1
            o2 = p2 if o2 is None else o2 + p2
        r1 = 1.0 / o1[V_DIM:V_DIM + 1, :]
        r2 = lam / o2[V_DIM:V_DIM + 1, :]
        o = (o1[0:V_DIM, :] * r1 - o2[0:V_DIM, :] * r2).T
        ms = jnp.mean(o * o, axis=-1, keepdims=True)
        o = o * lax.rsqrt(ms + EPS) * sub_ref[...] * (1.0 - lam_init)
        ga = _head(ga_ref, bi, h, n_heads)[r:r + sub, :].astype(F32)
        o_ref[bi, r:r + sub, h * V_DIM:(h + 1) * V_DIM] = (o * ga).astype(o_ref.dtype)


def _attention(lam, q, k, v, cache_k, cache_v, ga, subln, lam_init):
    b, l, d_attn = q.shape
    n_heads = d_attn // V_DIM
    tq = min(ATTN_Q_TILE, l)
    nq = l // tq
    nbk = max(1, ATTN_Q_TILE // l)
    has_cache = cache_k is not None
    q_spec = pl.BlockSpec((nbk, tq, d_attn), lambda i, j: (i, j, 0))
    whole = lambda a: pl.BlockSpec((nbk,) + a.shape[1:], lambda i, j: (i,) + (0,) * (a.ndim - 1))
    in_specs = [pl.BlockSpec(VREG_TILE, lambda i, j: (0, 0)), q_spec, whole(k), whole(v)]
    args = [lam, q, k, v]
    if has_cache:
        in_specs += [whole(cache_k), whole(cache_v)]
        args += [cache_k, cache_v]
    in_specs += [q_spec, pl.BlockSpec((1, V_DIM), lambda i, j: (0, 0))]
    args += [ga, subln.reshape(1, V_DIM).astype(F32)]
    lk = l + (cache_k.shape[1] // n_heads if has_cache else 0)
    return dict(
        kernel=functools.partial(_attn_kernel, has_cache=has_cache, lam_init=lam_init),
        grid=(b // nbk, nq), in_specs=in_specs, args=args,
        out_specs=[q_spec], out_shape=[jax.ShapeDtypeStruct((b, l, d_attn), BF16)],
        scratch=[pltpu.VMEM((nbk * n_heads, lk, V_DIM), BF16),
                 pltpu.VMEM((nbk * n_heads, V_DIM + SUM_ROWS, lk), BF16)],
        name="attn_cache" if has_cache else "attn")


def _launch(*parts):
    grid = parts[0]["grid"]
    assert all(p["grid"] == grid for p in parts)
    n_in = [len(p["in_specs"]) for p in parts]
    n_out = [len(p["out_specs"]) for p in parts]
    n_scr = [len(p["scratch"]) for p in parts]

    def body(*refs):
        ins, outs, scrs = refs[:sum(n_in)], refs[sum(n_in):sum(n_in) + sum(n_out)], refs[sum(n_in) + sum(n_out):]
        for n, p in enumerate(parts):
            take = lambda seq, counts: seq[sum(counts[:n]):sum(counts[:n + 1])]
            p["kernel"](*take(ins, n_in), *take(outs, n_out), *take(scrs, n_scr))

    outs = pl.pallas_call(
        body,
        grid=grid,
        in_specs=[spec for p in parts for spec in p["in_specs"]],
        out_specs=[spec for p in parts for spec in p["out_specs"]],
        out_shape=[shape for p in parts for shape in p["out_shape"]],
        scratch_shapes=[scr for p in parts for scr in p["scratch"]],
        compiler_params=_params(*(["arbitrary"] * len(grid))),
        name="_".join(p["name"] for p in parts),
    )(*[a for p in parts for a in p["args"]])
    return [outs[sum(n_out[:n]):sum(n_out[:n + 1])] for n in range(len(parts))]


def _ssm_kernel(*refs, batch, n_chunks, with_state):
    if with_state:
        u_ref, toept_ref, wt_ref, vt_ref, a_ref, h0_ref, y_ref, fin_ref, zt_scr, wc_scr, sin_scr, yt_scr = refs
    else:
        u_ref, toept_ref, wt_ref, vt_ref, a_ref, h0_ref, y_ref, zt_scr, wc_scr, sin_scr, yt_scr = refs
    t = u_ref.shape[0]
    gb = zt_scr.shape[0]
    hc = zt_scr.shape[1] // t
    sw = a_ref.shape[-1]
    hw = sw // 2
    for s in range(t):
        at = u_ref[s].T
        for g in range(gb):
            zt_scr[g, s * hc:(s + 1) * hc, :] = at[g * hc:(g + 1) * hc, :]
    for g in range(gb):
        wc_scr[g] = jnp.dot(wt_ref[g], zt_scr[g], preferred_element_type=F32).astype(BF16).T.astype(F32)
    fwd = lax.broadcasted_iota(jnp.int32, (1, sw), 1) < hw

    def step(j, carry):
        rf = pl.ds(pl.multiple_of(j * batch, batch), batch)
        rb = pl.ds(pl.multiple_of((n_chunks - 1 - j) * batch, batch), batch)
        out = []
        for g in range(gb):
            sr, si = carry[g]
            sin_scr[g, rf, 0:hw] = sr[:, 0:hw]
            sin_scr[g, rb, hw:sw] = sr[:, hw:sw]
            sin_scr[g, rf, sw:sw + hw] = si[:, 0:hw]
            sin_scr[g, rb, sw + hw:2 * sw] = si[:, hw:sw]
            wr = jnp.where(fwd, wc_scr[g, rf, 0:sw], wc_scr[g, rb, 0:sw])
            wi = jnp.where(fwd, wc_scr[g, rf, sw:2 * sw], wc_scr[g, rb, sw:2 * sw])
            ar = a_ref[g, 0]
            ai = a_ref[g, 1]
            out.append((ar * sr - ai * si + wr, ar * si + ai * sr + wi))
        return tuple(out)

    fin = lax.fori_loop(0, n_chunks, step, tuple((h0_ref[g, 0], h0_ref[g, 1]) for g in range(gb)),
                        unroll=True)
    for g in range(gb):
        sint = sin_scr[g].astype(BF16).T
        y2t = (jnp.dot(toept_ref[g], zt_scr[g], preferred_element_type=F32)
               + jnp.dot(vt_ref[g], sint, preferred_element_type=F32))
        for s in range(t):
            yt_scr[s, g * hc:(g + 1) * hc, :] = y2t[s * hc:(s + 1) * hc, :].astype(yt_scr.dtype)
        if with_state:
            fin_ref[g, 0] = fin[g][0]
            fin_ref[g, 1] = fin[g][1]
    for s in range(t):
        y_ref[s] = yt_scr[s].T


def _ssm(u3, ssm_ops, h0, with_state):
    toept, wt, vt, a_t = ssm_ops
    n_slabs, t, rows, cw = u3.shape
    batch = h0.shape[2]
    n_chunks = rows // batch
    g = toept.shape[0]
    gb = g // n_slabs
    slab = pl.BlockSpec((None, t, rows, cw), lambda i, j: (i, 0, 0, 0))
    blk = lambda a: pl.BlockSpec((gb,) + a.shape[1:], lambda i, j: (i,) + (0,) * (a.ndim - 1))
    out_specs = [slab]
    out_shape = [jax.ShapeDtypeStruct(u3.shape, u3.dtype)]
    if with_state:
        out_specs.append(blk(h0))
        out_shape.append(jax.ShapeDtypeStruct(h0.shape, F32))
    nk = toept.shape[-1]
    return dict(
        kernel=functools.partial(_ssm_kernel, batch=batch, n_chunks=n_chunks, with_state=with_state),
        grid=(g // gb, 1), in_specs=[slab, blk(toept), blk(wt), blk(vt), blk(a_t), blk(h0)],
        args=[u3, toept, wt, vt, a_t, h0], out_specs=out_specs, out_shape=out_shape,
        scratch=[pltpu.VMEM((gb, nk, rows), BF16),
                 pltpu.VMEM((gb, rows, wt.shape[1]), F32),
                 pltpu.VMEM((gb, rows, wt.shape[1]), F32),
                 pltpu.VMEM((t, cw, rows), u3.dtype)],
        name="ssm_state" if with_state else "ssm")


def _state_to_lanes(h0):
    b, nd, nc, g, p = h0.shape
    return h0.astype(F32).transpose(3, 2, 0, 1, 4).reshape(g, nc, b, nd * p)


def _state_from_lanes(fin, nd):
    g, nc, b, w = fin.shape
    return fin.reshape(g, nc, b, nd, w // nd).transpose(2, 3, 1, 0, 4)


def _outproj_kernel(x_ref, a_ref, y_ref, u_ref, gs_ref, gate_ref, dskip_ref,
                    wglu_ref, bglu_ref, wout_ref, g_ref, o_ref):
    nb, tp, d = x_ref.shape
    y = _scan_block(y_ref) + dskip_ref[...] * _scan_block(u_ref)
    t, srows, d_ssm = y.shape
    rows = lambda ref: ref[...].reshape(nb * tp, ref.shape[-1])
    y = y.reshape(t * srows, d_ssm)
    c0 = math.sqrt(2.0 / math.pi)
    hy = 0.5 * y
    ys = hy + hy * jnp.tanh(y * (c0 + (c0 * 0.044715) * (y * y)))
    z = jnp.dot(ys.astype(BF16), wglu_ref[...], preferred_element_type=F32) + bglu_ref[...]
    ys = _from_scan_layout((ys * jax.nn.sigmoid(z)).reshape(t, srows, d_ssm), nb) * rows(gs_ref).astype(F32)
    cat = jnp.concatenate([rows(a_ref), ys.astype(BF16)], axis=-1)
    out = jnp.dot(cat, wout_ref[...], preferred_element_type=F32)
    ms = jnp.mean(out * out, axis=-1, keepdims=True)
    out = out * lax.rsqrt(ms + EPS)
    o_ref[...] = x_ref[...] + (g_ref[...] * gate_ref[...]) * out.reshape(nb, tp, d)


def _outproj(x, a_out, y_ssm, u, gs, gate, d_skip, w_glu_bf, b_glu, w_out_bf, norm_post):
    b, l, d = x.shape
    d_attn = a_out.shape[-1]
    d_ssm = gs.shape[-1]
    tp = TOKEN_TILE // b
    t = SSM_CHUNK
    const = lambda a: pl.BlockSpec(a.shape, lambda j: (0,) * a.ndim)
    tok = lambda w: pl.BlockSpec((b, tp, w), lambda j: (0, j, 0))
    scan = pl.BlockSpec((d_ssm // SCAN_SLAB, t, tp // t * b, SCAN_SLAB), lambda j: (0, 0, j, 0))
    consts = [gate, d_skip.reshape(1, d_ssm).astype(F32), w_glu_bf, b_glu.reshape(1, d_ssm).astype(F32),
              w_out_bf, norm_post.reshape(1, d).astype(F32)]
    return pl.pallas_call(
        _outproj_kernel,
        grid=(l // tp,),
        in_specs=[tok(d), tok(d_attn), scan, scan, tok(d_ssm)] + [const(a) for a in consts],
        out_specs=tok(d),
        out_shape=jax.ShapeDtypeStruct((b, l, d), F32),
        compiler_params=_params("arbitrary"),
        name="outproj",
    )(x, a_out, y_ssm, u, gs, *consts)


def _mixer(x, mod, h0_lanes, cache_k, cache_v, rope_tabs, emit_caches, with_state, lam, lam_init,
           norm_pre, w_in, ssm_ops, subln, d_skip, w_glu_bf, b_glu, w_out_bf, norm_post):
    b, l, d = x.shape
    d_ssm = w_glu_bf.shape[0]
    d_attn = (w_in.shape[1] - 2 * d_ssm) // 4
    shift, scale, gate = mod
    q, k, v, ga, u, gs, *caches = _inproj(x, shift, scale, norm_pre, w_in, rope_tabs, emit_caches,
                                          d_attn, d_ssm)
    attn = _attention(lam, q, k, v, cache_k, cache_v, ga, subln, lam_init)
    scan = _ssm(u, ssm_ops, h0_lanes, with_state)
    if attn["grid"] == scan["grid"]:
        (a_out,), outs = _launch(attn, scan)
    else:
        (a_out,), = _launch(attn)
        outs, = _launch(scan)
    y = _outproj(x, a_out, outs[0], u, gs, gate, d_skip, w_glu_bf, b_glu, w_out_bf, norm_post)
    fin = outs[1] if with_state else None
    return y, caches, fin


def kernel(x_prompt, x_sample, cache_k, cache_v, state_ssm, c, c_ctx, w_ada, b_ada, norm_pre, norm_post, w_in, lambda_qk, subln, ssm_A_re, ssm_A_im, ssm_log_dt, ssm_B_re, ssm_B_im, ssm_C_re, ssm_C_im, ssm_D, w_glu, b_glu, w_out):
    xp, xs = x_prompt, x_sample
    bp, lp, d = xp.shape
    bs, ls, _ = xs.shape
    depth = w_in.shape[0]
    nd = ssm_A_re.shape[1]
    g, p = ssm_A_re.shape[-2:]
    d_ssm = w_glu.shape[-1]
    d_attn = (w_in.shape[-1] - 2 * d_ssm) // 4
    assert bs + 1 <= COND_ROWS and nd == 2
    assert d_ssm % SCAN_SLAB == 0 and SCAN_SLAB % (d_ssm // g) == 0
    assert all(TOKEN_TILE % b == 0 and (TOKEN_TILE // b) % SSM_CHUNK == 0 and l % (TOKEN_TILE // b) == 0
               for b, l in ((bp, lp), (bs, ls)))
    assert all(l % min(ATTN_Q_TILE, l) == 0 and min(ATTN_Q_TILE, l) % min(ATTN_Q_SUB, l) == 0
               and b % max(1, ATTN_Q_TILE // l) == 0 for b, l in ((bp, lp), (bs, ls)))
    rope_tabs = _rope_tables(ls, d_attn)
    cond = jnp.zeros((COND_ROWS, d), F32).at[:bs].set(c.astype(F32)).at[bs].set(c_ctx.astype(F32))
    new_k, new_v, new_s = [], [], []
    for layer in range(depth):
        lam_init = 0.8 - 0.6 * math.exp(-0.3 * layer)
        m, ssm_ops, lam = _prep(
            cond, w_ada[layer], b_ada[layer],
            ssm_A_re[layer], ssm_A_im[layer], ssm_log_dt[layer], ssm_B_re[layer], ssm_B_im[layer],
            ssm_C_re[layer], ssm_C_im[layer], lambda_qk[layer], lam_init)
        mod_s = tuple(m[:bs, i * d:(i + 1) * d].reshape(bs, 1, d) for i in range(3))
        mod_p = tuple(m[bs:bs + 1, i * d:(i + 1) * d].reshape(1, 1, d) for i in range(3))
        shared = (lam, lam_init, norm_pre[layer], w_in[layer], ssm_ops, subln[layer],
                  ssm_D[layer], w_glu[layer].astype(BF16), b_glu[layer], w_out[layer].astype(BF16),
                  norm_post[layer])

        h0_p = jnp.zeros((g, 2, bp, nd * p), F32)
        xp, (k_p, v_p), fin = _mixer(xp, mod_p, h0_p, None, None, None, True, True, *shared)
        new_k.append(k_p.reshape(bp, lp, d_attn // V_DIM, V_DIM))
        new_v.append(v_p.reshape(bp, lp, d_attn // V_DIM, V_DIM))
        new_s.append(_state_from_lanes(fin, nd).astype(xp.dtype))

        h0_s = _state_to_lanes(state_ssm[:, layer])
        ck = cache_k[:, layer].reshape(bs, -1, V_DIM)
        cv = cache_v[:, layer].reshape(bs, -1, V_DIM)
        xs, _, _ = _mixer(xs, mod_s, h0_s, ck, cv, rope_tabs, False, False, *shared)

    return (xp, xs, jnp.stack(new_k, axis=1), jnp.stack(new_v, axis=1), jnp.stack(new_s, axis=1))
```

```python
import functools
import math

import jax
import jax.numpy as jnp
import numpy as np
from jax import lax
from jax.experimental import pallas as pl
from jax.experimental.pallas import tpu as pltpu

F32 = jnp.float32
BF16 = jnp.bfloat16

GRID_W = 64
QK_DIM = 64
V_DIM = 2 * QK_DIM
ROPE_THETA = 10000.0
EPS = 1e-6
Q_PRESCALE = math.log2(math.e) * QK_DIM ** -0.5

VREG_TILE = (8, 128)
SUM_ROWS = 16
COND_ROWS = 16
TOKEN_TILE = 1024
ATTN_Q_TILE = 1024
ATTN_Q_SUB = 512
ATTN_KEY_BLOCK = 256
SSM_CHUNK = 16
SCAN_SLAB = 128
VMEM_LIMIT = 48 * 1024 * 1024


def _silu(x):
    return x * jax.nn.sigmoid(x)


def _params(*sem):
    return pltpu.CompilerParams(dimension_semantics=sem, vmem_limit_bytes=VMEM_LIMIT)


def _adaln_kernel(c_ref, w_ref, b_ref, o_ref):
    @pl.when(pl.program_id(0) == 0)
    def _():
        o_ref[...] = jnp.broadcast_to(b_ref[...], o_ref.shape)

    s = _silu(c_ref[...]).astype(BF16)
    o_ref[...] += jnp.dot(s, w_ref[...].astype(BF16), preferred_element_type=F32)


def _ssmprep_kernel(are_ref, aim_ref, ldt_ref, bre_ref, bim_ref, cre_ref, cim_ref, lq_ref,
                    toept_ref, wt_ref, vt_ref, a_ref, lam_ref,
                    cplre_scr, cplimn_scr, w_scr, v_scr, toep_scr, *, lam_init, chunk):
    a_re = are_ref[...]
    a_im = aim_ref[...]
    dt = jnp.exp(ldt_ref[...])
    mag = jnp.exp(a_re * dt)
    ab_re = mag * jnp.cos(a_im * dt)
    ab_im = mag * jnp.sin(a_im * dt)
    nr, ni = ab_re - 1.0, ab_im
    den = a_re * a_re + a_im * a_im
    f_re = (nr * a_re + ni * a_im) / den
    f_im = (ni * a_re - nr * a_im) / den
    b_re = bre_ref[...]
    b_im = bim_ref[...]
    bb_re = f_re * b_re - f_im * b_im
    bb_im = f_re * b_im + f_im * b_re
    c_re = cre_ref[...]
    c_im = cim_ref[...]
    hc = b_re.shape[1]
    sw = a_re.shape[-1]
    hw = sw // 2
    lo, hi = slice(0, hw), slice(hw, sw)
    lo_im, hi_im = slice(sw, sw + hw), slice(sw + hw, 2 * sw)
    blk = lambda j: slice(j * hc, (j + 1) * hc)
    last = chunk - 1
    cplre_scr[...] = jnp.zeros_like(cplre_scr)
    cplimn_scr[...] = jnp.zeros_like(cplimn_scr)
    pw_re = jnp.ones_like(ab_re)
    pw_im = jnp.zeros_like(ab_im)
    for tau in range(chunk + 1):
        cp_re = c_re * pw_re - c_im * pw_im
        cp_imn = -(c_re * pw_im + c_im * pw_re)
        if tau < chunk:
            cplre_scr[:, blk(last + tau), lo] = cp_re[:, :, lo]
            cplre_scr[:, blk(last - tau), hi] = cp_re[:, :, hi]
            cplimn_scr[:, blk(last + tau), lo] = cp_imn[:, :, lo]
            cplimn_scr[:, blk(last - tau), hi] = cp_imn[:, :, hi]
            e_re = pw_re * bb_re - pw_im * bb_im
            e_im = pw_re * bb_im + pw_im * bb_re
            w_scr[:, blk(last - tau), lo] = e_re[:, :, lo]
            w_scr[:, blk(tau), hi] = e_re[:, :, hi]
            w_scr[:, blk(last - tau), lo_im] = e_im[:, :, lo]
            w_scr[:, blk(tau), hi_im] = e_im[:, :, hi]
        if tau >= 1:
            v_scr[:, blk(tau - 1), lo] = cp_re[:, :, lo]
            v_scr[:, blk(chunk - tau), hi] = cp_re[:, :, hi]
            v_scr[:, blk(tau - 1), lo_im] = cp_imn[:, :, lo]
            v_scr[:, blk(chunk - tau), hi_im] = cp_imn[:, :, hi]
        if tau < chunk:
            pw_re, pw_im = pw_re * ab_re - pw_im * ab_im, pw_re * ab_im + pw_im * ab_re
    a_ref[:, 0] = pw_re
    a_ref[:, 1] = pw_im
    dims = (((2,), (2,)), ((0,), (0,)))
    taps = (lax.dot_general(bb_re.astype(BF16), cplre_scr[...].astype(BF16), dims,
                            preferred_element_type=F32)
            + lax.dot_general(bb_im.astype(BF16), cplimn_scr[...].astype(BF16), dims,
                              preferred_element_type=F32))
    nk = chunk * hc
    for s in range(chunk):
        toep_scr[:, blk(s), :] = taps[:, :, (last - s) * hc:(last - s) * hc + nk]
    for g in range(toept_ref.shape[0]):
        toept_ref[g] = toep_scr[g].astype(toept_ref.dtype).T
        wt_ref[g] = w_scr[g].astype(wt_ref.dtype).T
    vt_ref[...] = v_scr[...].astype(vt_ref.dtype)
    lq = lq_ref[...]
    s01 = jnp.sum(lq[0:1] * lq[1:2], axis=-1, keepdims=True)
    s23 = jnp.sum(lq[2:3] * lq[3:4], axis=-1, keepdims=True)
    lam = jnp.exp(s01) - jnp.exp(s23) + lam_init
    lam_ref[...] = jnp.broadcast_to(lam, lam_ref.shape)


def _prep_kernel(*refs, lam_init, chunk):
    c_ref, wada_ref, bada_ref = refs[0:3]
    prep_in = refs[3:11]
    mod_ref = refs[11]
    prep_out = refs[12:17]
    scratch = refs[17:]
    _adaln_kernel(c_ref, wada_ref, bada_ref, mod_ref)
    _ssmprep_kernel(*prep_in, *prep_out, *scratch, lam_init=lam_init, chunk=chunk)


def _prep(cond, w_ada, b_ada, a_re, a_im, log_dt, b_re, b_im, c_re, c_im, lq, lam_init):
    rows, d = cond.shape
    n = w_ada.shape[1]
    nd, g, p = a_re.shape
    hc = b_re.shape[-1]
    t = SSM_CHUNK
    sw = nd * p
    gb = SCAN_SLAB // hc
    steps = g // gb
    bk = d // steps
    row = lambda a: a.astype(F32).transpose(1, 0, 2).reshape(g, 1, sw)
    ldt = jnp.broadcast_to(log_dt[..., None], (nd, g, p))
    bt = lambda a: a.astype(F32).transpose(1, 3, 0, 2).reshape(g, hc, sw)
    ct = lambda a: a.astype(F32).transpose(1, 2, 0, 3).reshape(g, hc, sw)
    sds = jax.ShapeDtypeStruct
    nk = t * hc
    blk = lambda shape: pl.BlockSpec((gb,) + shape[1:], lambda i: (i,) + (0,) * (len(shape) - 1))
    mat = (g, nk, 2 * sw)
    mod, toept, wt, vt, a_t, lam = pl.pallas_call(
        functools.partial(_prep_kernel, lam_init=lam_init, chunk=t),
        grid=(steps,),
        in_specs=[pl.BlockSpec((rows, bk), lambda i: (0, i)),
                  pl.BlockSpec((bk, n), lambda i: (i, 0)),
                  pl.BlockSpec((1, n), lambda i: (0, 0))]
                 + [blk((g, 1, sw))] * 3 + [blk((g, hc, sw))] * 4 + [pl.BlockSpec(lq.shape, lambda i: (0, 0))],
        out_specs=[pl.BlockSpec((rows, n), lambda i: (0, 0)),
                   blk((g, nk, nk)), blk((g, 2 * sw, nk)), blk(mat), blk((g, 2, 1, sw)),
                   pl.BlockSpec(VREG_TILE, lambda i: (0, 0))],
        out_shape=(sds((rows, n), F32),
                   sds((g, nk, nk), BF16), sds((g, 2 * sw, nk), BF16), sds(mat, BF16),
                   sds((g, 2, 1, sw), F32), sds(VREG_TILE, F32)),
        scratch_shapes=[pltpu.VMEM((gb, 2 * nk, sw), F32), pltpu.VMEM((gb, 2 * nk, sw), F32),
                        pltpu.VMEM((gb, nk, 2 * sw), F32), pltpu.VMEM((gb, nk, 2 * sw), F32),
                        pltpu.VMEM((gb, nk, nk), F32)],
        compiler_params=_params("arbitrary"),
        name="prep",
    )(cond, w_ada, b_ada.reshape(1, n),
      row(a_re), row(a_im), row(ldt), bt(b_re), bt(b_im), ct(c_re), ct(c_im), lq.astype(F32))
    return mod, (toept, wt, vt, a_t), lam


def _to_scan_layout(u, o_ref, batch):
    n_slabs, t, _, cw = o_ref.shape
    tp = u.shape[0] // batch
    u_tb = jnp.swapaxes(u.reshape(batch, tp, u.shape[-1]), 0, 1)
    for c in range(tp // t):
        for s in range(t):
            for v in range(n_slabs):
                o_ref[v, s, c * batch:(c + 1) * batch, :] = u_tb[c * t + s][:, v * cw:(v + 1) * cw].astype(o_ref.dtype)


def _scan_block(ref):
    return jnp.concatenate([ref[v] for v in range(ref.shape[0])], axis=-1).astype(F32)


def _from_scan_layout(ref, batch):
    t, rows, d = ref.shape
    tp = rows // batch * t
    u_tb = jnp.stack([ref[s, c * batch:(c + 1) * batch, :] for c in range(tp // t) for s in range(t)], axis=0)
    return jnp.swapaxes(u_tb, 0, 1).reshape(batch * tp, d)


def _inproj_kernel(*refs, rope, emit_caches, d_attn):
    n_in = 7 if rope else 5
    x_ref, shift_ref, scale_ref, g_ref, w_ref = refs[:5]
    cos_ref, sin_ref = refs[5:7] if rope else (None, None)
    q_ref, k_ref, v_ref, ga_ref, u_ref, gs_ref = refs[n_in:n_in + 6]
    kc_ref, vc_ref = refs[n_in + 6:] if emit_caches else (None, None)
    x = x_ref[...]
    nb, tp, d = x.shape
    ms = jnp.mean(x * x, axis=-1, keepdims=True)
    h = x * lax.rsqrt(ms + EPS) * (g_ref[...] * (1.0 + scale_ref[...])) + shift_ref[...]
    hb = h.reshape(nb * tp, d).astype(BF16)
    proj = lambda lo, hi: jnp.dot(hb, w_ref[:, lo:hi].astype(BF16), preferred_element_type=F32)
    da = d_attn
    d_ssm = u_ref.shape[0] * u_ref.shape[-1]
    tok = lambda z: z.reshape(nb, tp, z.shape[-1])
    q = proj(0, da)
    k = proj(da, 2 * da)
    if rope:
        lane = lax.broadcasted_iota(jnp.int32, (1, da), 1)
        low = (lane % (QK_DIM // 2)) < (QK_DIM // 4)
        cos = cos_ref[...]
        sin = sin_ref[...]

        def rot(z):
            partner = jnp.where(low, pltpu.roll(z, da - QK_DIM // 4, 1), pltpu.roll(z, QK_DIM // 4, 1))
            return tok(z) * cos + tok(partner) * sin

        q = rot(q)
        k = rot(k)
    else:
        q = tok(q)
        k = tok(k)
    q_ref[...] = (q * Q_PRESCALE).astype(q_ref.dtype)
    k_ref[...] = k.astype(k_ref.dtype)
    n_heads = da // V_DIM
    if emit_caches:
        for hd in range(n_heads):
            kc_ref[:, pl.ds(hd, tp, stride=n_heads), :] = k[:, :, hd * V_DIM:(hd + 1) * V_DIM]
    v = tok(proj(2 * da, 3 * da))
    v_ref[...] = v.astype(v_ref.dtype)
    if emit_caches:
        for hd in range(n_heads):
            vc_ref[:, pl.ds(hd, tp, stride=n_heads), :] = v[:, :, hd * V_DIM:(hd + 1) * V_DIM]
    ga_ref[...] = tok(_silu(proj(3 * da, 4 * da))).astype(ga_ref.dtype)
    gs_ref[...] = tok(_silu(proj(4 * da + d_ssm, 4 * da + 2 * d_ssm))).astype(gs_ref.dtype)
    _to_scan_layout(proj(4 * da, 4 * da + d_ssm), u_ref, nb)


def _inproj(x, shift, scale, norm_pre, w_in, rope_tabs, emit_caches, d_attn, d_ssm):
    b, l, d = x.shape
    tp = TOKEN_TILE // b
    t = SSM_CHUNK
    rope = rope_tabs is not None
    const = lambda a: pl.BlockSpec(a.shape, lambda j: (0,) * a.ndim)
    tok = lambda w: pl.BlockSpec((b, tp, w), lambda j: (0, j, 0))
    g2 = norm_pre.reshape(1, d).astype(F32)
    w_spec = pl.BlockSpec(w_in.shape, lambda j: (0, 0), pipeline_mode=pl.Buffered(1))
    in_specs = [tok(d), const(shift), const(scale), const(g2), w_spec]
    args = [x, shift, scale, g2, w_in]
    if rope:
        in_specs += [pl.BlockSpec((tp, d_attn), lambda j: (j, 0))] * 2
        args += list(rope_tabs)
    scan_rows = tp // t * b
    attn_shape = jax.ShapeDtypeStruct((b, l, d_attn), BF16)
    out_specs = [tok(d_attn), tok(d_attn), tok(d_attn), tok(d_attn),
                 pl.BlockSpec((d_ssm // SCAN_SLAB, t, scan_rows, SCAN_SLAB), lambda j: (0, 0, j, 0)),
                 tok(d_ssm)]
    out_shape = [attn_shape, attn_shape, attn_shape, attn_shape,
                 jax.ShapeDtypeStruct((d_ssm // SCAN_SLAB, t, l // t * b, SCAN_SLAB), BF16),
                 jax.ShapeDtypeStruct((b, l, d_ssm), BF16)]
    if emit_caches:
        n_heads = d_attn // V_DIM
        out_specs += [pl.BlockSpec((b, tp * n_heads, V_DIM), lambda j: (0, j, 0))] * 2
        out_shape += [jax.ShapeDtypeStruct((b, l * n_heads, V_DIM), F32)] * 2
    return pl.pallas_call(
        functools.partial(_inproj_kernel, rope=rope, emit_caches=emit_caches, d_attn=d_attn),
        grid=(l // tp,),
        in_specs=in_specs,
        out_specs=out_specs,
        out_shape=out_shape,
        compiler_params=_params("arbitrary"),
        name="inproj_rope" if rope else "inproj",
    )(*args)


def _rope_tables(l, d_attn):
    rows = l // GRID_W
    row = np.repeat(np.arange(rows, dtype=np.float64), GRID_W)
    col = np.tile(np.arange(GRID_W, dtype=np.float64), rows)
    n_freq = QK_DIM // 4
    inv = ROPE_THETA ** (-np.arange(n_freq, dtype=np.float64) / n_freq)
    ang_r = row[:, None] * inv
    ang_c = col[:, None] * inv
    cos64 = np.concatenate([np.cos(ang_r), np.cos(ang_r), np.cos(ang_c), np.cos(ang_c)], axis=-1)
    sin64 = np.concatenate([-np.sin(ang_r), np.sin(ang_r), -np.sin(ang_c), np.sin(ang_c)], axis=-1)
    reps = d_attn // QK_DIM
    return (jnp.asarray(np.tile(cos64, (1, reps)), dtype=F32),
            jnp.asarray(np.tile(sin64, (1, reps)), dtype=F32))


def _head(ref, bi, h, n_heads):
    if ref.shape[-1] == V_DIM:
        return ref[bi, pl.ds(h, ref.shape[1] // n_heads, stride=n_heads), :]
    return ref[bi, :, h * V_DIM:(h + 1) * V_DIM]


def _attn_kernel(*refs, has_cache, lam_init):
    if has_cache:
        lam_ref, q_ref, k_ref, v_ref, ck_ref, cv_ref, ga_ref, sub_ref, o_ref, k_scr, vt_scr = refs
    else:
        lam_ref, q_ref, k_ref, v_ref, ga_ref, sub_ref, o_ref, k_scr, vt_scr = refs
    nbk, tq = q_ref.shape[0], q_ref.shape[1]
    n_heads, lk = k_scr.shape[0] // nbk, k_scr.shape[1]
    ln = k_ref.shape[1]
    lam = lam_ref[0:1, 0:1]
    lane = lax.broadcasted_iota(jnp.int32, (1, V_DIM), 1)
    first = lane < QK_DIM
    dims = (((1,), (1,)), ((), ()))
    slot = lambda bi, h: bi * n_heads + h

    @pl.when(pl.program_id(1) == 0)
    def _():
        for bi in range(nbk):
            for h in range(n_heads):
                n = slot(bi, h)
                if has_cache:
                    k_scr[n, 0:lk - ln, :] = _head(ck_ref, bi, h, n_heads).astype(BF16)
                    vt_scr[n, 0:V_DIM, 0:lk - ln] = _head(cv_ref, bi, h, n_heads).astype(F32).T.astype(BF16)
                k_scr[n, lk - ln:lk, :] = _head(k_ref, bi, h, n_heads).astype(BF16)
                vt_scr[n, 0:V_DIM, lk - ln:lk] = _head(v_ref, bi, h, n_heads).astype(F32).T.astype(BF16)
                vt_scr[n, V_DIM:, :] = jnp.ones((vt_scr.shape[1] - V_DIM, lk), BF16)

    sub = min(tq, ATTN_Q_SUB)
    items = [(bi, h, r) for bi in range(nbk) for h in range(n_heads) for r in range(0, tq, sub)]

    def scores(item):
        bi, h, r = item
        q = _head(q_ref, bi, h, n_heads)[r:r + sub, :]
        zero = jnp.zeros_like(q)
        k = k_scr[slot(bi, h)]
        return (lax.dot_general(k, jnp.where(first, q, zero), dims, preferred_element_type=F32),
                lax.dot_general(k, jnp.where(first, zero, q), dims, preferred_element_type=F32))

    nxt = scores(items[0])
    for n, (bi, h, r) in enumerate(items):
        st1, st2 = nxt
        if n + 1 < len(items):
            nxt = scores(items[n + 1])
        vt = vt_scr[slot(bi, h)]
        m1 = jnp.max(st1, axis=0, keepdims=True)
        m2 = jnp.max(st2, axis=0, keepdims=True)
        o1 = o2 = None
        for lo in range(0, lk, ATTN_KEY_BLOCK):
            hi = min(lk, lo + ATTN_KEY_BLOCK)
            e1 = jnp.exp2(st1[lo:hi, :] - m1).astype(BF16)
            e2 = jnp.exp2(st2[lo:hi, :] - m2).astype(BF16)
            p1 = jnp.dot(vt[:, lo:hi], e1, preferred_element_type=F32)
            p2 = jnp.dot(vt[:, lo:hi], e2, preferred_element_type=F32)
            o1 = p1 if o1 is None else o1 + p1
            o2 = p2 if o2 is None else o2 + p2
        r1 = 1.0 / o1[V_DIM:V_DIM + 1, :]
        r2 = lam / o2[V_DIM:V_DIM + 1, :]
        o = (o1[0:V_DIM, :] * r1 - o2[0:V_DIM, :] * r2).T
        ms = jnp.mean(o * o, axis=-1, keepdims=True)
        o = o * lax.rsqrt(ms + EPS) * sub_ref[...] * (1.0 - lam_init)
        ga = _head(ga_ref, bi, h, n_heads)[r:r + sub, :].astype(F32)
        o_ref[bi, r:r + sub, h * V_DIM:(h + 1) * V_DIM] = (o * ga).astype(o_ref.dtype)


def _attention(lam, q, k, v, cache_k, cache_v, ga, subln, lam_init):
    b, l, d_attn = q.shape
    n_heads = d_attn // V_DIM
    tq = min(ATTN_Q_TILE, l)
    nq = l // tq
    nbk = max(1, ATTN_Q_TILE // l)
    has_cache = cache_k is not None
    q_spec = pl.BlockSpec((nbk, tq, d_attn), lambda i, j: (i, j, 0))
    whole = lambda a: pl.BlockSpec((nbk,) + a.shape[1:], lambda i, j: (i,) + (0,) * (a.ndim - 1))
    in_specs = [pl.BlockSpec(VREG_TILE, lambda i, j: (0, 0)), q_spec, whole(k), whole(v)]
    args = [lam, q, k, v]
    if has_cache:
        in_specs += [whole(cache_k), whole(cache_v)]
        args += [cache_k, cache_v]
    in_specs += [q_spec, pl.BlockSpec((1, V_DIM), lambda i, j: (0, 0))]
    args += [ga, subln.reshape(1, V_DIM).astype(F32)]
    lk = l + (cache_k.shape[1] // n_heads if has_cache else 0)
    return dict(
        kernel=functools.partial(_attn_kernel, has_cache=has_cache, lam_init=lam_init),
        grid=(b // nbk, nq), in_specs=in_specs, args=args,
        out_specs=[q_spec], out_shape=[jax.ShapeDtypeStruct((b, l, d_attn), BF16)],
        scratch=[pltpu.VMEM((nbk * n_heads, lk, V_DIM), BF16),
                 pltpu.VMEM((nbk * n_heads, V_DIM + SUM_ROWS, lk), BF16)],
        name="attn_cache" if has_cache else "attn")


def _launch(*parts):
    grid = parts[0]["grid"]
    assert all(p["grid"] == grid for p in parts)
    n_in = [len(p["in_specs"]) for p in parts]
    n_out = [len(p["out_specs"]) for p in parts]
    n_scr = [len(p["scratch"]) for p in parts]

    def body(*refs):
        ins, outs, scrs = refs[:sum(n_in)], refs[sum(n_in):sum(n_in) + sum(n_out)], refs[sum(n_in) + sum(n_out):]
        for n, p in enumerate(parts):
            take = lambda seq, counts: seq[sum(counts[:n]):sum(counts[:n + 1])]
            p["kernel"](*take(ins, n_in), *take(outs, n_out), *take(scrs, n_scr))

    outs = pl.pallas_call(
        body,
        grid=grid,
        in_specs=[spec for p in parts for spec in p["in_specs"]],
        out_specs=[spec for p in parts for spec in p["out_specs"]],
        out_shape=[shape for p in parts for shape in p["out_shape"]],
        scratch_shapes=[scr for p in parts for scr in p["scratch"]],
        compiler_params=_params(*(["arbitrary"] * len(grid))),
        name="_".join(p["name"] for p in parts),
    )(*[a for p in parts for a in p["args"]])
    return [outs[sum(n_out[:n]):sum(n_out[:n + 1])] for n in range(len(parts))]


def _ssm_kernel(*refs, batch, n_chunks, with_state):
    if with_state:
        u_ref, toept_ref, wt_ref, vt_ref, a_ref, h0_ref, y_ref, fin_ref, zt_scr, wc_scr, sin_scr, yt_scr = refs
    else:
        u_ref, toept_ref, wt_ref, vt_ref, a_ref, h0_ref, y_ref, zt_scr, wc_scr, sin_scr, yt_scr = refs
    t = u_ref.shape[0]
    gb = zt_scr.shape[0]
    hc = zt_scr.shape[1] // t
    sw = a_ref.shape[-1]
    hw = sw // 2
    for s in range(t):
        at = u_ref[s].T
        for g in range(gb):
            zt_scr[g, s * hc:(s + 1) * hc, :] = at[g * hc:(g + 1) * hc, :]
    for g in range(gb):
        wc_scr[g] = jnp.dot(wt_ref[g], zt_scr[g], preferred_element_type=F32).astype(BF16).T.astype(F32)
    fwd = lax.broadcasted_iota(jnp.int32, (1, sw), 1) < hw

    def step(j, carry):
        rf = pl.ds(pl.multiple_of(j * batch, batch), batch)
        rb = pl.ds(pl.multiple_of((n_chunks - 1 - j) * batch, batch), batch)
        out = []
        for g in range(gb):
            sr, si = carry[g]
            sin_scr[g, rf, 0:hw] = sr[:, 0:hw]
            sin_scr[g, rb, hw:sw] = sr[:, hw:sw]
            sin_scr[g, rf, sw:sw + hw] = si[:, 0:hw]
            sin_scr[g, rb, sw + hw:2 * sw] = si[:, hw:sw]
            wr = jnp.where(fwd, wc_scr[g, rf, 0:sw], wc_scr[g, rb, 0:sw])
            wi = jnp.where(fwd, wc_scr[g, rf, sw:2 * sw], wc_scr[g, rb, sw:2 * sw])
            ar = a_ref[g, 0]
            ai = a_ref[g, 1]
            out.append((ar * sr - ai * si + wr, ar * si + ai * sr + wi))
        return tuple(out)

    fin = lax.fori_loop(0, n_chunks, step, tuple((h0_ref[g, 0], h0_ref[g, 1]) for g in range(gb)),
                        unroll=True)
    for g in range(gb):
        sint = sin_scr[g].astype(BF16).T
        y2t = (jnp.dot(toept_ref[g], zt_scr[g], preferred_element_type=F32)
               + jnp.dot(vt_ref[g], sint, preferred_element_type=F32))
        for s in range(t):
            yt_scr[s, g * hc:(g + 1) * hc, :] = y2t[s * hc:(s + 1) * hc, :].astype(yt_scr.dtype)
        if with_state:
            fin_ref[g, 0] = fin[g][0]
            fin_ref[g, 1] = fin[g][1]
    for s in range(t):
        y_ref[s] = yt_scr[s].T


def _ssm(u3, ssm_ops, h0, with_state):
    toept, wt, vt, a_t = ssm_ops
    n_slabs, t, rows, cw = u3.shape
    batch = h0.shape[2]
    n_chunks = rows // batch
    g = toept.shape[0]
    gb = g // n_slabs
    slab = pl.BlockSpec((None, t, rows, cw), lambda i, j: (i, 0, 0, 0))
    blk = lambda a: pl.BlockSpec((gb,) + a.shape[1:], lambda i, j: (i,) + (0,) * (a.ndim - 1))
    out_specs = [slab]
    out_shape = [jax.ShapeDtypeStruct(u3.shape, u3.dtype)]
    if with_state:
        out_specs.append(blk(h0))
        out_shape.append(jax.ShapeDtypeStruct(h0.shape, F32))
    nk = toept.shape[-1]
    return dict(
        kernel=functools.partial(_ssm_kernel, batch=batch, n_chunks=n_chunks, with_state=with_state),
        grid=(g // gb, 1), in_specs=[slab, blk(toept), blk(wt), blk(vt), blk(a_t), blk(h0)],
        args=[u3, toept, wt, vt, a_t, h0], out_specs=out_specs, out_shape=out_shape,
        scratch=[pltpu.VMEM((gb, nk, rows), BF16),
                 pltpu.VMEM((gb, rows, wt.shape[1]), F32),
                 pltpu.VMEM((gb, rows, wt.shape[1]), F32),
                 pltpu.VMEM((t, cw, rows), u3.dtype)],
        name="ssm_state" if with_state else "ssm")


def _state_to_lanes(h0):
    b, nd, nc, g, p = h0.shape
    return h0.astype(F32).transpose(3, 2, 0, 1, 4).reshape(g, nc, b, nd * p)


def _state_from_lanes(fin, nd):
    g, nc, b, w = fin.shape
    return fin.reshape(g, nc, b, nd, w // nd).transpose(2, 3, 1, 0, 4)


def _outproj_kernel(x_ref, a_ref, y_ref, u_ref, gs_ref, gate_ref, dskip_ref,
                    wglu_ref, bglu_ref, wout_ref, g_ref, o_ref):
    nb, tp, d = x_ref.shape
    y = _scan_block(y_ref) + dskip_ref[...] * _scan_block(u_ref)
    t, srows, d_ssm = y.shape
    rows = lambda ref: ref[...].reshape(nb * tp, ref.shape[-1])
    y = y.reshape(t * srows, d_ssm)
    c0 = math.sqrt(2.0 / math.pi)
    hy = 0.5 * y
    ys = hy + hy * jnp.tanh(y * (c0 + (c0 * 0.044715) * (y * y)))
    z = jnp.dot(ys.astype(BF16), wglu_ref[...], preferred_element_type=F32) + bglu_ref[...]
    ys = _from_scan_layout((ys * jax.nn.sigmoid(z)).reshape(t, srows, d_ssm), nb) * rows(gs_ref).astype(F32)
    cat = jnp.concatenate([rows(a_ref), ys.astype(BF16)], axis=-1)
    out = jnp.dot(cat, wout_ref[...], preferred_element_type=F32)
    ms = jnp.mean(out * out, axis=-1, keepdims=True)
    out = out * lax.rsqrt(ms + EPS)
    o_ref[...] = x_ref[...] + (g_ref[...] * gate_ref[...]) * out.reshape(nb, tp, d)


def _outproj(x, a_out, y_ssm, u, gs, gate, d_skip, w_glu_bf, b_glu, w_out_bf, norm_post):
    b, l, d = x.shape
    d_attn = a_out.shape[-1]
    d_ssm = gs.shape[-1]
    tp = TOKEN_TILE // b
    t = SSM_CHUNK
    const = lambda a: pl.BlockSpec(a.shape, lambda j: (0,) * a.ndim)
    tok = lambda w: pl.BlockSpec((b, tp, w), lambda j: (0, j, 0))
    scan = pl.BlockSpec((d_ssm // SCAN_SLAB, t, tp // t * b, SCAN_SLAB), lambda j: (0, 0, j, 0))
    consts = [gate, d_skip.reshape(1, d_ssm).astype(F32), w_glu_bf, b_glu.reshape(1, d_ssm).astype(F32),
              w_out_bf, norm_post.reshape(1, d).astype(F32)]
    in_specs = [tok(d), tok(d_attn), scan, scan, tok(d_ssm)] + [const(a) for a in consts]
    stream = pltpu.emit_pipeline(_outproj_kernel, grid=(l // tp,), in_specs=in_specs, out_specs=[tok(d)])
    anywhere = pl.BlockSpec(memory_space=pl.ANY)
    return pl.pallas_call(
        lambda *refs: stream(*refs),
        in_specs=[anywhere] * len(in_specs),
        out_specs=anywhere,
        out_shape=jax.ShapeDtypeStruct((b, l, d), F32),
        compiler_params=_params(),
        name="outproj",
    )(x, a_out, y_ssm, u, gs, *consts)


def _mixer(x, mod, h0_lanes, cache_k, cache_v, rope_tabs, emit_caches, with_state, lam, lam_init,
           norm_pre, w_in, ssm_ops, subln, d_skip, w_glu_bf, b_glu, w_out_bf, norm_post):
    b, l, d = x.shape
    d_ssm = w_glu_bf.shape[0]
    d_attn = (w_in.shape[1] - 2 * d_ssm) // 4
    shift, scale, gate = mod
    q, k, v, ga, u, gs, *caches = _inproj(x, shift, scale, norm_pre, w_in, rope_tabs, emit_caches,
                                          d_attn, d_ssm)
    attn = _attention(lam, q, k, v, cache_k, cache_v, ga, subln, lam_init)
    scan = _ssm(u, ssm_ops, h0_lanes, with_state)
    if attn["grid"] == scan["grid"]:
        (a_out,), outs = _launch(attn, scan)
    else:
        (a_out,), = _launch(attn)
        outs, = _launch(scan)
    y = _outproj(x, a_out, outs[0], u, gs, gate, d_skip, w_glu_bf, b_glu, w_out_bf, norm_post)
    fin = outs[1] if with_state else None
    return y, caches, fin


def kernel(x_prompt, x_sample, cache_k, cache_v, state_ssm, c, c_ctx, w_ada, b_ada, norm_pre, norm_post, w_in, lambda_qk, subln, ssm_A_re, ssm_A_im, ssm_log_dt, ssm_B_re, ssm_B_im, ssm_C_re, ssm_C_im, ssm_D, w_glu, b_glu, w_out):
    xp, xs = x_prompt, x_sample
    bp, lp, d = xp.shape
    bs, ls, _ = xs.shape
    depth = w_in.shape[0]
    nd = ssm_A_re.shape[1]
    g, p = ssm_A_re.shape[-2:]
    d_ssm = w_glu.shape[-1]
    d_attn = (w_in.shape[-1] - 2 * d_ssm) // 4
    assert bs + 1 <= COND_ROWS and nd == 2
    assert d_ssm % SCAN_SLAB == 0 and SCAN_SLAB % (d_ssm // g) == 0
    assert all(TOKEN_TILE % b == 0 and (TOKEN_TILE // b) % SSM_CHUNK == 0 and l % (TOKEN_TILE // b) == 0
               for b, l in ((bp, lp), (bs, ls)))
    assert all(l % min(ATTN_Q_TILE, l) == 0 and min(ATTN_Q_TILE, l) % min(ATTN_Q_SUB, l) == 0
               and b % max(1, ATTN_Q_TILE // l) == 0 for b, l in ((bp, lp), (bs, ls)))
    rope_tabs = _rope_tables(ls, d_attn)
    cond = jnp.zeros((COND_ROWS, d), F32).at[:bs].set(c.astype(F32)).at[bs].set(c_ctx.astype(F32))
    new_k, new_v, new_s = [], [], []
    for layer in range(depth):
        lam_init = 0.8 - 0.6 * math.exp(-0.3 * layer)
        m, ssm_ops, lam = _prep(
            cond, w_ada[layer], b_ada[layer],
            ssm_A_re[layer], ssm_A_im[layer], ssm_log_dt[layer], ssm_B_re[layer], ssm_B_im[layer],
            ssm_C_re[layer], ssm_C_im[layer], lambda_qk[layer], lam_init)
        mod_s = tuple(m[:bs, i * d:(i + 1) * d].reshape(bs, 1, d) for i in range(3))
        mod_p = tuple(m[bs:bs + 1, i * d:(i + 1) * d].reshape(1, 1, d) for i in range(3))
        shared = (lam, lam_init, norm_pre[layer], w_in[layer], ssm_ops, subln[layer],
                  ssm_D[layer], w_glu[layer].astype(BF16), b_glu[layer], w_out[layer].astype(BF16),
                  norm_post[layer])

        h0_p = jnp.zeros((g, 2, bp, nd * p), F32)
        xp, (k_p, v_p), fin = _mixer(xp, mod_p, h0_p, None, None, None, True, True, *shared)
        new_k.append(k_p.reshape(bp, lp, d_attn // V_DIM, V_DIM))
        new_v.append(v_p.reshape(bp, lp, d_attn // V_DIM, V_DIM))
        new_s.append(_state_from_lanes(fin, nd).astype(xp.dtype))

        h0_s = _state_to_lanes(state_ssm[:, layer])
        ck = cache_k[:, layer].reshape(bs, -1, V_DIM)
        cv = cache_v[:, layer].reshape(bs, -1, V_DIM)
        xs, _, _ = _mixer(xs, mod_s, h0_s, ck, cv, rope_tabs, False, False, *shared)

    return (xp, xs, jnp.stack(new_k, axis=1), jnp.stack(new_v, axis=1), jnp.stack(new_s, axis=1))
```

```python
import functools
import math

import jax
import jax.numpy as jnp
import numpy as np
from jax import lax
from jax.experimental import pallas as pl
from jax.experimental.pallas import tpu as pltpu

F32 = jnp.float32
BF16 = jnp.bfloat16

GRID_W = 64
QK_DIM = 64
V_DIM = 2 * QK_DIM
ROPE_THETA = 10000.0
EPS = 1e-6
Q_PRESCALE = math.log2(math.e) * QK_DIM ** -0.5

VREG_TILE = (8, 128)
SUM_ROWS = 16
COND_ROWS = 16
TOKEN_TILE = 1024
ATTN_Q_TILE = 1024
ATTN_Q_SUB = 512
ATTN_KEY_BLOCK = 256
SSM_CHUNK = 16
SCAN_SLAB = 128
VMEM_LIMIT = 48 * 1024 * 1024


def _silu(x):
    return x * jax.nn.sigmoid(x)


def _params(*sem):
    return pltpu.CompilerParams(dimension_semantics=sem, vmem_limit_bytes=VMEM_LIMIT)


def _adaln_kernel(c_ref, w_ref, b_ref, o_ref):
    @pl.when(pl.program_id(0) == 0)
    def _():
        o_ref[...] = jnp.broadcast_to(b_ref[...], o_ref.shape)

    s = _silu(c_ref[...]).astype(BF16)
    o_ref[...] += jnp.dot(s, w_ref[...].astype(BF16), preferred_element_type=F32)


def _ssmprep_kernel(are_ref, aim_ref, ldt_ref, bre_ref, bim_ref, cre_ref, cim_ref, lq_ref,
                    toept_ref, wt_ref, vt_ref, a_ref, lam_ref,
                    cplre_scr, cplimn_scr, w_scr, v_scr, toep_scr, *, lam_init, chunk):
    a_re = are_ref[...]
    a_im = aim_ref[...]
    dt = jnp.exp(ldt_ref[...])
    mag = jnp.exp(a_re * dt)
    ab_re = mag * jnp.cos(a_im * dt)
    ab_im = mag * jnp.sin(a_im * dt)
    nr, ni = ab_re - 1.0, ab_im
    den = a_re * a_re + a_im * a_im
    f_re = (nr * a_re + ni * a_im) / den
    f_im = (ni * a_re - nr * a_im) / den
    b_re = bre_ref[...]
    b_im = bim_ref[...]
    bb_re = f_re * b_re - f_im * b_im
    bb_im = f_re * b_im + f_im * b_re
    c_re = cre_ref[...]
    c_im = cim_ref[...]
    hc = b_re.shape[1]
    sw = a_re.shape[-1]
    hw = sw // 2
    lo, hi = slice(0, hw), slice(hw, sw)
    lo_im, hi_im = slice(sw, sw + hw), slice(sw + hw, 2 * sw)
    blk = lambda j: slice(j * hc, (j + 1) * hc)
    last = chunk - 1
    cplre_scr[...] = jnp.zeros_like(cplre_scr)
    cplimn_scr[...] = jnp.zeros_like(cplimn_scr)
    pw_re = jnp.ones_like(ab_re)
    pw_im = jnp.zeros_like(ab_im)
    for tau in range(chunk + 1):
        cp_re = c_re * pw_re - c_im * pw_im
        cp_imn = -(c_re * pw_im + c_im * pw_re)
        if tau < chunk:
            cplre_scr[:, blk(last + tau), lo] = cp_re[:, :, lo]
            cplre_scr[:, blk(last - tau), hi] = cp_re[:, :, hi]
            cplimn_scr[:, blk(last + tau), lo] = cp_imn[:, :, lo]
            cplimn_scr[:, blk(last - tau), hi] = cp_imn[:, :, hi]
            e_re = pw_re * bb_re - pw_im * bb_im
            e_im = pw_re * bb_im + pw_im * bb_re
            w_scr[:, blk(last - tau), lo] = e_re[:, :, lo]
            w_scr[:, blk(tau), hi] = e_re[:, :, hi]
            w_scr[:, blk(last - tau), lo_im] = e_im[:, :, lo]
            w_scr[:, blk(tau), hi_im] = e_im[:, :, hi]
        if tau >= 1:
            v_scr[:, blk(tau - 1), lo] = cp_re[:, :, lo]
            v_scr[:, blk(chunk - tau), hi] = cp_re[:, :, hi]
            v_scr[:, blk(tau - 1), lo_im] = cp_imn[:, :, lo]
            v_scr[:, blk(chunk - tau), hi_im] = cp_imn[:, :, hi]
        if tau < chunk:
            pw_re, pw_im = pw_re * ab_re - pw_im * ab_im, pw_re * ab_im + pw_im * ab_re
    a_ref[:, 0] = pw_re
    a_ref[:, 1] = pw_im
    dims = (((2,), (2,)), ((0,), (0,)))
    taps = (lax.dot_general(bb_re.astype(BF16), cplre_scr[...].astype(BF16), dims,
                            preferred_element_type=F32)
            + lax.dot_general(bb_im.astype(BF16), cplimn_scr[...].astype(BF16), dims,
                              preferred_element_type=F32))
    nk = chunk * hc
    for s in range(chunk):
        toep_scr[:, blk(s), :] = taps[:, :, (last - s) * hc:(last - s) * hc + nk]
    for g in range(toept_ref.shape[0]):
        toept_ref[g] = toep_scr[g].astype(toept_ref.dtype).T
        wt_ref[g] = w_scr[g].astype(wt_ref.dtype).T
    vt_ref[...] = v_scr[...].astype(vt_ref.dtype)
    lq = lq_ref[...]
    s01 = jnp.sum(lq[0:1] * lq[1:2], axis=-1, keepdims=True)
    s23 = jnp.sum(lq[2:3] * lq[3:4], axis=-1, keepdims=True)
    lam = jnp.exp(s01) - jnp.exp(s23) + lam_init
    lam_ref[...] = jnp.broadcast_to(lam, lam_ref.shape)


def _prep_kernel(*refs, lam_init, chunk):
    c_ref, wada_ref, bada_ref = refs[0:3]
    prep_in = refs[3:11]
    mod_ref = refs[11]
    prep_out = refs[12:17]
    scratch = refs[17:]
    _adaln_kernel(c_ref, wada_ref, bada_ref, mod_ref)
    _ssmprep_kernel(*prep_in, *prep_out, *scratch, lam_init=lam_init, chunk=chunk)


def _prep(cond, w_ada, b_ada, a_re, a_im, log_dt, b_re, b_im, c_re, c_im, lq, lam_init):
    rows, d = cond.shape
    n = w_ada.shape[1]
    nd, g, p = a_re.shape
    hc = b_re.shape[-1]
    t = SSM_CHUNK
    sw = nd * p
    gb = SCAN_SLAB // hc
    steps = g // gb
    bk = d // steps
    row = lambda a: a.astype(F32).transpose(1, 0, 2).reshape(g, 1, sw)
    ldt = jnp.broadcast_to(log_dt[..., None], (nd, g, p))
    bt = lambda a: a.astype(F32).transpose(1, 3, 0, 2).reshape(g, hc, sw)
    ct = lambda a: a.astype(F32).transpose(1, 2, 0, 3).reshape(g, hc, sw)
    sds = jax.ShapeDtypeStruct
    nk = t * hc
    blk = lambda shape: pl.BlockSpec((gb,) + shape[1:], lambda i: (i,) + (0,) * (len(shape) - 1))
    mat = (g, nk, 2 * sw)
    mod, toept, wt, vt, a_t, lam = pl.pallas_call(
        functools.partial(_prep_kernel, lam_init=lam_init, chunk=t),
        grid=(steps,),
        in_specs=[pl.BlockSpec((rows, bk), lambda i: (0, i)),
                  pl.BlockSpec((bk, n), lambda i: (i, 0)),
                  pl.BlockSpec((1, n), lambda i: (0, 0))]
                 + [blk((g, 1, sw))] * 3 + [blk((g, hc, sw))] * 4 + [pl.BlockSpec(lq.shape, lambda i: (0, 0))],
        out_specs=[pl.BlockSpec((rows, n), lambda i: (0, 0)),
                   blk((g, nk, nk)), blk((g, 2 * sw, nk)), blk(mat), blk((g, 2, 1, sw)),
                   pl.BlockSpec(VREG_TILE, lambda i: (0, 0))],
        out_shape=(sds((rows, n), F32),
                   sds((g, nk, nk), BF16), sds((g, 2 * sw, nk), BF16), sds(mat, BF16),
                   sds((g, 2, 1, sw), F32), sds(VREG_TILE, F32)),
        scratch_shapes=[pltpu.VMEM((gb, 2 * nk, sw), F32), pltpu.VMEM((gb, 2 * nk, sw), F32),
                        pltpu.VMEM((gb, nk, 2 * sw), F32), pltpu.VMEM((gb, nk, 2 * sw), F32),
                        pltpu.VMEM((gb, nk, nk), F32)],
        compiler_params=_params("arbitrary"),
        name="prep",
    )(cond, w_ada, b_ada.reshape(1, n),
      row(a_re), row(a_im), row(ldt), bt(b_re), bt(b_im), ct(c_re), ct(c_im), lq.astype(F32))
    return mod, (toept, wt, vt, a_t), lam


def _to_scan_layout(u, o_ref, batch):
    n_slabs, t, _, cw = o_ref.shape
    tp = u.shape[0] // batch
    u_tb = jnp.swapaxes(u.reshape(batch, tp, u.shape[-1]), 0, 1)
    for c in range(tp // t):
        for s in range(t):
            for v in range(n_slabs):
                o_ref[v, s, c * batch:(c + 1) * batch, :] = u_tb[c * t + s][:, v * cw:(v + 1) * cw].astype(o_ref.dtype)


def _scan_block(ref):
    return jnp.concatenate([ref[v] for v in range(ref.shape[0])], axis=-1).astype(F32)


def _from_scan_layout(ref, batch):
    t, rows, d = ref.shape
    tp = rows // batch * t
    u_tb = jnp.stack([ref[s, c * batch:(c + 1) * batch, :] for c in range(tp // t) for s in range(t)], axis=0)
    return jnp.swapaxes(u_tb, 0, 1).reshape(batch * tp, d)


def _inproj_kernel(*refs, rope, emit_caches, d_attn):
    n_in = 7 if rope else 5
    x_ref, shift_ref, scale_ref, g_ref, w_ref = refs[:5]
    cos_ref, sin_ref = refs[5:7] if rope else (None, None)
    q_ref, k_ref, v_ref, ga_ref, u_ref, gs_ref = refs[n_in:n_in + 6]
    kc_ref, vc_ref = refs[n_in + 6:] if emit_caches else (None, None)
    x = x_ref[...]
    nb, tp, d = x.shape
    ms = jnp.mean(x * x, axis=-1, keepdims=True)
    h = x * lax.rsqrt(ms + EPS) * (g_ref[...] * (1.0 + scale_ref[...])) + shift_ref[...]
    hb = h.reshape(nb * tp, d).astype(BF16)
    proj = lambda lo, hi: jnp.dot(hb, w_ref[:, lo:hi].astype(BF16), preferred_element_type=F32)
    da = d_attn
    d_ssm = u_ref.shape[0] * u_ref.shape[-1]
    tok = lambda z: z.reshape(nb, tp, z.shape[-1])
    q = proj(0, da)
    k = proj(da, 2 * da)
    if rope:
        lane = lax.broadcasted_iota(jnp.int32, (1, da), 1)
        low = (lane % (QK_DIM // 2)) < (QK_DIM // 4)
        cos = cos_ref[...]
        sin = sin_ref[...]

        def rot(z):
            partner = jnp.where(low, pltpu.roll(z, da - QK_DIM // 4, 1), pltpu.roll(z, QK_DIM // 4, 1))
            return tok(z) * cos + tok(partner) * sin

        q = rot(q)
        k = rot(k)
    else:
        q = tok(q)
        k = tok(k)
    q_ref[...] = (q * Q_PRESCALE).astype(q_ref.dtype)
    k_ref[...] = k.astype(k_ref.dtype)
    n_heads = da // V_DIM
    if emit_caches:
        for hd in range(n_heads):
            kc_ref[:, pl.ds(hd, tp, stride=n_heads), :] = k[:, :, hd * V_DIM:(hd + 1) * V_DIM]
    v = tok(proj(2 * da, 3 * da))
    v_ref[...] = v.astype(v_ref.dtype)
    if emit_caches:
        for hd in range(n_heads):
            vc_ref[:, pl.ds(hd, tp, stride=n_heads), :] = v[:, :, hd * V_DIM:(hd + 1) * V_DIM]
    ga_ref[...] = tok(_silu(proj(3 * da, 4 * da))).astype(ga_ref.dtype)
    gs_ref[...] = tok(_silu(proj(4 * da + d_ssm, 4 * da + 2 * d_ssm))).astype(gs_ref.dtype)
    _to_scan_layout(proj(4 * da, 4 * da + d_ssm), u_ref, nb)


def _inproj(x, shift, scale, norm_pre, w_in, rope_tabs, emit_caches, d_attn, d_ssm):
    b, l, d = x.shape
    tp = TOKEN_TILE // b
    t = SSM_CHUNK
    rope = rope_tabs is not None
    const = lambda a: pl.BlockSpec(a.shape, lambda j: (0,) * a.ndim)
    tok = lambda w: pl.BlockSpec((b, tp, w), lambda j: (0, j, 0))
    g2 = norm_pre.reshape(1, d).astype(F32)
    in_specs = [tok(d), const(shift), const(scale), const(g2)]
    args = [x, shift, scale, g2]
    if rope:
        in_specs += [pl.BlockSpec((tp, d_attn), lambda j: (j, 0))] * 2
        args += list(rope_tabs)
    scan_rows = tp // t * b
    attn_shape = jax.ShapeDtypeStruct((b, l, d_attn), BF16)
    out_specs = [tok(d_attn), tok(d_attn), tok(d_attn), tok(d_attn),
                 pl.BlockSpec((d_ssm // SCAN_SLAB, t, scan_rows, SCAN_SLAB), lambda j: (0, 0, j, 0)),
                 tok(d_ssm)]
    out_shape = [attn_shape, attn_shape, attn_shape, attn_shape,
                 jax.ShapeDtypeStruct((d_ssm // SCAN_SLAB, t, l // t * b, SCAN_SLAB), BF16),
                 jax.ShapeDtypeStruct((b, l, d_ssm), BF16)]
    if emit_caches:
        n_heads = d_attn // V_DIM
        out_specs += [pl.BlockSpec((b, tp * n_heads, V_DIM), lambda j: (0, j, 0))] * 2
        out_shape += [jax.ShapeDtypeStruct((b, l * n_heads, V_DIM), F32)] * 2
    body = functools.partial(_inproj_kernel, rope=rope, emit_caches=emit_caches, d_attn=d_attn)

    def stream(w_ref, *refs):
        step = lambda x_ref, shift_ref, scale_ref, g_ref, *rest: body(x_ref, shift_ref, scale_ref, g_ref,
                                                                       w_ref, *rest)
        pltpu.emit_pipeline(step, grid=(l // tp,), in_specs=in_specs, out_specs=out_specs)(*refs)

    anywhere = pl.BlockSpec(memory_space=pl.ANY)
    return pl.pallas_call(
        stream,
        in_specs=[pl.BlockSpec(memory_space=pltpu.VMEM)] + [anywhere] * len(in_specs),
        out_specs=[anywhere] * len(out_specs),
        out_shape=out_shape,
        compiler_params=_params(),
        name="inproj_rope" if rope else "inproj",
    )(w_in, *args)


def _rope_tables(l, d_attn):
    rows = l // GRID_W
    row = np.repeat(np.arange(rows, dtype=np.float64), GRID_W)
    col = np.tile(np.arange(GRID_W, dtype=np.float64), rows)
    n_freq = QK_DIM // 4
    inv = ROPE_THETA ** (-np.arange(n_freq, dtype=np.float64) / n_freq)
    ang_r = row[:, None] * inv
    ang_c = col[:, None] * inv
    cos64 = np.concatenate([np.cos(ang_r), np.cos(ang_r), np.cos(ang_c), np.cos(ang_c)], axis=-1)
    sin64 = np.concatenate([-np.sin(ang_r), np.sin(ang_r), -np.sin(ang_c), np.sin(ang_c)], axis=-1)
    reps = d_attn // QK_DIM
    return (jnp.asarray(np.tile(cos64, (1, reps)), dtype=F32),
            jnp.asarray(np.tile(sin64, (1, reps)), dtype=F32))


def _head(ref, bi, h, n_heads):
    if ref.shape[-1] == V_DIM:
        return ref[bi, pl.ds(h, ref.shape[1] // n_heads, stride=n_heads), :]
    return ref[bi, :, h * V_DIM:(h + 1) * V_DIM]


def _attn_kernel(*refs, has_cache, lam_init):
    if has_cache:
        lam_ref, q_ref, k_ref, v_ref, ck_ref, cv_ref, ga_ref, sub_ref, o_ref, k_scr, vt_scr = refs
    else:
        lam_ref, q_ref, k_ref, v_ref, ga_ref, sub_ref, o_ref, k_scr, vt_scr = refs
    nbk, tq = q_ref.shape[0], q_ref.shape[1]
    n_heads, lk = k_scr.shape[0] // nbk, k_scr.shape[1]
    ln = k_ref.shape[1]
    lam = lam_ref[0:1, 0:1]
    lane = lax.broadcasted_iota(jnp.int32, (1, V_DIM), 1)
    first = lane < QK_DIM
    dims = (((1,), (1,)), ((), ()))
    slot = lambda bi, h: bi * n_heads + h

    @pl.when(pl.program_id(1) == 0)
    def _():
        for bi in range(nbk):
            for h in range(n_heads):
                n = slot(bi, h)
                if has_cache:
                    k_scr[n, 0:lk - ln, :] = _head(ck_ref, bi, h, n_heads).astype(BF16)
                    vt_scr[n, 0:V_DIM, 0:lk - ln] = _head(cv_ref, bi, h, n_heads).astype(F32).T.astype(BF16)
                k_scr[n, lk - ln:lk, :] = _head(k_ref, bi, h, n_heads).astype(BF16)
                vt_scr[n, 0:V_DIM, lk - ln:lk] = _head(v_ref, bi, h, n_heads).astype(F32).T.astype(BF16)
                vt_scr[n, V_DIM:, :] = jnp.ones((vt_scr.shape[1] - V_DIM, lk), BF16)

    sub = min(tq, ATTN_Q_SUB)
    items = [(bi, h, r) for bi in range(nbk) for h in range(n_heads) for r in range(0, tq, sub)]

    def scores(item):
        bi, h, r = item
        q = _head(q_ref, bi, h, n_heads)[r:r + sub, :]
        zero = jnp.zeros_like(q)
        k = k_scr[slot(bi, h)]
        return (lax.dot_general(k, jnp.where(first, q, zero), dims, preferred_element_type=F32),
                lax.dot_general(k, jnp.where(first, zero, q), dims, preferred_element_type=F32))

    nxt = scores(items[0])
    for n, (bi, h, r) in enumerate(items):
        st1, st2 = nxt
        if n + 1 < len(items):
            nxt = scores(items[n + 1])
        vt = vt_scr[slot(bi, h)]
        m1 = jnp.max(st1, axis=0, keepdims=True)
        m2 = jnp.max(st2, axis=0, keepdims=True)
        o1 = o2 = None
        for lo in range(0, lk, ATTN_KEY_BLOCK):
            hi = min(lk, lo + ATTN_KEY_BLOCK)
            e1 = jnp.exp2(st1[lo:hi, :] - m1).astype(BF16)
            e2 = jnp.exp2(st2[lo:hi, :] - m2).astype(BF16)
            p1 = jnp.dot(vt[:, lo:hi], e1, preferred_element_type=F32)
            p2 = jnp.dot(vt[:, lo:hi], e2, preferred_element_type=F32)
            o1 = p1 if o1 is None else o1 + p1
            o2 = p2 if o2 is None else o2 + p2
        r1 = 1.0 / o1[V_DIM:V_DIM + 1, :]
        r2 = lam / o2[V_DIM:V_DIM + 1, :]
        o = (o1[0:V_DIM, :] * r1 - o2[0:V_DIM, :] * r2).T
        ms = jnp.mean(o * o, axis=-1, keepdims=True)
        o = o * lax.rsqrt(ms + EPS) * sub_ref[...] * (1.0 - lam_init)
        ga = _head(ga_ref, bi, h, n_heads)[r:r + sub, :].astype(F32)
        o_ref[bi, r:r + sub, h * V_DIM:(h + 1) * V_DIM] = (o * ga).astype(o_ref.dtype)


def _attention(lam, q, k, v, cache_k, cache_v, ga, subln, lam_init):
    b, l, d_attn = q.shape
    n_heads = d_attn // V_DIM
    tq = min(ATTN_Q_TILE, l)
    nq = l // tq
    nbk = max(1, ATTN_Q_TILE // l)
    has_cache = cache_k is not None
    q_spec = pl.BlockSpec((nbk, tq, d_attn), lambda i, j: (i, j, 0))
    whole = lambda a: pl.BlockSpec((nbk,) + a.shape[1:], lambda i, j: (i,) + (0,) * (a.ndim - 1))
    in_specs = [pl.BlockSpec(VREG_TILE, lambda i, j: (0, 0)), q_spec, whole(k), whole(v)]
    args = [lam, q, k, v]
    if has_cache:
        in_specs += [whole(cache_k), whole(cache_v)]
        args += [cache_k, cache_v]
    in_specs += [q_spec, pl.BlockSpec((1, V_DIM), lambda i, j: (0, 0))]
    args += [ga, subln.reshape(1, V_DIM).astype(F32)]
    lk = l + (cache_k.shape[1] // n_heads if has_cache else 0)
    return dict(
        kernel=functools.partial(_attn_kernel, has_cache=has_cache, lam_init=lam_init),
        grid=(b // nbk, nq), in_specs=in_specs, args=args,
        out_specs=[q_spec], out_shape=[jax.ShapeDtypeStruct((b, l, d_attn), BF16)],
        scratch=[pltpu.VMEM((nbk * n_heads, lk, V_DIM), BF16),
                 pltpu.VMEM((nbk * n_heads, V_DIM + SUM_ROWS, lk), BF16)],
        name="attn_cache" if has_cache else "attn")


def _launch(*parts):
    grid = parts[0]["grid"]
    assert all(p["grid"] == grid for p in parts)
    n_in = [len(p["in_specs"]) for p in parts]
    n_out = [len(p["out_specs"]) for p in parts]
    n_scr = [len(p["scratch"]) for p in parts]

    def body(*refs):
        ins, outs, scrs = refs[:sum(n_in)], refs[sum(n_in):sum(n_in) + sum(n_out)], refs[sum(n_in) + sum(n_out):]
        for n, p in enumerate(parts):
            take = lambda seq, counts: seq[sum(counts[:n]):sum(counts[:n + 1])]
            p["kernel"](*take(ins, n_in), *take(outs, n_out), *take(scrs, n_scr))

    outs = pl.pallas_call(
        body,
        grid=grid,
        in_specs=[spec for p in parts for spec in p["in_specs"]],
        out_specs=[spec for p in parts for spec in p["out_specs"]],
        out_shape=[shape for p in parts for shape in p["out_shape"]],
        scratch_shapes=[scr for p in parts for scr in p["scratch"]],
        compiler_params=_params(*(["arbitrary"] * len(grid))),
        name="_".join(p["name"] for p in parts),
    )(*[a for p in parts for a in p["args"]])
    return [outs[sum(n_out[:n]):sum(n_out[:n + 1])] for n in range(len(parts))]


def _ssm_kernel(*refs, batch, n_chunks, with_state):
    if with_state:
        u_ref, toept_ref, wt_ref, vt_ref, a_ref, h0_ref, y_ref, fin_ref, zt_scr, wc_scr, sin_scr, yt_scr = refs
    else:
        u_ref, toept_ref, wt_ref, vt_ref, a_ref, h0_ref, y_ref, zt_scr, wc_scr, sin_scr, yt_scr = refs
    t = u_ref.shape[0]
    gb = zt_scr.shape[0]
    hc = zt_scr.shape[1] // t
    sw = a_ref.shape[-1]
    hw = sw // 2
    for s in range(t):
        at = u_ref[s].T
        for g in range(gb):
            zt_scr[g, s * hc:(s + 1) * hc, :] = at[g * hc:(g + 1) * hc, :]
    for g in range(gb):
        wc_scr[g] = jnp.dot(wt_ref[g], zt_scr[g], preferred_element_type=F32).astype(BF16).T.astype(F32)
    fwd = lax.broadcasted_iota(jnp.int32, (1, sw), 1) < hw

    def step(j, carry):
        rf = pl.ds(pl.multiple_of(j * batch, batch), batch)
        rb = pl.ds(pl.multiple_of((n_chunks - 1 - j) * batch, batch), batch)
        out = []
        for g in range(gb):
            sr, si = carry[g]
            sin_scr[g, rf, 0:hw] = sr[:, 0:hw]
            sin_scr[g, rb, hw:sw] = sr[:, hw:sw]
            sin_scr[g, rf, sw:sw + hw] = si[:, 0:hw]
            sin_scr[g, rb, sw + hw:2 * sw] = si[:, hw:sw]
            wr = jnp.where(fwd, wc_scr[g, rf, 0:sw], wc_scr[g, rb, 0:sw])
            wi = jnp.where(fwd, wc_scr[g, rf, sw:2 * sw], wc_scr[g, rb, sw:2 * sw])
            ar = a_ref[g, 0]
            ai = a_ref[g, 1]
            out.append((ar * sr - ai * si + wr, ar * si + ai * sr + wi))
        return tuple(out)

    fin = lax.fori_loop(0, n_chunks, step, tuple((h0_ref[g, 0], h0_ref[g, 1]) for g in range(gb)),
                        unroll=True)
    for g in range(gb):
        sint = sin_scr[g].astype(BF16).T
        y2t = (jnp.dot(toept_ref[g], zt_scr[g], preferred_element_type=F32)
               + jnp.dot(vt_ref[g], sint, preferred_element_type=F32))
        for s in range(t):
            yt_scr[s, g * hc:(g + 1) * hc, :] = y2t[s * hc:(s + 1) * hc, :].astype(yt_scr.dtype)
        if with_state:
            fin_ref[g, 0] = fin[g][0]
            fin_ref[g, 1] = fin[g][1]
    for s in range(t):
        y_ref[s] = yt_scr[s].T


def _ssm(u3, ssm_ops, h0, with_state):
    toept, wt, vt, a_t = ssm_ops
    n_slabs, t, rows, cw = u3.shape
    batch = h0.shape[2]
    n_chunks = rows // batch
    g = toept.shape[0]
    gb = g // n_slabs
    slab = pl.BlockSpec((None, t, rows, cw), lambda i, j: (i, 0, 0, 0))
    blk = lambda a: pl.BlockSpec((gb,) + a.shape[1:], lambda i, j: (i,) + (0,) * (a.ndim - 1))
    out_specs = [slab]
    out_shape = [jax.ShapeDtypeStruct(u3.shape, u3.dtype)]
    if with_state:
        out_specs.append(blk(h0))
        out_shape.append(jax.ShapeDtypeStruct(h0.shape, F32))
    nk = toept.shape[-1]
    return dict(
        kernel=functools.partial(_ssm_kernel, batch=batch, n_chunks=n_chunks, with_state=with_state),
        grid=(g // gb, 1), in_specs=[slab, blk(toept), blk(wt), blk(vt), blk(a_t), blk(h0)],
        args=[u3, toept, wt, vt, a_t, h0], out_specs=out_specs, out_shape=out_shape,
        scratch=[pltpu.VMEM((gb, nk, rows), BF16),
                 pltpu.VMEM((gb, rows, wt.shape[1]), F32),
                 pltpu.VMEM((gb, rows, wt.shape[1]), F32),
                 pltpu.VMEM((t, cw, rows), u3.dtype)],
        name="ssm_state" if with_state else "ssm")


def _state_to_lanes(h0):
    b, nd, nc, g, p = h0.shape
    return h0.astype(F32).transpose(3, 2, 0, 1, 4).reshape(g, nc, b, nd * p)


def _state_from_lanes(fin, nd):
    g, nc, b, w = fin.shape
    return fin.reshape(g, nc, b, nd, w // nd).transpose(2, 3, 1, 0, 4)


def _outproj_kernel(x_ref, a_ref, y_ref, u_ref, gs_ref, gate_ref, dskip_ref,
                    wglu_ref, bglu_ref, wout_ref, g_ref, o_ref):
    nb, tp, d = x_ref.shape
    y = _scan_block(y_ref) + dskip_ref[...] * _scan_block(u_ref)
    t, srows, d_ssm = y.shape
    rows = lambda ref: ref[...].reshape(nb * tp, ref.shape[-1])
    y = y.reshape(t * srows, d_ssm)
    c0 = math.sqrt(2.0 / math.pi)
    hy = 0.5 * y
    ys = hy + hy * jnp.tanh(y * (c0 + (c0 * 0.044715) * (y * y)))
    z = jnp.dot(ys.astype(BF16), wglu_ref[...], preferred_element_type=F32) + bglu_ref[...]
    ys = _from_scan_layout((ys * jax.nn.sigmoid(z)).reshape(t, srows, d_ssm), nb) * rows(gs_ref).astype(F32)
    cat = jnp.concatenate([rows(a_ref), ys.astype(BF16)], axis=-1)
    out = jnp.dot(cat, wout_ref[...], preferred_element_type=F32)
    ms = jnp.mean(out * out, axis=-1, keepdims=True)
    out = out * lax.rsqrt(ms + EPS)
    o_ref[...] = x_ref[...] + (g_ref[...] * gate_ref[...]) * out.reshape(nb, tp, d)


def _outproj(x, a_out, y_ssm, u, gs, gate, d_skip, w_glu_bf, b_glu, w_out_bf, norm_post):
    b, l, d = x.shape
    d_attn = a_out.shape[-1]
    d_ssm = gs.shape[-1]
    tp = TOKEN_TILE // b
    t = SSM_CHUNK
    const = lambda a: pl.BlockSpec(a.shape, lambda j: (0,) * a.ndim)
    tok = lambda w: pl.BlockSpec((b, tp, w), lambda j: (0, j, 0))
    scan = pl.BlockSpec((d_ssm // SCAN_SLAB, t, tp // t * b, SCAN_SLAB), lambda j: (0, 0, j, 0))
    consts = [gate, d_skip.reshape(1, d_ssm).astype(F32), w_glu_bf, b_glu.reshape(1, d_ssm).astype(F32),
              w_out_bf, norm_post.reshape(1, d).astype(F32)]
    in_specs = [tok(d), tok(d_attn), scan, scan, tok(d_ssm)] + [const(a) for a in consts]
    stream = pltpu.emit_pipeline(_outproj_kernel, grid=(l // tp,), in_specs=in_specs, out_specs=[tok(d)])
    anywhere = pl.BlockSpec(memory_space=pl.ANY)
    return pl.pallas_call(
        lambda *refs: stream(*refs),
        in_specs=[anywhere] * len(in_specs),
        out_specs=anywhere,
        out_shape=jax.ShapeDtypeStruct((b, l, d), F32),
        compiler_params=_params(),
        name="outproj",
    )(x, a_out, y_ssm, u, gs, *consts)


def _mixer(x, mod, h0_lanes, cache_k, cache_v, rope_tabs, emit_caches, with_state, lam, lam_init,
           norm_pre, w_in, ssm_ops, subln, d_skip, w_glu_bf, b_glu, w_out_bf, norm_post):
    b, l, d = x.shape
    d_ssm = w_glu_bf.shape[0]
    d_attn = (w_in.shape[1] - 2 * d_ssm) // 4
    shift, scale, gate = mod
    q, k, v, ga, u, gs, *caches = _inproj(x, shift, scale, norm_pre, w_in, rope_tabs, emit_caches,
                                          d_attn, d_ssm)
    attn = _attention(lam, q, k, v, cache_k, cache_v, ga, subln, lam_init)
    scan = _ssm(u, ssm_ops, h0_lanes, with_state)
    if attn["grid"] == scan["grid"]:
        (a_out,), outs = _launch(attn, scan)
    else:
        (a_out,), = _launch(attn)
        outs, = _launch(scan)
    y = _outproj(x, a_out, outs[0], u, gs, gate, d_skip, w_glu_bf, b_glu, w_out_bf, norm_post)
    fin = outs[1] if with_state else None
    return y, caches, fin


def kernel(x_prompt, x_sample, cache_k, cache_v, state_ssm, c, c_ctx, w_ada, b_ada, norm_pre, norm_post, w_in, lambda_qk, subln, ssm_A_re, ssm_A_im, ssm_log_dt, ssm_B_re, ssm_B_im, ssm_C_re, ssm_C_im, ssm_D, w_glu, b_glu, w_out):
    xp, xs = x_prompt, x_sample
    bp, lp, d = xp.shape
    bs, ls, _ = xs.shape
    depth = w_in.shape[0]
    nd = ssm_A_re.shape[1]
    g, p = ssm_A_re.shape[-2:]
    d_ssm = w_glu.shape[-1]
    d_attn = (w_in.shape[-1] - 2 * d_ssm) // 4
    assert bs + 1 <= COND_ROWS and nd == 2
    assert d_ssm % SCAN_SLAB == 0 and SCAN_SLAB % (d_ssm // g) == 0
    assert all(TOKEN_TILE % b == 0 and (TOKEN_TILE // b) % SSM_CHUNK == 0 and l % (TOKEN_TILE // b) == 0
               for b, l in ((bp, lp), (bs, ls)))
    assert all(l % min(ATTN_Q_TILE, l) == 0 and min(ATTN_Q_TILE, l) % min(ATTN_Q_SUB, l) == 0
               and b % max(1, ATTN_Q_TILE // l) == 0 for b, l in ((bp, lp), (bs, ls)))
    rope_tabs = _rope_tables(ls, d_attn)
    cond = jnp.zeros((COND_ROWS, d), F32).at[:bs].set(c.astype(F32)).at[bs].set(c_ctx.astype(F32))
    new_k, new_v, new_s = [], [], []
    for layer in range(depth):
        lam_init = 0.8 - 0.6 * math.exp(-0.3 * layer)
        m, ssm_ops, lam = _prep(
            cond, w_ada[layer], b_ada[layer],
            ssm_A_re[layer], ssm_A_im[layer], ssm_log_dt[layer], ssm_B_re[layer], ssm_B_im[layer],
            ssm_C_re[layer], ssm_C_im[layer], lambda_qk[layer], lam_init)
        mod_s = tuple(m[:bs, i * d:(i + 1) * d].reshape(bs, 1, d) for i in range(3))
        mod_p = tuple(m[bs:bs + 1, i * d:(i + 1) * d].reshape(1, 1, d) for i in range(3))
        shared = (lam, lam_init, norm_pre[layer], w_in[layer], ssm_ops, subln[layer],
                  ssm_D[layer], w_glu[layer].astype(BF16), b_glu[layer], w_out[layer].astype(BF16),
                  norm_post[layer])

        h0_p = jnp.zeros((g, 2, bp, nd * p), F32)
        xp, (k_p, v_p), fin = _mixer(xp, mod_p, h0_p, None, None, None, True, True, *shared)
        new_k.append(k_p.reshape(bp, lp, d_attn // V_DIM, V_DIM))
        new_v.append(v_p.reshape(bp, lp, d_attn // V_DIM, V_DIM))
        new_s.append(_state_from_lanes(fin, nd).astype(xp.dtype))

        h0_s = _state_to_lanes(state_ssm[:, layer])
        ck = cache_k[:, layer].reshape(bs, -1, V_DIM)
        cv = cache_v[:, layer].reshape(bs, -1, V_DIM)
        xs, _, _ = _mixer(xs, mod_s, h0_s, ck, cv, rope_tabs, False, False, *shared)

    return (xp, xs, jnp.stack(new_k, axis=1), jnp.stack(new_v, axis=1), jnp.stack(new_s, axis=1))
```

```python
import functools
import math

import jax
import jax.numpy as jnp
import numpy as np
from jax import lax
from jax.experimental import pallas as pl
from jax.experimental.pallas import tpu as pltpu

F32 = jnp.float32
BF16 = jnp.bfloat16

GRID_W = 64
QK_DIM = 64
V_DIM = 2 * QK_DIM
ROPE_THETA = 10000.0
EPS = 1e-6
Q_PRESCALE = math.log2(math.e) * QK_DIM ** -0.5

VREG_TILE = (8, 128)
SUM_ROWS = 16
COND_ROWS = 16
TOKEN_TILE = 1024
ATTN_Q_TILE = 1024
ATTN_Q_SUB = 512
ATTN_KEY_BLOCK = 256
SSM_CHUNK = 16
SCAN_SLAB = 128
VMEM_LIMIT = 48 * 1024 * 1024


def _silu(x):
    return x * jax.nn.sigmoid(x)


def _params(*sem):
    return pltpu.CompilerParams(dimension_semantics=sem, vmem_limit_bytes=VMEM_LIMIT)


def _adaln_kernel(c_ref, w_ref, b_ref, o_ref):
    @pl.when(pl.program_id(0) == 0)
    def _():
        o_ref[...] = jnp.broadcast_to(b_ref[...], o_ref.shape)

    s = _silu(c_ref[...]).astype(BF16)
    o_ref[...] += jnp.dot(s, w_ref[...].astype(BF16), preferred_element_type=F32)


def _ssmprep_kernel(are_ref, aim_ref, ldt_ref, bre_ref, bim_ref, cre_ref, cim_ref, lq_ref,
                    toept_ref, wt_ref, vt_ref, a_ref, lam_ref,
                    cplre_scr, cplimn_scr, w_scr, v_scr, toep_scr, *, lam_init, chunk):
    a_re = are_ref[...]
    a_im = aim_ref[...]
    dt = jnp.exp(ldt_ref[...])
    mag = jnp.exp(a_re * dt)
    ab_re = mag * jnp.cos(a_im * dt)
    ab_im = mag * jnp.sin(a_im * dt)
    nr, ni = ab_re - 1.0, ab_im
    den = a_re * a_re + a_im * a_im
    f_re = (nr * a_re + ni * a_im) / den
    f_im = (ni * a_re - nr * a_im) / den
    b_re = bre_ref[...]
    b_im = bim_ref[...]
    bb_re = f_re * b_re - f_im * b_im
    bb_im = f_re * b_im + f_im * b_re
    c_re = cre_ref[...]
    c_im = cim_ref[...]
    hc = b_re.shape[1]
    sw = a_re.shape[-1]
    hw = sw // 2
    lo, hi = slice(0, hw), slice(hw, sw)
    lo_im, hi_im = slice(sw, sw + hw), slice(sw + hw, 2 * sw)
    blk = lambda j: slice(j * hc, (j + 1) * hc)
    last = chunk - 1
    cplre_scr[...] = jnp.zeros_like(cplre_scr)
    cplimn_scr[...] = jnp.zeros_like(cplimn_scr)
    pw_re = jnp.ones_like(ab_re)
    pw_im = jnp.zeros_like(ab_im)
    for tau in range(chunk + 1):
        cp_re = c_re * pw_re - c_im * pw_im
        cp_imn = -(c_re * pw_im + c_im * pw_re)
        if tau < chunk:
            cplre_scr[:, blk(last + tau), lo] = cp_re[:, :, lo]
            cplre_scr[:, blk(last - tau), hi] = cp_re[:, :, hi]
            cplimn_scr[:, blk(last + tau), lo] = cp_imn[:, :, lo]
            cplimn_scr[:, blk(last - tau), hi] = cp_imn[:, :, hi]
            e_re = pw_re * bb_re - pw_im * bb_im
            e_im = pw_re * bb_im + pw_im * bb_re
            w_scr[:, blk(last - tau), lo] = e_re[:, :, lo]
            w_scr[:, blk(tau), hi] = e_re[:, :, hi]
            w_scr[:, blk(last - tau), lo_im] = e_im[:, :, lo]
            w_scr[:, blk(tau), hi_im] = e_im[:, :, hi]
        if tau >= 1:
            v_scr[:, blk(tau - 1), lo] = cp_re[:, :, lo]
            v_scr[:, blk(chunk - tau), hi] = cp_re[:, :, hi]
            v_scr[:, blk(tau - 1), lo_im] = cp_imn[:, :, lo]
            v_scr[:, blk(chunk - tau), hi_im] = cp_imn[:, :, hi]
        if tau < chunk:
            pw_re, pw_im = pw_re * ab_re - pw_im * ab_im, pw_re * ab_im + pw_im * ab_re
    a_ref[:, 0] = pw_re
    a_ref[:, 1] = pw_im
    dims = (((2,), (2,)), ((0,), (0,)))
    taps = (lax.dot_general(bb_re.astype(BF16), cplre_scr[...].astype(BF16), dims,
                            preferred_element_type=F32)
            + lax.dot_general(bb_im.astype(BF16), cplimn_scr[...].astype(BF16), dims,
                              preferred_element_type=F32))
    nk = chunk * hc
    for s in range(chunk):
        toep_scr[:, blk(s), :] = taps[:, :, (last - s) * hc:(last - s) * hc + nk]
    for g in range(toept_ref.shape[0]):
        toept_ref[g] = toep_scr[g].astype(toept_ref.dtype).T
        wt_ref[g] = w_scr[g].astype(wt_ref.dtype).T
    vt_ref[...] = v_scr[...].astype(vt_ref.dtype)
    lq = lq_ref[...]
    s01 = jnp.sum(lq[0:1] * lq[1:2], axis=-1, keepdims=True)
    s23 = jnp.sum(lq[2:3] * lq[3:4], axis=-1, keepdims=True)
    lam = jnp.exp(s01) - jnp.exp(s23) + lam_init
    lam_ref[...] = jnp.broadcast_to(lam, lam_ref.shape)


def _prep_kernel(*refs, lam_init, chunk):
    c_ref, wada_ref, bada_ref = refs[0:3]
    prep_in = refs[3:11]
    mod_ref = refs[11]
    prep_out = refs[12:17]
    scratch = refs[17:]
    _adaln_kernel(c_ref, wada_ref, bada_ref, mod_ref)
    _ssmprep_kernel(*prep_in, *prep_out, *scratch, lam_init=lam_init, chunk=chunk)


def _prep(cond, w_ada, b_ada, a_re, a_im, log_dt, b_re, b_im, c_re, c_im, lq, lam_init):
    rows, d = cond.shape
    n = w_ada.shape[1]
    nd, g, p = a_re.shape
    hc = b_re.shape[-1]
    t = SSM_CHUNK
    sw = nd * p
    gb = SCAN_SLAB // hc
    steps = g // gb
    bk = d // steps
    row = lambda a: a.astype(F32).transpose(1, 0, 2).reshape(g, 1, sw)
    ldt = jnp.broadcast_to(log_dt[..., None], (nd, g, p))
    bt = lambda a: a.astype(F32).transpose(1, 3, 0, 2).reshape(g, hc, sw)
    ct = lambda a: a.astype(F32).transpose(1, 2, 0, 3).reshape(g, hc, sw)
    sds = jax.ShapeDtypeStruct
    nk = t * hc
    blk = lambda shape: pl.BlockSpec((gb,) + shape[1:], lambda i: (i,) + (0,) * (len(shape) - 1))
    mat = (g, nk, 2 * sw)
    mod, toept, wt, vt, a_t, lam = pl.pallas_call(
        functools.partial(_prep_kernel, lam_init=lam_init, chunk=t),
        grid=(steps,),
        in_specs=[pl.BlockSpec((rows, bk), lambda i: (0, i)),
                  pl.BlockSpec((bk, n), lambda i: (i, 0)),
                  pl.BlockSpec((1, n), lambda i: (0, 0))]
                 + [blk((g, 1, sw))] * 3 + [blk((g, hc, sw))] * 4 + [pl.BlockSpec(lq.shape, lambda i: (0, 0))],
        out_specs=[pl.BlockSpec((rows, n), lambda i: (0, 0)),
                   blk((g, nk, nk)), blk((g, 2 * sw, nk)), blk(mat), blk((g, 2, 1, sw)),
                   pl.BlockSpec(VREG_TILE, lambda i: (0, 0))],
        out_shape=(sds((rows, n), F32),
                   sds((g, nk, nk), BF16), sds((g, 2 * sw, nk), BF16), sds(mat, BF16),
                   sds((g, 2, 1, sw), F32), sds(VREG_TILE, F32)),
        scratch_shapes=[pltpu.VMEM((gb, 2 * nk, sw), F32), pltpu.VMEM((gb, 2 * nk, sw), F32),
                        pltpu.VMEM((gb, nk, 2 * sw), F32), pltpu.VMEM((gb, nk, 2 * sw), F32),
                        pltpu.VMEM((gb, nk, nk), F32)],
        compiler_params=_params("arbitrary"),
        name="prep",
    )(cond, w_ada, b_ada.reshape(1, n),
      row(a_re), row(a_im), row(ldt), bt(b_re), bt(b_im), ct(c_re), ct(c_im), lq.astype(F32))
    return mod, (toept, wt, vt, a_t), lam


def _to_scan_layout(u, o_ref, batch):
    n_slabs, t, _, cw = o_ref.shape
    tp = u.shape[0] // batch
    u_tb = jnp.swapaxes(u.reshape(batch, tp, u.shape[-1]), 0, 1)
    for c in range(tp // t):
        for s in range(t):
            for v in range(n_slabs):
                o_ref[v, s, c * batch:(c + 1) * batch, :] = u_tb[c * t + s][:, v * cw:(v + 1) * cw].astype(o_ref.dtype)


def _scan_block(ref):
    return jnp.concatenate([ref[v] for v in range(ref.shape[0])], axis=-1).astype(F32)


def _from_scan_layout(ref, batch):
    t, rows, d = ref.shape
    tp = rows // batch * t
    u_tb = jnp.stack([ref[s, c * batch:(c + 1) * batch, :] for c in range(tp // t) for s in range(t)], axis=0)
    return jnp.swapaxes(u_tb, 0, 1).reshape(batch * tp, d)


def _inproj_kernel(*refs, rope, emit_caches, d_attn):
    n_in = 7 if rope else 5
    x_ref, shift_ref, scale_ref, g_ref, w_ref = refs[:5]
    cos_ref, sin_ref = refs[5:7] if rope else (None, None)
    q_ref, k_ref, v_ref, ga_ref, u_ref, gs_ref = refs[n_in:n_in + 6]
    kc_ref, vc_ref = refs[n_in + 6:] if emit_caches else (None, None)
    x = x_ref[...]
    nb, tp, d = x.shape
    ms = jnp.mean(x * x, axis=-1, keepdims=True)
    h = x * lax.rsqrt(ms + EPS) * (g_ref[...] * (1.0 + scale_ref[...])) + shift_ref[...]
    hb = h.reshape(nb * tp, d).astype(BF16)
    proj = lambda lo, hi: jnp.dot(hb, w_ref[:, lo:hi].astype(BF16), preferred_element_type=F32)
    da = d_attn
    d_ssm = u_ref.shape[0] * u_ref.shape[-1]
    tok = lambda z: z.reshape(nb, tp, z.shape[-1])
    q = proj(0, da)
    k = proj(da, 2 * da)
    if rope:
        lane = lax.broadcasted_iota(jnp.int32, (1, da), 1)
        low = (lane % (QK_DIM // 2)) < (QK_DIM // 4)
        cos = cos_ref[...]
        sin = sin_ref[...]

        def rot(z):
            partner = jnp.where(low, pltpu.roll(z, da - QK_DIM // 4, 1), pltpu.roll(z, QK_DIM // 4, 1))
            return tok(z) * cos + tok(partner) * sin

        q = rot(q)
        k = rot(k)
    else:
        q = tok(q)
        k = tok(k)
    q_ref[...] = (q * Q_PRESCALE).astype(q_ref.dtype)
    k_ref[...] = k.astype(k_ref.dtype)
    n_heads = da // V_DIM
    if emit_caches:
        for hd in range(n_heads):
            kc_ref[:, pl.ds(hd, tp, stride=n_heads), :] = k[:, :, hd * V_DIM:(hd + 1) * V_DIM]
    v = tok(proj(2 * da, 3 * da))
    v_ref[...] = v.astype(v_ref.dtype)
    if emit_caches:
        for hd in range(n_heads):
            vc_ref[:, pl.ds(hd, tp, stride=n_heads), :] = v[:, :, hd * V_DIM:(hd + 1) * V_DIM]
    ga_ref[...] = tok(_silu(proj(3 * da, 4 * da))).astype(ga_ref.dtype)
    gs_ref[...] = tok(_silu(proj(4 * da + d_ssm, 4 * da + 2 * d_ssm))).astype(gs_ref.dtype)
    _to_scan_layout(proj(4 * da, 4 * da + d_ssm), u_ref, nb)


def _inproj(x, shift, scale, norm_pre, w_in, rope_tabs, emit_caches, d_attn, d_ssm):
    b, l, d = x.shape
    tp = TOKEN_TILE // b
    t = SSM_CHUNK
    rope = rope_tabs is not None
    const = lambda a: pl.BlockSpec(a.shape, lambda j: (0,) * a.ndim)
    tok = lambda w: pl.BlockSpec((b, tp, w), lambda j: (0, j, 0))
    g2 = norm_pre.reshape(1, d).astype(F32)
    w_spec = pl.BlockSpec(w_in.shape, lambda j: (0, 0), pipeline_mode=pl.Buffered(1))
    in_specs = [tok(d), const(shift), const(scale), const(g2), w_spec]
    args = [x, shift, scale, g2, w_in]
    if rope:
        in_specs += [pl.BlockSpec((tp, d_attn), lambda j: (j, 0))] * 2
        args += list(rope_tabs)
    scan_rows = tp // t * b
    attn_shape = jax.ShapeDtypeStruct((b, l, d_attn), BF16)
    out_specs = [tok(d_attn), tok(d_attn), tok(d_attn), tok(d_attn),
                 pl.BlockSpec((d_ssm // SCAN_SLAB, t, scan_rows, SCAN_SLAB), lambda j: (0, 0, j, 0)),
                 tok(d_ssm)]
    out_shape = [attn_shape, attn_shape, attn_shape, attn_shape,
                 jax.ShapeDtypeStruct((d_ssm // SCAN_SLAB, t, l // t * b, SCAN_SLAB), BF16),
                 jax.ShapeDtypeStruct((b, l, d_ssm), BF16)]
    if emit_caches:
        n_heads = d_attn // V_DIM
        out_specs += [pl.BlockSpec((b, tp * n_heads, V_DIM), lambda j: (0, j, 0))] * 2
        out_shape += [jax.ShapeDtypeStruct((b, l * n_heads, V_DIM), F32)] * 2
    return pl.pallas_call(
        functools.partial(_inproj_kernel, rope=rope, emit_caches=emit_caches, d_attn=d_attn),
        grid=(l // tp,),
        in_specs=in_specs,
        out_specs=out_specs,
        out_shape=out_shape,
        compiler_params=_params("arbitrary"),
        name="inproj_rope" if rope else "inproj",
    )(*args)


def _rope_tables(l, d_attn):
    rows = l // GRID_W
    row = np.repeat(np.arange(rows, dtype=np.float64), GRID_W)
    col = np.tile(np.arange(GRID_W, dtype=np.float64), rows)
    n_freq = QK_DIM // 4
    inv = ROPE_THETA ** (-np.arange(n_freq, dtype=np.float64) / n_freq)
    ang_r = row[:, None] * inv
    ang_c = col[:, None] * inv
    cos64 = np.concatenate([np.cos(ang_r), np.cos(ang_r), np.cos(ang_c), np.cos(ang_c)], axis=-1)
    sin64 = np.concatenate([-np.sin(ang_r), np.sin(ang_r), -np.sin(ang_c), np.sin(ang_c)], axis=-1)
    reps = d_attn // QK_DIM
    return (jnp.asarray(np.tile(cos64, (1, reps)), dtype=F32),
            jnp.asarray(np.tile(sin64, (1, reps)), dtype=F32))


def _head(ref, bi, h, n_heads):
    if ref.shape[-1] == V_DIM:
        return ref[bi, pl.ds(h, ref.shape[1] // n_heads, stride=n_heads), :]
    return ref[bi, :, h * V_DIM:(h + 1) * V_DIM]


def _attn_kernel(*refs, has_cache, lam_init):
    if has_cache:
        lam_ref, q_ref, k_ref, v_ref, ck_ref, cv_ref, ga_ref, sub_ref, o_ref, k_scr, vt_scr = refs
    else:
        lam_ref, q_ref, k_ref, v_ref, ga_ref, sub_ref, o_ref, k_scr, vt_scr = refs
    nbk, tq = q_ref.shape[0], q_ref.shape[1]
    n_heads, lk = k_scr.shape[0] // nbk, k_scr.shape[1]
    ln = k_ref.shape[1]
    lam = lam_ref[0:1, 0:1]
    lane = lax.broadcasted_iota(jnp.int32, (1, V_DIM), 1)
    first = lane < QK_DIM
    dims = (((1,), (1,)), ((), ()))
    slot = lambda bi, h: bi * n_heads + h

    @pl.when(pl.program_id(1) == 0)
    def _():
        for bi in range(nbk):
            for h in range(n_heads):
                n = slot(bi, h)
                if has_cache:
                    k_scr[n, 0:lk - ln, :] = _head(ck_ref, bi, h, n_heads).astype(BF16)
                    vt_scr[n, 0:V_DIM, 0:lk - ln] = _head(cv_ref, bi, h, n_heads).astype(F32).T.astype(BF16)
                k_scr[n, lk - ln:lk, :] = _head(k_ref, bi, h, n_heads).astype(BF16)
                vt_scr[n, 0:V_DIM, lk - ln:lk] = _head(v_ref, bi, h, n_heads).astype(F32).T.astype(BF16)
                vt_scr[n, V_DIM:, :] = jnp.ones((vt_scr.shape[1] - V_DIM, lk), BF16)

    sub = min(tq, ATTN_Q_SUB)
    items = [(bi, h, r) for bi in range(nbk) for h in range(n_heads) for r in range(0, tq, sub)]

    def scores(item):
        bi, h, r = item
        q = _head(q_ref, bi, h, n_heads)[r:r + sub, :]
        zero = jnp.zeros_like(q)
        k = k_scr[slot(bi, h)]
        return (lax.dot_general(k, jnp.where(first, q, zero), dims, preferred_element_type=F32),
                lax.dot_general(k, jnp.where(first, zero, q), dims, preferred_element_type=F32))

    nxt = scores(items[0])
    for n, (bi, h, r) in enumerate(items):
        st1, st2 = nxt
        if n + 1 < len(items):
            nxt = scores(items[n + 1])
        vt = vt_scr[slot(bi, h)]
        m1 = jnp.max(st1, axis=0, keepdims=True)
        m2 = jnp.max(st2, axis=0, keepdims=True)
        o1 = o2 = None
        for lo in range(0, lk, ATTN_KEY_BLOCK):
            hi = min(lk, lo + ATTN_KEY_BLOCK)
            e1 = jnp.exp2(st1[lo:hi, :] - m1).astype(BF16)
            e2 = jnp.exp2(st2[lo:hi, :] - m2).astype(BF16)
            p1 = jnp.dot(vt[:, lo:hi], e1, preferred_element_type=F32)
            p2 = jnp.dot(vt[:, lo:hi], e2, preferred_element_type=F32)
            o1 = p1 if o1 is None else o1 + p1
            o2 = p2 if o2 is None else o2 + p2
        r1 = 1.0 / o1[V_DIM:V_DIM + 1, :]
        r2 = lam / o2[V_DIM:V_DIM + 1, :]
        o = (o1[0:V_DIM, :] * r1 - o2[0:V_DIM, :] * r2).T
        ms = jnp.mean(o * o, axis=-1, keepdims=True)
        o = o * lax.rsqrt(ms + EPS) * sub_ref[...] * (1.0 - lam_init)
        ga = _head(ga_ref, bi, h, n_heads)[r:r + sub, :].astype(F32)
        o_ref[bi, r:r + sub, h * V_DIM:(h + 1) * V_DIM] = (o * ga).astype(o_ref.dtype)


def _attention(lam, q, k, v, cache_k, cache_v, ga, subln, lam_init):
    b, l, d_attn = q.shape
    n_heads = d_attn // V_DIM
    tq = min(ATTN_Q_TILE, l)
    nq = l // tq
    nbk = max(1, ATTN_Q_TILE // l)
    has_cache = cache_k is not None
    q_spec = pl.BlockSpec((nbk, tq, d_attn), lambda i, j: (i, j, 0))
    whole = lambda a: pl.BlockSpec((nbk,) + a.shape[1:], lambda i, j: (i,) + (0,) * (a.ndim - 1))
    in_specs = [pl.BlockSpec(VREG_TILE, lambda i, j: (0, 0)), q_spec, whole(k), whole(v)]
    args = [lam, q, k, v]
    if has_cache:
        in_specs += [whole(cache_k), whole(cache_v)]
        args += [cache_k, cache_v]
    in_specs += [q_spec, pl.BlockSpec((1, V_DIM), lambda i, j: (0, 0))]
    args += [ga, subln.reshape(1, V_DIM).astype(F32)]
    lk = l + (cache_k.shape[1] // n_heads if has_cache else 0)
    return dict(
        kernel=functools.partial(_attn_kernel, has_cache=has_cache, lam_init=lam_init),
        grid=(b // nbk, nq), in_specs=in_specs, args=args,
        out_specs=[q_spec], out_shape=[jax.ShapeDtypeStruct((b, l, d_attn), BF16)],
        scratch=[pltpu.VMEM((nbk * n_heads, lk, V_DIM), BF16),
                 pltpu.VMEM((nbk * n_heads, V_DIM + SUM_ROWS, lk), BF16)],
        name="attn_cache" if has_cache else "attn")


def _launch(*parts):
    grid = parts[0]["grid"]
    assert all(p["grid"] == grid for p in parts)
    n_in = [len(p["in_specs"]) for p in parts]
    n_out = [len(p["out_specs"]) for p in parts]
    n_scr = [len(p["scratch"]) for p in parts]

    def body(*refs):
        ins, outs, scrs = refs[:sum(n_in)], refs[sum(n_in):sum(n_in) + sum(n_out)], refs[sum(n_in) + sum(n_out):]
        for n, p in enumerate(parts):
            take = lambda seq, counts: seq[sum(counts[:n]):sum(counts[:n + 1])]
            p["kernel"](*take(ins, n_in), *take(outs, n_out), *take(scrs, n_scr))

    outs = pl.pallas_call(
        body,
        grid=grid,
        in_specs=[spec for p in parts for spec in p["in_specs"]],
        out_specs=[spec for p in parts for spec in p["out_specs"]],
        out_shape=[shape for p in parts for shape in p["out_shape"]],
        scratch_shapes=[scr for p in parts for scr in p["scratch"]],
        compiler_params=_params(*(["arbitrary"] * len(grid))),
        name="_".join(p["name"] for p in parts),
    )(*[a for p in parts for a in p["args"]])
    return [outs[sum(n_out[:n]):sum(n_out[:n + 1])] for n in range(len(parts))]


def _ssm_kernel(*refs, batch, n_chunks, with_state):
    if with_state:
        u_ref, toept_ref, wt_ref, vt_ref, a_ref, h0_ref, y_ref, fin_ref, zt_scr, wc_scr, sin_scr, yt_scr = refs
    else:
        u_ref, toept_ref, wt_ref, vt_ref, a_ref, h0_ref, y_ref, zt_scr, wc_scr, sin_scr, yt_scr = refs
    t = u_ref.shape[0]
    gb = zt_scr.shape[0]
    hc = zt_scr.shape[1] // t
    sw = a_ref.shape[-1]
    hw = sw // 2
    for s in range(t):
        at = u_ref[s].T
        for g in range(gb):
            zt_scr[g, s * hc:(s + 1) * hc, :] = at[g * hc:(g + 1) * hc, :]
    for g in range(gb):
        wc_scr[g] = jnp.dot(wt_ref[g], zt_scr[g], preferred_element_type=F32).astype(BF16).T.astype(F32)
    fwd = lax.broadcasted_iota(jnp.int32, (1, sw), 1) < hw

    def step(j, carry):
        rf = pl.ds(pl.multiple_of(j * batch, batch), batch)
        rb = pl.ds(pl.multiple_of((n_chunks - 1 - j) * batch, batch), batch)
        out = []
        for g in range(gb):
            sr, si = carry[g]
            sin_scr[g, rf, 0:hw] = sr[:, 0:hw]
            sin_scr[g, rb, hw:sw] = sr[:, hw:sw]
            sin_scr[g, rf, sw:sw + hw] = si[:, 0:hw]
            sin_scr[g, rb, sw + hw:2 * sw] = si[:, hw:sw]
            wr = jnp.where(fwd, wc_scr[g, rf, 0:sw], wc_scr[g, rb, 0:sw])
            wi = jnp.where(fwd, wc_scr[g, rf, sw:2 * sw], wc_scr[g, rb, sw:2 * sw])
            ar = a_ref[g, 0]
            ai = a_ref[g, 1]
            out.append((ar * sr - ai * si + wr, ar * si + ai * sr + wi))
        return tuple(out)

    fin = lax.fori_loop(0, n_chunks, step, tuple((h0_ref[g, 0], h0_ref[g, 1]) for g in range(gb)),
                        unroll=True)
    for g in range(gb):
        sint = sin_scr[g].astype(BF16).T
        y2t = (jnp.dot(toept_ref[g], zt_scr[g], preferred_element_type=F32)
               + jnp.dot(vt_ref[g], sint, preferred_element_type=F32))
        for s in range(t):
            yt_scr[s, g * hc:(g + 1) * hc, :] = y2t[s * hc:(s + 1) * hc, :].astype(yt_scr.dtype)
        if with_state:
            fin_ref[g, 0] = fin[g][0]
            fin_ref[g, 1] = fin[g][1]
    for s in range(t):
        y_ref[s] = yt_scr[s].T


def _ssm(u3, ssm_ops, h0, with_state):
    toept, wt, vt, a_t = ssm_ops
    n_slabs, t, rows, cw = u3.shape
    batch = h0.shape[2]
    n_chunks = rows // batch
    g = toept.shape[0]
    gb = g // n_slabs
    slab = pl.BlockSpec((None, t, rows, cw), lambda i, j: (i, 0, 0, 0))
    blk = lambda a: pl.BlockSpec((gb,) + a.shape[1:], lambda i, j: (i,) + (0,) * (a.ndim - 1))
    out_specs = [slab]
    out_shape = [jax.ShapeDtypeStruct(u3.shape, u3.dtype)]
    if with_state:
        out_specs.append(blk(h0))
        out_shape.append(jax.ShapeDtypeStruct(h0.shape, F32))
    nk = toept.shape[-1]
    return dict(
        kernel=functools.partial(_ssm_kernel, batch=batch, n_chunks=n_chunks, with_state=with_state),
        grid=(g // gb, 1), in_specs=[slab, blk(toept), blk(wt), blk(vt), blk(a_t), blk(h0)],
        args=[u3, toept, wt, vt, a_t, h0], out_specs=out_specs, out_shape=out_shape,
        scratch=[pltpu.VMEM((gb, nk, rows), BF16),
                 pltpu.VMEM((gb, rows, wt.shape[1]), F32),
                 pltpu.VMEM((gb, rows, wt.shape[1]), F32),
                 pltpu.VMEM((t, cw, rows), u3.dtype)],
        name="ssm_state" if with_state else "ssm")


def _state_to_lanes(h0):
    b, nd, nc, g, p = h0.shape
    return h0.astype(F32).transpose(3, 2, 0, 1, 4).reshape(g, nc, b, nd * p)


def _state_from_lanes(fin, nd):
    g, nc, b, w = fin.shape
    return fin.reshape(g, nc, b, nd, w // nd).transpose(2, 3, 1, 0, 4)


def _outproj_kernel(x_ref, a_ref, y_ref, u_ref, gs_ref, gate_ref, dskip_ref,
                    wglu_ref, bglu_ref, wout_ref, g_ref, o_ref):
    nb, tp, d = x_ref.shape
    y = _scan_block(y_ref) + dskip_ref[...] * _scan_block(u_ref)
    t, srows, d_ssm = y.shape
    rows = lambda ref: ref[...].reshape(nb * tp, ref.shape[-1])
    y = y.reshape(t * srows, d_ssm)
    c0 = math.sqrt(2.0 / math.pi)
    hy = 0.5 * y
    ys = hy + hy * jnp.tanh(y * (c0 + (c0 * 0.044715) * (y * y)))
    z = jnp.dot(ys.astype(BF16), wglu_ref[...], preferred_element_type=F32) + bglu_ref[...]
    ys = _from_scan_layout((ys * jax.nn.sigmoid(z)).reshape(t, srows, d_ssm), nb) * rows(gs_ref).astype(F32)
    cat = jnp.concatenate([rows(a_ref), ys.astype(BF16)], axis=-1)
    out = jnp.dot(cat, wout_ref[...], preferred_element_type=F32)
    ms = jnp.mean(out * out, axis=-1, keepdims=True)
    out = out * lax.rsqrt(ms + EPS)
    o_ref[...] = x_ref[...] + (g_ref[...] * gate_ref[...]) * out.reshape(nb, tp, d)


def _outproj(x, a_out, y_ssm, u, gs, gate, d_skip, w_glu_bf, b_glu, w_out_bf, norm_post):
    b, l, d = x.shape
    d_attn = a_out.shape[-1]
    d_ssm = gs.shape[-1]
    tp = TOKEN_TILE // b
    t = SSM_CHUNK
    const = lambda a: pl.BlockSpec(a.shape, lambda j: (0,) * a.ndim)
    tok = lambda w: pl.BlockSpec((b, tp, w), lambda j: (0, j, 0))
    scan = pl.BlockSpec((d_ssm // SCAN_SLAB, t, tp // t * b, SCAN_SLAB), lambda j: (0, 0, j, 0))
    consts = [gate, d_skip.reshape(1, d_ssm).astype(F32), w_glu_bf, b_glu.reshape(1, d_ssm).astype(F32),
              w_out_bf, norm_post.reshape(1, d).astype(F32)]
    in_specs = [tok(d), tok(d_attn), scan, scan, tok(d_ssm)] + [const(a) for a in consts]
    stream = pltpu.emit_pipeline(_outproj_kernel, grid=(l // tp,), in_specs=in_specs, out_specs=[tok(d)])
    anywhere = pl.BlockSpec(memory_space=pl.ANY)
    return pl.pallas_call(
        lambda *refs: stream(*refs),
        in_specs=[anywhere] * len(in_specs),
        out_specs=anywhere,
        out_shape=jax.ShapeDtypeStruct((b, l, d), F32),
        compiler_params=_params(),
        name="outproj",
    )(x, a_out, y_ssm, u, gs, *consts)


def _mixer(x, mod, h0_lanes, cache_k, cache_v, rope_tabs, emit_caches, with_state, lam, lam_init,
           norm_pre, w_in, ssm_ops, subln, d_skip, w_glu_bf, b_glu, w_out_bf, norm_post):
    b, l, d = x.shape
    d_ssm = w_glu_bf.shape[0]
    d_attn = (w_in.shape[1] - 2 * d_ssm) // 4
    shift, scale, gate = mod
    q, k, v, ga, u, gs, *caches = _inproj(x, shift, scale, norm_pre, w_in, rope_tabs, emit_caches,
                                          d_attn, d_ssm)
    attn = _attention(lam, q, k, v, cache_k, cache_v, ga, subln, lam_init)
    scan = _ssm(u, ssm_ops, h0_lanes, with_state)
    if attn["grid"] == scan["grid"]:
        outs, (a_out,) = _launch(scan, attn)
    else:
        (a_out,), = _launch(attn)
        outs, = _launch(scan)
    y = _outproj(x, a_out, outs[0], u, gs, gate, d_skip, w_glu_bf, b_glu, w_out_bf, norm_post)
    fin = outs[1] if with_state else None
    return y, caches, fin


def kernel(x_prompt, x_sample, cache_k, cache_v, state_ssm, c, c_ctx, w_ada, b_ada, norm_pre, norm_post, w_in, lambda_qk, subln, ssm_A_re, ssm_A_im, ssm_log_dt, ssm_B_re, ssm_B_im, ssm_C_re, ssm_C_im, ssm_D, w_glu, b_glu, w_out):
    xp, xs = x_prompt, x_sample
    bp, lp, d = xp.shape
    bs, ls, _ = xs.shape
    depth = w_in.shape[0]
    nd = ssm_A_re.shape[1]
    g, p = ssm_A_re.shape[-2:]
    d_ssm = w_glu.shape[-1]
    d_attn = (w_in.shape[-1] - 2 * d_ssm) // 4
    assert bs + 1 <= COND_ROWS and nd == 2
    assert d_ssm % SCAN_SLAB == 0 and SCAN_SLAB % (d_ssm // g) == 0
    assert all(TOKEN_TILE % b == 0 and (TOKEN_TILE // b) % SSM_CHUNK == 0 and l % (TOKEN_TILE // b) == 0
               for b, l in ((bp, lp), (bs, ls)))
    assert all(l % min(ATTN_Q_TILE, l) == 0 and min(ATTN_Q_TILE, l) % min(ATTN_Q_SUB, l) == 0
               and b % max(1, ATTN_Q_TILE // l) == 0 for b, l in ((bp, lp), (bs, ls)))
    rope_tabs = _rope_tables(ls, d_attn)
    cond = jnp.zeros((COND_ROWS, d), F32).at[:bs].set(c.astype(F32)).at[bs].set(c_ctx.astype(F32))
    new_k, new_v, new_s = [], [], []
    for layer in range(depth):
        lam_init = 0.8 - 0.6 * math.exp(-0.3 * layer)
        m, ssm_ops, lam = _prep(
            cond, w_ada[layer], b_ada[layer],
            ssm_A_re[layer], ssm_A_im[layer], ssm_log_dt[layer], ssm_B_re[layer], ssm_B_im[layer],
            ssm_C_re[layer], ssm_C_im[layer], lambda_qk[layer], lam_init)
        mod_s = tuple(m[:bs, i * d:(i + 1) * d].reshape(bs, 1, d) for i in range(3))
        mod_p = tuple(m[bs:bs + 1, i * d:(i + 1) * d].reshape(1, 1, d) for i in range(3))
        shared = (lam, lam_init, norm_pre[layer], w_in[layer], ssm_ops, subln[layer],
                  ssm_D[layer], w_glu[layer].astype(BF16), b_glu[layer], w_out[layer].astype(BF16),
                  norm_post[layer])

        h0_p = jnp.zeros((g, 2, bp, nd * p), F32)
        xp, (k_p, v_p), fin = _mixer(xp, mod_p, h0_p, None, None, None, True, True, *shared)
        new_k.append(k_p.reshape(bp, lp, d_attn // V_DIM, V_DIM))
        new_v.append(v_p.reshape(bp, lp, d_attn // V_DIM, V_DIM))
        new_s.append(_state_from_lanes(fin, nd).astype(xp.dtype))

        h0_s = _state_to_lanes(state_ssm[:, layer])
        ck = cache_k[:, layer].reshape(bs, -1, V_DIM)
        cv = cache_v[:, layer].reshape(bs, -1, V_DIM)
        xs, _, _ = _mixer(xs, mod_s, h0_s, ck, cv, rope_tabs, False, False, *shared)

    return (xp, xs, jnp.stack(new_k, axis=1), jnp.stack(new_v, axis=1), jnp.stack(new_s, axis=1))
```

```python
import functools
import math

import jax
import jax.numpy as jnp
import numpy as np
from jax import lax
from jax.experimental import pallas as pl
from jax.experimental.pallas import tpu as pltpu

F32 = jnp.float32
BF16 = jnp.bfloat16

GRID_W = 64
QK_DIM = 64
V_DIM = 2 * QK_DIM
ROPE_THETA = 10000.0
EPS = 1e-6
Q_PRESCALE = math.log2(math.e) * QK_DIM ** -0.5

VREG_TILE = (8, 128)
SUM_ROWS = 16
COND_ROWS = 16
TOKEN_TILE = 1024
ATTN_Q_TILE = 1024
ATTN_Q_SUB = 512
ATTN_KEY_BLOCK = 256
SSM_CHUNK = 16
SCAN_SLAB = 128
VMEM_LIMIT = 48 * 1024 * 1024


def _silu(x):
    return x * jax.nn.sigmoid(x)


def _params(*sem):
    return pltpu.CompilerParams(dimension_semantics=sem, vmem_limit_bytes=VMEM_LIMIT)


def _adaln_kernel(c_ref, w_ref, b_ref, o_ref):
    @pl.when(pl.program_id(0) == 0)
    def _():
        o_ref[...] = jnp.broadcast_to(b_ref[...], o_ref.shape)

    s = _silu(c_ref[...]).astype(BF16)
    o_ref[...] += jnp.dot(s, w_ref[...].astype(BF16), preferred_element_type=F32)


def _ssmprep_kernel(are_ref, aim_ref, ldt_ref, bre_ref, bim_ref, cre_ref, cim_ref, lq_ref,
                    toept_ref, wt_ref, vt_ref, a_ref, lam_ref,
                    cplre_scr, cplimn_scr, w_scr, v_scr, toep_scr, *, lam_init, chunk):
    a_re = are_ref[...]
    a_im = aim_ref[...]
    dt = jnp.exp(ldt_ref[...])
    mag = jnp.exp(a_re * dt)
    ab_re = mag * jnp.cos(a_im * dt)
    ab_im = mag * jnp.sin(a_im * dt)
    nr, ni = ab_re - 1.0, ab_im
    den = a_re * a_re + a_im * a_im
    f_re = (nr * a_re + ni * a_im) / den
    f_im = (ni * a_re - nr * a_im) / den
    b_re = bre_ref[...]
    b_im = bim_ref[...]
    bb_re = f_re * b_re - f_im * b_im
    bb_im = f_re * b_im + f_im * b_re
    c_re = cre_ref[...]
    c_im = cim_ref[...]
    hc = b_re.shape[1]
    sw = a_re.shape[-1]
    hw = sw // 2
    lo, hi = slice(0, hw), slice(hw, sw)
    lo_im, hi_im = slice(sw, sw + hw), slice(sw + hw, 2 * sw)
    blk = lambda j: slice(j * hc, (j + 1) * hc)
    last = chunk - 1
    cplre_scr[...] = jnp.zeros_like(cplre_scr)
    cplimn_scr[...] = jnp.zeros_like(cplimn_scr)
    pw_re = jnp.ones_like(ab_re)
    pw_im = jnp.zeros_like(ab_im)
    for tau in range(chunk + 1):
        cp_re = c_re * pw_re - c_im * pw_im
        cp_imn = -(c_re * pw_im + c_im * pw_re)
        if tau < chunk:
            cplre_scr[:, blk(last + tau), lo] = cp_re[:, :, lo]
            cplre_scr[:, blk(last - tau), hi] = cp_re[:, :, hi]
            cplimn_scr[:, blk(last + tau), lo] = cp_imn[:, :, lo]
            cplimn_scr[:, blk(last - tau), hi] = cp_imn[:, :, hi]
            e_re = pw_re * bb_re - pw_im * bb_im
            e_im = pw_re * bb_im + pw_im * bb_re
            w_scr[:, blk(last - tau), lo] = e_re[:, :, lo]
            w_scr[:, blk(tau), hi] = e_re[:, :, hi]
            w_scr[:, blk(last - tau), lo_im] = e_im[:, :, lo]
            w_scr[:, blk(tau), hi_im] = e_im[:, :, hi]
        if tau >= 1:
            v_scr[:, blk(tau - 1), lo] = cp_re[:, :, lo]
            v_scr[:, blk(chunk - tau), hi] = cp_re[:, :, hi]
            v_scr[:, blk(tau - 1), lo_im] = cp_imn[:, :, lo]
            v_scr[:, blk(chunk - tau), hi_im] = cp_imn[:, :, hi]
        if tau < chunk:
            pw_re, pw_im = pw_re * ab_re - pw_im * ab_im, pw_re * ab_im + pw_im * ab_re
    a_ref[:, 0] = pw_re
    a_ref[:, 1] = pw_im
    dims = (((2,), (2,)), ((0,), (0,)))
    taps = (lax.dot_general(bb_re.astype(BF16), cplre_scr[...].astype(BF16), dims,
                            preferred_element_type=F32)
            + lax.dot_general(bb_im.astype(BF16), cplimn_scr[...].astype(BF16), dims,
                              preferred_element_type=F32))
    nk = chunk * hc
    for s in range(chunk):
        toep_scr[:, blk(s), :] = taps[:, :, (last - s) * hc:(last - s) * hc + nk]
    for g in range(toept_ref.shape[0]):
        toept_ref[g] = toep_scr[g].astype(toept_ref.dtype).T
        wt_ref[g] = w_scr[g].astype(wt_ref.dtype).T
    vt_ref[...] = v_scr[...].astype(vt_ref.dtype)
    lq = lq_ref[...]
    s01 = jnp.sum(lq[0:1] * lq[1:2], axis=-1, keepdims=True)
    s23 = jnp.sum(lq[2:3] * lq[3:4], axis=-1, keepdims=True)
    lam = jnp.exp(s01) - jnp.exp(s23) + lam_init
    lam_ref[...] = jnp.broadcast_to(lam, lam_ref.shape)


def _prep_kernel(*refs, lam_init, chunk):
    c_ref, wada_ref, bada_ref = refs[0:3]
    prep_in = refs[3:11]
    mod_ref = refs[11]
    prep_out = refs[12:17]
    scratch = refs[17:]
    _adaln_kernel(c_ref, wada_ref, bada_ref, mod_ref)
    _ssmprep_kernel(*prep_in, *prep_out, *scratch, lam_init=lam_init, chunk=chunk)


def _prep(cond, w_ada, b_ada, a_re, a_im, log_dt, b_re, b_im, c_re, c_im, lq, lam_init):
    rows, d = cond.shape
    n = w_ada.shape[1]
    nd, g, p = a_re.shape
    hc = b_re.shape[-1]
    t = SSM_CHUNK
    sw = nd * p
    gb = SCAN_SLAB // hc
    steps = g // gb
    bk = d // steps
    row = lambda a: a.astype(F32).transpose(1, 0, 2).reshape(g, 1, sw)
    ldt = jnp.broadcast_to(log_dt[..., None], (nd, g, p))
    bt = lambda a: a.astype(F32).transpose(1, 3, 0, 2).reshape(g, hc, sw)
    ct = lambda a: a.astype(F32).transpose(1, 2, 0, 3).reshape(g, hc, sw)
    sds = jax.ShapeDtypeStruct
    nk = t * hc
    blk = lambda shape: pl.BlockSpec((gb,) + shape[1:], lambda i: (i,) + (0,) * (len(shape) - 1))
    mat = (g, nk, 2 * sw)
    mod, toept, wt, vt, a_t, lam = pl.pallas_call(
        functools.partial(_prep_kernel, lam_init=lam_init, chunk=t),
        grid=(steps,),
        in_specs=[pl.BlockSpec((rows, bk), lambda i: (0, i)),
                  pl.BlockSpec((bk, n), lambda i: (i, 0)),
                  pl.BlockSpec((1, n), lambda i: (0, 0))]
                 + [blk((g, 1, sw))] * 3 + [blk((g, hc, sw))] * 4 + [pl.BlockSpec(lq.shape, lambda i: (0, 0))],
        out_specs=[pl.BlockSpec((rows, n), lambda i: (0, 0)),
                   blk((g, nk, nk)), blk((g, 2 * sw, nk)), blk(mat), blk((g, 2, 1, sw)),
                   pl.BlockSpec(VREG_TILE, lambda i: (0, 0))],
        out_shape=(sds((rows, n), F32),
                   sds((g, nk, nk), BF16), sds((g, 2 * sw, nk), BF16), sds(mat, BF16),
                   sds((g, 2, 1, sw), F32), sds(VREG_TILE, F32)),
        scratch_shapes=[pltpu.VMEM((gb, 2 * nk, sw), F32), pltpu.VMEM((gb, 2 * nk, sw), F32),
                        pltpu.VMEM((gb, nk, 2 * sw), F32), pltpu.VMEM((gb, nk, 2 * sw), F32),
                        pltpu.VMEM((gb, nk, nk), F32)],
        compiler_params=_params("arbitrary"),
        name="prep",
    )(cond, w_ada, b_ada.reshape(1, n),
      row(a_re), row(a_im), row(ldt), bt(b_re), bt(b_im), ct(c_re), ct(c_im), lq.astype(F32))
    return mod, (toept, wt, vt, a_t), lam


def _to_scan_layout(u, o_ref, batch):
    n_slabs, t, _, cw = o_ref.shape
    tp = u.shape[0] // batch
    u_tb = jnp.swapaxes(u.reshape(batch, tp, u.shape[-1]), 0, 1)
    for c in range(tp // t):
        for s in range(t):
            for v in range(n_slabs):
                o_ref[v, s, c * batch:(c + 1) * batch, :] = u_tb[c * t + s][:, v * cw:(v + 1) * cw].astype(o_ref.dtype)


def _scan_block(ref):
    return jnp.concatenate([ref[v] for v in range(ref.shape[0])], axis=-1).astype(F32)


def _from_scan_layout(ref, batch):
    t, rows, d = ref.shape
    tp = rows // batch * t
    u_tb = jnp.stack([ref[s, c * batch:(c + 1) * batch, :] for c in range(tp // t) for s in range(t)], axis=0)
    return jnp.swapaxes(u_tb, 0, 1).reshape(batch * tp, d)


def _inproj_kernel(*refs, rope, emit_caches, d_attn):
    n_in = 7 if rope else 5
    x_ref, shift_ref, scale_ref, g_ref, w_ref = refs[:5]
    cos_ref, sin_ref = refs[5:7] if rope else (None, None)
    q_ref, k_ref, v_ref, ga_ref, u_ref, gs_ref = refs[n_in:n_in + 6]
    kc_ref, vc_ref = refs[n_in + 6:] if emit_caches else (None, None)
    x = x_ref[...]
    nb, tp, d = x.shape
    ms = jnp.mean(x * x, axis=-1, keepdims=True)
    h = x * lax.rsqrt(ms + EPS) * (g_ref[...] * (1.0 + scale_ref[...])) + shift_ref[...]
    hb = h.reshape(nb * tp, d).astype(BF16)
    proj = lambda lo, hi: jnp.dot(hb, w_ref[:, lo:hi].astype(BF16), preferred_element_type=F32)
    da = d_attn
    d_ssm = u_ref.shape[0] * u_ref.shape[-1]
    tok = lambda z: z.reshape(nb, tp, z.shape[-1])
    q = proj(0, da)
    k = proj(da, 2 * da)
    if rope:
        lane = lax.broadcasted_iota(jnp.int32, (1, da), 1)
        low = (lane % (QK_DIM // 2)) < (QK_DIM // 4)
        cos = cos_ref[...]
        sin = sin_ref[...]

        def rot(z):
            partner = jnp.where(low, pltpu.roll(z, da - QK_DIM // 4, 1), pltpu.roll(z, QK_DIM // 4, 1))
            return tok(z) * cos + tok(partner) * sin

        q = rot(q)
        k = rot(k)
    else:
        q = tok(q)
        k = tok(k)
    q_ref[...] = (q * Q_PRESCALE).astype(q_ref.dtype)
    k_ref[...] = k.astype(k_ref.dtype)
    n_heads = da // V_DIM
    if emit_caches:
        for hd in range(n_heads):
            kc_ref[:, pl.ds(hd, tp, stride=n_heads), :] = k[:, :, hd * V_DIM:(hd + 1) * V_DIM]
    v = tok(proj(2 * da, 3 * da))
    v_ref[...] = v.astype(v_ref.dtype)
    if emit_caches:
        for hd in range(n_heads):
            vc_ref[:, pl.ds(hd, tp, stride=n_heads), :] = v[:, :, hd * V_DIM:(hd + 1) * V_DIM]
    ga_ref[...] = tok(_silu(proj(3 * da, 4 * da))).astype(ga_ref.dtype)
    gs_ref[...] = tok(_silu(proj(4 * da + d_ssm, 4 * da + 2 * d_ssm))).astype(gs_ref.dtype)
    _to_scan_layout(proj(4 * da, 4 * da + d_ssm), u_ref, nb)


def _inproj(x, shift, scale, norm_pre, w_in, rope_tabs, emit_caches, d_attn, d_ssm):
    b, l, d = x.shape
    tp = TOKEN_TILE // b
    t = SSM_CHUNK
    rope = rope_tabs is not None
    const = lambda a: pl.BlockSpec(a.shape, lambda j: (0,) * a.ndim)
    tok = lambda w: pl.BlockSpec((b, tp, w), lambda j: (0, j, 0))
    g2 = norm_pre.reshape(1, d).astype(F32)
    w_spec = pl.BlockSpec(w_in.shape, lambda j: (0, 0), pipeline_mode=pl.Buffered(1))
    in_specs = [tok(d), const(shift), const(scale), const(g2), w_spec]
    args = [x, shift, scale, g2, w_in]
    if rope:
        in_specs += [pl.BlockSpec((tp, d_attn), lambda j: (j, 0))] * 2
        args += list(rope_tabs)
    scan_rows = tp // t * b
    attn_shape = jax.ShapeDtypeStruct((b, l, d_attn), BF16)
    out_specs = [tok(d_attn), tok(d_attn), tok(d_attn), tok(d_attn),
                 pl.BlockSpec((d_ssm // SCAN_SLAB, t, scan_rows, SCAN_SLAB), lambda j: (0, 0, j, 0)),
                 tok(d_ssm)]
    out_shape = [attn_shape, attn_shape, attn_shape, attn_shape,
                 jax.ShapeDtypeStruct((d_ssm // SCAN_SLAB, t, l // t * b, SCAN_SLAB), BF16),
                 jax.ShapeDtypeStruct((b, l, d_ssm), BF16)]
    if emit_caches:
        n_heads = d_attn // V_DIM
        out_specs += [pl.BlockSpec((b, tp * n_heads, V_DIM), lambda j: (0, j, 0))] * 2
        out_shape += [jax.ShapeDtypeStruct((b, l * n_heads, V_DIM), F32)] * 2
    return pl.pallas_call(
        functools.partial(_inproj_kernel, rope=rope, emit_caches=emit_caches, d_attn=d_attn),
        grid=(l // tp,),
        in_specs=in_specs,
        out_specs=out_specs,
        out_shape=out_shape,
        compiler_params=_params("arbitrary"),
        name="inproj_rope" if rope else "inproj",
    )(*args)


def _rope_tables(l, d_attn):
    rows = l // GRID_W
    row = np.repeat(np.arange(rows, dtype=np.float64), GRID_W)
    col = np.tile(np.arange(GRID_W, dtype=np.float64), rows)
    n_freq = QK_DIM // 4
    inv = ROPE_THETA ** (-np.arange(n_freq, dtype=np.float64) / n_freq)
    ang_r = row[:, None] * inv
    ang_c = col[:, None] * inv
    cos64 = np.concatenate([np.cos(ang_r), np.cos(ang_r), np.cos(ang_c), np.cos(ang_c)], axis=-1)
    sin64 = np.concatenate([-np.sin(ang_r), np.sin(ang_r), -np.sin(ang_c), np.sin(ang_c)], axis=-1)
    reps = d_attn // QK_DIM
    return (jnp.asarray(np.tile(cos64, (1, reps)), dtype=F32),
            jnp.asarray(np.tile(sin64, (1, reps)), dtype=F32))


def _head(ref, bi, h, n_heads):
    if ref.shape[-1] == V_DIM:
        return ref[bi, pl.ds(h, ref.shape[1] // n_heads, stride=n_heads), :]
    return ref[bi, :, h * V_DIM:(h + 1) * V_DIM]


def _attn_kernel(*refs, has_cache, lam_init):
    if has_cache:
        lam_ref, q_ref, k_ref, v_ref, ck_ref, cv_ref, ga_ref, sub_ref, o_ref, k_scr, vt_scr = refs
    else:
        lam_ref, q_ref, k_ref, v_ref, ga_ref, sub_ref, o_ref, k_scr, vt_scr = refs
    nbk, tq = q_ref.shape[0], q_ref.shape[1]
    n_heads, lk = k_scr.shape[0] // nbk, k_scr.shape[1]
    ln = k_ref.shape[1]
    lam = lam_ref[0:1, 0:1]
    lane = lax.broadcasted_iota(jnp.int32, (1, V_DIM), 1)
    first = lane < QK_DIM
    dims = (((1,), (1,)), ((), ()))
    slot = lambda bi, h: bi * n_heads + h

    @pl.when(pl.program_id(1) == 0)
    def _():
        for bi in range(nbk):
            for h in range(n_heads):
                n = slot(bi, h)
                if has_cache:
                    k_scr[n, 0:lk - ln, :] = _head(ck_ref, bi, h, n_heads).astype(BF16)
                    vt_scr[n, 0:V_DIM, 0:lk - ln] = _head(cv_ref, bi, h, n_heads).astype(F32).T.astype(BF16)
                k_scr[n, lk - ln:lk, :] = _head(k_ref, bi, h, n_heads).astype(BF16)
                vt_scr[n, 0:V_DIM, lk - ln:lk] = _head(v_ref, bi, h, n_heads).astype(F32).T.astype(BF16)
                vt_scr[n, V_DIM:, :] = jnp.ones((vt_scr.shape[1] - V_DIM, lk), BF16)

    sub = min(tq, ATTN_Q_SUB)
    items = [(bi, h, r) for bi in range(nbk) for h in range(n_heads) for r in range(0, tq, sub)]

    def scores(item):
        bi, h, r = item
        q = _head(q_ref, bi, h, n_heads)[r:r + sub, :]
        zero = jnp.zeros_like(q)
        k = k_scr[slot(bi, h)]
        return (lax.dot_general(k, jnp.where(first, q, zero), dims, preferred_element_type=F32),
                lax.dot_general(k, jnp.where(first, zero, q), dims, preferred_element_type=F32))

    nxt = scores(items[0])
    for n, (bi, h, r) in enumerate(items):
        st1, st2 = nxt
        if n + 1 < len(items):
            nxt = scores(items[n + 1])
        vt = vt_scr[slot(bi, h)]
        m1 = jnp.max(st1, axis=0, keepdims=True)
        m2 = jnp.max(st2, axis=0, keepdims=True)
        o1 = o2 = None
        for lo in range(0, lk, ATTN_KEY_BLOCK):
            hi = min(lk, lo + ATTN_KEY_BLOCK)
            e1 = jnp.exp2(st1[lo:hi, :] - m1).astype(BF16)
            e2 = jnp.exp2(st2[lo:hi, :] - m2).astype(BF16)
            p1 = jnp.dot(vt[:, lo:hi], e1, preferred_element_type=F32)
            p2 = jnp.dot(vt[:, lo:hi], e2, preferred_element_type=F32)
            o1 = p1 if o1 is None else o1 + p1
            o2 = p2 if o2 is None else o2 + p2
        r1 = 1.0 / o1[V_DIM:V_DIM + 1, :]
        r2 = lam / o2[V_DIM:V_DIM + 1, :]
        o = (o1[0:V_DIM, :] * r1 - o2[0:V_DIM, :] * r2).T
        ms = jnp.mean(o * o, axis=-1, keepdims=True)
        o = o * lax.rsqrt(ms + EPS) * sub_ref[...] * (1.0 - lam_init)
        ga = _head(ga_ref, bi, h, n_heads)[r:r + sub, :].astype(F32)
        o_ref[bi, r:r + sub, h * V_DIM:(h + 1) * V_DIM] = (o * ga).astype(o_ref.dtype)


def _attention(lam, q, k, v, cache_k, cache_v, ga, subln, lam_init):
    b, l, d_attn = q.shape
    n_heads = d_attn // V_DIM
    tq = min(ATTN_Q_TILE, l)
    nq = l // tq
    nbk = max(1, ATTN_Q_TILE // l)
    has_cache = cache_k is not None
    q_spec = pl.BlockSpec((nbk, tq, d_attn), lambda i, j: (i, j, 0))
    whole = lambda a: pl.BlockSpec((nbk,) + a.shape[1:], lambda i, j: (i,) + (0,) * (a.ndim - 1))
    in_specs = [pl.BlockSpec(VREG_TILE, lambda i, j: (0, 0)), q_spec, whole(k), whole(v)]
    args = [lam, q, k, v]
    if has_cache:
        in_specs += [whole(cache_k), whole(cache_v)]
        args += [cache_k, cache_v]
    in_specs += [q_spec, pl.BlockSpec((1, V_DIM), lambda i, j: (0, 0))]
    args += [ga, subln.reshape(1, V_DIM).astype(F32)]
    lk = l + (cache_k.shape[1] // n_heads if has_cache else 0)
    return dict(
        kernel=functools.partial(_attn_kernel, has_cache=has_cache, lam_init=lam_init),
        grid=(b // nbk, nq), in_specs=in_specs, args=args,
        out_specs=[q_spec], out_shape=[jax.ShapeDtypeStruct((b, l, d_attn), BF16)],
        scratch=[pltpu.VMEM((nbk * n_heads, lk, V_DIM), BF16),
                 pltpu.VMEM((nbk * n_heads, V_DIM + SUM_ROWS, lk), BF16)],
        name="attn_cache" if has_cache else "attn")


def _launch(*parts):
    grid = parts[0]["grid"]
    assert all(p["grid"] == grid for p in parts)
    n_in = [len(p["in_specs"]) for p in parts]
    n_out = [len(p["out_specs"]) for p in parts]
    n_scr = [len(p["scratch"]) for p in parts]

    def body(*refs):
        ins, outs, scrs = refs[:sum(n_in)], refs[sum(n_in):sum(n_in) + sum(n_out)], refs[sum(n_in) + sum(n_out):]
        for n, p in enumerate(parts):
            take = lambda seq, counts: seq[sum(counts[:n]):sum(counts[:n + 1])]
            p["kernel"](*take(ins, n_in), *take(outs, n_out), *take(scrs, n_scr))

    outs = pl.pallas_call(
        body,
        grid=grid,
        in_specs=[spec for p in parts for spec in p["in_specs"]],
        out_specs=[spec for p in parts for spec in p["out_specs"]],
        out_shape=[shape for p in parts for shape in p["out_shape"]],
        scratch_shapes=[scr for p in parts for scr in p["scratch"]],
        compiler_params=_params(*(["arbitrary"] * len(grid))),
        name="_".join(p["name"] for p in parts),
    )(*[a for p in parts for a in p["args"]])
    return [outs[sum(n_out[:n]):sum(n_out[:n + 1])] for n in range(len(parts))]


def _ssm_kernel(*refs, batch, n_chunks, with_state):
    if with_state:
        u_ref, toept_ref, wt_ref, vt_ref, a_ref, h0_ref, y_ref, fin_ref, zt_scr, wc_scr, sin_scr, yt_scr = refs
    else:
        u_ref, toept_ref, wt_ref, vt_ref, a_ref, h0_ref, y_ref, zt_scr, wc_scr, sin_scr, yt_scr = refs
    t = u_ref.shape[0]
    gb = zt_scr.shape[0]
    hc = zt_scr.shape[1] // t
    sw = a_ref.shape[-1]
    hw = sw // 2
    for s in range(t):
        at = u_ref[s].T
        for g in range(gb):
            zt_scr[g, s * hc:(s + 1) * hc, :] = at[g * hc:(g + 1) * hc, :]
    for g in range(gb):
        wc_scr[g] = jnp.dot(wt_ref[g], zt_scr[g], preferred_element_type=F32).astype(BF16).T.astype(F32)
    fwd = lax.broadcasted_iota(jnp.int32, (1, sw), 1) < hw

    def step(j, carry):
        rf = pl.ds(pl.multiple_of(j * batch, batch), batch)
        rb = pl.ds(pl.multiple_of((n_chunks - 1 - j) * batch, batch), batch)
        out = []
        for g in range(gb):
            sr, si = carry[g]
            sin_scr[g, rf, 0:hw] = sr[:, 0:hw]
            sin_scr[g, rb, hw:sw] = sr[:, hw:sw]
            sin_scr[g, rf, sw:sw + hw] = si[:, 0:hw]
            sin_scr[g, rb, sw + hw:2 * sw] = si[:, hw:sw]
            wr = jnp.where(fwd, wc_scr[g, rf, 0:sw], wc_scr[g, rb, 0:sw])
            wi = jnp.where(fwd, wc_scr[g, rf, sw:2 * sw], wc_scr[g, rb, sw:2 * sw])
            ar = a_ref[g, 0]
            ai = a_ref[g, 1]
            out.append((ar * sr - ai * si + wr, ar * si + ai * sr + wi))
        return tuple(out)

    fin = lax.fori_loop(0, n_chunks, step, tuple((h0_ref[g, 0], h0_ref[g, 1]) for g in range(gb)),
                        unroll=True)
    for g in range(gb):
        sint = sin_scr[g].astype(BF16).T
        y2t = (jnp.dot(toept_ref[g], zt_scr[g], preferred_element_type=F32)
               + jnp.dot(vt_ref[g], sint, preferred_element_type=F32))
        for s in range(t):
            yt_scr[s, g * hc:(g + 1) * hc, :] = y2t[s * hc:(s + 1) * hc, :].astype(yt_scr.dtype)
        if with_state:
            fin_ref[g, 0] = fin[g][0]
            fin_ref[g, 1] = fin[g][1]
    for s in range(t):
        y_ref[s] = yt_scr[s].T


def _ssm(u3, ssm_ops, h0, with_state):
    toept, wt, vt, a_t = ssm_ops
    n_slabs, t, rows, cw = u3.shape
    batch = h0.shape[2]
    n_chunks = rows // batch
    g = toept.shape[0]
    gb = g // n_slabs
    slab = pl.BlockSpec((None, t, rows, cw), lambda i, j: (i, 0, 0, 0))
    blk = lambda a: pl.BlockSpec((gb,) + a.shape[1:], lambda i, j: (i,) + (0,) * (a.ndim - 1))
    out_specs = [slab]
    out_shape = [jax.ShapeDtypeStruct(u3.shape, u3.dtype)]
    if with_state:
        out_specs.append(blk(h0))
        out_shape.append(jax.ShapeDtypeStruct(h0.shape, F32))
    nk = toept.shape[-1]
    return dict(
        kernel=functools.partial(_ssm_kernel, batch=batch, n_chunks=n_chunks, with_state=with_state),
        grid=(g // gb, 1), in_specs=[slab, blk(toept), blk(wt), blk(vt), blk(a_t), blk(h0)],
        args=[u3, toept, wt, vt, a_t, h0], out_specs=out_specs, out_shape=out_shape,
        scratch=[pltpu.VMEM((gb, nk, rows), BF16),
                 pltpu.VMEM((gb, rows, wt.shape[1]), F32),
                 pltpu.VMEM((gb, rows, wt.shape[1]), F32),
                 pltpu.VMEM((t, cw, rows), u3.dtype)],
        name="ssm_state" if with_state else "ssm")


def _state_to_lanes(h0):
    b, nd, nc, g, p = h0.shape
    return h0.astype(F32).transpose(3, 2, 0, 1, 4).reshape(g, nc, b, nd * p)


def _state_from_lanes(fin, nd):
    g, nc, b, w = fin.shape
    return fin.reshape(g, nc, b, nd, w // nd).transpose(2, 3, 1, 0, 4)


def _outproj_kernel(x_ref, a_ref, y_ref, u_ref, gs_ref, gate_ref, dskip_ref,
                    wglu_ref, bglu_ref, wout_ref, g_ref, o_ref):
    nb, tp, d = x_ref.shape
    y = _scan_block(y_ref) + dskip_ref[...] * _scan_block(u_ref)
    t, srows, d_ssm = y.shape
    rows = lambda ref: ref[...].reshape(nb * tp, ref.shape[-1])
    y = y.reshape(t * srows, d_ssm)
    c0 = math.sqrt(2.0 / math.pi)
    hy = 0.5 * y
    ys = hy + hy * jnp.tanh(y * (c0 + (c0 * 0.044715) * (y * y)))
    z = jnp.dot(ys.astype(BF16), wglu_ref[...], preferred_element_type=F32) + bglu_ref[...]
    ys = _from_scan_layout((ys * jax.nn.sigmoid(z)).reshape(t, srows, d_ssm), nb) * rows(gs_ref).astype(F32)
    cat = jnp.concatenate([rows(a_ref), ys.astype(BF16)], axis=-1)
    out = jnp.dot(cat, wout_ref[...], preferred_element_type=F32)
    ms = jnp.mean(out * out, axis=-1, keepdims=True)
    out = out * lax.rsqrt(ms + EPS)
    o_ref[...] = x_ref[...] + (g_ref[...] * gate_ref[...]) * out.reshape(nb, tp, d)


def _outproj(x, a_out, y_ssm, u, gs, gate, d_skip, w_glu_bf, b_glu, w_out_bf, norm_post):
    b, l, d = x.shape
    d_attn = a_out.shape[-1]
    d_ssm = gs.shape[-1]
    tp = TOKEN_TILE // b
    t = SSM_CHUNK
    const = lambda a: pl.BlockSpec(a.shape, lambda j: (0,) * a.ndim)
    tok = lambda w: pl.BlockSpec((b, tp, w), lambda j: (0, j, 0))
    scan = pl.BlockSpec((d_ssm // SCAN_SLAB, t, tp // t * b, SCAN_SLAB), lambda j: (0, 0, j, 0))
    consts = [gate, d_skip.reshape(1, d_ssm).astype(F32), w_glu_bf, b_glu.reshape(1, d_ssm).astype(F32),
              w_out_bf, norm_post.reshape(1, d).astype(F32)]
    deep = lambda w: pl.BlockSpec((b, tp, w), lambda j: (0, j, 0), pipeline_mode=pl.Buffered(3))
    in_specs = [deep(d), deep(d_attn), scan, scan, deep(d_ssm)] + [const(a) for a in consts]
    stream = pltpu.emit_pipeline(_outproj_kernel, grid=(l // tp,), in_specs=in_specs, out_specs=[tok(d)])
    anywhere = pl.BlockSpec(memory_space=pl.ANY)
    return pl.pallas_call(
        lambda *refs: stream(*refs),
        in_specs=[anywhere] * len(in_specs),
        out_specs=anywhere,
        out_shape=jax.ShapeDtypeStruct((b, l, d), F32),
        compiler_params=_params(),
        name="outproj",
    )(x, a_out, y_ssm, u, gs, *consts)


def _mixer(x, mod, h0_lanes, cache_k, cache_v, rope_tabs, emit_caches, with_state, lam, lam_init,
           norm_pre, w_in, ssm_ops, subln, d_skip, w_glu_bf, b_glu, w_out_bf, norm_post):
    b, l, d = x.shape
    d_ssm = w_glu_bf.shape[0]
    d_attn = (w_in.shape[1] - 2 * d_ssm) // 4
    shift, scale, gate = mod
    q, k, v, ga, u, gs, *caches = _inproj(x, shift, scale, norm_pre, w_in, rope_tabs, emit_caches,
                                          d_attn, d_ssm)
    attn = _attention(lam, q, k, v, cache_k, cache_v, ga, subln, lam_init)
    scan = _ssm(u, ssm_ops, h0_lanes, with_state)
    if attn["grid"] == scan["grid"]:
        outs, (a_out,) = _launch(scan, attn)
    else:
        (a_out,), = _launch(attn)
        outs, = _launch(scan)
    y = _outproj(x, a_out, outs[0], u, gs, gate, d_skip, w_glu_bf, b_glu, w_out_bf, norm_post)
    fin = outs[1] if with_state else None
    return y, caches, fin


def kernel(x_prompt, x_sample, cache_k, cache_v, state_ssm, c, c_ctx, w_ada, b_ada, norm_pre, norm_post, w_in, lambda_qk, subln, ssm_A_re, ssm_A_im, ssm_log_dt, ssm_B_re, ssm_B_im, ssm_C_re, ssm_C_im, ssm_D, w_glu, b_glu, w_out):
    xp, xs = x_prompt, x_sample
    bp, lp, d = xp.shape
    bs, ls, _ = xs.shape
    depth = w_in.shape[0]
    nd = ssm_A_re.shape[1]
    g, p = ssm_A_re.shape[-2:]
    d_ssm = w_glu.shape[-1]
    d_attn = (w_in.shape[-1] - 2 * d_ssm) // 4
    assert bs + 1 <= COND_ROWS and nd == 2
    assert d_ssm % SCAN_SLAB == 0 and SCAN_SLAB % (d_ssm // g) == 0
    assert all(TOKEN_TILE % b == 0 and (TOKEN_TILE // b) % SSM_CHUNK == 0 and l % (TOKEN_TILE // b) == 0
               for b, l in ((bp, lp), (bs, ls)))
    assert all(l % min(ATTN_Q_TILE, l) == 0 and min(ATTN_Q_TILE, l) % min(ATTN_Q_SUB, l) == 0
               and b % max(1, ATTN_Q_TILE // l) == 0 for b, l in ((bp, lp), (bs, ls)))
    rope_tabs = _rope_tables(ls, d_attn)
    cond = jnp.zeros((COND_ROWS, d), F32).at[:bs].set(c.astype(F32)).at[bs].set(c_ctx.astype(F32))
    new_k, new_v, new_s = [], [], []
    for layer in range(depth):
        lam_init = 0.8 - 0.6 * math.exp(-0.3 * layer)
        m, ssm_ops, lam = _prep(
            cond, w_ada[layer], b_ada[layer],
            ssm_A_re[layer], ssm_A_im[layer], ssm_log_dt[layer], ssm_B_re[layer], ssm_B_im[layer],
            ssm_C_re[layer], ssm_C_im[layer], lambda_qk[layer], lam_init)
        mod_s = tuple(m[:bs, i * d:(i + 1) * d].reshape(bs, 1, d) for i in range(3))
        mod_p = tuple(m[bs:bs + 1, i * d:(i + 1) * d].reshape(1, 1, d) for i in range(3))
        shared = (lam, lam_init, norm_pre[layer], w_in[layer], ssm_ops, subln[layer],
                  ssm_D[layer], w_glu[layer].astype(BF16), b_glu[layer], w_out[layer].astype(BF16),
                  norm_post[layer])

        h0_p = jnp.zeros((g, 2, bp, nd * p), F32)
        xp, (k_p, v_p), fin = _mixer(xp, mod_p, h0_p, None, None, None, True, True, *shared)
        new_k.append(k_p.reshape(bp, lp, d_attn // V_DIM, V_DIM))
        new_v.append(v_p.reshape(bp, lp, d_attn // V_DIM, V_DIM))
        new_s.append(_state_from_lanes(fin, nd).astype(xp.dtype))

        h0_s = _state_to_lanes(state_ssm[:, layer])
        ck = cache_k[:, layer].reshape(bs, -1, V_DIM)
        cv = cache_v[:, layer].reshape(bs, -1, V_DIM)
        xs, _, _ = _mixer(xs, mod_s, h0_s, ck, cv, rope_tabs, False, False, *shared)

    return (xp, xs, jnp.stack(new_k, axis=1), jnp.stack(new_v, axis=1), jnp.stack(new_s, axis=1))
```
